```python
import math
import jax, jax.numpy as jnp
from jax import lax
import numpy as np

D_MODEL = 2048
BATCH = 4
SEQ = 2048
DEPTH = 4
DEC_BATCH = 32
DEC_SEQ = 8
PAST_LEN = 16384
PAGE_SIZE = 128

N_MIXERS = 3
N_SWA = (DEPTH + 2) // 3
N_GLA = (DEPTH + 1) // 3
N_RWKV = DEPTH // 3

ALPHA = (2 * DEPTH) ** 0.25
BETA = (8 * DEPTH) ** -0.25
LN_EPS = 1e-5

SWA_HEADS = 32
SWA_KV_HEADS = 4
SWA_GROUP = SWA_HEADS // SWA_KV_HEADS
SWA_HEAD_DIM = 64
SWA_WIDTH = SWA_HEADS * SWA_HEAD_DIM
SWA_KV_WIDTH = SWA_KV_HEADS * SWA_HEAD_DIM
WINDOW = 128
SWA_BLOCK = WINDOW
SWA_BUF = min(WINDOW, PAST_LEN)
ROT_DIM = SWA_HEAD_DIM // 4
ROPE_THETA = 500000.0
SWA_IN = SWA_WIDTH + 2 * SWA_KV_WIDTH + SWA_WIDTH

GLA_HEADS = 4
GLA_KEY_DIM = D_MODEL // 2
GLA_VAL_DIM = D_MODEL
GLA_DK = GLA_KEY_DIM // GLA_HEADS
GLA_DV = GLA_VAL_DIM // GLA_HEADS
GLA_GATE_RANK = 16
GLA_GATE_TEMP = 16.0
GLA_CHUNK = 64
GLA_NORM_EPS = 1e-5
GLA_IN = 2 * GLA_KEY_DIM + 2 * GLA_VAL_DIM + GLA_GATE_RANK

RWKV_HEAD_DIM = 64
RWKV_WIDTH = D_MODEL
RWKV_HEADS = RWKV_WIDTH // RWKV_HEAD_DIM
RWKV_DECAY_RANK = max(32, int(round(math.sqrt(D_MODEL) * 1.8 / 32)) * 32)
RWKV_A_RANK = max(32, int(round(math.sqrt(D_MODEL) * 1.8 / 32)) * 32)
RWKV_GN_EPS = 64e-5

kernel_name = 'hybrid_swa_gla_rwkv7_deepnorm_step'

F32 = jnp.float32


def layer_norm(x, g, b):
    xf = x.astype(F32)
    mu = jnp.mean(xf, -1, keepdims=True)
    var = jnp.mean(jnp.square(xf - mu), -1, keepdims=True)
    return ((xf - mu) * lax.rsqrt(var + LN_EPS) * g.astype(F32) + b.astype(F32)).astype(x.dtype)


def rope_partial(x, pos):
    half = ROT_DIM // 2
    inv = ROPE_THETA ** (-(jnp.arange(half, dtype=F32) * 2.0 / ROT_DIM))
    ang = pos[:, None] * inv[None, :]
    cos = jnp.cos(ang)[None, :, None, :]
    sin = jnp.sin(ang)[None, :, None, :]
    xf = x.astype(F32)
    x1, x2, rest = xf[..., :half], xf[..., half:ROT_DIM], xf[..., ROT_DIM:]
    out = jnp.concatenate([x1 * cos - x2 * sin, x2 * cos + x1 * sin, rest], axis=-1)
    return out.astype(x.dtype)


def sink_softmax(scores, mask, sink):
    s = jnp.where(mask, scores, -jnp.inf)
    m = jnp.maximum(jnp.max(s, -1, keepdims=True), sink)
    p = jnp.exp(s - m)
    return p / (jnp.sum(p, -1, keepdims=True) + jnp.exp(sink - m))


def swa_project(x, w_in, pos):
    B, L, _ = x.shape
    h = x @ w_in
    q, k, v, gate = jnp.split(h, [SWA_WIDTH, SWA_WIDTH + SWA_KV_WIDTH, SWA_WIDTH + 2 * SWA_KV_WIDTH], axis=-1)
    q = rope_partial(q.reshape(B, L, SWA_HEADS, SWA_HEAD_DIM), pos)
    k = rope_partial(k.reshape(B, L, SWA_KV_HEADS, SWA_HEAD_DIM), pos)
    v = v.reshape(B, L, SWA_KV_HEADS, SWA_HEAD_DIM)
    return q, k, v, gate


def swa_prompt(x, w_in, sink, w_out):
    B, S, _ = x.shape
    q, k, v, gate = swa_project(x, w_in, jnp.arange(S, dtype=F32))
    nb = S // SWA_BLOCK
    qb = q.astype(F32).reshape(B, nb, SWA_BLOCK, SWA_KV_HEADS, SWA_GROUP, SWA_HEAD_DIM)

    def band_keys(t):
        tb = t.astype(F32).reshape(B, nb, SWA_BLOCK, SWA_KV_HEADS, SWA_HEAD_DIM)
        prev = jnp.concatenate([jnp.zeros_like(tb[:, :1]), tb[:, :-1]], axis=1)
        return jnp.concatenate([prev, tb], axis=2)

    kk, vv = band_keys(k), band_keys(v)
    scores = jnp.einsum('bnqkgd,bnskd->bnkgqs', qb, kk) * (SWA_HEAD_DIM ** -0.5)
    qi = jnp.arange(SWA_BLOCK)[:, None]
    sj = jnp.arange(2 * SWA_BLOCK)[None, :]
    rel = qi + SWA_BLOCK - sj
    band = (rel >= 0) & (rel <= WINDOW)
    valid = (jnp.arange(nb)[:, None, None] > 0) | (sj >= SWA_BLOCK)[None]
    mask = (band[None] & valid)[None, :, None, None]
    snk = sink.astype(F32).reshape(SWA_KV_HEADS, SWA_GROUP)[None, None, :, :, None, None]
    probs = sink_softmax(scores, mask, snk)
    o = jnp.einsum('bnkgqs,bnskd->bnqkgd', probs, vv).reshape(B, S, SWA_WIDTH)
    out = (o.astype(x.dtype) * jax.nn.silu(gate)) @ w_out
    return out, k[:, S - SWA_BUF:], v[:, S - SWA_BUF:]


def swa_sample(x, cache_k, cache_v, w_in, sink, w_out):
    B, L, _ = x.shape
    q, k, v, gate = swa_project(x, w_in, PAST_LEN + jnp.arange(L, dtype=F32))
    kk = jnp.concatenate([cache_k.astype(k.dtype), k], axis=1)
    vv = jnp.concatenate([cache_v.astype(v.dtype), v], axis=1)
    qg = q.astype(F32).reshape(B, L, SWA_KV_HEADS, SWA_GROUP, SWA_HEAD_DIM)
    scores = jnp.einsum('bqkgd,bskd->bkgqs', qg, kk.astype(F32)) * (SWA_HEAD_DIM ** -0.5)
    rel = jnp.arange(L)[:, None] + SWA_BUF - jnp.arange(SWA_BUF + L)[None, :]
    mask = (rel >= 0) & (rel <= WINDOW)
    snk = sink.astype(F32).reshape(SWA_KV_HEADS, SWA_GROUP)[None, :, :, None, None]
    probs = sink_softmax(scores, mask, snk)
    o = jnp.einsum('bkgqs,bskd->bqkgd', probs, vv.astype(F32)).reshape(B, L, SWA_WIDTH)
    out = (o.astype(x.dtype) * jax.nn.silu(gate)) @ w_out
    return out, kk[:, -SWA_BUF:], vv[:, -SWA_BUF:]


def gla_project(x, w_in, w_a2, b_a):
    B, L, _ = x.shape
    h = x @ w_in
    q, k, v, gate, a_low = jnp.split(
        h, [GLA_KEY_DIM, 2 * GLA_KEY_DIM, 2 * GLA_KEY_DIM + GLA_VAL_DIM, 2 * GLA_KEY_DIM + 2 * GLA_VAL_DIM], axis=-1)
    g = jax.nn.log_sigmoid((a_low @ w_a2 + b_a).astype(F32)) / GLA_GATE_TEMP
    q = q.astype(F32).reshape(B, L, GLA_HEADS, GLA_DK) * (GLA_DK ** -0.5)
    k = k.astype(F32).reshape(B, L, GLA_HEADS, GLA_DK)
    v = v.astype(F32).reshape(B, L, GLA_HEADS, GLA_DV)
    g = g.reshape(B, L, GLA_HEADS, GLA_DK)
    return q, k, v, g, gate


def gla_chunk(state, q, k, v, g):
    L = q.shape[1]
    b = jnp.cumsum(g, axis=1)
    b_last = b[:, -1]
    causal = jnp.tril(jnp.ones((L, L), dtype=bool))[None, :, :, None, None]
    decay = jnp.exp(jnp.where(causal, b[:, :, None] - b[:, None, :], -jnp.inf))
    attn = jnp.einsum('bthk,btshk,bshk->bhts', q, decay, k)
    o = jnp.einsum('bthk,bhkv->bthv', q * jnp.exp(b), state) + jnp.einsum('bhts,bshv->bthv', attn, v)
    new_state = jnp.exp(b_last)[..., None] * state + jnp.einsum(
        'bshk,bshv->bhkv', k * jnp.exp(b_last[:, None] - b), v)
    return new_state, o


def gla_output(o, gate, norm_g, w_out, dtype):
    B, L = o.shape[0], o.shape[1]
    on = o * lax.rsqrt(jnp.mean(o * o, -1, keepdims=True) + GLA_NORM_EPS) * norm_g.astype(F32)
    return (on.reshape(B, L, GLA_VAL_DIM).astype(dtype) * jax.nn.silu(gate)) @ w_out


def gla_prompt(x, w_in, w_a2, b_a, norm_g, w_out):
    B, S, _ = x.shape
    q, k, v, g, gate = gla_project(x, w_in, w_a2, b_a)
    nc = S // GLA_CHUNK

    def to_chunks(t):
        return jnp.moveaxis(t.reshape(B, nc, GLA_CHUNK, t.shape[2], t.shape[3]), 1, 0)

    state0 = jnp.zeros((B, GLA_HEADS, GLA_DK, GLA_DV), F32)
    state, o = lax.scan(lambda s, inp: gla_chunk(s, inp[0], inp[1], inp[2], inp[3]), state0,
                        (to_chunks(q), to_chunks(k), to_chunks(v), to_chunks(g)))
    o = jnp.moveaxis(o, 0, 1).reshape(B, S, GLA_HEADS, GLA_DV)
    return gla_output(o, gate, norm_g, w_out, x.dtype), state


def gla_sample(x, state_in, w_in, w_a2, b_a, norm_g, w_out):
    q, k, v, g, gate = gla_project(x, w_in, w_a2, b_a)
    state, o = gla_chunk(state_in.astype(F32), q, k, v, g)
    return gla_output(o, gate, norm_g, w_out, x.dtype), state


def rwkv_step(S, inp):
    r_t, w_t, k_t, v_t, a_t, b_t = inp
    sa = jnp.einsum('bhij,bhj->bhi', S, a_t)
    S = S * w_t[:, :, None, :] + sa[..., None] * b_t[:, :, None, :] + v_t[..., None] * k_t[:, :, None, :]
    return S, jnp.einsum('bhij,bhj->bhi', S, r_t)


def rwkv_mixer(x, shift, state, mu, w_rkvg, w0, w1, w2, a0, a1, a2, k_k, k_a, r_k, gn_g, gn_b, w_out):
    B, L, _ = x.shape
    x_prev = jnp.concatenate([shift[:, None, :].astype(x.dtype), x[:, :-1]], axis=1)
    xx = x_prev - x
    xr, xw, xk, xv, xa, xg = (x + xx * mu[i].astype(x.dtype) for i in range(6))
    w_log = -jax.nn.softplus(-(w0 + jnp.tanh(xw @ w1) @ w2).astype(F32)) - 0.5
    decay = jnp.exp(-jnp.exp(w_log))
    a = jax.nn.sigmoid((a0 + (xa @ a1) @ a2).astype(F32))

    def hd(t):
        return t.astype(F32).reshape(B, L, RWKV_HEADS, RWKV_HEAD_DIM)

    def hn(p):
        return p.astype(F32).reshape(RWKV_HEADS, RWKV_HEAD_DIM)

    r, k, v, decay, a = hd(xr @ w_rkvg[0]), hd(xk @ w_rkvg[1]), hd(xv @ w_rkvg[2]), hd(decay), hd(a)
    gate = jax.nn.silu(xg @ w_rkvg[3])
    kk = k * hn(k_k)
    kk = kk / jnp.maximum(jnp.sqrt(jnp.sum(kk * kk, -1, keepdims=True)), 1e-12)
    k = k * (1.0 + (a - 1.0) * hn(k_a))

    def tm(t):
        return jnp.moveaxis(t, 1, 0)

    S, y = lax.scan(rwkv_step, state.astype(F32), (tm(r), tm(decay), tm(k), tm(v), tm(-kk), tm(kk * a)))
    y = jnp.moveaxis(y, 0, 1)
    ym = jnp.mean(y, -1, keepdims=True)
    yv = jnp.mean(jnp.square(y - ym), -1, keepdims=True)
    yn = (y - ym) * lax.rsqrt(yv + RWKV_GN_EPS) * hn(gn_g) + hn(gn_b)
    o = yn + jnp.sum(r * k * r_k.astype(F32), -1, keepdims=True) * v
    out = (o.reshape(B, L, RWKV_WIDTH).astype(x.dtype) * gate) @ w_out
    return out, S, x[:, -1]


def trunk(x, cache, w):
    prompt = cache is None
    B = x.shape[0]
    new_k, new_v, new_gla, new_wkv, new_shift = [], [], [], [], []
    for layer in range(DEPTH):
        kind, j = layer % N_MIXERS, layer // N_MIXERS
        if kind == 0:
            if prompt:
                h, nk, nv = swa_prompt(x, w['swa_w_in'][j], w['swa_sink'][j], w['swa_w_out'][j])
            else:
                h, nk, nv = swa_sample(x, cache['k'][j], cache['v'][j], w['swa_w_in'][j], w['swa_sink'][j],
                                       w['swa_w_out'][j])
            new_k.append(nk)
            new_v.append(nv)
        elif kind == 1:
            if prompt:
                h, st = gla_prompt(x, w['gla_w_in'][j], w['gla_w_a2'][j], w['gla_b_a'][j], w['gla_norm_g'][j],
                                   w['gla_w_out'][j])
            else:
                h, st = gla_sample(x, cache['gla'][j], w['gla_w_in'][j], w['gla_w_a2'][j], w['gla_b_a'][j],
                                   w['gla_norm_g'][j], w['gla_w_out'][j])
            new_gla.append(st)
        else:
            if prompt:
                shift0 = jnp.zeros((B, D_MODEL), x.dtype)
                s0 = jnp.zeros((B, RWKV_HEADS, RWKV_HEAD_DIM, RWKV_HEAD_DIM), F32)
            else:
                shift0, s0 = cache['shift'][j], cache['wkv'][j]
            h, st, sh = rwkv_mixer(x, shift0, s0, w['rwkv_mu'][j], w['rwkv_w_rkvg'][j], w['rwkv_w0'][j],
                                   w['rwkv_w1'][j], w['rwkv_w2'][j], w['rwkv_a0'][j], w['rwkv_a1'][j],
                                   w['rwkv_a2'][j], w['rwkv_k_k'][j], w['rwkv_k_a'][j], w['rwkv_r_k'][j],
                                   w['rwkv_gn_g'][j], w['rwkv_gn_b'][j], w['rwkv_w_out'][j])
            new_wkv.append(st)
            new_shift.append(sh)
        x = layer_norm(ALPHA * x + h, w['ln_g'][layer], w['ln_b'][layer])
    return x, jnp.stack(new_k), jnp.stack(new_v), jnp.stack(new_gla), jnp.stack(new_wkv), jnp.stack(new_shift)


def setup_inputs(seed: int = 0) -> dict:
    key = jax.random.key(seed)
    ks = iter(jax.random.split(key, 40))

    def nrm(shape, scale):
        return scale * jax.random.normal(next(ks), shape, F32)

    D = D_MODEL
    return {
        'x_prompt': nrm((BATCH, SEQ, D), 1.0),
        'x_sample': nrm((DEC_BATCH, DEC_SEQ, D), 1.0),
        'cache_swa_k': nrm((N_SWA, DEC_BATCH, SWA_BUF, SWA_KV_HEADS, SWA_HEAD_DIM), 1.0),
        'cache_swa_v': nrm((N_SWA, DEC_BATCH, SWA_BUF, SWA_KV_HEADS, SWA_HEAD_DIM), 1.0),
        'state_gla': nrm((N_GLA, DEC_BATCH, GLA_HEADS, GLA_DK, GLA_DV), 1.0),
        'state_rwkv': nrm((N_RWKV, DEC_BATCH, RWKV_HEADS, RWKV_HEAD_DIM, RWKV_HEAD_DIM), 0.5),
        'state_rwkv_shift': nrm((N_RWKV, DEC_BATCH, D), 1.0),
        'ln_g': 1.0 + nrm((DEPTH, D), 0.02),
        'ln_b': nrm((DEPTH, D), 0.02),
        'swa_w_in': nrm((N_SWA, D, SWA_IN), D ** -0.5),
        'swa_sink': nrm((N_SWA, SWA_HEADS), 0.5),
        'swa_w_out': nrm((N_SWA, SWA_WIDTH, D), SWA_WIDTH ** -0.5 * BETA),
        'gla_w_in': nrm((N_GLA, D, GLA_IN), D ** -0.5),
        'gla_w_a2': nrm((N_GLA, GLA_GATE_RANK, GLA_KEY_DIM), GLA_GATE_RANK ** -0.5),
        'gla_b_a': nrm((N_GLA, GLA_KEY_DIM), 0.1),
        'gla_norm_g': 1.0 + nrm((N_GLA, GLA_DV), 0.02),
        'gla_w_out': nrm((N_GLA, GLA_VAL_DIM, D), GLA_VAL_DIM ** -0.5 * BETA),
        'rwkv_mu': jax.random.uniform(next(ks), (N_RWKV, 6, D), F32),
        'rwkv_w_rkvg': nrm((N_RWKV, 4, D, RWKV_WIDTH), D ** -0.5),
        'rwkv_w0': nrm((N_RWKV, RWKV_WIDTH), 0.5) - 1.0,
        'rwkv_w1': nrm((N_RWKV, D, RWKV_DECAY_RANK), D ** -0.5),
        'rwkv_w2': nrm((N_RWKV, RWKV_DECAY_RANK, RWKV_WIDTH), 0.1 * RWKV_DECAY_RANK ** -0.5),
        'rwkv_a0': nrm((N_RWKV, RWKV_WIDTH), 0.1),
        'rwkv_a1': nrm((N_RWKV, D, RWKV_A_RANK), D ** -0.5),
        'rwkv_a2': nrm((N_RWKV, RWKV_A_RANK, RWKV_WIDTH), 0.1 * RWKV_A_RANK ** -0.5),
        'rwkv_k_k': 1.0 + nrm((N_RWKV, RWKV_WIDTH), 0.02),
        'rwkv_k_a': 1.0 + nrm((N_RWKV, RWKV_WIDTH), 0.02),
        'rwkv_r_k': nrm((N_RWKV, RWKV_HEADS, RWKV_HEAD_DIM), 0.1),
        'rwkv_gn_g': 1.0 + nrm((N_RWKV, RWKV_WIDTH), 0.02),
        'rwkv_gn_b': nrm((N_RWKV, RWKV_WIDTH), 0.02),
        'rwkv_w_out': nrm((N_RWKV, RWKV_WIDTH, D), RWKV_WIDTH ** -0.5 * BETA),
    }


def reference(x_prompt, x_sample, cache_swa_k, cache_swa_v, state_gla, state_rwkv, state_rwkv_shift,
              ln_g, ln_b, swa_w_in, swa_sink, swa_w_out, gla_w_in, gla_w_a2, gla_b_a, gla_norm_g, gla_w_out,
              rwkv_mu, rwkv_w_rkvg, rwkv_w0, rwkv_w1, rwkv_w2, rwkv_a0, rwkv_a1, rwkv_a2, rwkv_k_k, rwkv_k_a,
              rwkv_r_k, rwkv_gn_g, rwkv_gn_b, rwkv_w_out):
    w = dict(ln_g=ln_g, ln_b=ln_b, swa_w_in=swa_w_in, swa_sink=swa_sink, swa_w_out=swa_w_out,
             gla_w_in=gla_w_in, gla_w_a2=gla_w_a2, gla_b_a=gla_b_a, gla_norm_g=gla_norm_g, gla_w_out=gla_w_out,
             rwkv_mu=rwkv_mu, rwkv_w_rkvg=rwkv_w_rkvg, rwkv_w0=rwkv_w0, rwkv_w1=rwkv_w1, rwkv_w2=rwkv_w2,
             rwkv_a0=rwkv_a0, rwkv_a1=rwkv_a1, rwkv_a2=rwkv_a2, rwkv_k_k=rwkv_k_k, rwkv_k_a=rwkv_k_a,
             rwkv_r_k=rwkv_r_k, rwkv_gn_g=rwkv_gn_g, rwkv_gn_b=rwkv_gn_b, rwkv_w_out=rwkv_w_out)
    y_prompt, p_k, p_v, p_gla, p_wkv, p_shift = trunk(x_prompt, None, w)
    cache = dict(k=cache_swa_k, v=cache_swa_v, gla=state_gla, wkv=state_rwkv, shift=state_rwkv_shift)
    y_sample, s_k, s_v, s_gla, s_wkv, s_shift = trunk(x_sample, cache, w)
    return (y_prompt, y_sample, p_k, p_v, p_gla, p_wkv, p_shift, s_k, s_v, s_gla, s_wkv, s_shift)
```

```python
import functools
import math

import jax
import jax.numpy as jnp
from jax import lax
from jax.experimental import pallas as pl
from jax.experimental.pallas import tpu as pltpu

F32 = jnp.float32
BF16 = jnp.bfloat16

D_MODEL = 2048
DEPTH = 4
PAST_LEN = 16384
ALPHA = (2 * DEPTH) ** 0.25
LN_EPS = 1e-5

SWA_HEADS = 32
SWA_KV_HEADS = 4
SWA_GROUP = SWA_HEADS // SWA_KV_HEADS
SWA_HEAD_DIM = 64
SWA_WIDTH = SWA_HEADS * SWA_HEAD_DIM
SWA_KV_WIDTH = SWA_KV_HEADS * SWA_HEAD_DIM
WINDOW = 128
ROT_DIM = SWA_HEAD_DIM // 4
ROPE_THETA = 500000.0

GLA_HEADS = 4
GLA_KEY_DIM = D_MODEL // 2
GLA_VAL_DIM = D_MODEL
GLA_DK = GLA_KEY_DIM // GLA_HEADS
GLA_DV = GLA_VAL_DIM // GLA_HEADS
GLA_GATE_RANK = 16
GLA_GATE_TEMP = 16.0
GLA_CHUNK = 64
GLA_SUB = 16
GLA_NORM_EPS = 1e-5

RWKV_HEAD_DIM = 64
RWKV_HEADS = D_MODEL // RWKV_HEAD_DIM
RWKV_GN_EPS = 64e-5

LANES = 128
VMEM_LIMIT = 56 * 1024 * 1024
NEG_BIG = -1e30


def _cparams(sem):
    return pltpu.CompilerParams(dimension_semantics=sem, vmem_limit_bytes=VMEM_LIMIT)


def _silu(x):
    return x * (1.0 / (1.0 + jnp.exp(-x)))


def _log_sigmoid(z):
    return jnp.minimum(z, 0.0) - jnp.log(1.0 + jnp.exp(-jnp.abs(z)))


def _mm_kernel(x_ref, w_ref, o_ref):
    o_ref[...] = jnp.dot(x_ref[...].astype(BF16), w_ref[...], preferred_element_type=F32).astype(o_ref.dtype)


def _mm(x, w, col0, ncols, tm, tn=512, out_dtype=F32):
    M, K = x.shape
    assert M % tm == 0 and ncols % tn == 0 and col0 % tn == 0
    cb = col0 // tn
    return pl.pallas_call(
        _mm_kernel,
        grid=(M // tm, ncols // tn),
        in_specs=[pl.BlockSpec((tm, K), lambda i, j: (i, 0)),
                  pl.BlockSpec((K, tn), lambda i, j: (0, cb + j))],
        out_specs=pl.BlockSpec((tm, tn), lambda i, j: (i, j)),
        out_shape=jax.ShapeDtypeStruct((M, ncols), out_dtype),
        compiler_params=_cparams(("parallel", "arbitrary")),
        name="proj_mm",
    )(x, w)


def _mm_ln_kernel(a_ref, w_ref, x_ref, g_ref, b_ref, o_ref):
    h = jnp.dot(a_ref[...], w_ref[...], preferred_element_type=F32)
    z = ALPHA * x_ref[...] + h
    mu = jnp.mean(z, axis=-1, keepdims=True)
    zc = z - mu
    var = jnp.mean(zc * zc, axis=-1, keepdims=True)
    o_ref[...] = zc * lax.rsqrt(var + LN_EPS) * g_ref[...] + b_ref[...]


def _mm_ln(a, w, x, g, b, tm=256):
    M, K = a.shape
    D = w.shape[1]
    return pl.pallas_call(
        _mm_ln_kernel,
        grid=(M // tm,),
        in_specs=[pl.BlockSpec((tm, K), lambda i: (i, 0)),
                  pl.BlockSpec((K, D), lambda i: (0, 0)),
                  pl.BlockSpec((tm, D), lambda i: (i, 0)),
                  pl.BlockSpec((1, D), lambda i: (0, 0)),
                  pl.BlockSpec((1, D), lambda i: (0, 0))],
        out_specs=pl.BlockSpec((tm, D), lambda i: (i, 0)),
        out_shape=jax.ShapeDtypeStruct((M, D), F32),
        compiler_params=_cparams(("parallel",)),
        name="out_proj_ln",
    )(a, w, x, g.reshape(1, D), b.reshape(1, D))


def _rope_tables(pos):
    half = ROT_DIM // 2
    inv = ROPE_THETA ** (-(jnp.arange(half, dtype=F32) * 2.0 / ROT_DIM))
    ang = pos[:, None] * inv[None, :]
    cos, sin = jnp.cos(ang), jnp.sin(ang)
    L = pos.shape[0]
    ones = jnp.ones((L, SWA_HEAD_DIM - ROT_DIM), F32)
    zeros_r = jnp.zeros((L, SWA_HEAD_DIM - ROT_DIM), F32)
    zeros_h = jnp.zeros((L, half), F32)
    c = jnp.concatenate([cos, cos, ones], axis=1)
    s1 = jnp.concatenate([-sin, zeros_h, zeros_r], axis=1)
    s2 = jnp.concatenate([zeros_h, sin, zeros_r], axis=1)
    rep = LANES // SWA_HEAD_DIM
    return jnp.tile(c, (1, rep)), jnp.tile(s1, (1, rep)), jnp.tile(s2, (1, rep))


def _rope128(x, c, s1, s2):
    return x * c + pltpu.roll(x, LANES - ROT_DIM // 2, 1) * s1 + pltpu.roll(x, ROT_DIM // 2, 1) * s2


def _rope_wide(x, c, s1, s2):
    return jnp.concatenate([_rope128(x[:, i * LANES:(i + 1) * LANES], c, s1, s2)
                            for i in range(x.shape[1] // LANES)], axis=1)


def _swa_kernel(sink_ref, q_ref, kc_ref, vc_ref, kp_ref, vp_ref, gate_ref, tc_ref, tp_ref,
                o_ref, ko_ref, vo_ref, *, lq, prompt):
    c, s1, s2 = tc_ref[0], tc_ref[1], tc_ref[2]
    k_cur = _rope_wide(kc_ref[...], c, s1, s2)
    v_cur = vc_ref[...]
    if prompt:
        k_prev = _rope_wide(kp_ref[...], tp_ref[0], tp_ref[1], tp_ref[2])
        v_prev = vp_ref[...]
        ko_ref[0] = k_cur
        vo_ref[0] = v_cur
    else:
        k_prev = kp_ref[0]
        v_prev = vp_ref[0]
        ko_ref[0] = jnp.concatenate([k_prev[lq:], k_cur], axis=0)
        vo_ref[0] = jnp.concatenate([v_prev[lq:], v_cur], axis=0)
        zpad = jnp.zeros((WINDOW - lq, SWA_KV_WIDTH), F32)
        k_cur = jnp.concatenate([k_cur, zpad], axis=0)
        v_cur = jnp.concatenate([v_cur, zpad], axis=0)
    k_all = jnp.concatenate([k_prev, k_cur], axis=0).astype(BF16)
    v_all = jnp.concatenate([v_prev, v_cur], axis=0).astype(BF16)
    nk = 2 * WINDOW
    qi = lax.broadcasted_iota(jnp.int32, (lq, nk), 0)
    sj = lax.broadcasted_iota(jnp.int32, (lq, nk), 1)
    rel = qi + WINDOW - sj
    mask = jnp.logical_and(rel >= 0, rel <= WINDOW)
    if prompt:
        lo = jnp.where(pl.program_id(1) == 0, WINDOW, 0)
        mask = jnp.logical_and(mask, sj >= lo)
    scale = SWA_HEAD_DIM ** -0.5
    for p in range(SWA_HEADS // 2):
        q2 = _rope128(q_ref[:, p * LANES:(p + 1) * LANES], c, s1, s2).astype(BF16)
        outs = []
        for hh in range(2):
            h = 2 * p + hh
            kh = h // SWA_GROUP
            qh = q2[:, hh * SWA_HEAD_DIM:(hh + 1) * SWA_HEAD_DIM]
            kk = k_all[:, kh * SWA_HEAD_DIM:(kh + 1) * SWA_HEAD_DIM]
            vv = v_all[:, kh * SWA_HEAD_DIM:(kh + 1) * SWA_HEAD_DIM]
            s = lax.dot_general(qh, kk, (((1,), (1,)), ((), ())), preferred_element_type=F32) * scale
            s = jnp.where(mask, s, NEG_BIG)
            snk = sink_ref[h]
            m = jnp.maximum(jnp.max(s, axis=-1, keepdims=True), snk)
            e = jnp.exp(s - m)
            den = jnp.sum(e, axis=-1, keepdims=True) + jnp.exp(snk - m)
            probs = (e / den).astype(BF16)
            outs.append(jnp.dot(probs, vv, preferred_element_type=F32))
        o2 = jnp.concatenate(outs, axis=1)
        g2 = gate_ref[:, p * LANES:(p + 1) * LANES]
        o_ref[:, p * LANES:(p + 1) * LANES] = (o2 * _silu(g2)).astype(o_ref.dtype)


def _swa_prompt(qkv, gate, sink, B, S):
    nb = S // WINDOW
    pos = jnp.arange(S, dtype=F32)
    tabs = jnp.stack(_rope_tables(pos))
    kcol = SWA_WIDTH // SWA_KV_WIDTH
    prev = lambda b, n: b * nb + jnp.maximum(n - 1, 0)
    return pl.pallas_call(
        functools.partial(_swa_kernel, lq=WINDOW, prompt=True),
        grid=(B, nb),
        in_specs=[pl.BlockSpec(memory_space=pltpu.SMEM),
                  pl.BlockSpec((WINDOW, SWA_WIDTH), lambda b, n: (b * nb + n, 0)),
                  pl.BlockSpec((WINDOW, SWA_KV_WIDTH), lambda b, n: (b * nb + n, kcol)),
                  pl.BlockSpec((WINDOW, SWA_KV_WIDTH), lambda b, n: (b * nb + n, kcol + 1)),
                  pl.BlockSpec((WINDOW, SWA_KV_WIDTH), lambda b, n: (prev(b, n), kcol)),
                  pl.BlockSpec((WINDOW, SWA_KV_WIDTH), lambda b, n: (prev(b, n), kcol + 1)),
                  pl.BlockSpec((WINDOW, SWA_WIDTH), lambda b, n: (b * nb + n, 0)),
                  pl.BlockSpec((3, WINDOW, LANES), lambda b, n: (0, n, 0)),
                  pl.BlockSpec((3, WINDOW, LANES), lambda b, n: (0, jnp.maximum(n - 1, 0), 0))],
        out_specs=[pl.BlockSpec((WINDOW, SWA_WIDTH), lambda b, n: (b * nb + n, 0)),
                   pl.BlockSpec((1, WINDOW, SWA_KV_WIDTH), lambda b, n: (b, 0, 0)),
                   pl.BlockSpec((1, WINDOW, SWA_KV_WIDTH), lambda b, n: (b, 0, 0))],
        out_shape=[jax.ShapeDtypeStruct((B * S, SWA_WIDTH), BF16),
                   jax.ShapeDtypeStruct((B, WINDOW, SWA_KV_WIDTH), F32),
                   jax.ShapeDtypeStruct((B, WINDOW, SWA_KV_WIDTH), F32)],
        compiler_params=_cparams(("parallel", "arbitrary")),
        name="swa_prompt",
    )(sink, qkv, qkv, qkv, qkv, qkv, gate, tabs, tabs)


def _swa_sample(qkv, gate, sink, cache_k, cache_v, B, L):
    pos = PAST_LEN + jnp.arange(L, dtype=F32)
    tabs = jnp.stack(_rope_tables(pos))
    kcol = SWA_WIDTH // SWA_KV_WIDTH
    return pl.pallas_call(
        functools.partial(_swa_kernel, lq=L, prompt=False),
        grid=(B,),
        in_specs=[pl.BlockSpec(memory_space=pltpu.SMEM),
                  pl.BlockSpec((L, SWA_WIDTH), lambda b: (b, 0)),
                  pl.BlockSpec((L, SWA_KV_WIDTH), lambda b: (b, kcol)),
                  pl.BlockSpec((L, SWA_KV_WIDTH), lambda b: (b, kcol + 1)),
                  pl.BlockSpec((1, WINDOW, SWA_KV_WIDTH), lambda b: (b, 0, 0)),
                  pl.BlockSpec((1, WINDOW, SWA_KV_WIDTH), lambda b: (b, 0, 0)),
                  pl.BlockSpec((L, SWA_WIDTH), lambda b: (b, 0)),
                  pl.BlockSpec((3, L, LANES), lambda b: (0, 0, 0)),
                  pl.BlockSpec((3, L, LANES), lambda b: (0, 0, 0))],
        out_specs=[pl.BlockSpec((L, SWA_WIDTH), lambda b: (b, 0)),
                   pl.BlockSpec((1, WINDOW, SWA_KV_WIDTH), lambda b: (b, 0, 0)),
                   pl.BlockSpec((1, WINDOW, SWA_KV_WIDTH), lambda b: (b, 0, 0))],
        out_shape=[jax.ShapeDtypeStruct((B * L, SWA_WIDTH), BF16),
                   jax.ShapeDtypeStruct((B, WINDOW, SWA_KV_WIDTH), F32),
                   jax.ShapeDtypeStruct((B, WINDOW, SWA_KV_WIDTH), F32)],
        compiler_params=_cparams(("parallel",)),
        name="swa_sample",
    )(sink, qkv, qkv, qkv, cache_k, cache_v, gate, tabs, tabs)


def _swa_layer(x, cache, w_in, sink, w_out, ln_g, ln_b, B, L, tm):
    qkv = _mm(x, w_in, 0, SWA_WIDTH + 2 * SWA_KV_WIDTH, tm)
    gate = _mm(x, w_in, SWA_WIDTH + 2 * SWA_KV_WIDTH, SWA_WIDTH, tm)
    if cache is None:
        o, nk, nv = _swa_prompt(qkv, gate, sink, B, L)
    else:
        ck = cache[0].reshape(B, WINDOW, SWA_KV_WIDTH)
        cv = cache[1].reshape(B, WINDOW, SWA_KV_WIDTH)
        o, nk, nv = _swa_sample(qkv, gate, sink, ck, cv, B, L)
    shape = (B, WINDOW, SWA_KV_HEADS, SWA_HEAD_DIM)
    return _mm_ln(o, w_out, x, ln_g, ln_b), nk.reshape(shape), nv.reshape(shape)


def _gla_gate_kernel(x_ref, wl_ref, wa_ref, ba_ref, g_ref):
    a_low = jnp.dot(x_ref[...].astype(BF16), wl_ref[...], preferred_element_type=F32)
    z = jnp.dot(a_low.astype(BF16), wa_ref[...], preferred_element_type=F32) + ba_ref[...]
    g_ref[...] = _log_sigmoid(z) * (1.0 / GLA_GATE_TEMP)


def _gla_gate(x, w_low, w_a2, b_a, tm):
    M = x.shape[0]
    return pl.pallas_call(
        _gla_gate_kernel,
        grid=(M // tm,),
        in_specs=[pl.BlockSpec((tm, D_MODEL), lambda i: (i, 0)),
                  pl.BlockSpec((D_MODEL, LANES), lambda i: (0, 0)),
                  pl.BlockSpec((LANES, GLA_KEY_DIM), lambda i: (0, 0)),
                  pl.BlockSpec((1, GLA_KEY_DIM), lambda i: (0, 0))],
        out_specs=pl.BlockSpec((tm, GLA_KEY_DIM), lambda i: (i, 0)),
        out_shape=jax.ShapeDtypeStruct((M, GLA_KEY_DIM), F32),
        compiler_params=_cparams(("parallel",)),
        name="gla_gate",
    )(x, w_low, w_a2, b_a.reshape(1, GLA_KEY_DIM))


def _gla_chunk_kernel(q_ref, k_ref, v_ref, gate_ref, g_ref, ng_ref, s0_ref, o_ref, so_ref, st_ref, *, C, SB):
    c_idx = pl.program_id(2)

    @pl.when(c_idx == 0)
    def _():
        st_ref[...] = s0_ref[0, 0].T

    g = g_ref[...]
    ti = lax.broadcasted_iota(jnp.int32, (C, C), 0)
    si = lax.broadcasted_iota(jnp.int32, (C, C), 1)
    tril = (si <= ti).astype(F32)
    b = jnp.dot(tril, g, preferred_element_type=F32, precision=lax.Precision.HIGHEST)
    q = q_ref[...] * (GLA_DK ** -0.5)
    k = k_ref[...]
    v16 = v_ref[...].astype(BF16)
    st = st_ref[...]
    qe = (q * jnp.exp(b)).astype(BF16)
    o_inter = lax.dot_general(qe, st.astype(BF16), (((1,), (1,)), ((), ())), preferred_element_type=F32)
    row = lax.broadcasted_iota(jnp.int32, (C, GLA_DK), 0)
    a_parts = []
    for i in range(C // SB):
        r0, r1 = i * SB, (i + 1) * SB
        bn = b[r0:r0 + 1, :]
        qi = (q[r0:r1] * jnp.exp(b[r0:r1] - bn)).astype(BF16)
        kj = (k * jnp.exp(jnp.where(row < r1, bn - b, 0.0))).astype(BF16)
        a_parts.append(lax.dot_general(qi, kj, (((1,), (1,)), ((), ())), preferred_element_type=F32))
    a = jnp.concatenate(a_parts, axis=0) if len(a_parts) > 1 else a_parts[0]
    a = jnp.where(si <= ti, a, 0.0)
    o = o_inter + jnp.dot(a.astype(BF16), v16, preferred_element_type=F32)
    b_last = b[C - 1:C, :]
    kd = (k * jnp.exp(b_last - b)).astype(BF16)
    st_new = st * jnp.exp(b_last) + lax.dot_general(v16, kd, (((0,), (0,)), ((), ())), preferred_element_type=F32)
    st_ref[...] = st_new

    @pl.when(c_idx == pl.num_programs(2) - 1)
    def _():
        so_ref[0, 0] = st_new.T

    on = o * lax.rsqrt(jnp.mean(o * o, axis=-1, keepdims=True) + GLA_NORM_EPS) * ng_ref[...]
    o_ref[...] = (on * _silu(gate_ref[...])).astype(o_ref.dtype)


def _gla_chunk(h, g, norm_g, state0, B, L):
    C = min(GLA_CHUNK, L)
    SB = min(GLA_SUB, C)
    nc = L // C
    kq = GLA_KEY_DIM // GLA_DK
    vq = 2 * GLA_KEY_DIM // GLA_DV
    gq = vq + GLA_HEADS
    row = lambda b, hd, c: b * nc + c
    return pl.pallas_call(
        functools.partial(_gla_chunk_kernel, C=C, SB=SB),
        grid=(B, GLA_HEADS, nc),
        in_specs=[pl.BlockSpec((C, GLA_DK), lambda b, hd, c: (row(b, hd, c), hd)),
                  pl.BlockSpec((C, GLA_DK), lambda b, hd, c: (row(b, hd, c), kq + hd)),
                  pl.BlockSpec((C, GLA_DV), lambda b, hd, c: (row(b, hd, c), vq + hd)),
                  pl.BlockSpec((C, GLA_DV), lambda b, hd, c: (row(b, hd, c), gq + hd)),
                  pl.BlockSpec((C, GLA_DK), lambda b, hd, c: (row(b, hd, c), hd)),
                  pl.BlockSpec((1, GLA_DV), lambda b, hd, c: (0, 0)),
                  pl.BlockSpec((1, 1, GLA_DK, GLA_DV), lambda b, hd, c: (b, hd, 0, 0))],
        out_specs=[pl.BlockSpec((C, GLA_DV), lambda b, hd, c: (row(b, hd, c), hd)),
                   pl.BlockSpec((1, 1, GLA_DK, GLA_DV), lambda b, hd, c: (b, hd, 0, 0))],
        out_shape=[jax.ShapeDtypeStruct((B * L, GLA_VAL_DIM), BF16),
                   jax.ShapeDtypeStruct((B, GLA_HEADS, GLA_DK, GLA_DV), F32)],
        scratch_shapes=[pltpu.VMEM((GLA_DV, GLA_DK), F32)],
        compiler_params=_cparams(("parallel", "parallel", "arbitrary")),
        name="gla_chunk",
    )(h, h, h, h, g, norm_g.reshape(1, GLA_DV), state0)


def _gla_layer(x, state, w_in, w_low, w_a2, b_a, norm_g, w_out, ln_g, ln_b, B, L, tm):
    h = _mm(x, w_in, 0, 2 * GLA_KEY_DIM + 2 * GLA_VAL_DIM, tm)
    g = _gla_gate(x, w_low, w_a2, b_a, tm)
    if state is None:
        state = jnp.zeros((B, GLA_HEADS, GLA_DK, GLA_DV), F32)
    o, st = _gla_chunk(h, g, norm_g, state, B, L)
    return _mm_ln(o, w_out, x, ln_g, ln_b), st


def _rwkv_proj_kernel(x_ref, xp_ref, mu_ref, w_ref, o_ref):
    x = x_ref[...]
    xm = x + (xp_ref[...] - x) * mu_ref[0]
    o_ref[0] = jnp.dot(xm.astype(BF16), w_ref[0], preferred_element_type=F32)


def _rwkv_proj(x, xprev, mu4, w4, tm, tn=512):
    M = x.shape[0]
    return pl.pallas_call(
        _rwkv_proj_kernel,
        grid=(M // tm, 4, D_MODEL // tn),
        in_specs=[pl.BlockSpec((tm, D_MODEL), lambda i, m, j: (i, 0)),
                  pl.BlockSpec((tm, D_MODEL), lambda i, m, j: (i, 0)),
                  pl.BlockSpec((1, 1, D_MODEL), lambda i, m, j: (m, 0, 0)),
                  pl.BlockSpec((1, D_MODEL, tn), lambda i, m, j: (m, 0, j))],
        out_specs=pl.BlockSpec((1, tm, tn), lambda i, m, j: (m, i, j)),
        out_shape=jax.ShapeDtypeStruct((4, M, D_MODEL), F32),
        compiler_params=_cparams(("parallel", "arbitrary", "arbitrary")),
        name="rwkv_proj",
    )(x, xprev, mu4, w4)


def _rwkv_lora_kernel(x_ref, xp_ref, mu_ref, w1_ref, w2_ref, w0_ref, a1_ref, a2_ref, a0_ref, dec_ref, a_ref):
    x = x_ref[...]
    xx = xp_ref[...] - x
    xw = (x + xx * mu_ref[0]).astype(BF16)
    xa = (x + xx * mu_ref[1]).astype(BF16)
    t = jnp.tanh(jnp.dot(xw, w1_ref[...], preferred_element_type=F32))
    wl = w0_ref[...] + jnp.dot(t.astype(BF16), w2_ref[...], preferred_element_type=F32)
    w_log = _log_sigmoid(wl) - 0.5
    dec_ref[...] = jnp.exp(-jnp.exp(w_log))
    al = jnp.dot(xa, a1_ref[...], preferred_element_type=F32)
    az = a0_ref[...] + jnp.dot(al.astype(BF16), a2_ref[...], preferred_element_type=F32)
    a_ref[...] = 1.0 / (1.0 + jnp.exp(-az))


def _rwkv_lora(x, xprev, mu2, w1, w2, w0, a1, a2, a0, tm):
    M = x.shape[0]
    R = w1.shape[1]
    full = lambda shape: pl.BlockSpec(shape, lambda i: tuple(0 for _ in shape))
    return pl.pallas_call(
        _rwkv_lora_kernel,
        grid=(M // tm,),
        in_specs=[pl.BlockSpec((tm, D_MODEL), lambda i: (i, 0)),
                  pl.BlockSpec((tm, D_MODEL), lambda i: (i, 0)),
                  full((2, 1, D_MODEL)), full((D_MODEL, R)), full((R, D_MODEL)), full((1, D_MODEL)),
                  full((D_MODEL, R)), full((R, D_MODEL)), full((1, D_MODEL))],
        out_specs=[pl.BlockSpec((tm, D_MODEL), lambda i: (i, 0)),
                   pl.BlockSpec((tm, D_MODEL), lambda i: (i, 0))],
        out_shape=[jax.ShapeDtypeStruct((M, D_MODEL), F32), jax.ShapeDtypeStruct((M, D_MODEL), F32)],
        compiler_params=_cparams(("parallel",)),
        name="rwkv_lora",
    )(x, xprev, mu2, w1, w2, w0.reshape(1, D_MODEL), a1, a2, a0.reshape(1, D_MODEL))


def _head_ones():
    r = lax.broadcasted_iota(jnp.int32, (LANES, LANES), 0) // RWKV_HEAD_DIM
    c = lax.broadcasted_iota(jnp.int32, (LANES, LANES), 1) // RWKV_HEAD_DIM
    e = (r == c).astype(BF16)
    return jnp.concatenate([e, e], axis=0)


def _segsum(x, e2):
    hi = x.astype(BF16)
    lo = (x - hi.astype(F32)).astype(BF16)
    return jnp.dot(jnp.concatenate([hi, lo], axis=1), e2, preferred_element_type=F32)


RWKV_PAIRS = 8
RWKV_TSUB = 8


def _rwkv_scan_kernel(r_ref, k_ref, v_ref, gt_ref, dec_ref, a_ref, kk_ref, ka_ref, rk_ref, gng_ref, gnb_ref, s0_ref,
                      o_ref, so_ref, st_ref, av_ref, bv_ref, kv_ref, y_ref, *, tc):
    t_idx = pl.program_id(2)
    e2 = _head_ones()
    N = RWKV_HEAD_DIM

    @pl.when(t_idx == 0)
    def _():
        for p in range(RWKV_PAIRS):
            st_ref[p] = jnp.concatenate([s0_ref[0, 2 * p], s0_ref[0, 2 * p + 1]], axis=1)

    for p in range(RWKV_PAIRS):
        ls = slice(p * LANES, (p + 1) * LANES)
        k = k_ref[0, :, ls]
        a = a_ref[:, ls]
        kk = k * kk_ref[:, ls]
        nrm = jnp.maximum(jnp.sqrt(_segsum(kk * kk, e2)), 1e-12)
        kk = kk / nrm
        av_ref[:, ls] = -kk
        bv_ref[:, ls] = kk * a
        kv_ref[:, ls] = k * (1.0 + (a - 1.0) * ka_ref[:, ls])

    ri = lax.broadcasted_iota(jnp.int32, (N, LANES), 0)
    ci = lax.broadcasted_iota(jnp.int32, (N, LANES), 1)
    diag = (ri == ci % N).astype(F32)

    def group(gi, carry):
        t0 = pl.multiple_of(gi * RWKV_TSUB, RWKV_TSUB)
        rows = pl.ds(t0, RWKV_TSUB)
        for p in range(RWKV_PAIRS):
            ls = slice(p * LANES, (p + 1) * LANES)
            r8 = r_ref[0, rows, ls]
            w8 = dec_ref[rows, ls]
            v8 = v_ref[0, rows, ls]
            a8 = av_ref[rows, ls]
            b8 = bv_ref[rows, ls]
            k8 = kv_ref[rows, ls]
            S = st_ref[p]
            ys = []
            for s in range(RWKV_TSUB):
                bc = lambda z: jnp.broadcast_to(z[s:s + 1, :], (N, LANES))
                sa = _segsum(S * bc(a8), e2)
                vb = _segsum(bc(v8) * diag, e2)
                S = S * bc(w8) + sa * bc(b8) + vb * bc(k8)
                yb = _segsum(S * bc(r8), e2)
                ys.append(jnp.sum(yb * diag, axis=0, keepdims=True))
            st_ref[p] = S
            y_ref[rows, ls] = jnp.concatenate(ys, axis=0)
        return carry

    lax.fori_loop(0, tc // RWKV_TSUB, group, 0)

    @pl.when(t_idx == pl.num_programs(2) - 1)
    def _():
        for p in range(RWKV_PAIRS):
            S = st_ref[p]
            so_ref[0, 2 * p] = S[:, :N]
            so_ref[0, 2 * p + 1] = S[:, N:]

    inv_n = 1.0 / N
    for p in range(RWKV_PAIRS):
        ls = slice(p * LANES, (p + 1) * LANES)
        y = y_ref[:, ls]
        ym = _segsum(y, e2) * inv_n
        yc = y - ym
        yv = _segsum(yc * yc, e2) * inv_n
        yn = yc * lax.rsqrt(yv + RWKV_GN_EPS) * gng_ref[:, ls] + gnb_ref[:, ls]
        bonus = _segsum(r_ref[0, :, ls] * kv_ref[:, ls] * rk_ref[:, ls], e2)
        o = yn + bonus * v_ref[0, :, ls]
        o_ref[:, ls] = (o * _silu(gt_ref[0, :, ls])).astype(o_ref.dtype)


def _rwkv_scan(proj, dec, a, k_k, k_a, r_k, gn_g, gn_b, state0, B, L):
    tc = min(L, 128)
    nt = L // tc
    W = RWKV_PAIRS * LANES
    ng = D_MODEL // W
    hpg = 2 * RWKV_PAIRS
    row = lambda b, g, t: b * nt + t
    tok = lambda m: pl.BlockSpec((1, tc, W), lambda b, g, t: (m, row(b, g, t), g))
    vec = pl.BlockSpec((tc, W), lambda b, g, t: (row(b, g, t), g))
    par = pl.BlockSpec((1, W), lambda b, g, t: (0, g))
    st = pl.BlockSpec((1, hpg, RWKV_HEAD_DIM, RWKV_HEAD_DIM), lambda b, g, t: (b, g, 0, 0))
    return pl.pallas_call(
        functools.partial(_rwkv_scan_kernel, tc=tc),
        grid=(B, ng, nt),
        in_specs=[tok(0), tok(1), tok(2), tok(3), vec, vec, par, par, par, par, par, st],
        out_specs=[vec, st],
        out_shape=[jax.ShapeDtypeStruct((B * L, D_MODEL), BF16),
                   jax.ShapeDtypeStruct((B, RWKV_HEADS, RWKV_HEAD_DIM, RWKV_HEAD_DIM), F32)],
        scratch_shapes=[pltpu.VMEM((RWKV_PAIRS, RWKV_HEAD_DIM, LANES), F32),
                        pltpu.VMEM((tc, W), F32), pltpu.VMEM((tc, W), F32), pltpu.VMEM((tc, W), F32),
                        pltpu.VMEM((tc, W), F32)],
        compiler_params=_cparams(("parallel", "parallel", "arbitrary")),
        name="rwkv_scan",
    )(proj, proj, proj, proj, dec, a, k_k.reshape(1, D_MODEL), k_a.reshape(1, D_MODEL), r_k.reshape(1, D_MODEL),
      gn_g.reshape(1, D_MODEL), gn_b.reshape(1, D_MODEL), state0)


def _rwkv_layer(x, shift, state, p, ln_g, ln_b, B, L, tm):
    x3 = x.reshape(B, L, D_MODEL)
    xprev = jnp.concatenate([shift[:, None, :], x3[:, :-1]], axis=1).reshape(B * L, D_MODEL)
    proj = _rwkv_proj(x, xprev, p["mu4"], p["w4"], tm)
    dec, a = _rwkv_lora(x, xprev, p["mu2"], p["w1"], p["w2"], p["w0"], p["a1"], p["a2"], p["a0"], tm)
    o, st = _rwkv_scan(proj, dec, a, p["k_k"], p["k_a"], p["r_k"], p["gn_g"], p["gn_b"], state, B, L)
    return _mm_ln(o, p["w_out"], x, ln_g, ln_b), st, x3[:, -1]


def _pad_rank(w, axis):
    pad = [(0, 0), (0, 0)]
    pad[axis] = (0, LANES - w.shape[axis])
    return jnp.pad(w, pad).astype(BF16)


def _trunk(x3, cache, w):
    B, L, _ = x3.shape
    prompt = cache is None
    tm = 512 if B * L >= 512 else B * L
    x = x3.reshape(B * L, D_MODEL)
    new_k, new_v, new_gla, new_wkv, new_shift = [], [], [], [], []
    for layer in range(DEPTH):
        kind, j = layer % 3, layer // 3
        g, b = w["ln_g"][layer], w["ln_b"][layer]
        if kind == 0:
            c = None if prompt else (cache["k"][j], cache["v"][j])
            x, nk, nv = _swa_layer(x, c, w["swa_w_in"][j], w["swa_sink"][j], w["swa_w_out"][j], g, b, B, L, tm)
            new_k.append(nk)
            new_v.append(nv)
        elif kind == 1:
            st = None if prompt else cache["gla"][j]
            x, st = _gla_layer(x, st, w["gla_w_in"][j], w["gla_w_low"][j], w["gla_w_a2"][j], w["gla_b_a"][j],
                               w["gla_norm_g"][j], w["gla_w_out"][j], g, b, B, L, tm)
            new_gla.append(st)
        else:
            if prompt:
                shift0 = jnp.zeros((B, D_MODEL), F32)
                s0 = jnp.zeros((B, RWKV_HEADS, RWKV_HEAD_DIM, RWKV_HEAD_DIM), F32)
            else:
                shift0, s0 = cache["shift"][j], cache["wkv"][j]
            x, st, sh = _rwkv_layer(x, shift0, s0, w["rwkv"][j], g, b, B, L, tm)
            new_wkv.append(st)
            new_shift.append(sh)
    return (x.reshape(B, L, D_MODEL), jnp.stack(new_k), jnp.stack(new_v), jnp.stack(new_gla), jnp.stack(new_wkv),
            jnp.stack(new_shift))


def kernel(x_prompt, x_sample, cache_swa_k, cache_swa_v, state_gla, state_rwkv, state_rwkv_shift, ln_g, ln_b, swa_w_in, swa_sink, swa_w_out, gla_w_in, gla_w_a2, gla_b_a, gla_norm_g, gla_w_out, rwkv_mu, rwkv_w_rkvg, rwkv_w0, rwkv_w1, rwkv_w2, rwkv_a0, rwkv_a1, rwkv_a2, rwkv_k_k, rwkv_k_a, rwkv_r_k, rwkv_gn_g, rwkv_gn_b, rwkv_w_out):
    n_rwkv = rwkv_mu.shape[0]
    gla_main = 2 * GLA_KEY_DIM + 2 * GLA_VAL_DIM
    rwkv = []
    for j in range(n_rwkv):
        rwkv.append(dict(
            mu4=rwkv_mu[j][jnp.array([0, 2, 3, 5])].reshape(4, 1, D_MODEL),
            mu2=rwkv_mu[j][jnp.array([1, 4])].reshape(2, 1, D_MODEL),
            w4=rwkv_w_rkvg[j].astype(BF16),
            w1=_pad_rank(rwkv_w1[j], 1), w2=_pad_rank(rwkv_w2[j], 0), w0=rwkv_w0[j],
            a1=_pad_rank(rwkv_a1[j], 1), a2=_pad_rank(rwkv_a2[j], 0), a0=rwkv_a0[j],
            k_k=rwkv_k_k[j], k_a=rwkv_k_a[j], r_k=rwkv_r_k[j], gn_g=rwkv_gn_g[j], gn_b=rwkv_gn_b[j],
            w_out=rwkv_w_out[j].astype(BF16)))
    w = dict(ln_g=ln_g, ln_b=ln_b,
             swa_w_in=swa_w_in.astype(BF16), swa_sink=swa_sink, swa_w_out=swa_w_out.astype(BF16),
             gla_w_in=gla_w_in.astype(BF16),
             gla_w_low=[_pad_rank(gla_w_in[j][:, gla_main:], 1) for j in range(gla_w_in.shape[0])],
             gla_w_a2=[_pad_rank(gla_w_a2[j], 0) for j in range(gla_w_a2.shape[0])],
             gla_b_a=gla_b_a, gla_norm_g=gla_norm_g, gla_w_out=gla_w_out.astype(BF16), rwkv=rwkv)
    y_p, p_k, p_v, p_gla, p_wkv, p_shift = _trunk(x_prompt, None, w)
    cache = dict(k=cache_swa_k, v=cache_swa_v, gla=state_gla, wkv=state_rwkv, shift=state_rwkv_shift)
    y_s, s_k, s_v, s_gla, s_wkv, s_shift = _trunk(x_sample, cache, w)
    return (y_p, y_s, p_k, p_v, p_gla, p_wkv, p_shift, s_k, s_v, s_gla, s_wkv, s_shift)
```

```python
import functools
import math

import jax
import jax.numpy as jnp
from jax import lax
from jax.experimental import pallas as pl
from jax.experimental.pallas import tpu as pltpu

F32 = jnp.float32
BF16 = jnp.bfloat16

D_MODEL = 2048
DEPTH = 4
PAST_LEN = 16384
ALPHA = (2 * DEPTH) ** 0.25
LN_EPS = 1e-5

SWA_HEADS = 32
SWA_KV_HEADS = 4
SWA_GROUP = SWA_HEADS // SWA_KV_HEADS
SWA_HEAD_DIM = 64
SWA_WIDTH = SWA_HEADS * SWA_HEAD_DIM
SWA_KV_WIDTH = SWA_KV_HEADS * SWA_HEAD_DIM
WINDOW = 128
ROT_DIM = SWA_HEAD_DIM // 4
ROPE_THETA = 500000.0

GLA_HEADS = 4
GLA_KEY_DIM = D_MODEL // 2
GLA_VAL_DIM = D_MODEL
GLA_DK = GLA_KEY_DIM // GLA_HEADS
GLA_DV = GLA_VAL_DIM // GLA_HEADS
GLA_GATE_RANK = 16
GLA_GATE_TEMP = 16.0
GLA_CHUNK = 64
GLA_SUB = 16
GLA_NORM_EPS = 1e-5

RWKV_HEAD_DIM = 64
RWKV_HEADS = D_MODEL // RWKV_HEAD_DIM
RWKV_GN_EPS = 64e-5

LANES = 128
VMEM_LIMIT = 56 * 1024 * 1024
NEG_BIG = -1e30


def _cparams(sem):
    return pltpu.CompilerParams(dimension_semantics=sem, vmem_limit_bytes=VMEM_LIMIT)


def _silu(x):
    return x * (1.0 / (1.0 + jnp.exp(-x)))


def _log_sigmoid(z):
    return jnp.minimum(z, 0.0) - jnp.log(1.0 + jnp.exp(-jnp.abs(z)))


def _nt(a, b):
    return lax.dot_general(a, b, (((1,), (1,)), ((), ())), preferred_element_type=F32)


def _mm_kernel(x_ref, w_ref, o_ref):
    o_ref[...] = jnp.dot(x_ref[...].astype(BF16), w_ref[...], preferred_element_type=F32).astype(o_ref.dtype)


def _mm(x, w, col0, ncols, tm, tn=512, out_dtype=F32):
    M, K = x.shape
    assert M % tm == 0 and ncols % tn == 0 and col0 % tn == 0
    cb = col0 // tn
    return pl.pallas_call(
        _mm_kernel,
        grid=(M // tm, ncols // tn),
        in_specs=[pl.BlockSpec((tm, K), lambda i, j: (i, 0)),
                  pl.BlockSpec((K, tn), lambda i, j: (0, cb + j))],
        out_specs=pl.BlockSpec((tm, tn), lambda i, j: (i, j)),
        out_shape=jax.ShapeDtypeStruct((M, ncols), out_dtype),
        compiler_params=_cparams(("parallel", "arbitrary")),
        name="proj_mm",
    )(x, w)


def _mm_ln_kernel(a_ref, w_ref, x_ref, g_ref, b_ref, o_ref):
    h = jnp.dot(a_ref[...], w_ref[...], preferred_element_type=F32)
    z = ALPHA * x_ref[...] + h
    mu = jnp.mean(z, axis=-1, keepdims=True)
    zc = z - mu
    var = jnp.mean(zc * zc, axis=-1, keepdims=True)
    o_ref[...] = zc * lax.rsqrt(var + LN_EPS) * g_ref[...] + b_ref[...]


def _mm_ln(a, w, x, g, b, tm=256):
    M, K = a.shape
    D = w.shape[1]
    return pl.pallas_call(
        _mm_ln_kernel,
        grid=(M // tm,),
        in_specs=[pl.BlockSpec((tm, K), lambda i: (i, 0)),
                  pl.BlockSpec((K, D), lambda i: (0, 0)),
                  pl.BlockSpec((tm, D), lambda i: (i, 0)),
                  pl.BlockSpec((1, D), lambda i: (0, 0)),
                  pl.BlockSpec((1, D), lambda i: (0, 0))],
        out_specs=pl.BlockSpec((tm, D), lambda i: (i, 0)),
        out_shape=jax.ShapeDtypeStruct((M, D), F32),
        compiler_params=_cparams(("parallel",)),
        name="out_proj_ln",
    )(a, w, x, g.reshape(1, D), b.reshape(1, D))


def _rope_tables(pos):
    half = ROT_DIM // 2
    inv = ROPE_THETA ** (-(jnp.arange(half, dtype=F32) * 2.0 / ROT_DIM))
    ang = pos[:, None] * inv[None, :]
    cos, sin = jnp.cos(ang), jnp.sin(ang)
    L = pos.shape[0]
    ones = jnp.ones((L, SWA_HEAD_DIM - ROT_DIM), F32)
    zeros_r = jnp.zeros((L, SWA_HEAD_DIM - ROT_DIM), F32)
    zeros_h = jnp.zeros((L, half), F32)
    c = jnp.concatenate([cos, cos, ones], axis=1)
    s1 = jnp.concatenate([-sin, zeros_h, zeros_r], axis=1)
    s2 = jnp.concatenate([zeros_h, sin, zeros_r], axis=1)
    rep = LANES // SWA_HEAD_DIM
    return jnp.tile(c, (1, rep)), jnp.tile(s1, (1, rep)), jnp.tile(s2, (1, rep))


def _rope128(x, c, s1, s2):
    return x * c + pltpu.roll(x, LANES - ROT_DIM // 2, 1) * s1 + pltpu.roll(x, ROT_DIM // 2, 1) * s2


def _rope_wide(x, c, s1, s2):
    return jnp.concatenate([_rope128(x[:, i * LANES:(i + 1) * LANES], c, s1, s2)
                            for i in range(x.shape[1] // LANES)], axis=1)


def _swa_kernel(sink_ref, q_ref, kc_ref, vc_ref, kp_ref, vp_ref, gate_ref, tc_ref, tp_ref,
                o_ref, ko_ref, vo_ref, *, lq, prompt):
    c, s1, s2 = tc_ref[0], tc_ref[1], tc_ref[2]
    k_cur = _rope_wide(kc_ref[...], c, s1, s2)
    v_cur = vc_ref[...]
    if prompt:
        k_prev = _rope_wide(kp_ref[...], tp_ref[0], tp_ref[1], tp_ref[2])
        v_prev = vp_ref[...]
        ko_ref[0] = k_cur
        vo_ref[0] = v_cur
    else:
        k_prev = kp_ref[0]
        v_prev = vp_ref[0]
        ko_ref[0] = jnp.concatenate([k_prev[lq:], k_cur], axis=0)
        vo_ref[0] = jnp.concatenate([v_prev[lq:], v_cur], axis=0)
        zpad = jnp.zeros((WINDOW - lq, SWA_KV_WIDTH), F32)
        k_cur = jnp.concatenate([k_cur, zpad], axis=0)
        v_cur = jnp.concatenate([v_cur, zpad], axis=0)
    k_all = jnp.concatenate([k_prev, k_cur], axis=0)
    v_all = jnp.concatenate([v_prev, v_cur], axis=0)
    nk = 2 * WINDOW
    nh = SWA_GROUP
    qi = lax.broadcasted_iota(jnp.int32, (nh * lq, nk), 0) % lq
    sj = lax.broadcasted_iota(jnp.int32, (nh * lq, nk), 1)
    rel = qi + WINDOW - sj
    mask = jnp.logical_and(rel >= 0, rel <= WINDOW)
    if prompt:
        lo = jnp.where(pl.program_id(1) == 0, WINDOW, 0)
        mask = jnp.logical_and(mask, sj >= lo)
    m0 = lax.broadcasted_iota(jnp.int32, (lq, LANES), 1) < SWA_HEAD_DIM
    scale = SWA_HEAD_DIM ** -0.5
    hd = SWA_HEAD_DIM
    for kh in range(SWA_KV_HEADS):
        kg = k_all[:, kh * hd:(kh + 1) * hd]
        vg = v_all[:, kh * hd:(kh + 1) * hd]
        kk2 = jnp.concatenate([kg, kg], axis=1).astype(BF16)
        vv2 = jnp.concatenate([vg, vg], axis=1).astype(BF16)
        p0 = kh * (nh // 2)
        rows, sinks = [], []
        for p in range(p0, p0 + nh // 2):
            q2 = _rope128(q_ref[:, p * LANES:(p + 1) * LANES], c, s1, s2) * scale
            rows += [jnp.where(m0, q2, 0.0), jnp.where(m0, 0.0, q2)]
            sinks += [jnp.full((lq, 1), sink_ref[2 * p], F32), jnp.full((lq, 1), sink_ref[2 * p + 1], F32)]
        qs = jnp.concatenate(rows, axis=0).astype(BF16)
        snk = jnp.concatenate(sinks, axis=0)
        s = jnp.where(mask, _nt(qs, kk2), NEG_BIG)
        m = jnp.maximum(jnp.max(s, axis=-1, keepdims=True), snk)
        e = jnp.exp(s - m)
        den = jnp.sum(e, axis=-1, keepdims=True) + jnp.exp(snk - m)
        o = jnp.dot(e.astype(BF16), vv2, preferred_element_type=F32) / den
        for i, p in enumerate(range(p0, p0 + nh // 2)):
            o2 = jnp.where(m0, o[2 * i * lq:(2 * i + 1) * lq], o[(2 * i + 1) * lq:(2 * i + 2) * lq])
            g2 = gate_ref[:, p * LANES:(p + 1) * LANES]
            o_ref[:, p * LANES:(p + 1) * LANES] = (o2 * _silu(g2)).astype(o_ref.dtype)


def _swa_prompt(qkv, gate, sink, B, S):
    nb = S // WINDOW
    pos = jnp.arange(S, dtype=F32)
    tabs = jnp.stack(_rope_tables(pos))
    kcol = SWA_WIDTH // SWA_KV_WIDTH
    prev = lambda b, n: b * nb + jnp.maximum(n - 1, 0)
    return pl.pallas_call(
        functools.partial(_swa_kernel, lq=WINDOW, prompt=True),
        grid=(B, nb),
        in_specs=[pl.BlockSpec(memory_space=pltpu.SMEM),
                  pl.BlockSpec((WINDOW, SWA_WIDTH), lambda b, n: (b * nb + n, 0)),
                  pl.BlockSpec((WINDOW, SWA_KV_WIDTH), lambda b, n: (b * nb + n, kcol)),
                  pl.BlockSpec((WINDOW, SWA_KV_WIDTH), lambda b, n: (b * nb + n, kcol + 1)),
                  pl.BlockSpec((WINDOW, SWA_KV_WIDTH), lambda b, n: (prev(b, n), kcol)),
                  pl.BlockSpec((WINDOW, SWA_KV_WIDTH), lambda b, n: (prev(b, n), kcol + 1)),
                  pl.BlockSpec((WINDOW, SWA_WIDTH), lambda b, n: (b * nb + n, 0)),
                  pl.BlockSpec((3, WINDOW, LANES), lambda b, n: (0, n, 0)),
                  pl.BlockSpec((3, WINDOW, LANES), lambda b, n: (0, jnp.maximum(n - 1, 0), 0))],
        out_specs=[pl.BlockSpec((WINDOW, SWA_WIDTH), lambda b, n: (b * nb + n, 0)),
                   pl.BlockSpec((1, WINDOW, SWA_KV_WIDTH), lambda b, n: (b, 0, 0)),
                   pl.BlockSpec((1, WINDOW, SWA_KV_WIDTH), lambda b, n: (b, 0, 0))],
        out_shape=[jax.ShapeDtypeStruct((B * S, SWA_WIDTH), BF16),
                   jax.ShapeDtypeStruct((B, WINDOW, SWA_KV_WIDTH), F32),
                   jax.ShapeDtypeStruct((B, WINDOW, SWA_KV_WIDTH), F32)],
        compiler_params=_cparams(("parallel", "arbitrary")),
        name="swa_prompt",
    )(sink, qkv, qkv, qkv, qkv, qkv, gate, tabs, tabs)


def _swa_sample(qkv, gate, sink, cache_k, cache_v, B, L):
    pos = PAST_LEN + jnp.arange(L, dtype=F32)
    tabs = jnp.stack(_rope_tables(pos))
    kcol = SWA_WIDTH // SWA_KV_WIDTH
    return pl.pallas_call(
        functools.partial(_swa_kernel, lq=L, prompt=False),
        grid=(B,),
        in_specs=[pl.BlockSpec(memory_space=pltpu.SMEM),
                  pl.BlockSpec((L, SWA_WIDTH), lambda b: (b, 0)),
                  pl.BlockSpec((L, SWA_KV_WIDTH), lambda b: (b, kcol)),
                  pl.BlockSpec((L, SWA_KV_WIDTH), lambda b: (b, kcol + 1)),
                  pl.BlockSpec((1, WINDOW, SWA_KV_WIDTH), lambda b: (b, 0, 0)),
                  pl.BlockSpec((1, WINDOW, SWA_KV_WIDTH), lambda b: (b, 0, 0)),
                  pl.BlockSpec((L, SWA_WIDTH), lambda b: (b, 0)),
                  pl.BlockSpec((3, L, LANES), lambda b: (0, 0, 0)),
                  pl.BlockSpec((3, L, LANES), lambda b: (0, 0, 0))],
        out_specs=[pl.BlockSpec((L, SWA_WIDTH), lambda b: (b, 0)),
                   pl.BlockSpec((1, WINDOW, SWA_KV_WIDTH), lambda b: (b, 0, 0)),
                   pl.BlockSpec((1, WINDOW, SWA_KV_WIDTH), lambda b: (b, 0, 0))],
        out_shape=[jax.ShapeDtypeStruct((B * L, SWA_WIDTH), BF16),
                   jax.ShapeDtypeStruct((B, WINDOW, SWA_KV_WIDTH), F32),
                   jax.ShapeDtypeStruct((B, WINDOW, SWA_KV_WIDTH), F32)],
        compiler_params=_cparams(("parallel",)),
        name="swa_sample",
    )(sink, qkv, qkv, qkv, cache_k, cache_v, gate, tabs, tabs)


def _swa_layer(x, cache, w_in, sink, w_out, ln_g, ln_b, B, L, tm):
    qkv = _mm(x, w_in, 0, SWA_WIDTH + 2 * SWA_KV_WIDTH, tm)
    gate = _mm(x, w_in, SWA_WIDTH + 2 * SWA_KV_WIDTH, SWA_WIDTH, tm)
    if cache is None:
        o, nk, nv = _swa_prompt(qkv, gate, sink, B, L)
    else:
        ck = cache[0].reshape(B, WINDOW, SWA_KV_WIDTH)
        cv = cache[1].reshape(B, WINDOW, SWA_KV_WIDTH)
        o, nk, nv = _swa_sample(qkv, gate, sink, ck, cv, B, L)
    shape = (B, WINDOW, SWA_KV_HEADS, SWA_HEAD_DIM)
    return _mm_ln(o, w_out, x, ln_g, ln_b), nk.reshape(shape), nv.reshape(shape)


def _gla_gate_kernel(x_ref, wl_ref, wa_ref, ba_ref, g_ref):
    a_low = jnp.dot(x_ref[...].astype(BF16), wl_ref[...], preferred_element_type=F32)
    z = jnp.dot(a_low.astype(BF16), wa_ref[...], preferred_element_type=F32) + ba_ref[...]
    g_ref[...] = _log_sigmoid(z) * (1.0 / GLA_GATE_TEMP)


def _gla_gate(x, w_low, w_a2, b_a, tm):
    M = x.shape[0]
    return pl.pallas_call(
        _gla_gate_kernel,
        grid=(M // tm,),
        in_specs=[pl.BlockSpec((tm, D_MODEL), lambda i: (i, 0)),
                  pl.BlockSpec((D_MODEL, LANES), lambda i: (0, 0)),
                  pl.BlockSpec((LANES, GLA_KEY_DIM), lambda i: (0, 0)),
                  pl.BlockSpec((1, GLA_KEY_DIM), lambda i: (0, 0))],
        out_specs=pl.BlockSpec((tm, GLA_KEY_DIM), lambda i: (i, 0)),
        out_shape=jax.ShapeDtypeStruct((M, GLA_KEY_DIM), F32),
        compiler_params=_cparams(("parallel",)),
        name="gla_gate",
    )(x, w_low, w_a2, b_a.reshape(1, GLA_KEY_DIM))


def _gla_chunk_kernel(q_ref, k_ref, v_ref, gate_ref, g_ref, ng_ref, s0_ref, o_ref, so_ref, st_ref, *, C, SB):
    c_idx = pl.program_id(2)

    @pl.when(c_idx == 0)
    def _():
        st_ref[...] = s0_ref[0, 0].T

    g = g_ref[...]
    ti = lax.broadcasted_iota(jnp.int32, (C, C), 0)
    si = lax.broadcasted_iota(jnp.int32, (C, C), 1)
    tril = (si <= ti).astype(F32)
    b = jnp.dot(tril, g, preferred_element_type=F32, precision=lax.Precision.HIGHEST)
    q = q_ref[...] * (GLA_DK ** -0.5)
    k = k_ref[...]
    v16 = v_ref[...].astype(BF16)
    st = st_ref[...]
    qe = (q * jnp.exp(b)).astype(BF16)
    o_inter = lax.dot_general(qe, st.astype(BF16), (((1,), (1,)), ((), ())), preferred_element_type=F32)
    row = lax.broadcasted_iota(jnp.int32, (C, GLA_DK), 0)
    a_parts = []
    for i in range(C // SB):
        r0, r1 = i * SB, (i + 1) * SB
        bn = b[r0:r0 + 1, :]
        qi = (q[r0:r1] * jnp.exp(b[r0:r1] - bn)).astype(BF16)
        kj = (k * jnp.exp(jnp.where(row < r1, bn - b, 0.0))).astype(BF16)
        a_parts.append(lax.dot_general(qi, kj, (((1,), (1,)), ((), ())), preferred_element_type=F32))
    a = jnp.concatenate(a_parts, axis=0) if len(a_parts) > 1 else a_parts[0]
    a = jnp.where(si <= ti, a, 0.0)
    o = o_inter + jnp.dot(a.astype(BF16), v16, preferred_element_type=F32)
    b_last = b[C - 1:C, :]
    kd = (k * jnp.exp(b_last - b)).astype(BF16)
    st_new = st * jnp.exp(b_last) + lax.dot_general(v16, kd, (((0,), (0,)), ((), ())), preferred_element_type=F32)
    st_ref[...] = st_new

    @pl.when(c_idx == pl.num_programs(2) - 1)
    def _():
        so_ref[0, 0] = st_new.T

    on = o * lax.rsqrt(jnp.mean(o * o, axis=-1, keepdims=True) + GLA_NORM_EPS) * ng_ref[...]
    o_ref[...] = (on * _silu(gate_ref[...])).astype(o_ref.dtype)


def _gla_chunk(h, g, norm_g, state0, B, L):
    C = min(GLA_CHUNK, L)
    SB = min(GLA_SUB, C)
    nc = L // C
    kq = GLA_KEY_DIM // GLA_DK
    vq = 2 * GLA_KEY_DIM // GLA_DV
    gq = vq + GLA_HEADS
    row = lambda b, hd, c: b * nc + c
    return pl.pallas_call(
        functools.partial(_gla_chunk_kernel, C=C, SB=SB),
        grid=(B, GLA_HEADS, nc),
        in_specs=[pl.BlockSpec((C, GLA_DK), lambda b, hd, c: (row(b, hd, c), hd)),
                  pl.BlockSpec((C, GLA_DK), lambda b, hd, c: (row(b, hd, c), kq + hd)),
                  pl.BlockSpec((C, GLA_DV), lambda b, hd, c: (row(b, hd, c), vq + hd)),
                  pl.BlockSpec((C, GLA_DV), lambda b, hd, c: (row(b, hd, c), gq + hd)),
                  pl.BlockSpec((C, GLA_DK), lambda b, hd, c: (row(b, hd, c), hd)),
                  pl.BlockSpec((1, GLA_DV), lambda b, hd, c: (0, 0)),
                  pl.BlockSpec((1, 1, GLA_DK, GLA_DV), lambda b, hd, c: (b, hd, 0, 0))],
        out_specs=[pl.BlockSpec((C, GLA_DV), lambda b, hd, c: (row(b, hd, c), hd)),
                   pl.BlockSpec((1, 1, GLA_DK, GLA_DV), lambda b, hd, c: (b, hd, 0, 0))],
        out_shape=[jax.ShapeDtypeStruct((B * L, GLA_VAL_DIM), BF16),
                   jax.ShapeDtypeStruct((B, GLA_HEADS, GLA_DK, GLA_DV), F32)],
        scratch_shapes=[pltpu.VMEM((GLA_DV, GLA_DK), F32)],
        compiler_params=_cparams(("parallel", "parallel", "arbitrary")),
        name="gla_chunk",
    )(h, h, h, h, g, norm_g.reshape(1, GLA_DV), state0)


def _gla_layer(x, state, w_in, w_low, w_a2, b_a, norm_g, w_out, ln_g, ln_b, B, L, tm):
    h = _mm(x, w_in, 0, 2 * GLA_KEY_DIM + 2 * GLA_VAL_DIM, tm)
    g = _gla_gate(x, w_low, w_a2, b_a, tm)
    if state is None:
        state = jnp.zeros((B, GLA_HEADS, GLA_DK, GLA_DV), F32)
    o, st = _gla_chunk(h, g, norm_g, state, B, L)
    return _mm_ln(o, w_out, x, ln_g, ln_b), st


def _rwkv_proj_kernel(x_ref, xp_ref, mu_ref, w_ref, o_ref):
    x = x_ref[...]
    xm = x + (xp_ref[...] - x) * mu_ref[0]
    o_ref[0] = jnp.dot(xm.astype(BF16), w_ref[0], preferred_element_type=F32)


def _rwkv_proj(x, xprev, mu4, w4, tm, tn=512):
    M = x.shape[0]
    return pl.pallas_call(
        _rwkv_proj_kernel,
        grid=(M // tm, 4, D_MODEL // tn),
        in_specs=[pl.BlockSpec((tm, D_MODEL), lambda i, m, j: (i, 0)),
                  pl.BlockSpec((tm, D_MODEL), lambda i, m, j: (i, 0)),
                  pl.BlockSpec((1, 1, D_MODEL), lambda i, m, j: (m, 0, 0)),
                  pl.BlockSpec((1, D_MODEL, tn), lambda i, m, j: (m, 0, j))],
        out_specs=pl.BlockSpec((1, tm, tn), lambda i, m, j: (m, i, j)),
        out_shape=jax.ShapeDtypeStruct((4, M, D_MODEL), F32),
        compiler_params=_cparams(("parallel", "arbitrary", "arbitrary")),
        name="rwkv_proj",
    )(x, xprev, mu4, w4)


def _rwkv_lora_kernel(x_ref, xp_ref, mu_ref, w1_ref, w2_ref, w0_ref, a1_ref, a2_ref, a0_ref, lw_ref, a_ref):
    x = x_ref[...]
    xx = xp_ref[...] - x
    xw = (x + xx * mu_ref[0]).astype(BF16)
    xa = (x + xx * mu_ref[1]).astype(BF16)
    t = jnp.tanh(jnp.dot(xw, w1_ref[...], preferred_element_type=F32))
    wl = w0_ref[...] + jnp.dot(t.astype(BF16), w2_ref[...], preferred_element_type=F32)
    w_log = _log_sigmoid(wl) - 0.5
    lw_ref[...] = -jnp.exp(w_log)
    al = jnp.dot(xa, a1_ref[...], preferred_element_type=F32)
    az = a0_ref[...] + jnp.dot(al.astype(BF16), a2_ref[...], preferred_element_type=F32)
    a_ref[...] = 1.0 / (1.0 + jnp.exp(-az))


def _rwkv_lora(x, xprev, mu2, w1, w2, w0, a1, a2, a0, tm):
    M = x.shape[0]
    R = w1.shape[1]
    full = lambda shape: pl.BlockSpec(shape, lambda i: tuple(0 for _ in shape))
    return pl.pallas_call(
        _rwkv_lora_kernel,
        grid=(M // tm,),
        in_specs=[pl.BlockSpec((tm, D_MODEL), lambda i: (i, 0)),
                  pl.BlockSpec((tm, D_MODEL), lambda i: (i, 0)),
                  full((2, 1, D_MODEL)), full((D_MODEL, R)), full((R, D_MODEL)), full((1, D_MODEL)),
                  full((D_MODEL, R)), full((R, D_MODEL)), full((1, D_MODEL))],
        out_specs=[pl.BlockSpec((tm, D_MODEL), lambda i: (i, 0)),
                   pl.BlockSpec((tm, D_MODEL), lambda i: (i, 0))],
        out_shape=[jax.ShapeDtypeStruct((M, D_MODEL), F32), jax.ShapeDtypeStruct((M, D_MODEL), F32)],
        compiler_params=_cparams(("parallel",)),
        name="rwkv_lora",
    )(x, xprev, mu2, w1, w2, w0.reshape(1, D_MODEL), a1, a2, a0.reshape(1, D_MODEL))


def _head_ones():
    r = lax.broadcasted_iota(jnp.int32, (LANES, LANES), 0) // RWKV_HEAD_DIM
    c = lax.broadcasted_iota(jnp.int32, (LANES, LANES), 1) // RWKV_HEAD_DIM
    e = (r == c).astype(BF16)
    return jnp.concatenate([e, e], axis=0)


def _segsum(x, e2):
    hi = x.astype(BF16)
    lo = (x - hi.astype(F32)).astype(BF16)
    return jnp.dot(jnp.concatenate([hi, lo], axis=1), e2, preferred_element_type=F32)


RWKV_PAIRS = 8
RWKV_CHUNK = 64


def _rwkv_chunk_kernel(r_ref, k_ref, v_ref, gt_ref, lw_ref, a_ref, kk_ref, ka_ref, rk_ref, gng_ref, gnb_ref, s0_ref,
                       o_ref, so_ref, st_ref, *, lreal):
    C, N = RWKV_CHUNK, RWKV_HEAD_DIM
    t_idx = pl.program_id(2)
    e2 = _head_ones()
    m0 = lax.broadcasted_iota(jnp.int32, (C, LANES), 1) < N
    ti = lax.broadcasted_iota(jnp.int32, (C, 2 * C), 0)
    si2 = lax.broadcasted_iota(jnp.int32, (C, 2 * C), 1)
    si = si2 % C
    strict, incl, left = si < ti, si <= ti, si2 < C
    tril3 = (lax.broadcasted_iota(jnp.int32, (C, 3 * C), 1) % C
             <= lax.broadcasted_iota(jnp.int32, (C, 3 * C), 0)).astype(BF16)
    bd = (lax.broadcasted_iota(jnp.int32, (LANES, LANES), 0) // N
          == lax.broadcasted_iota(jnp.int32, (LANES, LANES), 1) // N)
    zs = jnp.zeros((N, N), F32)

    @pl.when(t_idx == 0)
    def _():
        for p in range(RWKV_PAIRS):
            top = jnp.concatenate([s0_ref[0, 2 * p], zs], axis=1)
            bot = jnp.concatenate([zs, s0_ref[0, 2 * p + 1]], axis=1)
            st_ref[p] = jnp.concatenate([top, bot], axis=0)

    def both(x, y):
        parts = [x] if y is None else [x, y]
        return jnp.concatenate([jnp.where(m0, z, 0.0) for z in parts] + [jnp.where(m0, 0.0, z) for z in parts],
                               axis=0).astype(BF16)

    def load(ref3, ref2, ls):
        x = ref3[0, :, ls] if ref3 is not None else ref2[:, ls]
        if lreal < C:
            x = jnp.concatenate([x, jnp.zeros((C - lreal, LANES), F32)], axis=0)
        return x

    inv_n = 1.0 / N
    n_iter = C.bit_length() - 1
    lanes = [slice(p * LANES, (p + 1) * LANES) for p in range(RWKV_PAIRS)]

    def setup(p):
        ls = lanes[p]
        r, k, v = load(r_ref, None, ls), load(k_ref, None, ls), load(v_ref, None, ls)
        lw, a = load(None, lw_ref, ls), load(None, a_ref, ls)
        kk = k * kk_ref[:, ls]
        kk = kk / jnp.maximum(jnp.sqrt(_segsum(kk * kk, e2)), 1e-12)
        av, bv = -kk, kk * a
        kp = k * (1.0 + (a - 1.0) * ka_ref[:, ls])

        h1 = lw.astype(BF16)
        r1 = lw - h1.astype(F32)
        h2 = r1.astype(BF16)
        h3 = (r1 - h2.astype(F32)).astype(BF16)
        c = jnp.dot(tril3, jnp.concatenate([h1, h2, h3], axis=0), preferred_element_type=F32)
        e_c, e_nc = jnp.exp(c), jnp.exp(-c)
        at, rt = av * jnp.exp(c - lw), r * e_c
        bk = jnp.concatenate([bv * e_nc, kp * e_nc], axis=0).astype(BF16)

        g = _nt(both(at, rt), bk)
        aa0, rr0 = jnp.where(strict, g[0:C], 0.0), jnp.where(incl, g[C:2 * C], 0.0)
        aa1, rr1 = jnp.where(strict, g[2 * C:3 * C], 0.0), jnp.where(incl, g[3 * C:], 0.0)
        a_ab = jnp.where(left, aa0, pltpu.roll(aa1, C, 1))
        a_ak = jnp.where(left, pltpu.roll(aa0, C, 1), aa1)

        s_bd = st_ref[p]
        pq = _nt(jnp.concatenate([at, rt], axis=0).astype(BF16), s_bd.astype(BF16))
        x = pq[:C] + jnp.dot(a_ak.astype(BF16), both(v, None), preferred_element_type=F32)
        rr = jnp.concatenate([rr0, rr1], axis=1).astype(BF16)
        return dict(r=r, kp=kp, v=v, bk=bk, s_bd=s_bd, y0=pq[C:], rr=rr, e_last=e_c[C - 1:C, :], x=x, ac=a_ab)

    def neumann(d, it):
        ac16 = d["ac"].astype(BF16)
        xn = d["x"] + jnp.dot(ac16, both(d["x"], None), preferred_element_type=F32)
        if it < n_iter - 1:
            ac = d["ac"]
            a_bd = jnp.concatenate([jnp.where(left, ac, 0.0), jnp.where(left, 0.0, ac)], axis=0).astype(BF16)
            d["ac"] = jnp.dot(ac16, a_bd, preferred_element_type=F32)
        d["x"] = xn

    def finish(p, d):
        ls = lanes[p]
        u, v = d["x"], d["v"]
        y = d["y0"] + jnp.dot(d["rr"], both(u, v), preferred_element_type=F32)
        uv = jnp.concatenate([u, v], axis=0).astype(BF16)
        ds = lax.dot_general(uv, d["bk"], (((0,), (0,)), ((), ())), preferred_element_type=F32)
        s_new = (d["s_bd"] + jnp.where(bd, ds, 0.0)) * d["e_last"]
        y, rr_, kp_, vv_ = y[:lreal], d["r"][:lreal], d["kp"][:lreal], v[:lreal]
        ym = _segsum(y, e2) * inv_n
        yc = y - ym
        yv = _segsum(yc * yc, e2) * inv_n
        yn = yc * lax.rsqrt(yv + RWKV_GN_EPS) * gng_ref[:, ls] + gnb_ref[:, ls]
        bonus = _segsum(rr_ * kp_ * rk_ref[:, ls], e2)
        return s_new, ((yn + bonus * vv_) * _silu(gt_ref[0, :, ls])).astype(o_ref.dtype)

    work = [setup(p) for p in range(RWKV_PAIRS)]
    for it in range(n_iter):
        for d in work:
            neumann(d, it)
    done = [finish(p, d) for p, d in enumerate(work)]
    for p, (s_new, o) in enumerate(done):
        st_ref[p] = s_new
        o_ref[:, lanes[p]] = o

    @pl.when(t_idx == pl.num_programs(2) - 1)
    def _():
        for p in range(RWKV_PAIRS):
            s = st_ref[p]
            so_ref[0, 2 * p] = s[:N, :N]
            so_ref[0, 2 * p + 1] = s[N:, N:]


def _rwkv_scan(proj, lw, a, k_k, k_a, r_k, gn_g, gn_b, state0, B, L):
    tc = min(L, RWKV_CHUNK)
    nt = L // tc
    W = RWKV_PAIRS * LANES
    ng = D_MODEL // W
    hpg = 2 * RWKV_PAIRS
    row = lambda b, g, t: b * nt + t
    tok = lambda m: pl.BlockSpec((1, tc, W), lambda b, g, t: (m, row(b, g, t), g))
    vec = pl.BlockSpec((tc, W), lambda b, g, t: (row(b, g, t), g))
    par = pl.BlockSpec((1, W), lambda b, g, t: (0, g))
    st = pl.BlockSpec((1, hpg, RWKV_HEAD_DIM, RWKV_HEAD_DIM), lambda b, g, t: (b, g, 0, 0))
    return pl.pallas_call(
        functools.partial(_rwkv_chunk_kernel, lreal=tc),
        grid=(B, ng, nt),
        in_specs=[tok(0), tok(1), tok(2), tok(3), vec, vec, par, par, par, par, par, st],
        out_specs=[vec, st],
        out_shape=[jax.ShapeDtypeStruct((B * L, D_MODEL), BF16),
                   jax.ShapeDtypeStruct((B, RWKV_HEADS, RWKV_HEAD_DIM, RWKV_HEAD_DIM), F32)],
        scratch_shapes=[pltpu.VMEM((RWKV_PAIRS, LANES, LANES), F32)],
        compiler_params=_cparams(("parallel", "parallel", "arbitrary")),
        name="rwkv_scan",
    )(proj, proj, proj, proj, lw, a, k_k.reshape(1, D_MODEL), k_a.reshape(1, D_MODEL), r_k.reshape(1, D_MODEL),
      gn_g.reshape(1, D_MODEL), gn_b.reshape(1, D_MODEL), state0)


def _rwkv_layer(x, shift, state, p, ln_g, ln_b, B, L, tm):
    x3 = x.reshape(B, L, D_MODEL)
    xprev = jnp.concatenate([shift[:, None, :], x3[:, :-1]], axis=1).reshape(B * L, D_MODEL)
    proj = _rwkv_proj(x, xprev, p["mu4"], p["w4"], tm)
    lw, a = _rwkv_lora(x, xprev, p["mu2"], p["w1"], p["w2"], p["w0"], p["a1"], p["a2"], p["a0"], tm)
    o, st = _rwkv_scan(proj, lw, a, p["k_k"], p["k_a"], p["r_k"], p["gn_g"], p["gn_b"], state, B, L)
    return _mm_ln(o, p["w_out"], x, ln_g, ln_b), st, x3[:, -1]


def _pad_rank(w, axis):
    pad = [(0, 0), (0, 0)]
    pad[axis] = (0, LANES - w.shape[axis])
    return jnp.pad(w, pad).astype(BF16)


def _trunk(x3, cache, w):
    B, L, _ = x3.shape
    prompt = cache is None
    tm = 512 if B * L >= 512 else B * L
    x = x3.reshape(B * L, D_MODEL)
    new_k, new_v, new_gla, new_wkv, new_shift = [], [], [], [], []
    for layer in range(DEPTH):
        kind, j = layer % 3, layer // 3
        g, b = w["ln_g"][layer], w["ln_b"][layer]
        if kind == 0:
            c = None if prompt else (cache["k"][j], cache["v"][j])
            x, nk, nv = _swa_layer(x, c, w["swa_w_in"][j], w["swa_sink"][j], w["swa_w_out"][j], g, b, B, L, tm)
            new_k.append(nk)
            new_v.append(nv)
        elif kind == 1:
            st = None if prompt else cache["gla"][j]
            x, st = _gla_layer(x, st, w["gla_w_in"][j], w["gla_w_low"][j], w["gla_w_a2"][j], w["gla_b_a"][j],
                               w["gla_norm_g"][j], w["gla_w_out"][j], g, b, B, L, tm)
            new_gla.append(st)
        else:
            if prompt:
                shift0 = jnp.zeros((B, D_MODEL), F32)
                s0 = jnp.zeros((B, RWKV_HEADS, RWKV_HEAD_DIM, RWKV_HEAD_DIM), F32)
            else:
                shift0, s0 = cache["shift"][j], cache["wkv"][j]
            x, st, sh = _rwkv_layer(x, shift0, s0, w["rwkv"][j], g, b, B, L, tm)
            new_wkv.append(st)
            new_shift.append(sh)
    return (x.reshape(B, L, D_MODEL), jnp.stack(new_k), jnp.stack(new_v), jnp.stack(new_gla), jnp.stack(new_wkv),
            jnp.stack(new_shift))


def kernel(x_prompt, x_sample, cache_swa_k, cache_swa_v, state_gla, state_rwkv, state_rwkv_shift, ln_g, ln_b, swa_w_in, swa_sink, swa_w_out, gla_w_in, gla_w_a2, gla_b_a, gla_norm_g, gla_w_out, rwkv_mu, rwkv_w_rkvg, rwkv_w0, rwkv_w1, rwkv_w2, rwkv_a0, rwkv_a1, rwkv_a2, rwkv_k_k, rwkv_k_a, rwkv_r_k, rwkv_gn_g, rwkv_gn_b, rwkv_w_out):
    n_rwkv = rwkv_mu.shape[0]
    gla_main = 2 * GLA_KEY_DIM + 2 * GLA_VAL_DIM
    rwkv = []
    for j in range(n_rwkv):
        rwkv.append(dict(
            mu4=rwkv_mu[j][jnp.array([0, 2, 3, 5])].reshape(4, 1, D_MODEL),
            mu2=rwkv_mu[j][jnp.array([1, 4])].reshape(2, 1, D_MODEL),
            w4=rwkv_w_rkvg[j].astype(BF16),
            w1=_pad_rank(rwkv_w1[j], 1), w2=_pad_rank(rwkv_w2[j], 0), w0=rwkv_w0[j],
            a1=_pad_rank(rwkv_a1[j], 1), a2=_pad_rank(rwkv_a2[j], 0), a0=rwkv_a0[j],
            k_k=rwkv_k_k[j], k_a=rwkv_k_a[j], r_k=rwkv_r_k[j], gn_g=rwkv_gn_g[j], gn_b=rwkv_gn_b[j],
            w_out=rwkv_w_out[j].astype(BF16)))
    w = dict(ln_g=ln_g, ln_b=ln_b,
             swa_w_in=swa_w_in.astype(BF16), swa_sink=swa_sink, swa_w_out=swa_w_out.astype(BF16),
             gla_w_in=gla_w_in.astype(BF16),
             gla_w_low=[_pad_rank(gla_w_in[j][:, gla_main:], 1) for j in range(gla_w_in.shape[0])],
             gla_w_a2=[_pad_rank(gla_w_a2[j], 0) for j in range(gla_w_a2.shape[0])],
             gla_b_a=gla_b_a, gla_norm_g=gla_norm_g, gla_w_out=gla_w_out.astype(BF16), rwkv=rwkv)
    y_p, p_k, p_v, p_gla, p_wkv, p_shift = _trunk(x_prompt, None, w)
    cache = dict(k=cache_swa_k, v=cache_swa_v, gla=state_gla, wkv=state_rwkv, shift=state_rwkv_shift)
    y_s, s_k, s_v, s_gla, s_wkv, s_shift = _trunk(x_sample, cache, w)
    return (y_p, y_s, p_k, p_v, p_gla, p_wkv, p_shift, s_k, s_v, s_gla, s_wkv, s_shift)
```

```python
import functools
import math

import jax
import jax.numpy as jnp
from jax import lax
from jax.experimental import pallas as pl
from jax.experimental.pallas import tpu as pltpu

F32 = jnp.float32
BF16 = jnp.bfloat16

D_MODEL = 2048
DEPTH = 4
PAST_LEN = 16384
ALPHA = (2 * DEPTH) ** 0.25
LN_EPS = 1e-5

SWA_HEADS = 32
SWA_KV_HEADS = 4
SWA_GROUP = SWA_HEADS // SWA_KV_HEADS
SWA_HEAD_DIM = 64
SWA_WIDTH = SWA_HEADS * SWA_HEAD_DIM
SWA_KV_WIDTH = SWA_KV_HEADS * SWA_HEAD_DIM
WINDOW = 128
ROT_DIM = SWA_HEAD_DIM // 4
ROPE_THETA = 500000.0

GLA_HEADS = 4
GLA_KEY_DIM = D_MODEL // 2
GLA_VAL_DIM = D_MODEL
GLA_DK = GLA_KEY_DIM // GLA_HEADS
GLA_DV = GLA_VAL_DIM // GLA_HEADS
GLA_GATE_RANK = 16
GLA_GATE_TEMP = 16.0
GLA_CHUNK = 64
GLA_SUB = 16
GLA_NORM_EPS = 1e-5

RWKV_HEAD_DIM = 64
RWKV_HEADS = D_MODEL // RWKV_HEAD_DIM
RWKV_GN_EPS = 64e-5

LANES = 128
VMEM_LIMIT = 56 * 1024 * 1024
PROJ_ROWS = 1024
LORA_ROWS = 512
OUT_ROWS = 256
NEG_BIG = -1e30


def _cparams(sem):
    return pltpu.CompilerParams(dimension_semantics=sem, vmem_limit_bytes=VMEM_LIMIT)


def _silu(x):
    return x * (1.0 / (1.0 + jnp.exp(-x)))


def _log_sigmoid(z):
    return jnp.minimum(z, 0.0) - jnp.log(1.0 + jnp.exp(-jnp.abs(z)))


def _nt(a, b):
    return lax.dot_general(a, b, (((1,), (1,)), ((), ())), preferred_element_type=F32)


def _mm_kernel(x_ref, w_ref, o_ref):
    o_ref[...] = jnp.dot(x_ref[...].astype(BF16), w_ref[...], preferred_element_type=F32).astype(o_ref.dtype)


def _mm(x, w, col0, ncols, tn=512, out_dtype=F32):
    M, K = x.shape
    tm = min(M, PROJ_ROWS)
    assert M % tm == 0 and ncols % tn == 0 and col0 % tn == 0
    cb = col0 // tn
    return pl.pallas_call(
        _mm_kernel,
        grid=(M // tm, ncols // tn),
        in_specs=[pl.BlockSpec((tm, K), lambda i, j: (i, 0)),
                  pl.BlockSpec((K, tn), lambda i, j: (0, cb + j))],
        out_specs=pl.BlockSpec((tm, tn), lambda i, j: (i, j)),
        out_shape=jax.ShapeDtypeStruct((M, ncols), out_dtype),
        compiler_params=_cparams(("parallel", "arbitrary")),
        name="proj_mm",
    )(x, w)


def _mm_ln_kernel(a_ref, w_ref, x_ref, g_ref, b_ref, o_ref):
    h = jnp.dot(a_ref[...], w_ref[...], preferred_element_type=F32)
    z = ALPHA * x_ref[...] + h
    mu = jnp.mean(z, axis=-1, keepdims=True)
    zc = z - mu
    var = jnp.mean(zc * zc, axis=-1, keepdims=True)
    o_ref[...] = zc * lax.rsqrt(var + LN_EPS) * g_ref[...] + b_ref[...]


def _mm_ln(a, w, x, g, b):
    M, K = a.shape
    tm = min(M, OUT_ROWS)
    D = w.shape[1]
    return pl.pallas_call(
        _mm_ln_kernel,
        grid=(M // tm,),
        in_specs=[pl.BlockSpec((tm, K), lambda i: (i, 0)),
                  pl.BlockSpec((K, D), lambda i: (0, 0)),
                  pl.BlockSpec((tm, D), lambda i: (i, 0)),
                  pl.BlockSpec((1, D), lambda i: (0, 0)),
                  pl.BlockSpec((1, D), lambda i: (0, 0))],
        out_specs=pl.BlockSpec((tm, D), lambda i: (i, 0)),
        out_shape=jax.ShapeDtypeStruct((M, D), F32),
        compiler_params=_cparams(("parallel",)),
        name="out_proj_ln",
    )(a, w, x, g.reshape(1, D), b.reshape(1, D))


def _rope_tables(pos):
    half = ROT_DIM // 2
    inv = ROPE_THETA ** (-(jnp.arange(half, dtype=F32) * 2.0 / ROT_DIM))
    ang = pos[:, None] * inv[None, :]
    cos, sin = jnp.cos(ang), jnp.sin(ang)
    L = pos.shape[0]
    ones = jnp.ones((L, SWA_HEAD_DIM - ROT_DIM), F32)
    zeros_r = jnp.zeros((L, SWA_HEAD_DIM - ROT_DIM), F32)
    zeros_h = jnp.zeros((L, half), F32)
    c = jnp.concatenate([cos, cos, ones], axis=1)
    s1 = jnp.concatenate([-sin, zeros_h, zeros_r], axis=1)
    s2 = jnp.concatenate([zeros_h, sin, zeros_r], axis=1)
    rep = LANES // SWA_HEAD_DIM
    return jnp.tile(c, (1, rep)), jnp.tile(s1, (1, rep)), jnp.tile(s2, (1, rep))


def _rope128(x, c, s1, s2):
    return x * c + pltpu.roll(x, LANES - ROT_DIM // 2, 1) * s1 + pltpu.roll(x, ROT_DIM // 2, 1) * s2


def _rope_wide(x, c, s1, s2):
    return jnp.concatenate([_rope128(x[:, i * LANES:(i + 1) * LANES], c, s1, s2)
                            for i in range(x.shape[1] // LANES)], axis=1)


def _swa_kernel(sink_ref, q_ref, kc_ref, vc_ref, kp_ref, vp_ref, gate_ref, tc_ref, tp_ref,
                o_ref, ko_ref, vo_ref, *, lq, prompt):
    c, s1, s2 = tc_ref[0], tc_ref[1], tc_ref[2]
    k_cur = _rope_wide(kc_ref[...], c, s1, s2)
    v_cur = vc_ref[...]
    if prompt:
        k_prev = _rope_wide(kp_ref[...], tp_ref[0], tp_ref[1], tp_ref[2])
        v_prev = vp_ref[...]
        ko_ref[0] = k_cur
        vo_ref[0] = v_cur
    else:
        k_prev = kp_ref[0]
        v_prev = vp_ref[0]
        ko_ref[0] = jnp.concatenate([k_prev[lq:], k_cur], axis=0)
        vo_ref[0] = jnp.concatenate([v_prev[lq:], v_cur], axis=0)
        zpad = jnp.zeros((WINDOW - lq, SWA_KV_WIDTH), F32)
        k_cur = jnp.concatenate([k_cur, zpad], axis=0)
        v_cur = jnp.concatenate([v_cur, zpad], axis=0)
    k_all = jnp.concatenate([k_prev, k_cur], axis=0)
    v_all = jnp.concatenate([v_prev, v_cur], axis=0)
    nk = 2 * WINDOW
    nh = SWA_GROUP
    qi = lax.broadcasted_iota(jnp.int32, (nh * lq, nk), 0) % lq
    sj = lax.broadcasted_iota(jnp.int32, (nh * lq, nk), 1)
    rel = qi + WINDOW - sj
    mask = jnp.logical_and(rel >= 0, rel <= WINDOW)
    if prompt:
        lo = jnp.where(pl.program_id(1) == 0, WINDOW, 0)
        mask = jnp.logical_and(mask, sj >= lo)
    m0 = lax.broadcasted_iota(jnp.int32, (lq, LANES), 1) < SWA_HEAD_DIM
    scale = SWA_HEAD_DIM ** -0.5
    hd = SWA_HEAD_DIM
    for kh in range(SWA_KV_HEADS):
        kg = k_all[:, kh * hd:(kh + 1) * hd]
        vg = v_all[:, kh * hd:(kh + 1) * hd]
        kk2 = jnp.concatenate([kg, kg], axis=1).astype(BF16)
        vv2 = jnp.concatenate([vg, vg], axis=1).astype(BF16)
        p0 = kh * (nh // 2)
        rows, sinks = [], []
        for p in range(p0, p0 + nh // 2):
            q2 = _rope128(q_ref[:, p * LANES:(p + 1) * LANES], c, s1, s2) * scale
            rows += [jnp.where(m0, q2, 0.0), jnp.where(m0, 0.0, q2)]
            sinks += [jnp.full((lq, 1), sink_ref[2 * p], F32), jnp.full((lq, 1), sink_ref[2 * p + 1], F32)]
        qs = jnp.concatenate(rows, axis=0).astype(BF16)
        snk = jnp.concatenate(sinks, axis=0)
        s = jnp.where(mask, _nt(qs, kk2), NEG_BIG)
        m = jnp.maximum(jnp.max(s, axis=-1, keepdims=True), snk)
        e = jnp.exp(s - m)
        den = jnp.sum(e, axis=-1, keepdims=True) + jnp.exp(snk - m)
        o = jnp.dot(e.astype(BF16), vv2, preferred_element_type=F32) / den
        for i, p in enumerate(range(p0, p0 + nh // 2)):
            o2 = jnp.where(m0, o[2 * i * lq:(2 * i + 1) * lq], o[(2 * i + 1) * lq:(2 * i + 2) * lq])
            g2 = gate_ref[:, p * LANES:(p + 1) * LANES]
            o_ref[:, p * LANES:(p + 1) * LANES] = (o2 * _silu(g2)).astype(o_ref.dtype)


def _swa_prompt(qkv, gate, sink, B, S):
    nb = S // WINDOW
    pos = jnp.arange(S, dtype=F32)
    tabs = jnp.stack(_rope_tables(pos))
    kcol = SWA_WIDTH // SWA_KV_WIDTH
    prev = lambda b, n: b * nb + jnp.maximum(n - 1, 0)
    return pl.pallas_call(
        functools.partial(_swa_kernel, lq=WINDOW, prompt=True),
        grid=(B, nb),
        in_specs=[pl.BlockSpec(memory_space=pltpu.SMEM),
                  pl.BlockSpec((WINDOW, SWA_WIDTH), lambda b, n: (b * nb + n, 0)),
                  pl.BlockSpec((WINDOW, SWA_KV_WIDTH), lambda b, n: (b * nb + n, kcol)),
                  pl.BlockSpec((WINDOW, SWA_KV_WIDTH), lambda b, n: (b * nb + n, kcol + 1)),
                  pl.BlockSpec((WINDOW, SWA_KV_WIDTH), lambda b, n: (prev(b, n), kcol)),
                  pl.BlockSpec((WINDOW, SWA_KV_WIDTH), lambda b, n: (prev(b, n), kcol + 1)),
                  pl.BlockSpec((WINDOW, SWA_WIDTH), lambda b, n: (b * nb + n, 0)),
                  pl.BlockSpec((3, WINDOW, LANES), lambda b, n: (0, n, 0)),
                  pl.BlockSpec((3, WINDOW, LANES), lambda b, n: (0, jnp.maximum(n - 1, 0), 0))],
        out_specs=[pl.BlockSpec((WINDOW, SWA_WIDTH), lambda b, n: (b * nb + n, 0)),
                   pl.BlockSpec((1, WINDOW, SWA_KV_WIDTH), lambda b, n: (b, 0, 0)),
                   pl.BlockSpec((1, WINDOW, SWA_KV_WIDTH), lambda b, n: (b, 0, 0))],
        out_shape=[jax.ShapeDtypeStruct((B * S, SWA_WIDTH), BF16),
                   jax.ShapeDtypeStruct((B, WINDOW, SWA_KV_WIDTH), F32),
                   jax.ShapeDtypeStruct((B, WINDOW, SWA_KV_WIDTH), F32)],
        compiler_params=_cparams(("parallel", "arbitrary")),
        name="swa_prompt",
    )(sink, qkv, qkv, qkv, qkv, qkv, gate, tabs, tabs)


def _swa_sample(qkv, gate, sink, cache_k, cache_v, B, L):
    pos = PAST_LEN + jnp.arange(L, dtype=F32)
    tabs = jnp.stack(_rope_tables(pos))
    kcol = SWA_WIDTH // SWA_KV_WIDTH
    return pl.pallas_call(
        functools.partial(_swa_kernel, lq=L, prompt=False),
        grid=(B,),
        in_specs=[pl.BlockSpec(memory_space=pltpu.SMEM),
                  pl.BlockSpec((L, SWA_WIDTH), lambda b: (b, 0)),
                  pl.BlockSpec((L, SWA_KV_WIDTH), lambda b: (b, kcol)),
                  pl.BlockSpec((L, SWA_KV_WIDTH), lambda b: (b, kcol + 1)),
                  pl.BlockSpec((1, WINDOW, SWA_KV_WIDTH), lambda b: (b, 0, 0)),
                  pl.BlockSpec((1, WINDOW, SWA_KV_WIDTH), lambda b: (b, 0, 0)),
                  pl.BlockSpec((L, SWA_WIDTH), lambda b: (b, 0)),
                  pl.BlockSpec((3, L, LANES), lambda b: (0, 0, 0)),
                  pl.BlockSpec((3, L, LANES), lambda b: (0, 0, 0))],
        out_specs=[pl.BlockSpec((L, SWA_WIDTH), lambda b: (b, 0)),
                   pl.BlockSpec((1, WINDOW, SWA_KV_WIDTH), lambda b: (b, 0, 0)),
                   pl.BlockSpec((1, WINDOW, SWA_KV_WIDTH), lambda b: (b, 0, 0))],
        out_shape=[jax.ShapeDtypeStruct((B * L, SWA_WIDTH), BF16),
                   jax.ShapeDtypeStruct((B, WINDOW, SWA_KV_WIDTH), F32),
                   jax.ShapeDtypeStruct((B, WINDOW, SWA_KV_WIDTH), F32)],
        compiler_params=_cparams(("parallel",)),
        name="swa_sample",
    )(sink, qkv, qkv, qkv, cache_k, cache_v, gate, tabs, tabs)


def _swa_layer(x, cache, w_in, sink, w_out, ln_g, ln_b, B, L):
    qkv = _mm(x, w_in, 0, SWA_WIDTH + 2 * SWA_KV_WIDTH)
    gate = _mm(x, w_in, SWA_WIDTH + 2 * SWA_KV_WIDTH, SWA_WIDTH)
    if cache is None:
        o, nk, nv = _swa_prompt(qkv, gate, sink, B, L)
    else:
        ck = cache[0].reshape(B, WINDOW, SWA_KV_WIDTH)
        cv = cache[1].reshape(B, WINDOW, SWA_KV_WIDTH)
        o, nk, nv = _swa_sample(qkv, gate, sink, ck, cv, B, L)
    shape = (B, WINDOW, SWA_KV_HEADS, SWA_HEAD_DIM)
    return _mm_ln(o, w_out, x, ln_g, ln_b), nk.reshape(shape), nv.reshape(shape)


def _gla_gate_kernel(x_ref, wl_ref, wa_ref, ba_ref, g_ref):
    a_low = jnp.dot(x_ref[...].astype(BF16), wl_ref[...], preferred_element_type=F32)
    z = jnp.dot(a_low.astype(BF16), wa_ref[...], preferred_element_type=F32) + ba_ref[...]
    g_ref[...] = _log_sigmoid(z) * (1.0 / GLA_GATE_TEMP)


def _gla_gate(x, w_low, w_a2, b_a):
    M = x.shape[0]
    tm = min(M, PROJ_ROWS)
    return pl.pallas_call(
        _gla_gate_kernel,
        grid=(M // tm,),
        in_specs=[pl.BlockSpec((tm, D_MODEL), lambda i: (i, 0)),
                  pl.BlockSpec((D_MODEL, LANES), lambda i: (0, 0)),
                  pl.BlockSpec((LANES, GLA_KEY_DIM), lambda i: (0, 0)),
                  pl.BlockSpec((1, GLA_KEY_DIM), lambda i: (0, 0))],
        out_specs=pl.BlockSpec((tm, GLA_KEY_DIM), lambda i: (i, 0)),
        out_shape=jax.ShapeDtypeStruct((M, GLA_KEY_DIM), F32),
        compiler_params=_cparams(("parallel",)),
        name="gla_gate",
    )(x, w_low, w_a2, b_a.reshape(1, GLA_KEY_DIM))


def _tril3(C):
    return (lax.broadcasted_iota(jnp.int32, (C, 3 * C), 1) % C
            <= lax.broadcasted_iota(jnp.int32, (C, 3 * C), 0)).astype(BF16)


def _cumsum_rows(x, tril3):
    h1 = x.astype(BF16)
    r1 = x - h1.astype(F32)
    h2 = r1.astype(BF16)
    h3 = (r1 - h2.astype(F32)).astype(BF16)
    return jnp.dot(tril3, jnp.concatenate([h1, h2, h3], axis=0), preferred_element_type=F32)


def _gla_chunk_kernel(q_ref, k_ref, v_ref, gate_ref, g_ref, ng_ref, s0_ref, o_ref, so_ref, st_ref, *, C, SB):
    c_idx = pl.program_id(1)
    H, DK, DV = GLA_HEADS, GLA_DK, GLA_DV

    @pl.when(c_idx == 0)
    def _():
        for h in range(H):
            st_ref[h] = s0_ref[0, h].T

    b = _cumsum_rows(g_ref[...], _tril3(C))
    q = q_ref[...] * (DK ** -0.5)
    k = k_ref[...]
    v16 = v_ref[...].astype(BF16)
    b_last = b[C - 1:C, :]
    e_last = jnp.exp(b_last)
    qe = (q * jnp.exp(b)).astype(BF16)
    kd = (k * jnp.exp(b_last - b)).astype(BF16)
    row = lax.broadcasted_iota(jnp.int32, (C, H * DK), 0)
    qis, kjs = [], []
    for i in range(C // SB):
        r0, r1 = i * SB, (i + 1) * SB
        bn = b[r0:r0 + 1, :]
        qis.append((q[r0:r1] * jnp.exp(b[r0:r1] - bn)).astype(BF16))
        kjs.append((k * jnp.exp(jnp.where(row < r1, bn - b, 0.0))).astype(BF16))
    causal = lax.broadcasted_iota(jnp.int32, (C, C), 1) <= lax.broadcasted_iota(jnp.int32, (C, C), 0)

    results = []
    for h in range(H):
        ks, vs = slice(h * DK, (h + 1) * DK), slice(h * DV, (h + 1) * DV)
        st = st_ref[h]
        a_parts = [_nt(qi[:, ks], kj[:, ks]) for qi, kj in zip(qis, kjs)]
        a = jnp.concatenate(a_parts, axis=0) if len(a_parts) > 1 else a_parts[0]
        a = jnp.where(causal, a, 0.0).astype(BF16)
        o = _nt(qe[:, ks], st.astype(BF16)) + jnp.dot(a, v16[:, vs], preferred_element_type=F32)
        st_new = st * e_last[:, ks] + lax.dot_general(v16[:, vs], kd[:, ks], (((0,), (0,)), ((), ())),
                                                      preferred_element_type=F32)
        on = o * lax.rsqrt(jnp.mean(o * o, axis=-1, keepdims=True) + GLA_NORM_EPS) * ng_ref[...]
        results.append((st_new, (on * _silu(gate_ref[:, vs])).astype(o_ref.dtype)))
    for h, (st_new, out) in enumerate(results):
        st_ref[h] = st_new
        o_ref[:, h * DV:(h + 1) * DV] = out

    @pl.when(c_idx == pl.num_programs(1) - 1)
    def _():
        for h in range(H):
            so_ref[0, h] = st_ref[h].T


def _gla_chunk(h, g, norm_g, state0, B, L):
    C = min(GLA_CHUNK, L)
    SB = min(GLA_SUB, C)
    nc = L // C
    row = lambda b, c: b * nc + c
    st = pl.BlockSpec((1, GLA_HEADS, GLA_DK, GLA_DV), lambda b, c: (b, 0, 0, 0))
    return pl.pallas_call(
        functools.partial(_gla_chunk_kernel, C=C, SB=SB),
        grid=(B, nc),
        in_specs=[pl.BlockSpec((C, GLA_KEY_DIM), lambda b, c: (row(b, c), 0)),
                  pl.BlockSpec((C, GLA_KEY_DIM), lambda b, c: (row(b, c), 1)),
                  pl.BlockSpec((C, GLA_VAL_DIM), lambda b, c: (row(b, c), 1)),
                  pl.BlockSpec((C, GLA_VAL_DIM), lambda b, c: (row(b, c), 2)),
                  pl.BlockSpec((C, GLA_KEY_DIM), lambda b, c: (row(b, c), 0)),
                  pl.BlockSpec((1, GLA_DV), lambda b, c: (0, 0)),
                  st],
        out_specs=[pl.BlockSpec((C, GLA_VAL_DIM), lambda b, c: (row(b, c), 0)), st],
        out_shape=[jax.ShapeDtypeStruct((B * L, GLA_VAL_DIM), BF16),
                   jax.ShapeDtypeStruct((B, GLA_HEADS, GLA_DK, GLA_DV), F32)],
        scratch_shapes=[pltpu.VMEM((GLA_HEADS, GLA_DV, GLA_DK), F32)],
        compiler_params=_cparams(("parallel", "arbitrary")),
        name="gla_chunk",
    )(h, h, h, h, g, norm_g.reshape(1, GLA_DV), state0)


def _gla_layer(x, state, w_in, w_low, w_a2, b_a, norm_g, w_out, ln_g, ln_b, B, L):
    h = _mm(x, w_in, 0, 2 * GLA_KEY_DIM + 2 * GLA_VAL_DIM)
    g = _gla_gate(x, w_low, w_a2, b_a)
    if state is None:
        state = jnp.zeros((B, GLA_HEADS, GLA_DK, GLA_DV), F32)
    o, st = _gla_chunk(h, g, norm_g, state, B, L)
    return _mm_ln(o, w_out, x, ln_g, ln_b), st


def _rwkv_proj_kernel(x_ref, xp_ref, mu_ref, w_ref, o_ref):
    x = x_ref[...]
    xm = x + (xp_ref[...] - x) * mu_ref[0]
    o_ref[0] = jnp.dot(xm.astype(BF16), w_ref[0], preferred_element_type=F32)


def _rwkv_proj(x, xprev, mu4, w4, tn=512):
    M = x.shape[0]
    tm = min(M, PROJ_ROWS)
    return pl.pallas_call(
        _rwkv_proj_kernel,
        grid=(M // tm, 4, D_MODEL // tn),
        in_specs=[pl.BlockSpec((tm, D_MODEL), lambda i, m, j: (i, 0)),
                  pl.BlockSpec((tm, D_MODEL), lambda i, m, j: (i, 0)),
                  pl.BlockSpec((1, 1, D_MODEL), lambda i, m, j: (m, 0, 0)),
                  pl.BlockSpec((1, D_MODEL, tn), lambda i, m, j: (m, 0, j))],
        out_specs=pl.BlockSpec((1, tm, tn), lambda i, m, j: (m, i, j)),
        out_shape=jax.ShapeDtypeStruct((4, M, D_MODEL), F32),
        compiler_params=_cparams(("parallel", "arbitrary", "arbitrary")),
        name="rwkv_proj",
    )(x, xprev, mu4, w4)


def _rwkv_lora_kernel(x_ref, xp_ref, mu_ref, w1_ref, w2_ref, w0_ref, a1_ref, a2_ref, a0_ref, lw_ref, a_ref):
    x = x_ref[...]
    xx = xp_ref[...] - x
    xw = (x + xx * mu_ref[0]).astype(BF16)
    xa = (x + xx * mu_ref[1]).astype(BF16)
    t = jnp.tanh(jnp.dot(xw, w1_ref[...], preferred_element_type=F32))
    wl = w0_ref[...] + jnp.dot(t.astype(BF16), w2_ref[...], preferred_element_type=F32)
    w_log = _log_sigmoid(wl) - 0.5
    lw_ref[...] = -jnp.exp(w_log)
    al = jnp.dot(xa, a1_ref[...], preferred_element_type=F32)
    az = a0_ref[...] + jnp.dot(al.astype(BF16), a2_ref[...], preferred_element_type=F32)
    a_ref[...] = 1.0 / (1.0 + jnp.exp(-az))


def _rwkv_lora(x, xprev, mu2, w1, w2, w0, a1, a2, a0):
    M = x.shape[0]
    tm = min(M, LORA_ROWS)
    R = w1.shape[1]
    full = lambda shape: pl.BlockSpec(shape, lambda i: tuple(0 for _ in shape))
    return pl.pallas_call(
        _rwkv_lora_kernel,
        grid=(M // tm,),
        in_specs=[pl.BlockSpec((tm, D_MODEL), lambda i: (i, 0)),
                  pl.BlockSpec((tm, D_MODEL), lambda i: (i, 0)),
                  full((2, 1, D_MODEL)), full((D_MODEL, R)), full((R, D_MODEL)), full((1, D_MODEL)),
                  full((D_MODEL, R)), full((R, D_MODEL)), full((1, D_MODEL))],
        out_specs=[pl.BlockSpec((tm, D_MODEL), lambda i: (i, 0)),
                   pl.BlockSpec((tm, D_MODEL), lambda i: (i, 0))],
        out_shape=[jax.ShapeDtypeStruct((M, D_MODEL), F32), jax.ShapeDtypeStruct((M, D_MODEL), F32)],
        compiler_params=_cparams(("parallel",)),
        name="rwkv_lora",
    )(x, xprev, mu2, w1, w2, w0.reshape(1, D_MODEL), a1, a2, a0.reshape(1, D_MODEL))


def _head_ones():
    r = lax.broadcasted_iota(jnp.int32, (LANES, LANES), 0) // RWKV_HEAD_DIM
    c = lax.broadcasted_iota(jnp.int32, (LANES, LANES), 1) // RWKV_HEAD_DIM
    e = (r == c).astype(BF16)
    return jnp.concatenate([e, e], axis=0)


def _segsum(x, e2):
    hi = x.astype(BF16)
    lo = (x - hi.astype(F32)).astype(BF16)
    return jnp.dot(jnp.concatenate([hi, lo], axis=1), e2, preferred_element_type=F32)


def _segsum_wide(x, e2):
    rows, n = x.shape[0], x.shape[1] // LANES
    s = _segsum(jnp.concatenate([x[:, i * LANES:(i + 1) * LANES] for i in range(n)], axis=0), e2)
    return jnp.concatenate([s[i * rows:(i + 1) * rows] for i in range(n)], axis=1)


RWKV_PAIRS = 8
RWKV_CHUNK = 64


def _rwkv_chunk_kernel(r_ref, k_ref, v_ref, gt_ref, lw_ref, a_ref, kk_ref, ka_ref, rk_ref, gng_ref, gnb_ref, s0_ref,
                       o_ref, so_ref, st_ref, *, lreal):
    C, N = RWKV_CHUNK, RWKV_HEAD_DIM
    t_idx = pl.program_id(2)
    e2 = _head_ones()
    m0 = lax.broadcasted_iota(jnp.int32, (C, LANES), 1) < N
    ti = lax.broadcasted_iota(jnp.int32, (C, 2 * C), 0)
    si2 = lax.broadcasted_iota(jnp.int32, (C, 2 * C), 1)
    si = si2 % C
    strict, incl, left = si < ti, si <= ti, si2 < C
    bd = (lax.broadcasted_iota(jnp.int32, (LANES, LANES), 0) // N
          == lax.broadcasted_iota(jnp.int32, (LANES, LANES), 1) // N)
    zs = jnp.zeros((N, N), F32)

    @pl.when(t_idx == 0)
    def _():
        for p in range(RWKV_PAIRS):
            top = jnp.concatenate([s0_ref[0, 2 * p], zs], axis=1)
            bot = jnp.concatenate([zs, s0_ref[0, 2 * p + 1]], axis=1)
            st_ref[p] = jnp.concatenate([top, bot], axis=0)

    def both(x, y):
        parts = [x] if y is None else [x, y]
        return jnp.concatenate([jnp.where(m0, z, 0.0) for z in parts] + [jnp.where(m0, 0.0, z) for z in parts],
                               axis=0).astype(BF16)

    def load(x):
        if lreal < C:
            x = jnp.concatenate([x, jnp.zeros((C - lreal, x.shape[1]), F32)], axis=0)
        return x

    inv_n = 1.0 / N
    n_iter = C.bit_length() - 1
    lanes = [slice(p * LANES, (p + 1) * LANES) for p in range(RWKV_PAIRS)]

    r, k, v = load(r_ref[0]), load(k_ref[0]), load(v_ref[0])
    lw, a = load(lw_ref[...]), load(a_ref[...])
    kk = k * kk_ref[...]
    kk = kk / jnp.maximum(jnp.sqrt(_segsum_wide(kk * kk, e2)), 1e-12)
    kp = k * (1.0 + (a - 1.0) * ka_ref[...])
    c = _cumsum_rows(lw, _tril3(C))
    e_c, e_nc = jnp.exp(c), jnp.exp(-c)
    at_w, rt_w = -kk * jnp.exp(c - lw), r * e_c
    bt_w, kt_w = (kk * a * e_nc).astype(BF16), (kp * e_nc).astype(BF16)
    bonus = _segsum_wide((r * kp * rk_ref[...])[:lreal], e2)

    def setup(p):
        ls = lanes[p]
        at, rt = at_w[:, ls], rt_w[:, ls]
        bk = jnp.concatenate([bt_w[:, ls], kt_w[:, ls]], axis=0)
        g = _nt(both(at, rt), bk)
        aa0, rr0 = jnp.where(strict, g[0:C], 0.0), jnp.where(incl, g[C:2 * C], 0.0)
        aa1, rr1 = jnp.where(strict, g[2 * C:3 * C], 0.0), jnp.where(incl, g[3 * C:], 0.0)
        a_ab = jnp.where(left, aa0, pltpu.roll(aa1, C, 1))
        a_ak = jnp.where(left, pltpu.roll(aa0, C, 1), aa1)

        s_bd = st_ref[p]
        pq = _nt(jnp.concatenate([at, rt], axis=0).astype(BF16), s_bd.astype(BF16))
        x = pq[:C] + jnp.dot(a_ak.astype(BF16), both(v[:, ls], None), preferred_element_type=F32)
        rr = jnp.concatenate([rr0, rr1], axis=1).astype(BF16)
        return dict(bk=bk, s_bd=s_bd, y0=pq[C:], rr=rr, x=x, ac=a_ab)

    def neumann(d, it):
        ac16 = d["ac"].astype(BF16)
        xn = d["x"] + jnp.dot(ac16, both(d["x"], None), preferred_element_type=F32)
        if it < n_iter - 1:
            ac = d["ac"]
            a_bd = jnp.concatenate([jnp.where(left, ac, 0.0), jnp.where(left, 0.0, ac)], axis=0).astype(BF16)
            d["ac"] = jnp.dot(ac16, a_bd, preferred_element_type=F32)
        d["x"] = xn

    def finish(p, d):
        ls = lanes[p]
        u, vp = d["x"], v[:, ls]
        y = d["y0"] + jnp.dot(d["rr"], both(u, vp), preferred_element_type=F32)
        uv = jnp.concatenate([u, vp], axis=0).astype(BF16)
        ds = lax.dot_general(uv, d["bk"], (((0,), (0,)), ((), ())), preferred_element_type=F32)
        return (d["s_bd"] + jnp.where(bd, ds, 0.0)) * e_c[C - 1:C, ls], y[:lreal]

    work = [setup(p) for p in range(RWKV_PAIRS)]
    for it in range(n_iter):
        for d in work:
            neumann(d, it)
    done = [finish(p, d) for p, d in enumerate(work)]
    for p, (s_new, _) in enumerate(done):
        st_ref[p] = s_new

    y = jnp.concatenate([yp for _, yp in done], axis=1)
    yc = y - _segsum_wide(y, e2) * inv_n
    yv = _segsum_wide(yc * yc, e2) * inv_n
    yn = yc * lax.rsqrt(yv + RWKV_GN_EPS) * gng_ref[...] + gnb_ref[...]
    o_ref[...] = ((yn + bonus * v[:lreal]) * _silu(gt_ref[0])).astype(o_ref.dtype)

    @pl.when(t_idx == pl.num_programs(2) - 1)
    def _():
        for p in range(RWKV_PAIRS):
            s = st_ref[p]
            so_ref[0, 2 * p] = s[:N, :N]
            so_ref[0, 2 * p + 1] = s[N:, N:]


def _rwkv_scan(proj, lw, a, k_k, k_a, r_k, gn_g, gn_b, state0, B, L):
    tc = min(L, RWKV_CHUNK)
    nt = L // tc
    W = RWKV_PAIRS * LANES
    ng = D_MODEL // W
    hpg = 2 * RWKV_PAIRS
    row = lambda b, g, t: b * nt + t
    tok = lambda m: pl.BlockSpec((1, tc, W), lambda b, g, t: (m, row(b, g, t), g))
    vec = pl.BlockSpec((tc, W), lambda b, g, t: (row(b, g, t), g))
    par = pl.BlockSpec((1, W), lambda b, g, t: (0, g))
    st = pl.BlockSpec((1, hpg, RWKV_HEAD_DIM, RWKV_HEAD_DIM), lambda b, g, t: (b, g, 0, 0))
    return pl.pallas_call(
        functools.partial(_rwkv_chunk_kernel, lreal=tc),
        grid=(B, ng, nt),
        in_specs=[tok(0), tok(1), tok(2), tok(3), vec, vec, par, par, par, par, par, st],
        out_specs=[vec, st],
        out_shape=[jax.ShapeDtypeStruct((B * L, D_MODEL), BF16),
                   jax.ShapeDtypeStruct((B, RWKV_HEADS, RWKV_HEAD_DIM, RWKV_HEAD_DIM), F32)],
        scratch_shapes=[pltpu.VMEM((RWKV_PAIRS, LANES, LANES), F32)],
        compiler_params=_cparams(("parallel", "parallel", "arbitrary")),
        name="rwkv_scan",
    )(proj, proj, proj, proj, lw, a, k_k.reshape(1, D_MODEL), k_a.reshape(1, D_MODEL), r_k.reshape(1, D_MODEL),
      gn_g.reshape(1, D_MODEL), gn_b.reshape(1, D_MODEL), state0)


def _rwkv_layer(x, shift, state, p, ln_g, ln_b, B, L):
    x3 = x.reshape(B, L, D_MODEL)
    xprev = jnp.concatenate([shift[:, None, :], x3[:, :-1]], axis=1).reshape(B * L, D_MODEL)
    proj = _rwkv_proj(x, xprev, p["mu4"], p["w4"])
    lw, a = _rwkv_lora(x, xprev, p["mu2"], p["w1"], p["w2"], p["w0"], p["a1"], p["a2"], p["a0"])
    o, st = _rwkv_scan(proj, lw, a, p["k_k"], p["k_a"], p["r_k"], p["gn_g"], p["gn_b"], state, B, L)
    return _mm_ln(o, p["w_out"], x, ln_g, ln_b), st, x3[:, -1]


def _pad_rank(w, axis):
    pad = [(0, 0), (0, 0)]
    pad[axis] = (0, LANES - w.shape[axis])
    return jnp.pad(w, pad).astype(BF16)


def _trunk(x3, cache, w):
    B, L, _ = x3.shape
    prompt = cache is None
    x = x3.reshape(B * L, D_MODEL)
    new_k, new_v, new_gla, new_wkv, new_shift = [], [], [], [], []
    for layer in range(DEPTH):
        kind, j = layer % 3, layer // 3
        g, b = w["ln_g"][layer], w["ln_b"][layer]
        if kind == 0:
            c = None if prompt else (cache["k"][j], cache["v"][j])
            x, nk, nv = _swa_layer(x, c, w["swa_w_in"][j], w["swa_sink"][j], w["swa_w_out"][j], g, b, B, L)
            new_k.append(nk)
            new_v.append(nv)
        elif kind == 1:
            st = None if prompt else cache["gla"][j]
            x, st = _gla_layer(x, st, w["gla_w_in"][j], w["gla_w_low"][j], w["gla_w_a2"][j], w["gla_b_a"][j],
                               w["gla_norm_g"][j], w["gla_w_out"][j], g, b, B, L)
            new_gla.append(st)
        else:
            if prompt:
                shift0 = jnp.zeros((B, D_MODEL), F32)
                s0 = jnp.zeros((B, RWKV_HEADS, RWKV_HEAD_DIM, RWKV_HEAD_DIM), F32)
            else:
                shift0, s0 = cache["shift"][j], cache["wkv"][j]
            x, st, sh = _rwkv_layer(x, shift0, s0, w["rwkv"][j], g, b, B, L)
            new_wkv.append(st)
            new_shift.append(sh)
    return (x.reshape(B, L, D_MODEL), jnp.stack(new_k), jnp.stack(new_v), jnp.stack(new_gla), jnp.stack(new_wkv),
            jnp.stack(new_shift))


def kernel(x_prompt, x_sample, cache_swa_k, cache_swa_v, state_gla, state_rwkv, state_rwkv_shift, ln_g, ln_b, swa_w_in, swa_sink, swa_w_out, gla_w_in, gla_w_a2, gla_b_a, gla_norm_g, gla_w_out, rwkv_mu, rwkv_w_rkvg, rwkv_w0, rwkv_w1, rwkv_w2, rwkv_a0, rwkv_a1, rwkv_a2, rwkv_k_k, rwkv_k_a, rwkv_r_k, rwkv_gn_g, rwkv_gn_b, rwkv_w_out):
    n_rwkv = rwkv_mu.shape[0]
    gla_main = 2 * GLA_KEY_DIM + 2 * GLA_VAL_DIM
    rwkv = []
    for j in range(n_rwkv):
        rwkv.append(dict(
            mu4=rwkv_mu[j][jnp.array([0, 2, 3, 5])].reshape(4, 1, D_MODEL),
            mu2=rwkv_mu[j][jnp.array([1, 4])].reshape(2, 1, D_MODEL),
            w4=rwkv_w_rkvg[j].astype(BF16),
            w1=_pad_rank(rwkv_w1[j], 1), w2=_pad_rank(rwkv_w2[j], 0), w0=rwkv_w0[j],
            a1=_pad_rank(rwkv_a1[j], 1), a2=_pad_rank(rwkv_a2[j], 0), a0=rwkv_a0[j],
            k_k=rwkv_k_k[j], k_a=rwkv_k_a[j], r_k=rwkv_r_k[j], gn_g=rwkv_gn_g[j], gn_b=rwkv_gn_b[j],
            w_out=rwkv_w_out[j].astype(BF16)))
    w = dict(ln_g=ln_g, ln_b=ln_b,
             swa_w_in=swa_w_in.astype(BF16), swa_sink=swa_sink, swa_w_out=swa_w_out.astype(BF16),
             gla_w_in=gla_w_in.astype(BF16),
             gla_w_low=[_pad_rank(gla_w_in[j][:, gla_main:], 1) for j in range(gla_w_in.shape[0])],
             gla_w_a2=[_pad_rank(gla_w_a2[j], 0) for j in range(gla_w_a2.shape[0])],
             gla_b_a=gla_b_a, gla_norm_g=gla_norm_g, gla_w_out=gla_w_out.astype(BF16), rwkv=rwkv)
    y_p, p_k, p_v, p_gla, p_wkv, p_shift = _trunk(x_prompt, None, w)
    cache = dict(k=cache_swa_k, v=cache_swa_v, gla=state_gla, wkv=state_rwkv, shift=state_rwkv_shift)
    y_s, s_k, s_v, s_gla, s_wkv, s_shift = _trunk(x_sample, cache, w)
    return (y_p, y_s, p_k, p_v, p_gla, p_wkv, p_shift, s_k, s_v, s_gla, s_wkv, s_shift)
```

```python
import functools
import math

import jax
import jax.numpy as jnp
from jax import lax
from jax.experimental import pallas as pl
from jax.experimental.pallas import tpu as pltpu

F32 = jnp.float32
BF16 = jnp.bfloat16

D_MODEL = 2048
DEPTH = 4
PAST_LEN = 16384
ALPHA = (2 * DEPTH) ** 0.25
LN_EPS = 1e-5

SWA_HEADS = 32
SWA_KV_HEADS = 4
SWA_GROUP = SWA_HEADS // SWA_KV_HEADS
SWA_HEAD_DIM = 64
SWA_WIDTH = SWA_HEADS * SWA_HEAD_DIM
SWA_KV_WIDTH = SWA_KV_HEADS * SWA_HEAD_DIM
WINDOW = 128
ROT_DIM = SWA_HEAD_DIM // 4
ROPE_THETA = 500000.0

GLA_HEADS = 4
GLA_KEY_DIM = D_MODEL // 2
GLA_VAL_DIM = D_MODEL
GLA_DK = GLA_KEY_DIM // GLA_HEADS
GLA_DV = GLA_VAL_DIM // GLA_HEADS
GLA_GATE_RANK = 16
GLA_GATE_TEMP = 16.0
GLA_CHUNK = 64
GLA_SUB = 16
GLA_NORM_EPS = 1e-5

RWKV_HEAD_DIM = 64
RWKV_HEADS = D_MODEL // RWKV_HEAD_DIM
RWKV_GN_EPS = 64e-5

LANES = 128
VMEM_LIMIT = 56 * 1024 * 1024
PROJ_ROWS = 1024
LORA_ROWS = 512
OUT_ROWS = 512
LN_SUB_ROWS = 128
NEG_BIG = -1e30


def _cparams(sem):
    return pltpu.CompilerParams(dimension_semantics=sem, vmem_limit_bytes=VMEM_LIMIT)


def _silu(x):
    return x * (1.0 / (1.0 + jnp.exp(-x)))


def _log_sigmoid(z):
    return jnp.minimum(z, 0.0) - jnp.log(1.0 + jnp.exp(-jnp.abs(z)))


def _nt(a, b):
    return lax.dot_general(a, b, (((1,), (1,)), ((), ())), preferred_element_type=F32)


def _mm_kernel(x_ref, w_ref, o_ref):
    o_ref[...] = jnp.dot(x_ref[...].astype(BF16), w_ref[...], preferred_element_type=F32).astype(o_ref.dtype)


def _mm(x, w, col0, ncols, tn=512, out_dtype=F32):
    M, K = x.shape
    tm = min(M, PROJ_ROWS)
    assert M % tm == 0 and ncols % tn == 0 and col0 % tn == 0
    cb = col0 // tn
    return pl.pallas_call(
        _mm_kernel,
        grid=(M // tm, ncols // tn),
        in_specs=[pl.BlockSpec((tm, K), lambda i, j: (i, 0)),
                  pl.BlockSpec((K, tn), lambda i, j: (0, cb + j))],
        out_specs=pl.BlockSpec((tm, tn), lambda i, j: (i, j)),
        out_shape=jax.ShapeDtypeStruct((M, ncols), out_dtype),
        compiler_params=_cparams(("parallel", "arbitrary")),
        name="proj_mm",
    )(x, w)


def _mm_ln_kernel(a_ref, w_ref, x_ref, g_ref, b_ref, o_ref):
    for r0 in range(0, a_ref.shape[0], LN_SUB_ROWS):
        rows = slice(r0, min(r0 + LN_SUB_ROWS, a_ref.shape[0]))
        h = jnp.dot(a_ref[rows, :], w_ref[...], preferred_element_type=F32)
        z = ALPHA * x_ref[rows, :] + h
        mu = jnp.mean(z, axis=-1, keepdims=True)
        zc = z - mu
        var = jnp.mean(zc * zc, axis=-1, keepdims=True)
        o_ref[rows, :] = zc * lax.rsqrt(var + LN_EPS) * g_ref[...] + b_ref[...]


def _mm_ln(a, w, x, g, b):
    M, K = a.shape
    tm = min(M, OUT_ROWS)
    D = w.shape[1]
    return pl.pallas_call(
        _mm_ln_kernel,
        grid=(M // tm,),
        in_specs=[pl.BlockSpec((tm, K), lambda i: (i, 0)),
                  pl.BlockSpec((K, D), lambda i: (0, 0)),
                  pl.BlockSpec((tm, D), lambda i: (i, 0)),
                  pl.BlockSpec((1, D), lambda i: (0, 0)),
                  pl.BlockSpec((1, D), lambda i: (0, 0))],
        out_specs=pl.BlockSpec((tm, D), lambda i: (i, 0)),
        out_shape=jax.ShapeDtypeStruct((M, D), F32),
        compiler_params=_cparams(("parallel",)),
        name="out_proj_ln",
    )(a, w, x, g.reshape(1, D), b.reshape(1, D))


def _rope_tables(pos):
    half = ROT_DIM // 2
    inv = ROPE_THETA ** (-(jnp.arange(half, dtype=F32) * 2.0 / ROT_DIM))
    ang = pos[:, None] * inv[None, :]
    cos, sin = jnp.cos(ang), jnp.sin(ang)
    L = pos.shape[0]
    ones = jnp.ones((L, SWA_HEAD_DIM - ROT_DIM), F32)
    zeros_r = jnp.zeros((L, SWA_HEAD_DIM - ROT_DIM), F32)
    zeros_h = jnp.zeros((L, half), F32)
    c = jnp.concatenate([cos, cos, ones], axis=1)
    s1 = jnp.concatenate([-sin, zeros_h, zeros_r], axis=1)
    s2 = jnp.concatenate([zeros_h, sin, zeros_r], axis=1)
    rep = LANES // SWA_HEAD_DIM
    return jnp.tile(c, (1, rep)), jnp.tile(s1, (1, rep)), jnp.tile(s2, (1, rep))


def _rope128(x, c, s1, s2):
    return x * c + pltpu.roll(x, LANES - ROT_DIM // 2, 1) * s1 + pltpu.roll(x, ROT_DIM // 2, 1) * s2


def _rope_wide(x, c, s1, s2):
    return jnp.concatenate([_rope128(x[:, i * LANES:(i + 1) * LANES], c, s1, s2)
                            for i in range(x.shape[1] // LANES)], axis=1)


def _swa_kernel(sink_ref, q_ref, kc_ref, vc_ref, kp_ref, vp_ref, gate_ref, tc_ref, tp_ref,
                o_ref, ko_ref, vo_ref, *, lq, prompt):
    c, s1, s2 = tc_ref[0], tc_ref[1], tc_ref[2]
    k_cur = _rope_wide(kc_ref[...], c, s1, s2)
    v_cur = vc_ref[...]
    if prompt:
        k_prev = _rope_wide(kp_ref[...], tp_ref[0], tp_ref[1], tp_ref[2])
        v_prev = vp_ref[...]
        ko_ref[0] = k_cur
        vo_ref[0] = v_cur
    else:
        k_prev = kp_ref[0]
        v_prev = vp_ref[0]
        ko_ref[0] = jnp.concatenate([k_prev[lq:], k_cur], axis=0)
        vo_ref[0] = jnp.concatenate([v_prev[lq:], v_cur], axis=0)
        zpad = jnp.zeros((WINDOW - lq, SWA_KV_WIDTH), F32)
        k_cur = jnp.concatenate([k_cur, zpad], axis=0)
        v_cur = jnp.concatenate([v_cur, zpad], axis=0)
    k_all = jnp.concatenate([k_prev, k_cur], axis=0)
    v_all = jnp.concatenate([v_prev, v_cur], axis=0)
    nk = 2 * WINDOW
    nh = SWA_GROUP
    qi = lax.broadcasted_iota(jnp.int32, (lq, nk), 0)
    sj = lax.broadcasted_iota(jnp.int32, (lq, nk), 1)
    rel = qi + WINDOW - sj
    mask = jnp.logical_and(rel >= 0, rel <= WINDOW)
    if prompt:
        lo = jnp.where(pl.program_id(1) == 0, WINDOW, 0)
        mask = jnp.logical_and(mask, sj >= lo)
    m0 = lax.broadcasted_iota(jnp.int32, (lq, LANES), 1) < SWA_HEAD_DIM
    scale = SWA_HEAD_DIM ** -0.5
    hd = SWA_HEAD_DIM
    for kh in range(SWA_KV_HEADS):
        kg = k_all[:, kh * hd:(kh + 1) * hd]
        vg = v_all[:, kh * hd:(kh + 1) * hd]
        kk2 = jnp.concatenate([kg, kg], axis=1).astype(BF16)
        vv2 = jnp.concatenate([vg, vg], axis=1).astype(BF16)
        p0 = kh * (nh // 2)
        rows = []
        for p in range(p0, p0 + nh // 2):
            q2 = _rope128(q_ref[:, p * LANES:(p + 1) * LANES], c, s1, s2) * scale
            rows += [jnp.where(m0, q2, 0.0), jnp.where(m0, 0.0, q2)]
        qs = jnp.concatenate(rows, axis=0).astype(BF16)
        s_all = _nt(qs, kk2)
        es, dens = [], []
        for i in range(nh):
            snk = sink_ref[kh * nh + i]
            s = jnp.where(mask, s_all[i * lq:(i + 1) * lq], NEG_BIG)
            m = jnp.maximum(jnp.max(s, axis=-1, keepdims=True), snk)
            e = jnp.exp(s - m)
            dens.append(jnp.sum(e, axis=-1, keepdims=True) + jnp.exp(snk - m))
            es.append(e.astype(BF16))
        o_all = jnp.dot(jnp.concatenate(es, axis=0), vv2, preferred_element_type=F32)
        for i, p in enumerate(range(p0, p0 + nh // 2)):
            oa = o_all[2 * i * lq:(2 * i + 1) * lq] / dens[2 * i]
            ob = o_all[(2 * i + 1) * lq:(2 * i + 2) * lq] / dens[2 * i + 1]
            g2 = gate_ref[:, p * LANES:(p + 1) * LANES]
            o_ref[:, p * LANES:(p + 1) * LANES] = (jnp.where(m0, oa, ob) * _silu(g2)).astype(o_ref.dtype)


def _swa_prompt(qkv, gate, sink, B, S):
    nb = S // WINDOW
    pos = jnp.arange(S, dtype=F32)
    tabs = jnp.stack(_rope_tables(pos))
    kcol = SWA_WIDTH // SWA_KV_WIDTH
    prev = lambda b, n: b * nb + jnp.maximum(n - 1, 0)
    return pl.pallas_call(
        functools.partial(_swa_kernel, lq=WINDOW, prompt=True),
        grid=(B, nb),
        in_specs=[pl.BlockSpec(memory_space=pltpu.SMEM),
                  pl.BlockSpec((WINDOW, SWA_WIDTH), lambda b, n: (b * nb + n, 0)),
                  pl.BlockSpec((WINDOW, SWA_KV_WIDTH), lambda b, n: (b * nb + n, kcol)),
                  pl.BlockSpec((WINDOW, SWA_KV_WIDTH), lambda b, n: (b * nb + n, kcol + 1)),
                  pl.BlockSpec((WINDOW, SWA_KV_WIDTH), lambda b, n: (prev(b, n), kcol)),
                  pl.BlockSpec((WINDOW, SWA_KV_WIDTH), lambda b, n: (prev(b, n), kcol + 1)),
                  pl.BlockSpec((WINDOW, SWA_WIDTH), lambda b, n: (b * nb + n, 0)),
                  pl.BlockSpec((3, WINDOW, LANES), lambda b, n: (0, n, 0)),
                  pl.BlockSpec((3, WINDOW, LANES), lambda b, n: (0, jnp.maximum(n - 1, 0), 0))],
        out_specs=[pl.BlockSpec((WINDOW, SWA_WIDTH), lambda b, n: (b * nb + n, 0)),
                   pl.BlockSpec((1, WINDOW, SWA_KV_WIDTH), lambda b, n: (b, 0, 0)),
                   pl.BlockSpec((1, WINDOW, SWA_KV_WIDTH), lambda b, n: (b, 0, 0))],
        out_shape=[jax.ShapeDtypeStruct((B * S, SWA_WIDTH), BF16),
                   jax.ShapeDtypeStruct((B, WINDOW, SWA_KV_WIDTH), F32),
                   jax.ShapeDtypeStruct((B, WINDOW, SWA_KV_WIDTH), F32)],
        compiler_params=_cparams(("parallel", "arbitrary")),
        name="swa_prompt",
    )(sink, qkv, qkv, qkv, qkv, qkv, gate, tabs, tabs)


def _swa_sample(qkv, gate, sink, cache_k, cache_v, B, L):
    pos = PAST_LEN + jnp.arange(L, dtype=F32)
    tabs = jnp.stack(_rope_tables(pos))
    kcol = SWA_WIDTH // SWA_KV_WIDTH
    return pl.pallas_call(
        functools.partial(_swa_kernel, lq=L, prompt=False),
        grid=(B,),
        in_specs=[pl.BlockSpec(memory_space=pltpu.SMEM),
                  pl.BlockSpec((L, SWA_WIDTH), lambda b: (b, 0)),
                  pl.BlockSpec((L, SWA_KV_WIDTH), lambda b: (b, kcol)),
                  pl.BlockSpec((L, SWA_KV_WIDTH), lambda b: (b, kcol + 1)),
                  pl.BlockSpec((1, WINDOW, SWA_KV_WIDTH), lambda b: (b, 0, 0)),
                  pl.BlockSpec((1, WINDOW, SWA_KV_WIDTH), lambda b: (b, 0, 0)),
                  pl.BlockSpec((L, SWA_WIDTH), lambda b: (b, 0)),
                  pl.BlockSpec((3, L, LANES), lambda b: (0, 0, 0)),
                  pl.BlockSpec((3, L, LANES), lambda b: (0, 0, 0))],
        out_specs=[pl.BlockSpec((L, SWA_WIDTH), lambda b: (b, 0)),
                   pl.BlockSpec((1, WINDOW, SWA_KV_WIDTH), lambda b: (b, 0, 0)),
                   pl.BlockSpec((1, WINDOW, SWA_KV_WIDTH), lambda b: (b, 0, 0))],
        out_shape=[jax.ShapeDtypeStruct((B * L, SWA_WIDTH), BF16),
                   jax.ShapeDtypeStruct((B, WINDOW, SWA_KV_WIDTH), F32),
                   jax.ShapeDtypeStruct((B, WINDOW, SWA_KV_WIDTH), F32)],
        compiler_params=_cparams(("parallel",)),
        name="swa_sample",
    )(sink, qkv, qkv, qkv, cache_k, cache_v, gate, tabs, tabs)


def _swa_layer(x, cache, w_in, sink, w_out, ln_g, ln_b, B, L):
    qkv = _mm(x, w_in, 0, SWA_WIDTH + 2 * SWA_KV_WIDTH)
    gate = _mm(x, w_in, SWA_WIDTH + 2 * SWA_KV_WIDTH, SWA_WIDTH)
    if cache is None:
        o, nk, nv = _swa_prompt(qkv, gate, sink, B, L)
    else:
        ck = cache[0].reshape(B, WINDOW, SWA_KV_WIDTH)
        cv = cache[1].reshape(B, WINDOW, SWA_KV_WIDTH)
        o, nk, nv = _swa_sample(qkv, gate, sink, ck, cv, B, L)
    shape = (B, WINDOW, SWA_KV_HEADS, SWA_HEAD_DIM)
    return _mm_ln(o, w_out, x, ln_g, ln_b), nk.reshape(shape), nv.reshape(shape)


def _gla_gate_kernel(x_ref, wl_ref, wa_ref, ba_ref, g_ref):
    a_low = jnp.dot(x_ref[...].astype(BF16), wl_ref[...], preferred_element_type=F32)
    z = jnp.dot(a_low.astype(BF16), wa_ref[...], preferred_element_type=F32) + ba_ref[...]
    g_ref[...] = _log_sigmoid(z) * (1.0 / GLA_GATE_TEMP)


def _gla_gate(x, w_low, w_a2, b_a):
    M = x.shape[0]
    tm = min(M, PROJ_ROWS)
    return pl.pallas_call(
        _gla_gate_kernel,
        grid=(M // tm,),
        in_specs=[pl.BlockSpec((tm, D_MODEL), lambda i: (i, 0)),
                  pl.BlockSpec((D_MODEL, LANES), lambda i: (0, 0)),
                  pl.BlockSpec((LANES, GLA_KEY_DIM), lambda i: (0, 0)),
                  pl.BlockSpec((1, GLA_KEY_DIM), lambda i: (0, 0))],
        out_specs=pl.BlockSpec((tm, GLA_KEY_DIM), lambda i: (i, 0)),
        out_shape=jax.ShapeDtypeStruct((M, GLA_KEY_DIM), F32),
        compiler_params=_cparams(("parallel",)),
        name="gla_gate",
    )(x, w_low, w_a2, b_a.reshape(1, GLA_KEY_DIM))


def _tril3(C):
    return (lax.broadcasted_iota(jnp.int32, (C, 3 * C), 1) % C
            <= lax.broadcasted_iota(jnp.int32, (C, 3 * C), 0)).astype(BF16)


def _cumsum_rows(x, tril3):
    h1 = x.astype(BF16)
    r1 = x - h1.astype(F32)
    h2 = r1.astype(BF16)
    h3 = (r1 - h2.astype(F32)).astype(BF16)
    return jnp.dot(tril3, jnp.concatenate([h1, h2, h3], axis=0), preferred_element_type=F32)


def _gla_chunk_kernel(q_ref, k_ref, v_ref, gate_ref, g_ref, ng_ref, s0_ref, o_ref, so_ref, st_ref, *, C, SB):
    c_idx = pl.program_id(1)
    H, DK, DV = GLA_HEADS, GLA_DK, GLA_DV

    @pl.when(c_idx == 0)
    def _():
        for h in range(H):
            st_ref[h] = s0_ref[0, h].T

    b = _cumsum_rows(g_ref[...], _tril3(C))
    q = q_ref[...] * (DK ** -0.5)
    k = k_ref[...]
    v16 = v_ref[...].astype(BF16)
    b_last = b[C - 1:C, :]
    e_last = jnp.exp(b_last)
    qe = (q * jnp.exp(b)).astype(BF16)
    kd = (k * jnp.exp(b_last - b)).astype(BF16)
    row = lax.broadcasted_iota(jnp.int32, (C, H * DK), 0)
    qis, kjs = [], []
    for i in range(C // SB):
        r0, r1 = i * SB, (i + 1) * SB
        bn = b[r0:r0 + 1, :]
        qis.append((q[r0:r1] * jnp.exp(b[r0:r1] - bn)).astype(BF16))
        kjs.append((k * jnp.exp(jnp.where(row < r1, bn - b, 0.0))).astype(BF16))
    causal = lax.broadcasted_iota(jnp.int32, (C, C), 1) <= lax.broadcasted_iota(jnp.int32, (C, C), 0)

    results = []
    for h in range(H):
        ks, vs = slice(h * DK, (h + 1) * DK), slice(h * DV, (h + 1) * DV)
        st = st_ref[h]
        a_parts = [_nt(qi[:, ks], kj[:, ks]) for qi, kj in zip(qis, kjs)]
        a = jnp.concatenate(a_parts, axis=0) if len(a_parts) > 1 else a_parts[0]
        a = jnp.where(causal, a, 0.0).astype(BF16)
        o = _nt(qe[:, ks], st.astype(BF16)) + jnp.dot(a, v16[:, vs], preferred_element_type=F32)
        st_new = st * e_last[:, ks] + lax.dot_general(v16[:, vs], kd[:, ks], (((0,), (0,)), ((), ())),
                                                      preferred_element_type=F32)
        on = o * lax.rsqrt(jnp.mean(o * o, axis=-1, keepdims=True) + GLA_NORM_EPS) * ng_ref[...]
        results.append((st_new, (on * _silu(gate_ref[:, vs])).astype(o_ref.dtype)))
    for h, (st_new, out) in enumerate(results):
        st_ref[h] = st_new
        o_ref[:, h * DV:(h + 1) * DV] = out

    @pl.when(c_idx == pl.num_programs(1) - 1)
    def _():
        for h in range(H):
            so_ref[0, h] = st_ref[h].T


def _gla_chunk(h, g, norm_g, state0, B, L):
    C = min(GLA_CHUNK, L)
    SB = min(GLA_SUB, C)
    nc = L // C
    row = lambda b, c: b * nc + c
    st = pl.BlockSpec((1, GLA_HEADS, GLA_DK, GLA_DV), lambda b, c: (b, 0, 0, 0))
    return pl.pallas_call(
        functools.partial(_gla_chunk_kernel, C=C, SB=SB),
        grid=(B, nc),
        in_specs=[pl.BlockSpec((C, GLA_KEY_DIM), lambda b, c: (row(b, c), 0)),
                  pl.BlockSpec((C, GLA_KEY_DIM), lambda b, c: (row(b, c), 1)),
                  pl.BlockSpec((C, GLA_VAL_DIM), lambda b, c: (row(b, c), 1)),
                  pl.BlockSpec((C, GLA_VAL_DIM), lambda b, c: (row(b, c), 2)),
                  pl.BlockSpec((C, GLA_KEY_DIM), lambda b, c: (row(b, c), 0)),
                  pl.BlockSpec((1, GLA_DV), lambda b, c: (0, 0)),
                  st],
        out_specs=[pl.BlockSpec((C, GLA_VAL_DIM), lambda b, c: (row(b, c), 0)), st],
        out_shape=[jax.ShapeDtypeStruct((B * L, GLA_VAL_DIM), BF16),
                   jax.ShapeDtypeStruct((B, GLA_HEADS, GLA_DK, GLA_DV), F32)],
        scratch_shapes=[pltpu.VMEM((GLA_HEADS, GLA_DV, GLA_DK), F32)],
        compiler_params=_cparams(("parallel", "arbitrary")),
        name="gla_chunk",
    )(h, h, h, h, g, norm_g.reshape(1, GLA_DV), state0)


def _gla_layer(x, state, w_in, w_low, w_a2, b_a, norm_g, w_out, ln_g, ln_b, B, L):
    h = _mm(x, w_in, 0, 2 * GLA_KEY_DIM + 2 * GLA_VAL_DIM)
    g = _gla_gate(x, w_low, w_a2, b_a)
    if state is None:
        state = jnp.zeros((B, GLA_HEADS, GLA_DK, GLA_DV), F32)
    o, st = _gla_chunk(h, g, norm_g, state, B, L)
    return _mm_ln(o, w_out, x, ln_g, ln_b), st


def _rwkv_proj_kernel(x_ref, xp_ref, mu_ref, w_ref, o_ref):
    x = x_ref[...]
    xm = x + (xp_ref[...] - x) * mu_ref[0]
    o_ref[0] = jnp.dot(xm.astype(BF16), w_ref[0], preferred_element_type=F32)


def _rwkv_proj(x, xprev, mu4, w4, tn=512):
    M = x.shape[0]
    tm = min(M, PROJ_ROWS)
    return pl.pallas_call(
        _rwkv_proj_kernel,
        grid=(M // tm, 4, D_MODEL // tn),
        in_specs=[pl.BlockSpec((tm, D_MODEL), lambda i, m, j: (i, 0)),
                  pl.BlockSpec((tm, D_MODEL), lambda i, m, j: (i, 0)),
                  pl.BlockSpec((1, 1, D_MODEL), lambda i, m, j: (m, 0, 0)),
                  pl.BlockSpec((1, D_MODEL, tn), lambda i, m, j: (m, 0, j))],
        out_specs=pl.BlockSpec((1, tm, tn), lambda i, m, j: (m, i, j)),
        out_shape=jax.ShapeDtypeStruct((4, M, D_MODEL), F32),
        compiler_params=_cparams(("parallel", "arbitrary", "arbitrary")),
        name="rwkv_proj",
    )(x, xprev, mu4, w4)


def _rwkv_lora_kernel(x_ref, xp_ref, mu_ref, w1_ref, w2_ref, w0_ref, a1_ref, a2_ref, a0_ref, lw_ref, a_ref):
    x = x_ref[...]
    xx = xp_ref[...] - x
    xw = (x + xx * mu_ref[0]).astype(BF16)
    xa = (x + xx * mu_ref[1]).astype(BF16)
    t = jnp.tanh(jnp.dot(xw, w1_ref[...], preferred_element_type=F32))
    wl = w0_ref[...] + jnp.dot(t.astype(BF16), w2_ref[...], preferred_element_type=F32)
    w_log = _log_sigmoid(wl) - 0.5
    lw_ref[...] = -jnp.exp(w_log)
    al = jnp.dot(xa, a1_ref[...], preferred_element_type=F32)
    az = a0_ref[...] + jnp.dot(al.astype(BF16), a2_ref[...], preferred_element_type=F32)
    a_ref[...] = 1.0 / (1.0 + jnp.exp(-az))


def _rwkv_lora(x, xprev, mu2, w1, w2, w0, a1, a2, a0):
    M = x.shape[0]
    tm = min(M, LORA_ROWS)
    R = w1.shape[1]
    full = lambda shape: pl.BlockSpec(shape, lambda i: tuple(0 for _ in shape))
    return pl.pallas_call(
        _rwkv_lora_kernel,
        grid=(M // tm,),
        in_specs=[pl.BlockSpec((tm, D_MODEL), lambda i: (i, 0)),
                  pl.BlockSpec((tm, D_MODEL), lambda i: (i, 0)),
                  full((2, 1, D_MODEL)), full((D_MODEL, R)), full((R, D_MODEL)), full((1, D_MODEL)),
                  full((D_MODEL, R)), full((R, D_MODEL)), full((1, D_MODEL))],
        out_specs=[pl.BlockSpec((tm, D_MODEL), lambda i: (i, 0)),
                   pl.BlockSpec((tm, D_MODEL), lambda i: (i, 0))],
        out_shape=[jax.ShapeDtypeStruct((M, D_MODEL), F32), jax.ShapeDtypeStruct((M, D_MODEL), F32)],
        compiler_params=_cparams(("parallel",)),
        name="rwkv_lora",
    )(x, xprev, mu2, w1, w2, w0.reshape(1, D_MODEL), a1, a2, a0.reshape(1, D_MODEL))


def _head_ones():
    r = lax.broadcasted_iota(jnp.int32, (LANES, LANES), 0) // RWKV_HEAD_DIM
    c = lax.broadcasted_iota(jnp.int32, (LANES, LANES), 1) // RWKV_HEAD_DIM
    e = (r == c).astype(BF16)
    return jnp.concatenate([e, e], axis=0)


def _segsum(x, e2):
    hi = x.astype(BF16)
    lo = (x - hi.astype(F32)).astype(BF16)
    return jnp.dot(jnp.concatenate([hi, lo], axis=1), e2, preferred_element_type=F32)


def _segsum_wide(x, e2):
    rows, n = x.shape[0], x.shape[1] // LANES
    s = _segsum(jnp.concatenate([x[:, i * LANES:(i + 1) * LANES] for i in range(n)], axis=0), e2)
    return jnp.concatenate([s[i * rows:(i + 1) * rows] for i in range(n)], axis=1)


RWKV_PAIRS = 16
RWKV_CHUNK = 64


def _rwkv_chunk_kernel(r_ref, k_ref, v_ref, gt_ref, lw_ref, a_ref, kk_ref, ka_ref, rk_ref, gng_ref, gnb_ref, s0_ref,
                       o_ref, so_ref, st_ref, *, lreal):
    C, N = RWKV_CHUNK, RWKV_HEAD_DIM
    t_idx = pl.program_id(2)
    e2 = _head_ones()
    m0 = lax.broadcasted_iota(jnp.int32, (C, LANES), 1) < N
    ti = lax.broadcasted_iota(jnp.int32, (C, 2 * C), 0)
    si2 = lax.broadcasted_iota(jnp.int32, (C, 2 * C), 1)
    si = si2 % C
    strict, incl, left = si < ti, si <= ti, si2 < C
    bd = (lax.broadcasted_iota(jnp.int32, (LANES, LANES), 0) // N
          == lax.broadcasted_iota(jnp.int32, (LANES, LANES), 1) // N)
    zs = jnp.zeros((N, N), F32)

    @pl.when(t_idx == 0)
    def _():
        for p in range(RWKV_PAIRS):
            top = jnp.concatenate([s0_ref[0, 2 * p], zs], axis=1)
            bot = jnp.concatenate([zs, s0_ref[0, 2 * p + 1]], axis=1)
            st_ref[p] = jnp.concatenate([top, bot], axis=0)

    def both(x, y):
        parts = [x] if y is None else [x, y]
        return jnp.concatenate([jnp.where(m0, z, 0.0) for z in parts] + [jnp.where(m0, 0.0, z) for z in parts],
                               axis=0).astype(BF16)

    def load(x):
        if lreal < C:
            x = jnp.concatenate([x, jnp.zeros((C - lreal, x.shape[1]), F32)], axis=0)
        return x

    inv_n = 1.0 / N
    n_iter = C.bit_length() - 1
    lanes = [slice(p * LANES, (p + 1) * LANES) for p in range(RWKV_PAIRS)]

    r, k, v = load(r_ref[0]), load(k_ref[0]), load(v_ref[0])
    lw, a = load(lw_ref[...]), load(a_ref[...])
    kk = k * kk_ref[...]
    kk = kk / jnp.maximum(jnp.sqrt(_segsum_wide(kk * kk, e2)), 1e-12)
    kp = k * (1.0 + (a - 1.0) * ka_ref[...])
    c = _cumsum_rows(lw, _tril3(C))
    e_c, e_nc = jnp.exp(c), jnp.exp(-c)
    at_w, rt_w = -kk * jnp.exp(c - lw), r * e_c
    bt_w, kt_w = (kk * a * e_nc).astype(BF16), (kp * e_nc).astype(BF16)
    bonus = _segsum_wide((r * kp * rk_ref[...])[:lreal], e2)

    def setup(p):
        ls = lanes[p]
        at, rt = at_w[:, ls], rt_w[:, ls]
        bk = jnp.concatenate([bt_w[:, ls], kt_w[:, ls]], axis=0)
        g = _nt(both(at, rt), bk)
        aa0, rr0 = jnp.where(strict, g[0:C], 0.0), jnp.where(incl, g[C:2 * C], 0.0)
        aa1, rr1 = jnp.where(strict, g[2 * C:3 * C], 0.0), jnp.where(incl, g[3 * C:], 0.0)
        a_ab = jnp.where(left, aa0, pltpu.roll(aa1, C, 1))
        a_ak = jnp.where(left, pltpu.roll(aa0, C, 1), aa1)

        s_bd = st_ref[p]
        pq = _nt(jnp.concatenate([at, rt], axis=0).astype(BF16), s_bd.astype(BF16))
        x = pq[:C] + jnp.dot(a_ak.astype(BF16), both(v[:, ls], None), preferred_element_type=F32)
        rr = jnp.concatenate([rr0, rr1], axis=1).astype(BF16)
        return dict(bk=bk, s_bd=s_bd, y0=pq[C:], rr=rr, x=x, ac=a_ab)

    def neumann(d, it):
        ac = d["ac"]
        ac16 = ac.astype(BF16)
        rhs = both(d["x"], None)
        if it < n_iter - 1:
            a_bd = jnp.concatenate([jnp.where(left, ac, 0.0), jnp.where(left, 0.0, ac)], axis=0).astype(BF16)
            res = jnp.dot(ac16, jnp.concatenate([rhs, a_bd], axis=1), preferred_element_type=F32)
            d["x"] = d["x"] + res[:, :LANES]
            d["ac"] = res[:, LANES:]
        else:
            d["x"] = d["x"] + jnp.dot(ac16, rhs, preferred_element_type=F32)

    def finish(p, d):
        ls = lanes[p]
        u, vp = d["x"], v[:, ls]
        y = d["y0"] + jnp.dot(d["rr"], both(u, vp), preferred_element_type=F32)
        uv = jnp.concatenate([u, vp], axis=0).astype(BF16)
        ds = lax.dot_general(uv, d["bk"], (((0,), (0,)), ((), ())), preferred_element_type=F32)
        return (d["s_bd"] + jnp.where(bd, ds, 0.0)) * e_c[C - 1:C, ls], y[:lreal]

    work = [setup(p) for p in range(RWKV_PAIRS)]
    for it in range(n_iter):
        for d in work:
            neumann(d, it)
    done = [finish(p, d) for p, d in enumerate(work)]
    for p, (s_new, _) in enumerate(done):
        st_ref[p] = s_new

    y = jnp.concatenate([yp for _, yp in done], axis=1)
    yc = y - _segsum_wide(y, e2) * inv_n
    yv = _segsum_wide(yc * yc, e2) * inv_n
    yn = yc * lax.rsqrt(yv + RWKV_GN_EPS) * gng_ref[...] + gnb_ref[...]
    o_ref[...] = ((yn + bonus * v[:lreal]) * _silu(gt_ref[0])).astype(o_ref.dtype)

    @pl.when(t_idx == pl.num_programs(2) - 1)
    def _():
        for p in range(RWKV_PAIRS):
            s = st_ref[p]
            so_ref[0, 2 * p] = s[:N, :N]
            so_ref[0, 2 * p + 1] = s[N:, N:]


def _rwkv_scan(proj, lw, a, k_k, k_a, r_k, gn_g, gn_b, state0, B, L):
    tc = min(L, RWKV_CHUNK)
    nt = L // tc
    W = RWKV_PAIRS * LANES
    ng = D_MODEL // W
    hpg = 2 * RWKV_PAIRS
    row = lambda b, g, t: b * nt + t
    tok = lambda m: pl.BlockSpec((1, tc, W), lambda b, g, t: (m, row(b, g, t), g))
    vec = pl.BlockSpec((tc, W), lambda b, g, t: (row(b, g, t), g))
    par = pl.BlockSpec((1, W), lambda b, g, t: (0, g))
    st = pl.BlockSpec((1, hpg, RWKV_HEAD_DIM, RWKV_HEAD_DIM), lambda b, g, t: (b, g, 0, 0))
    return pl.pallas_call(
        functools.partial(_rwkv_chunk_kernel, lreal=tc),
        grid=(B, ng, nt),
        in_specs=[tok(0), tok(1), tok(2), tok(3), vec, vec, par, par, par, par, par, st],
        out_specs=[vec, st],
        out_shape=[jax.ShapeDtypeStruct((B * L, D_MODEL), BF16),
                   jax.ShapeDtypeStruct((B, RWKV_HEADS, RWKV_HEAD_DIM, RWKV_HEAD_DIM), F32)],
        scratch_shapes=[pltpu.VMEM((RWKV_PAIRS, LANES, LANES), F32)],
        compiler_params=_cparams(("parallel", "parallel", "arbitrary")),
        name="rwkv_scan",
    )(proj, proj, proj, proj, lw, a, k_k.reshape(1, D_MODEL), k_a.reshape(1, D_MODEL), r_k.reshape(1, D_MODEL),
      gn_g.reshape(1, D_MODEL), gn_b.reshape(1, D_MODEL), state0)


def _rwkv_layer(x, shift, state, p, ln_g, ln_b, B, L):
    x3 = x.reshape(B, L, D_MODEL)
    xprev = jnp.concatenate([shift[:, None, :], x3[:, :-1]], axis=1).reshape(B * L, D_MODEL)
    proj = _rwkv_proj(x, xprev, p["mu4"], p["w4"])
    lw, a = _rwkv_lora(x, xprev, p["mu2"], p["w1"], p["w2"], p["w0"], p["a1"], p["a2"], p["a0"])
    o, st = _rwkv_scan(proj, lw, a, p["k_k"], p["k_a"], p["r_k"], p["gn_g"], p["gn_b"], state, B, L)
    return _mm_ln(o, p["w_out"], x, ln_g, ln_b), st, x3[:, -1]


def _pad_rank(w, axis):
    pad = [(0, 0), (0, 0)]
    pad[axis] = (0, LANES - w.shape[axis])
    return jnp.pad(w, pad).astype(BF16)


def _trunk(x3, cache, w):
    B, L, _ = x3.shape
    prompt = cache is None
    x = x3.reshape(B * L, D_MODEL)
    new_k, new_v, new_gla, new_wkv, new_shift = [], [], [], [], []
    for layer in range(DEPTH):
        kind, j = layer % 3, layer // 3
        g, b = w["ln_g"][layer], w["ln_b"][layer]
        if kind == 0:
            c = None if prompt else (cache["k"][j], cache["v"][j])
            x, nk, nv = _swa_layer(x, c, w["swa_w_in"][j], w["swa_sink"][j], w["swa_w_out"][j], g, b, B, L)
            new_k.append(nk)
            new_v.append(nv)
        elif kind == 1:
            st = None if prompt else cache["gla"][j]
            x, st = _gla_layer(x, st, w["gla_w_in"][j], w["gla_w_low"][j], w["gla_w_a2"][j], w["gla_b_a"][j],
                               w["gla_norm_g"][j], w["gla_w_out"][j], g, b, B, L)
            new_gla.append(st)
        else:
            if prompt:
                shift0 = jnp.zeros((B, D_MODEL), F32)
                s0 = jnp.zeros((B, RWKV_HEADS, RWKV_HEAD_DIM, RWKV_HEAD_DIM), F32)
            else:
                shift0, s0 = cache["shift"][j], cache["wkv"][j]
            x, st, sh = _rwkv_layer(x, shift0, s0, w["rwkv"][j], g, b, B, L)
            new_wkv.append(st)
            new_shift.append(sh)
    return (x.reshape(B, L, D_MODEL), jnp.stack(new_k), jnp.stack(new_v), jnp.stack(new_gla), jnp.stack(new_wkv),
            jnp.stack(new_shift))


def kernel(x_prompt, x_sample, cache_swa_k, cache_swa_v, state_gla, state_rwkv, state_rwkv_shift, ln_g, ln_b, swa_w_in, swa_sink, swa_w_out, gla_w_in, gla_w_a2, gla_b_a, gla_norm_g, gla_w_out, rwkv_mu, rwkv_w_rkvg, rwkv_w0, rwkv_w1, rwkv_w2, rwkv_a0, rwkv_a1, rwkv_a2, rwkv_k_k, rwkv_k_a, rwkv_r_k, rwkv_gn_g, rwkv_gn_b, rwkv_w_out):
    n_rwkv = rwkv_mu.shape[0]
    gla_main = 2 * GLA_KEY_DIM + 2 * GLA_VAL_DIM
    rwkv = []
    for j in range(n_rwkv):
        rwkv.append(dict(
            mu4=rwkv_mu[j][jnp.array([0, 2, 3, 5])].reshape(4, 1, D_MODEL),
            mu2=rwkv_mu[j][jnp.array([1, 4])].reshape(2, 1, D_MODEL),
            w4=rwkv_w_rkvg[j].astype(BF16),
            w1=_pad_rank(rwkv_w1[j], 1), w2=_pad_rank(rwkv_w2[j], 0), w0=rwkv_w0[j],
            a1=_pad_rank(rwkv_a1[j], 1), a2=_pad_rank(rwkv_a2[j], 0), a0=rwkv_a0[j],
            k_k=rwkv_k_k[j], k_a=rwkv_k_a[j], r_k=rwkv_r_k[j], gn_g=rwkv_gn_g[j], gn_b=rwkv_gn_b[j],
            w_out=rwkv_w_out[j].astype(BF16)))
    w = dict(ln_g=ln_g, ln_b=ln_b,
             swa_w_in=swa_w_in.astype(BF16), swa_sink=swa_sink, swa_w_out=swa_w_out.astype(BF16),
             gla_w_in=gla_w_in[:, :, :gla_main].astype(BF16),
             gla_w_low=[_pad_rank(gla_w_in[j][:, gla_main:], 1) for j in range(gla_w_in.shape[0])],
             gla_w_a2=[_pad_rank(gla_w_a2[j], 0) for j in range(gla_w_a2.shape[0])],
             gla_b_a=gla_b_a, gla_norm_g=gla_norm_g, gla_w_out=gla_w_out.astype(BF16), rwkv=rwkv)
    y_p, p_k, p_v, p_gla, p_wkv, p_shift = _trunk(x_prompt, None, w)
    cache = dict(k=cache_swa_k, v=cache_swa_v, gla=state_gla, wkv=state_rwkv, shift=state_rwkv_shift)
    y_s, s_k, s_v, s_gla, s_wkv, s_shift = _trunk(x_sample, cache, w)
    return (y_p, y_s, p_k, p_v, p_gla, p_wkv, p_shift, s_k, s_v, s_gla, s_wkv, s_shift)
```

```python
import functools
import math

import jax
import jax.numpy as jnp
from jax import lax
from jax.experimental import pallas as pl
from jax.experimental.pallas import tpu as pltpu

F32 = jnp.float32
BF16 = jnp.bfloat16

D_MODEL = 2048
DEPTH = 4
PAST_LEN = 16384
ALPHA = (2 * DEPTH) ** 0.25
LN_EPS = 1e-5

SWA_HEADS = 32
SWA_KV_HEADS = 4
SWA_GROUP = SWA_HEADS // SWA_KV_HEADS
SWA_HEAD_DIM = 64
SWA_WIDTH = SWA_HEADS * SWA_HEAD_DIM
SWA_KV_WIDTH = SWA_KV_HEADS * SWA_HEAD_DIM
WINDOW = 128
ROT_DIM = SWA_HEAD_DIM // 4
ROPE_THETA = 500000.0

GLA_HEADS = 4
GLA_KEY_DIM = D_MODEL // 2
GLA_VAL_DIM = D_MODEL
GLA_DK = GLA_KEY_DIM // GLA_HEADS
GLA_DV = GLA_VAL_DIM // GLA_HEADS
GLA_GATE_RANK = 16
GLA_GATE_TEMP = 16.0
GLA_CHUNK = 64
GLA_SUB = 16
GLA_NORM_EPS = 1e-5

RWKV_HEAD_DIM = 64
RWKV_HEADS = D_MODEL // RWKV_HEAD_DIM
RWKV_GN_EPS = 64e-5

LANES = 128
VMEM_LIMIT = 56 * 1024 * 1024
PROJ_ROWS = 1024
LORA_ROWS = 512
OUT_ROWS = 512
LN_SUB_ROWS = 128
NEG_BIG = -1e30


def _cparams(sem):
    return pltpu.CompilerParams(dimension_semantics=sem, vmem_limit_bytes=VMEM_LIMIT)


def _silu(x):
    return x * (1.0 / (1.0 + jnp.exp(-x)))


def _log_sigmoid(z):
    return jnp.minimum(z, 0.0) - jnp.log(1.0 + jnp.exp(-jnp.abs(z)))


def _nt(a, b):
    return lax.dot_general(a, b, (((1,), (1,)), ((), ())), preferred_element_type=F32)


def _mm_kernel(x_ref, w_ref, o_ref):
    o_ref[...] = jnp.dot(x_ref[...].astype(BF16), w_ref[...], preferred_element_type=F32).astype(o_ref.dtype)


def _mm(x, w, col0, ncols, tn=512, out_dtype=F32):
    M, K = x.shape
    tm = min(M, PROJ_ROWS)
    assert M % tm == 0 and ncols % tn == 0 and col0 % tn == 0
    cb = col0 // tn
    return pl.pallas_call(
        _mm_kernel,
        grid=(M // tm, ncols // tn),
        in_specs=[pl.BlockSpec((tm, K), lambda i, j: (i, 0)),
                  pl.BlockSpec((K, tn), lambda i, j: (0, cb + j))],
        out_specs=pl.BlockSpec((tm, tn), lambda i, j: (i, j)),
        out_shape=jax.ShapeDtypeStruct((M, ncols), out_dtype),
        compiler_params=_cparams(("parallel", "arbitrary")),
        name="proj_mm",
    )(x, w)


def _mm_ln_kernel(a_ref, w_ref, x_ref, g_ref, b_ref, o_ref):
    for r0 in range(0, a_ref.shape[0], LN_SUB_ROWS):
        rows = slice(r0, min(r0 + LN_SUB_ROWS, a_ref.shape[0]))
        h = jnp.dot(a_ref[rows, :], w_ref[...], preferred_element_type=F32)
        z = ALPHA * x_ref[rows, :] + h
        mu = jnp.mean(z, axis=-1, keepdims=True)
        zc = z - mu
        var = jnp.mean(zc * zc, axis=-1, keepdims=True)
        o_ref[rows, :] = zc * lax.rsqrt(var + LN_EPS) * g_ref[...] + b_ref[...]


def _mm_ln(a, w, x, g, b):
    M, K = a.shape
    tm = min(M, OUT_ROWS)
    D = w.shape[1]
    return pl.pallas_call(
        _mm_ln_kernel,
        grid=(M // tm,),
        in_specs=[pl.BlockSpec((tm, K), lambda i: (i, 0)),
                  pl.BlockSpec((K, D), lambda i: (0, 0)),
                  pl.BlockSpec((tm, D), lambda i: (i, 0)),
                  pl.BlockSpec((1, D), lambda i: (0, 0)),
                  pl.BlockSpec((1, D), lambda i: (0, 0))],
        out_specs=pl.BlockSpec((tm, D), lambda i: (i, 0)),
        out_shape=jax.ShapeDtypeStruct((M, D), F32),
        compiler_params=_cparams(("parallel",)),
        name="out_proj_ln",
    )(a, w, x, g.reshape(1, D), b.reshape(1, D))


def _rope_tables(pos):
    half = ROT_DIM // 2
    inv = ROPE_THETA ** (-(jnp.arange(half, dtype=F32) * 2.0 / ROT_DIM))
    ang = pos[:, None] * inv[None, :]
    cos, sin = jnp.cos(ang), jnp.sin(ang)
    L = pos.shape[0]
    ones = jnp.ones((L, SWA_HEAD_DIM - ROT_DIM), F32)
    zeros_r = jnp.zeros((L, SWA_HEAD_DIM - ROT_DIM), F32)
    zeros_h = jnp.zeros((L, half), F32)
    c = jnp.concatenate([cos, cos, ones], axis=1)
    s1 = jnp.concatenate([-sin, zeros_h, zeros_r], axis=1)
    s2 = jnp.concatenate([zeros_h, sin, zeros_r], axis=1)
    rep = LANES // SWA_HEAD_DIM
    return jnp.tile(c, (1, rep)), jnp.tile(s1, (1, rep)), jnp.tile(s2, (1, rep))


def _rope128(x, c, s1, s2):
    return x * c + pltpu.roll(x, LANES - ROT_DIM // 2, 1) * s1 + pltpu.roll(x, ROT_DIM // 2, 1) * s2


def _rope_wide(x, c, s1, s2):
    return jnp.concatenate([_rope128(x[:, i * LANES:(i + 1) * LANES], c, s1, s2)
                            for i in range(x.shape[1] // LANES)], axis=1)


def _swa_kernel(sink_ref, q_ref, kc_ref, vc_ref, kp_ref, vp_ref, g0_ref, g1_ref, g2_ref, g3_ref, tc_ref, tp_ref,
                o_ref, ko_ref, vo_ref, *, lq, prompt):
    gate_refs = (g0_ref, g1_ref, g2_ref, g3_ref)
    ppt = SWA_GATE_TILE // LANES
    c, s1, s2 = tc_ref[0], tc_ref[1], tc_ref[2]
    k_cur = _rope_wide(kc_ref[...], c, s1, s2)
    v_cur = vc_ref[...]
    if prompt:
        k_prev = _rope_wide(kp_ref[...], tp_ref[0], tp_ref[1], tp_ref[2])
        v_prev = vp_ref[...]
        ko_ref[0] = k_cur
        vo_ref[0] = v_cur
    else:
        k_prev = kp_ref[0]
        v_prev = vp_ref[0]
        ko_ref[0] = jnp.concatenate([k_prev[lq:], k_cur], axis=0)
        vo_ref[0] = jnp.concatenate([v_prev[lq:], v_cur], axis=0)
        zpad = jnp.zeros((WINDOW - lq, SWA_KV_WIDTH), F32)
        k_cur = jnp.concatenate([k_cur, zpad], axis=0)
        v_cur = jnp.concatenate([v_cur, zpad], axis=0)
    k_all = jnp.concatenate([k_prev, k_cur], axis=0)
    v_all = jnp.concatenate([v_prev, v_cur], axis=0)
    nk = 2 * WINDOW
    nh = SWA_GROUP
    qi = lax.broadcasted_iota(jnp.int32, (lq, nk), 0)
    sj = lax.broadcasted_iota(jnp.int32, (lq, nk), 1)
    rel = qi + WINDOW - sj
    mask = jnp.logical_and(rel >= 0, rel <= WINDOW)
    if prompt:
        lo = jnp.where(pl.program_id(1) == 0, WINDOW, 0)
        mask = jnp.logical_and(mask, sj >= lo)
    m0 = lax.broadcasted_iota(jnp.int32, (lq, LANES), 1) < SWA_HEAD_DIM
    scale = SWA_HEAD_DIM ** -0.5
    hd = SWA_HEAD_DIM
    ppd = nh // 2
    for kh in range(SWA_KV_HEADS):
        kg = k_all[:, kh * hd:(kh + 1) * hd]
        vg = v_all[:, kh * hd:(kh + 1) * hd]
        kk2 = jnp.concatenate([kg, kg], axis=1).astype(BF16)
        vv2 = jnp.concatenate([vg, vg], axis=1).astype(BF16)
        for p0 in range(kh * (nh // 2), (kh + 1) * (nh // 2), ppd):
            pairs = range(p0, p0 + ppd)
            rows = []
            for p in pairs:
                q2 = _rope128(q_ref[:, p * LANES:(p + 1) * LANES], c, s1, s2) * scale
                rows += [jnp.where(m0, q2, 0.0), jnp.where(m0, 0.0, q2)]
            qs = jnp.concatenate(rows, axis=0).astype(BF16)
            s_all = _nt(qs, kk2)
            es, dens = [], []
            for i in range(2 * ppd):
                snk = sink_ref[2 * p0 + i]
                s = jnp.where(mask, s_all[i * lq:(i + 1) * lq], NEG_BIG)
                m = jnp.maximum(jnp.max(s, axis=-1, keepdims=True), snk)
                e = jnp.exp(s - m)
                dens.append(jnp.sum(e, axis=-1, keepdims=True) + jnp.exp(snk - m))
                es.append(e.astype(BF16))
            o_all = jnp.dot(jnp.concatenate(es, axis=0), vv2, preferred_element_type=F32)
            for i, p in enumerate(pairs):
                oa = o_all[2 * i * lq:(2 * i + 1) * lq] / dens[2 * i]
                ob = o_all[(2 * i + 1) * lq:(2 * i + 2) * lq] / dens[2 * i + 1]
                g2 = gate_refs[p // ppt][:, (p % ppt) * LANES:(p % ppt + 1) * LANES]
                o_ref[:, p * LANES:(p + 1) * LANES] = (jnp.where(m0, oa, ob) * _silu(g2)).astype(o_ref.dtype)


SWA_GATE_TILE = 512


def _swa_gate_specs(rows, index):
    g0 = (SWA_WIDTH + 2 * SWA_KV_WIDTH) // SWA_GATE_TILE
    return [pl.BlockSpec((rows, SWA_GATE_TILE), functools.partial(index, g0 + t))
            for t in range(SWA_WIDTH // SWA_GATE_TILE)]


def _swa_prompt(h, sink, B, S):
    nb = S // WINDOW
    pos = jnp.arange(S, dtype=F32)
    tabs = jnp.stack(_rope_tables(pos))
    kcol = SWA_WIDTH // SWA_KV_WIDTH
    prev = lambda b, n: b * nb + jnp.maximum(n - 1, 0)
    return pl.pallas_call(
        functools.partial(_swa_kernel, lq=WINDOW, prompt=True),
        grid=(B, nb),
        in_specs=[pl.BlockSpec(memory_space=pltpu.SMEM),
                  pl.BlockSpec((WINDOW, SWA_WIDTH), lambda b, n: (b * nb + n, 0)),
                  pl.BlockSpec((WINDOW, SWA_KV_WIDTH), lambda b, n: (b * nb + n, kcol)),
                  pl.BlockSpec((WINDOW, SWA_KV_WIDTH), lambda b, n: (b * nb + n, kcol + 1)),
                  pl.BlockSpec((WINDOW, SWA_KV_WIDTH), lambda b, n: (prev(b, n), kcol)),
                  pl.BlockSpec((WINDOW, SWA_KV_WIDTH), lambda b, n: (prev(b, n), kcol + 1)),
                  *_swa_gate_specs(WINDOW, lambda col, b, n: (b * nb + n, col)),
                  pl.BlockSpec((3, WINDOW, LANES), lambda b, n: (0, n, 0)),
                  pl.BlockSpec((3, WINDOW, LANES), lambda b, n: (0, jnp.maximum(n - 1, 0), 0))],
        out_specs=[pl.BlockSpec((WINDOW, SWA_WIDTH), lambda b, n: (b * nb + n, 0)),
                   pl.BlockSpec((1, WINDOW, SWA_KV_WIDTH), lambda b, n: (b, 0, 0)),
                   pl.BlockSpec((1, WINDOW, SWA_KV_WIDTH), lambda b, n: (b, 0, 0))],
        out_shape=[jax.ShapeDtypeStruct((B * S, SWA_WIDTH), BF16),
                   jax.ShapeDtypeStruct((B, WINDOW, SWA_KV_WIDTH), F32),
                   jax.ShapeDtypeStruct((B, WINDOW, SWA_KV_WIDTH), F32)],
        compiler_params=_cparams(("parallel", "arbitrary")),
        name="swa_prompt",
    )(sink, h, h, h, h, h, h, h, h, h, tabs, tabs)


def _swa_sample(h, sink, cache_k, cache_v, B, L):
    pos = PAST_LEN + jnp.arange(L, dtype=F32)
    tabs = jnp.stack(_rope_tables(pos))
    kcol = SWA_WIDTH // SWA_KV_WIDTH
    return pl.pallas_call(
        functools.partial(_swa_kernel, lq=L, prompt=False),
        grid=(B,),
        in_specs=[pl.BlockSpec(memory_space=pltpu.SMEM),
                  pl.BlockSpec((L, SWA_WIDTH), lambda b: (b, 0)),
                  pl.BlockSpec((L, SWA_KV_WIDTH), lambda b: (b, kcol)),
                  pl.BlockSpec((L, SWA_KV_WIDTH), lambda b: (b, kcol + 1)),
                  pl.BlockSpec((1, WINDOW, SWA_KV_WIDTH), lambda b: (b, 0, 0)),
                  pl.BlockSpec((1, WINDOW, SWA_KV_WIDTH), lambda b: (b, 0, 0)),
                  *_swa_gate_specs(L, lambda col, b: (b, col)),
                  pl.BlockSpec((3, L, LANES), lambda b: (0, 0, 0)),
                  pl.BlockSpec((3, L, LANES), lambda b: (0, 0, 0))],
        out_specs=[pl.BlockSpec((L, SWA_WIDTH), lambda b: (b, 0)),
                   pl.BlockSpec((1, WINDOW, SWA_KV_WIDTH), lambda b: (b, 0, 0)),
                   pl.BlockSpec((1, WINDOW, SWA_KV_WIDTH), lambda b: (b, 0, 0))],
        out_shape=[jax.ShapeDtypeStruct((B * L, SWA_WIDTH), BF16),
                   jax.ShapeDtypeStruct((B, WINDOW, SWA_KV_WIDTH), F32),
                   jax.ShapeDtypeStruct((B, WINDOW, SWA_KV_WIDTH), F32)],
        compiler_params=_cparams(("parallel",)),
        name="swa_sample",
    )(sink, h, h, h, cache_k, cache_v, h, h, h, h, tabs, tabs)


def _swa_layer(x, cache, w_in, sink, w_out, ln_g, ln_b, B, L):
    h = _mm(x, w_in, 0, 2 * SWA_WIDTH + 2 * SWA_KV_WIDTH)
    if cache is None:
        o, nk, nv = _swa_prompt(h, sink, B, L)
    else:
        ck = cache[0].reshape(B, WINDOW, SWA_KV_WIDTH)
        cv = cache[1].reshape(B, WINDOW, SWA_KV_WIDTH)
        o, nk, nv = _swa_sample(h, sink, ck, cv, B, L)
    shape = (B, WINDOW, SWA_KV_HEADS, SWA_HEAD_DIM)
    return _mm_ln(o, w_out, x, ln_g, ln_b), nk.reshape(shape), nv.reshape(shape)


def _gla_gate_kernel(x_ref, wl_ref, wa_ref, ba_ref, g_ref):
    a_low = jnp.dot(x_ref[...].astype(BF16), wl_ref[...], preferred_element_type=F32)
    z = jnp.dot(a_low.astype(BF16), wa_ref[...], preferred_element_type=F32) + ba_ref[...]
    g_ref[...] = _log_sigmoid(z) * (1.0 / GLA_GATE_TEMP)


def _gla_gate(x, w_low, w_a2, b_a):
    M = x.shape[0]
    tm = min(M, PROJ_ROWS)
    return pl.pallas_call(
        _gla_gate_kernel,
        grid=(M // tm,),
        in_specs=[pl.BlockSpec((tm, D_MODEL), lambda i: (i, 0)),
                  pl.BlockSpec((D_MODEL, LANES), lambda i: (0, 0)),
                  pl.BlockSpec((LANES, GLA_KEY_DIM), lambda i: (0, 0)),
                  pl.BlockSpec((1, GLA_KEY_DIM), lambda i: (0, 0))],
        out_specs=pl.BlockSpec((tm, GLA_KEY_DIM), lambda i: (i, 0)),
        out_shape=jax.ShapeDtypeStruct((M, GLA_KEY_DIM), F32),
        compiler_params=_cparams(("parallel",)),
        name="gla_gate",
    )(x, w_low, w_a2, b_a.reshape(1, GLA_KEY_DIM))


def _tril3(C):
    return (lax.broadcasted_iota(jnp.int32, (C, 3 * C), 1) % C
            <= lax.broadcasted_iota(jnp.int32, (C, 3 * C), 0)).astype(BF16)


def _cumsum_rows(x, tril3):
    h1 = x.astype(BF16)
    r1 = x - h1.astype(F32)
    h2 = r1.astype(BF16)
    h3 = (r1 - h2.astype(F32)).astype(BF16)
    return jnp.dot(tril3, jnp.concatenate([h1, h2, h3], axis=0), preferred_element_type=F32)


def _gla_chunk_kernel(q_ref, k_ref, v_ref, gate_ref, g_ref, ng_ref, s0_ref, o_ref, so_ref, st_ref, *, C, SB):
    c_idx = pl.program_id(1)
    H, DK, DV = GLA_HEADS, GLA_DK, GLA_DV

    @pl.when(c_idx == 0)
    def _():
        for h in range(H):
            st_ref[h] = s0_ref[0, h].T

    b = _cumsum_rows(g_ref[...], _tril3(C))
    q = q_ref[...] * (DK ** -0.5)
    k = k_ref[...]
    v16 = v_ref[...].astype(BF16)
    b_last = b[C - 1:C, :]
    e_last = jnp.exp(b_last)
    qe = (q * jnp.exp(b)).astype(BF16)
    kd = (k * jnp.exp(b_last - b)).astype(BF16)
    row = lax.broadcasted_iota(jnp.int32, (C, H * DK), 0)
    qis, kjs = [], []
    for i in range(C // SB):
        r0, r1 = i * SB, (i + 1) * SB
        bn = b[r0:r0 + 1, :]
        qis.append((q[r0:r1] * jnp.exp(b[r0:r1] - bn)).astype(BF16))
        kjs.append((k * jnp.exp(jnp.where(row < r1, bn - b, 0.0))).astype(BF16))
    causal = lax.broadcasted_iota(jnp.int32, (C, C), 1) <= lax.broadcasted_iota(jnp.int32, (C, C), 0)

    results = []
    for h in range(H):
        ks, vs = slice(h * DK, (h + 1) * DK), slice(h * DV, (h + 1) * DV)
        st = st_ref[h]
        a_parts = [_nt(qi[:, ks], kj[:, ks]) for qi, kj in zip(qis, kjs)]
        a = jnp.concatenate(a_parts, axis=0) if len(a_parts) > 1 else a_parts[0]
        a = jnp.where(causal, a, 0.0).astype(BF16)
        o = _nt(qe[:, ks], st.astype(BF16)) + jnp.dot(a, v16[:, vs], preferred_element_type=F32)
        st_new = st * e_last[:, ks] + lax.dot_general(v16[:, vs], kd[:, ks], (((0,), (0,)), ((), ())),
                                                      preferred_element_type=F32)
        on = o * lax.rsqrt(jnp.mean(o * o, axis=-1, keepdims=True) + GLA_NORM_EPS) * ng_ref[...]
        results.append((st_new, (on * _silu(gate_ref[:, vs])).astype(o_ref.dtype)))
    for h, (st_new, out) in enumerate(results):
        st_ref[h] = st_new
        o_ref[:, h * DV:(h + 1) * DV] = out

    @pl.when(c_idx == pl.num_programs(1) - 1)
    def _():
        for h in range(H):
            so_ref[0, h] = st_ref[h].T


def _gla_chunk(h, g, norm_g, state0, B, L):
    C = min(GLA_CHUNK, L)
    SB = min(GLA_SUB, C)
    nc = L // C
    row = lambda b, c: b * nc + c
    st = pl.BlockSpec((1, GLA_HEADS, GLA_DK, GLA_DV), lambda b, c: (b, 0, 0, 0))
    return pl.pallas_call(
        functools.partial(_gla_chunk_kernel, C=C, SB=SB),
        grid=(B, nc),
        in_specs=[pl.BlockSpec((C, GLA_KEY_DIM), lambda b, c: (row(b, c), 0)),
                  pl.BlockSpec((C, GLA_KEY_DIM), lambda b, c: (row(b, c), 1)),
                  pl.BlockSpec((C, GLA_VAL_DIM), lambda b, c: (row(b, c), 1)),
                  pl.BlockSpec((C, GLA_VAL_DIM), lambda b, c: (row(b, c), 2)),
                  pl.BlockSpec((C, GLA_KEY_DIM), lambda b, c: (row(b, c), 0)),
                  pl.BlockSpec((1, GLA_DV), lambda b, c: (0, 0)),
                  st],
        out_specs=[pl.BlockSpec((C, GLA_VAL_DIM), lambda b, c: (row(b, c), 0)), st],
        out_shape=[jax.ShapeDtypeStruct((B * L, GLA_VAL_DIM), BF16),
                   jax.ShapeDtypeStruct((B, GLA_HEADS, GLA_DK, GLA_DV), F32)],
        scratch_shapes=[pltpu.VMEM((GLA_HEADS, GLA_DV, GLA_DK), F32)],
        compiler_params=_cparams(("parallel", "arbitrary")),
        name="gla_chunk",
    )(h, h, h, h, g, norm_g.reshape(1, GLA_DV), state0)


def _gla_layer(x, state, w_in, w_low, w_a2, b_a, norm_g, w_out, ln_g, ln_b, B, L):
    h = _mm(x, w_in, 0, 2 * GLA_KEY_DIM + 2 * GLA_VAL_DIM)
    g = _gla_gate(x, w_low, w_a2, b_a)
    if state is None:
        state = jnp.zeros((B, GLA_HEADS, GLA_DK, GLA_DV), F32)
    o, st = _gla_chunk(h, g, norm_g, state, B, L)
    return _mm_ln(o, w_out, x, ln_g, ln_b), st


SUBLANES = 8


def _prev_rows(x, pa_ref, pb_ref, seq_len):
    if seq_len is None:
        return pa_ref[...]
    tm = x.shape[0]
    starts_seq = (pl.program_id(0) * tm) % seq_len == 0
    first = jnp.where(starts_seq, pb_ref[0], pa_ref[SUBLANES - 1:SUBLANES, :])
    rows = lax.broadcasted_iota(jnp.int32, x.shape, 0)
    return jnp.where(rows == 0, first, pltpu.roll(x, 1, 0))


def _prev_specs(x, xprev, shift, tm, seq_len, grid_rank):
    pad = (0,) * (grid_rank - 1)
    sh = shift.reshape(shift.shape[0], 1, D_MODEL)
    if xprev is not None:
        return (xprev, sh), [pl.BlockSpec((tm, D_MODEL), lambda i, *_: (i, 0)),
                             pl.BlockSpec((1, 1, D_MODEL), lambda i, *_: (0, 0, 0))], None
    assert seq_len % tm == 0
    per = tm // SUBLANES
    return (x, sh), [pl.BlockSpec((SUBLANES, D_MODEL), lambda i, *_: (jnp.maximum(i * per - 1, 0), 0)),
                     pl.BlockSpec((1, 1, D_MODEL), lambda i, *_: ((i * tm) // seq_len, 0, 0))], seq_len


def _rwkv_proj_kernel(x_ref, pa_ref, pb_ref, mu_ref, w_ref, o_ref, xm_ref, *, seq_len):
    @pl.when(pl.program_id(2) == 0)
    def _():
        x = x_ref[...]
        xm = (x + (_prev_rows(x, pa_ref, pb_ref, seq_len) - x) * mu_ref[0]).astype(BF16)
        xm_ref[...] = xm
        o_ref[0] = jnp.dot(xm, w_ref[0], preferred_element_type=F32)

    @pl.when(pl.program_id(2) != 0)
    def _():
        o_ref[0] = jnp.dot(xm_ref[...], w_ref[0], preferred_element_type=F32)


def _rwkv_proj(x, xprev, shift, seq_len, mu4, w4, tn=512):
    M = x.shape[0]
    tm = min(M, PROJ_ROWS)
    prev_ops, prev_specs, inline_len = _prev_specs(x, xprev, shift, tm, seq_len, 3)
    return pl.pallas_call(
        functools.partial(_rwkv_proj_kernel, seq_len=inline_len),
        grid=(M // tm, 4, D_MODEL // tn),
        in_specs=[pl.BlockSpec((tm, D_MODEL), lambda i, m, j: (i, 0)),
                  *prev_specs,
                  pl.BlockSpec((1, 1, D_MODEL), lambda i, m, j: (m, 0, 0)),
                  pl.BlockSpec((1, D_MODEL, tn), lambda i, m, j: (m, 0, j))],
        out_specs=pl.BlockSpec((1, tm, tn), lambda i, m, j: (m, i, j)),
        out_shape=jax.ShapeDtypeStruct((4, M, D_MODEL), F32),
        scratch_shapes=[pltpu.VMEM((tm, D_MODEL), BF16)],
        compiler_params=_cparams(("parallel", "arbitrary", "arbitrary")),
        name="rwkv_proj",
    )(x, *prev_ops, mu4, w4)


def _rwkv_lora_kernel(x_ref, pa_ref, pb_ref, mu_ref, w1_ref, w2_ref, w0_ref, a1_ref, a2_ref, a0_ref, lw_ref, a_ref,
                      *, seq_len):
    x = x_ref[...]
    xx = _prev_rows(x, pa_ref, pb_ref, seq_len) - x
    xw = (x + xx * mu_ref[0]).astype(BF16)
    xa = (x + xx * mu_ref[1]).astype(BF16)
    t = jnp.tanh(jnp.dot(xw, w1_ref[...], preferred_element_type=F32))
    wl = w0_ref[...] + jnp.dot(t.astype(BF16), w2_ref[...], preferred_element_type=F32)
    w_log = _log_sigmoid(wl) - 0.5
    lw_ref[...] = -jnp.exp(w_log)
    al = jnp.dot(xa, a1_ref[...], preferred_element_type=F32)
    az = a0_ref[...] + jnp.dot(al.astype(BF16), a2_ref[...], preferred_element_type=F32)
    a_ref[...] = 1.0 / (1.0 + jnp.exp(-az))


def _rwkv_lora(x, xprev, shift, seq_len, mu2, w1, w2, w0, a1, a2, a0):
    M = x.shape[0]
    tm = min(M, LORA_ROWS)
    R = w1.shape[1]
    full = lambda shape: pl.BlockSpec(shape, lambda i: tuple(0 for _ in shape))
    prev_ops, prev_specs, inline_len = _prev_specs(x, xprev, shift, tm, seq_len, 1)
    return pl.pallas_call(
        functools.partial(_rwkv_lora_kernel, seq_len=inline_len),
        grid=(M // tm,),
        in_specs=[pl.BlockSpec((tm, D_MODEL), lambda i: (i, 0)),
                  *prev_specs,
                  full((2, 1, D_MODEL)), full((D_MODEL, R)), full((R, D_MODEL)), full((1, D_MODEL)),
                  full((D_MODEL, R)), full((R, D_MODEL)), full((1, D_MODEL))],
        out_specs=[pl.BlockSpec((tm, D_MODEL), lambda i: (i, 0)),
                   pl.BlockSpec((tm, D_MODEL), lambda i: (i, 0))],
        out_shape=[jax.ShapeDtypeStruct((M, D_MODEL), F32), jax.ShapeDtypeStruct((M, D_MODEL), F32)],
        compiler_params=_cparams(("parallel",)),
        name="rwkv_lora",
    )(x, *prev_ops, mu2, w1, w2, w0.reshape(1, D_MODEL), a1, a2, a0.reshape(1, D_MODEL))


def _head_ones():
    r = lax.broadcasted_iota(jnp.int32, (LANES, LANES), 0) // RWKV_HEAD_DIM
    c = lax.broadcasted_iota(jnp.int32, (LANES, LANES), 1) // RWKV_HEAD_DIM
    e = (r == c).astype(BF16)
    return jnp.concatenate([e, e], axis=0)


def _segsum(x, e2):
    hi = x.astype(BF16)
    lo = (x - hi.astype(F32)).astype(BF16)
    return jnp.dot(jnp.concatenate([hi, lo], axis=1), e2, preferred_element_type=F32)


def _segsum_wide(x, e2):
    rows, n = x.shape[0], x.shape[1] // LANES
    s = _segsum(jnp.concatenate([x[:, i * LANES:(i + 1) * LANES] for i in range(n)], axis=0), e2)
    return jnp.concatenate([s[i * rows:(i + 1) * rows] for i in range(n)], axis=1)


RWKV_PAIRS = 16
RWKV_CHUNK = 64


def _rwkv_chunk_kernel(r_ref, k_ref, v_ref, gt_ref, lw_ref, a_ref, kk_ref, ka_ref, rk_ref, gng_ref, gnb_ref, s0_ref,
                       o_ref, so_ref, st_ref, *, lreal):
    C, N = RWKV_CHUNK, RWKV_HEAD_DIM
    t_idx = pl.program_id(2)
    e2 = _head_ones()
    m0 = lax.broadcasted_iota(jnp.int32, (C, LANES), 1) < N
    ti = lax.broadcasted_iota(jnp.int32, (C, 2 * C), 0)
    si2 = lax.broadcasted_iota(jnp.int32, (C, 2 * C), 1)
    si = si2 % C
    strict, incl, left = si < ti, si <= ti, si2 < C
    bd = (lax.broadcasted_iota(jnp.int32, (LANES, LANES), 0) // N
          == lax.broadcasted_iota(jnp.int32, (LANES, LANES), 1) // N)
    zs = jnp.zeros((N, N), F32)

    @pl.when(t_idx == 0)
    def _():
        for p in range(RWKV_PAIRS):
            top = jnp.concatenate([s0_ref[0, 2 * p], zs], axis=1)
            bot = jnp.concatenate([zs, s0_ref[0, 2 * p + 1]], axis=1)
            st_ref[p] = jnp.concatenate([top, bot], axis=0)

    def both(x, y):
        parts = [x] if y is None else [x, y]
        return jnp.concatenate([jnp.where(m0, z, 0.0) for z in parts] + [jnp.where(m0, 0.0, z) for z in parts],
                               axis=0).astype(BF16)

    def load(x):
        if lreal < C:
            x = jnp.concatenate([x, jnp.zeros((C - lreal, x.shape[1]), F32)], axis=0)
        return x

    inv_n = 1.0 / N
    n_iter = C.bit_length() - 1
    lanes = [slice(p * LANES, (p + 1) * LANES) for p in range(RWKV_PAIRS)]

    r, k, v = load(r_ref[0]), load(k_ref[0]), load(v_ref[0])
    lw, a = load(lw_ref[...]), load(a_ref[...])
    kk = k * kk_ref[...]
    kk = kk / jnp.maximum(jnp.sqrt(_segsum_wide(kk * kk, e2)), 1e-12)
    kp = k * (1.0 + (a - 1.0) * ka_ref[...])
    c = _cumsum_rows(lw, _tril3(C))
    e_c, e_nc = jnp.exp(c), jnp.exp(-c)
    at_w, rt_w = -kk * jnp.exp(c - lw), r * e_c
    bt_w, kt_w = (kk * a * e_nc).astype(BF16), (kp * e_nc).astype(BF16)
    bonus = _segsum_wide((r * kp * rk_ref[...])[:lreal], e2)

    def scores(p):
        ls = lanes[p]
        at, rt = at_w[:, ls], rt_w[:, ls]
        bk = jnp.concatenate([bt_w[:, ls], kt_w[:, ls]], axis=0)
        g = _nt(both(at, rt), bk)
        s_bd = st_ref[p]
        pq = _nt(jnp.concatenate([at, rt], axis=0).astype(BF16), s_bd.astype(BF16))
        return dict(bk=bk, s_bd=s_bd, g=g, pq=pq)

    def setup(p, d):
        g, pq = d.pop("g"), d.pop("pq")
        aa0, rr0 = jnp.where(strict, g[0:C], 0.0), jnp.where(incl, g[C:2 * C], 0.0)
        aa1, rr1 = jnp.where(strict, g[2 * C:3 * C], 0.0), jnp.where(incl, g[3 * C:], 0.0)
        a_ab = jnp.where(left, aa0, pltpu.roll(aa1, C, 1))
        a_ak = jnp.where(left, pltpu.roll(aa0, C, 1), aa1)
        x = pq[:C] + jnp.dot(a_ak.astype(BF16), both(v[:, lanes[p]], None), preferred_element_type=F32)
        d.update(y0=pq[C:], rr=jnp.concatenate([rr0, rr1], axis=1).astype(BF16), x=x, ac=a_ab)

    def neumann(d, it):
        ac = d["ac"]
        ac16 = ac.astype(BF16)
        rhs = both(d["x"], None)
        if it < n_iter - 1:
            a_bd = jnp.concatenate([jnp.where(left, ac, 0.0), jnp.where(left, 0.0, ac)], axis=0).astype(BF16)
            res = jnp.dot(ac16, jnp.concatenate([rhs, a_bd], axis=1), preferred_element_type=F32)
            d["x"] = d["x"] + res[:, :LANES]
            d["ac"] = res[:, LANES:]
        else:
            d["x"] = d["x"] + jnp.dot(ac16, rhs, preferred_element_type=F32)

    def finish(p, d):
        ls = lanes[p]
        u, vp = d["x"], v[:, ls]
        y = d["y0"] + jnp.dot(d["rr"], both(u, vp), preferred_element_type=F32)
        uv = jnp.concatenate([u, vp], axis=0).astype(BF16)
        ds = lax.dot_general(uv, d["bk"], (((0,), (0,)), ((), ())), preferred_element_type=F32)
        return (d["s_bd"] + jnp.where(bd, ds, 0.0)) * e_c[C - 1:C, ls], y[:lreal]

    work = [scores(p) for p in range(RWKV_PAIRS)]
    for p, d in enumerate(work):
        setup(p, d)
    for it in range(n_iter):
        for d in work:
            neumann(d, it)
    done = [finish(p, d) for p, d in enumerate(work)]
    for p, (s_new, _) in enumerate(done):
        st_ref[p] = s_new

    y = jnp.concatenate([yp for _, yp in done], axis=1)
    yc = y - _segsum_wide(y, e2) * inv_n
    yv = _segsum_wide(yc * yc, e2) * inv_n
    yn = yc * lax.rsqrt(yv + RWKV_GN_EPS) * gng_ref[...] + gnb_ref[...]
    o_ref[...] = ((yn + bonus * v[:lreal]) * _silu(gt_ref[0])).astype(o_ref.dtype)

    @pl.when(t_idx == pl.num_programs(2) - 1)
    def _():
        for p in range(RWKV_PAIRS):
            s = st_ref[p]
            so_ref[0, 2 * p] = s[:N, :N]
            so_ref[0, 2 * p + 1] = s[N:, N:]


def _rwkv_scan(proj, lw, a, k_k, k_a, r_k, gn_g, gn_b, state0, B, L):
    tc = min(L, RWKV_CHUNK)
    nt = L // tc
    W = RWKV_PAIRS * LANES
    ng = D_MODEL // W
    hpg = 2 * RWKV_PAIRS
    row = lambda b, g, t: b * nt + t
    tok = lambda m: pl.BlockSpec((1, tc, W), lambda b, g, t: (m, row(b, g, t), g))
    vec = pl.BlockSpec((tc, W), lambda b, g, t: (row(b, g, t), g))
    par = pl.BlockSpec((1, W), lambda b, g, t: (0, g))
    st = pl.BlockSpec((1, hpg, RWKV_HEAD_DIM, RWKV_HEAD_DIM), lambda b, g, t: (b, g, 0, 0))
    return pl.pallas_call(
        functools.partial(_rwkv_chunk_kernel, lreal=tc),
        grid=(B, ng, nt),
        in_specs=[tok(0), tok(1), tok(2), tok(3), vec, vec, par, par, par, par, par, st],
        out_specs=[vec, st],
        out_shape=[jax.ShapeDtypeStruct((B * L, D_MODEL), BF16),
                   jax.ShapeDtypeStruct((B, RWKV_HEADS, RWKV_HEAD_DIM, RWKV_HEAD_DIM), F32)],
        scratch_shapes=[pltpu.VMEM((RWKV_PAIRS, LANES, LANES), F32)],
        compiler_params=_cparams(("parallel", "parallel", "arbitrary")),
        name="rwkv_scan",
    )(proj, proj, proj, proj, lw, a, k_k.reshape(1, D_MODEL), k_a.reshape(1, D_MODEL), r_k.reshape(1, D_MODEL),
      gn_g.reshape(1, D_MODEL), gn_b.reshape(1, D_MODEL), state0)


def _rwkv_layer(x, shift, state, p, ln_g, ln_b, B, L):
    x3 = x.reshape(B, L, D_MODEL)
    if L % PROJ_ROWS == 0 and L % LORA_ROWS == 0:
        xprev = None
    else:
        xprev = jnp.concatenate([shift[:, None, :], x3[:, :-1]], axis=1).reshape(B * L, D_MODEL)
    proj = _rwkv_proj(x, xprev, shift, L, p["mu4"], p["w4"])
    lw, a = _rwkv_lora(x, xprev, shift, L, p["mu2"], p["w1"], p["w2"], p["w0"], p["a1"], p["a2"], p["a0"])
    o, st = _rwkv_scan(proj, lw, a, p["k_k"], p["k_a"], p["r_k"], p["gn_g"], p["gn_b"], state, B, L)
    return _mm_ln(o, p["w_out"], x, ln_g, ln_b), st, x3[:, -1]


def _pad_rank(w, axis):
    pad = [(0, 0), (0, 0)]
    pad[axis] = (0, LANES - w.shape[axis])
    return jnp.pad(w, pad).astype(BF16)


def _trunk(x3, cache, w):
    B, L, _ = x3.shape
    prompt = cache is None
    x = x3.reshape(B * L, D_MODEL)
    new_k, new_v, new_gla, new_wkv, new_shift = [], [], [], [], []
    for layer in range(DEPTH):
        kind, j = layer % 3, layer // 3
        g, b = w["ln_g"][layer], w["ln_b"][layer]
        if kind == 0:
            c = None if prompt else (cache["k"][j], cache["v"][j])
            x, nk, nv = _swa_layer(x, c, w["swa_w_in"][j], w["swa_sink"][j], w["swa_w_out"][j], g, b, B, L)
            new_k.append(nk)
            new_v.append(nv)
        elif kind == 1:
            st = None if prompt else cache["gla"][j]
            x, st = _gla_layer(x, st, w["gla_w_in"][j], w["gla_w_low"][j], w["gla_w_a2"][j], w["gla_b_a"][j],
                               w["gla_norm_g"][j], w["gla_w_out"][j], g, b, B, L)
            new_gla.append(st)
        else:
            if prompt:
                shift0 = jnp.zeros((B, D_MODEL), F32)
                s0 = jnp.zeros((B, RWKV_HEADS, RWKV_HEAD_DIM, RWKV_HEAD_DIM), F32)
            else:
                shift0, s0 = cache["shift"][j], cache["wkv"][j]
            x, st, sh = _rwkv_layer(x, shift0, s0, w["rwkv"][j], g, b, B, L)
            new_wkv.append(st)
            new_shift.append(sh)
    return (x.reshape(B, L, D_MODEL), jnp.stack(new_k), jnp.stack(new_v), jnp.stack(new_gla), jnp.stack(new_wkv),
            jnp.stack(new_shift))


def kernel(x_prompt, x_sample, cache_swa_k, cache_swa_v, state_gla, state_rwkv, state_rwkv_shift, ln_g, ln_b, swa_w_in, swa_sink, swa_w_out, gla_w_in, gla_w_a2, gla_b_a, gla_norm_g, gla_w_out, rwkv_mu, rwkv_w_rkvg, rwkv_w0, rwkv_w1, rwkv_w2, rwkv_a0, rwkv_a1, rwkv_a2, rwkv_k_k, rwkv_k_a, rwkv_r_k, rwkv_gn_g, rwkv_gn_b, rwkv_w_out):
    n_rwkv = rwkv_mu.shape[0]
    gla_main = 2 * GLA_KEY_DIM + 2 * GLA_VAL_DIM
    rwkv = []
    for j in range(n_rwkv):
        rwkv.append(dict(
            mu4=rwkv_mu[j][jnp.array([0, 2, 3, 5])].reshape(4, 1, D_MODEL),
            mu2=rwkv_mu[j][jnp.array([1, 4])].reshape(2, 1, D_MODEL),
            w4=rwkv_w_rkvg[j].astype(BF16),
            w1=_pad_rank(rwkv_w1[j], 1), w2=_pad_rank(rwkv_w2[j], 0), w0=rwkv_w0[j],
            a1=_pad_rank(rwkv_a1[j], 1), a2=_pad_rank(rwkv_a2[j], 0), a0=rwkv_a0[j],
            k_k=rwkv_k_k[j], k_a=rwkv_k_a[j], r_k=rwkv_r_k[j], gn_g=rwkv_gn_g[j], gn_b=rwkv_gn_b[j],
            w_out=rwkv_w_out[j].astype(BF16)))
    w = dict(ln_g=ln_g, ln_b=ln_b,
             swa_w_in=swa_w_in.astype(BF16), swa_sink=swa_sink, swa_w_out=swa_w_out.astype(BF16),
             gla_w_in=gla_w_in[:, :, :gla_main].astype(BF16),
             gla_w_low=[_pad_rank(gla_w_in[j][:, gla_main:], 1) for j in range(gla_w_in.shape[0])],
             gla_w_a2=[_pad_rank(gla_w_a2[j], 0) for j in range(gla_w_a2.shape[0])],
             gla_b_a=gla_b_a, gla_norm_g=gla_norm_g, gla_w_out=gla_w_out.astype(BF16), rwkv=rwkv)
    y_p, p_k, p_v, p_gla, p_wkv, p_shift = _trunk(x_prompt, None, w)
    cache = dict(k=cache_swa_k, v=cache_swa_v, gla=state_gla, wkv=state_rwkv, shift=state_rwkv_shift)
    y_s, s_k, s_v, s_gla, s_wkv, s_shift = _trunk(x_sample, cache, w)
    return (y_p, y_s, p_k, p_v, p_gla, p_wkv, p_shift, s_k, s_v, s_gla, s_wkv, s_shift)
```

```python
import functools
import math

import jax
import jax.numpy as jnp
from jax import lax
from jax.experimental import pallas as pl
from jax.experimental.pallas import tpu as pltpu

F32 = jnp.float32
BF16 = jnp.bfloat16

D_MODEL = 2048
DEPTH = 4
PAST_LEN = 16384
ALPHA = (2 * DEPTH) ** 0.25
LN_EPS = 1e-5

SWA_HEADS = 32
SWA_KV_HEADS = 4
SWA_GROUP = SWA_HEADS // SWA_KV_HEADS
SWA_HEAD_DIM = 64
SWA_WIDTH = SWA_HEADS * SWA_HEAD_DIM
SWA_KV_WIDTH = SWA_KV_HEADS * SWA_HEAD_DIM
WINDOW = 128
ROT_DIM = SWA_HEAD_DIM // 4
ROPE_THETA = 500000.0

GLA_HEADS = 4
GLA_KEY_DIM = D_MODEL // 2
GLA_VAL_DIM = D_MODEL
GLA_DK = GLA_KEY_DIM // GLA_HEADS
GLA_DV = GLA_VAL_DIM // GLA_HEADS
GLA_GATE_RANK = 16
GLA_GATE_TEMP = 16.0
GLA_CHUNK = 64
GLA_SUB = 16
GLA_NORM_EPS = 1e-5

RWKV_HEAD_DIM = 64
RWKV_HEADS = D_MODEL // RWKV_HEAD_DIM
RWKV_GN_EPS = 64e-5

LANES = 128
VMEM_LIMIT = 56 * 1024 * 1024
PROJ_ROWS = 1024
PROJ_COLS = 1536
LORA_ROWS = 512
OUT_ROWS = 512
LN_SUB_ROWS = 256
NEG_BIG = -1e30


def _cparams(sem):
    return pltpu.CompilerParams(dimension_semantics=sem, vmem_limit_bytes=VMEM_LIMIT)


def _silu(x):
    return x * (1.0 / (1.0 + jnp.exp(-x)))


def _log_sigmoid(z):
    return jnp.minimum(z, 0.0) - jnp.log(1.0 + jnp.exp(-jnp.abs(z)))


def _nt(a, b):
    return lax.dot_general(a, b, (((1,), (1,)), ((), ())), preferred_element_type=F32)


def _mm_kernel(x_ref, w_ref, o_ref):
    o_ref[...] = jnp.dot(x_ref[...].astype(BF16), w_ref[...], preferred_element_type=F32).astype(o_ref.dtype)


def _mm(x, w, col0, ncols, tn=PROJ_COLS, out_dtype=F32):
    M, K = x.shape
    tm = min(M, PROJ_ROWS)
    assert M % tm == 0 and ncols % tn == 0 and col0 % tn == 0
    cb = col0 // tn
    return pl.pallas_call(
        _mm_kernel,
        grid=(M // tm, ncols // tn),
        in_specs=[pl.BlockSpec((tm, K), lambda i, j: (i, 0)),
                  pl.BlockSpec((K, tn), lambda i, j: (0, cb + j))],
        out_specs=pl.BlockSpec((tm, tn), lambda i, j: (i, j)),
        out_shape=jax.ShapeDtypeStruct((M, ncols), out_dtype),
        compiler_params=_cparams(("parallel", "arbitrary")),
        name="proj_mm",
    )(x, w)


def _mm_ln_kernel(a_ref, w_ref, x_ref, g_ref, b_ref, o_ref):
    for r0 in range(0, a_ref.shape[0], LN_SUB_ROWS):
        rows = slice(r0, min(r0 + LN_SUB_ROWS, a_ref.shape[0]))
        h = jnp.dot(a_ref[rows, :], w_ref[...], preferred_element_type=F32)
        z = ALPHA * x_ref[rows, :] + h
        mu = jnp.mean(z, axis=-1, keepdims=True)
        zc = z - mu
        var = jnp.mean(zc * zc, axis=-1, keepdims=True)
        o_ref[rows, :] = zc * lax.rsqrt(var + LN_EPS) * g_ref[...] + b_ref[...]


def _mm_ln(a, w, x, g, b):
    M, K = a.shape
    tm = min(M, OUT_ROWS)
    D = w.shape[1]
    return pl.pallas_call(
        _mm_ln_kernel,
        grid=(M // tm,),
        in_specs=[pl.BlockSpec((tm, K), lambda i: (i, 0)),
                  pl.BlockSpec((K, D), lambda i: (0, 0)),
                  pl.BlockSpec((tm, D), lambda i: (i, 0)),
                  pl.BlockSpec((1, D), lambda i: (0, 0)),
                  pl.BlockSpec((1, D), lambda i: (0, 0))],
        out_specs=pl.BlockSpec((tm, D), lambda i: (i, 0)),
        out_shape=jax.ShapeDtypeStruct((M, D), F32),
        compiler_params=_cparams(("parallel",)),
        name="out_proj_ln",
    )(a, w, x, g.reshape(1, D), b.reshape(1, D))


def _rope_tables(pos):
    half = ROT_DIM // 2
    inv = ROPE_THETA ** (-(jnp.arange(half, dtype=F32) * 2.0 / ROT_DIM))
    ang = pos[:, None] * inv[None, :]
    cos, sin = jnp.cos(ang), jnp.sin(ang)
    L = pos.shape[0]
    ones = jnp.ones((L, SWA_HEAD_DIM - ROT_DIM), F32)
    zeros_r = jnp.zeros((L, SWA_HEAD_DIM - ROT_DIM), F32)
    zeros_h = jnp.zeros((L, half), F32)
    c = jnp.concatenate([cos, cos, ones], axis=1)
    s1 = jnp.concatenate([-sin, zeros_h, zeros_r], axis=1)
    s2 = jnp.concatenate([zeros_h, sin, zeros_r], axis=1)
    rep = LANES // SWA_HEAD_DIM
    return jnp.tile(c, (1, rep)), jnp.tile(s1, (1, rep)), jnp.tile(s2, (1, rep))


def _rope128(x, c, s1, s2):
    return x * c + pltpu.roll(x, LANES - ROT_DIM // 2, 1) * s1 + pltpu.roll(x, ROT_DIM // 2, 1) * s2


def _rope_wide(x, c, s1, s2):
    return jnp.concatenate([_rope128(x[:, i * LANES:(i + 1) * LANES], c, s1, s2)
                            for i in range(x.shape[1] // LANES)], axis=1)


def _swa_kernel(sink_ref, q_ref, kc_ref, vc_ref, kp_ref, vp_ref, g0_ref, g1_ref, g2_ref, g3_ref, tc_ref, tp_ref,
                o_ref, ko_ref, vo_ref, *, lq, prompt):
    gate_refs = (g0_ref, g1_ref, g2_ref, g3_ref)
    ppt = SWA_GATE_TILE // LANES
    c, s1, s2 = tc_ref[0], tc_ref[1], tc_ref[2]
    k_cur = _rope_wide(kc_ref[...], c, s1, s2)
    v_cur = vc_ref[...]
    if prompt:
        k_prev = _rope_wide(kp_ref[...], tp_ref[0], tp_ref[1], tp_ref[2])
        v_prev = vp_ref[...]
        ko_ref[0] = k_cur
        vo_ref[0] = v_cur
    else:
        k_prev = kp_ref[0]
        v_prev = vp_ref[0]
        ko_ref[0] = jnp.concatenate([k_prev[lq:], k_cur], axis=0)
        vo_ref[0] = jnp.concatenate([v_prev[lq:], v_cur], axis=0)
        zpad = jnp.zeros((WINDOW - lq, SWA_KV_WIDTH), F32)
        k_cur = jnp.concatenate([k_cur, zpad], axis=0)
        v_cur = jnp.concatenate([v_cur, zpad], axis=0)
    k_all = jnp.concatenate([k_prev, k_cur], axis=0)
    v_all = jnp.concatenate([v_prev, v_cur], axis=0)
    nk = 2 * WINDOW
    nh = SWA_GROUP
    qi = lax.broadcasted_iota(jnp.int32, (lq, nk), 0)
    sj = lax.broadcasted_iota(jnp.int32, (lq, nk), 1)
    rel = qi + WINDOW - sj
    mask = jnp.logical_and(rel >= 0, rel <= WINDOW)
    if prompt:
        lo = jnp.where(pl.program_id(1) == 0, WINDOW, 0)
        mask = jnp.logical_and(mask, sj >= lo)
    m0 = lax.broadcasted_iota(jnp.int32, (lq, LANES), 1) < SWA_HEAD_DIM
    scale = SWA_HEAD_DIM ** -0.5
    hd = SWA_HEAD_DIM
    ppd = nh // 2
    for kh in range(SWA_KV_HEADS):
        kg = k_all[:, kh * hd:(kh + 1) * hd]
        vg = v_all[:, kh * hd:(kh + 1) * hd]
        kk2 = jnp.concatenate([kg, kg], axis=1).astype(BF16)
        vv2 = jnp.concatenate([vg, vg], axis=1).astype(BF16)
        for p0 in range(kh * (nh // 2), (kh + 1) * (nh // 2), ppd):
            pairs = range(p0, p0 + ppd)
            rows = []
            for p in pairs:
                q2 = _rope128(q_ref[:, p * LANES:(p + 1) * LANES], c, s1, s2) * scale
                rows += [jnp.where(m0, q2, 0.0), jnp.where(m0, 0.0, q2)]
            qs = jnp.concatenate(rows, axis=0).astype(BF16)
            s_all = _nt(qs, kk2)
            es, dens = [], []
            for i in range(2 * ppd):
                snk = sink_ref[2 * p0 + i]
                s = jnp.where(mask, s_all[i * lq:(i + 1) * lq], NEG_BIG)
                m = jnp.maximum(jnp.max(s, axis=-1, keepdims=True), snk)
                e = jnp.exp(s - m)
                dens.append(jnp.sum(e, axis=-1, keepdims=True) + jnp.exp(snk - m))
                es.append(e.astype(BF16))
            o_all = jnp.dot(jnp.concatenate(es, axis=0), vv2, preferred_element_type=F32)
            for i, p in enumerate(pairs):
                oa = o_all[2 * i * lq:(2 * i + 1) * lq] / dens[2 * i]
                ob = o_all[(2 * i + 1) * lq:(2 * i + 2) * lq] / dens[2 * i + 1]
                g2 = gate_refs[p // ppt][:, (p % ppt) * LANES:(p % ppt + 1) * LANES]
                o_ref[:, p * LANES:(p + 1) * LANES] = (jnp.where(m0, oa, ob) * _silu(g2)).astype(o_ref.dtype)


SWA_GATE_TILE = 512


def _swa_gate_specs(rows, index):
    g0 = (SWA_WIDTH + 2 * SWA_KV_WIDTH) // SWA_GATE_TILE
    return [pl.BlockSpec((rows, SWA_GATE_TILE), functools.partial(index, g0 + t))
            for t in range(SWA_WIDTH // SWA_GATE_TILE)]


def _swa_prompt(h, sink, B, S):
    nb = S // WINDOW
    pos = jnp.arange(S, dtype=F32)
    tabs = jnp.stack(_rope_tables(pos))
    kcol = SWA_WIDTH // SWA_KV_WIDTH
    prev = lambda b, n: b * nb + jnp.maximum(n - 1, 0)
    return pl.pallas_call(
        functools.partial(_swa_kernel, lq=WINDOW, prompt=True),
        grid=(B, nb),
        in_specs=[pl.BlockSpec(memory_space=pltpu.SMEM),
                  pl.BlockSpec((WINDOW, SWA_WIDTH), lambda b, n: (b * nb + n, 0)),
                  pl.BlockSpec((WINDOW, SWA_KV_WIDTH), lambda b, n: (b * nb + n, kcol)),
                  pl.BlockSpec((WINDOW, SWA_KV_WIDTH), lambda b, n: (b * nb + n, kcol + 1)),
                  pl.BlockSpec((WINDOW, SWA_KV_WIDTH), lambda b, n: (prev(b, n), kcol)),
                  pl.BlockSpec((WINDOW, SWA_KV_WIDTH), lambda b, n: (prev(b, n), kcol + 1)),
                  *_swa_gate_specs(WINDOW, lambda col, b, n: (b * nb + n, col)),
                  pl.BlockSpec((3, WINDOW, LANES), lambda b, n: (0, n, 0)),
                  pl.BlockSpec((3, WINDOW, LANES), lambda b, n: (0, jnp.maximum(n - 1, 0), 0))],
        out_specs=[pl.BlockSpec((WINDOW, SWA_WIDTH), lambda b, n: (b * nb + n, 0)),
                   pl.BlockSpec((1, WINDOW, SWA_KV_WIDTH), lambda b, n: (b, 0, 0)),
                   pl.BlockSpec((1, WINDOW, SWA_KV_WIDTH), lambda b, n: (b, 0, 0))],
        out_shape=[jax.ShapeDtypeStruct((B * S, SWA_WIDTH), BF16),
                   jax.ShapeDtypeStruct((B, WINDOW, SWA_KV_WIDTH), F32),
                   jax.ShapeDtypeStruct((B, WINDOW, SWA_KV_WIDTH), F32)],
        compiler_params=_cparams(("parallel", "arbitrary")),
        name="swa_prompt",
    )(sink, h, h, h, h, h, h, h, h, h, tabs, tabs)


def _swa_sample(h, sink, cache_k, cache_v, j, B, L):
    cache = pl.BlockSpec((None, 1, WINDOW, SWA_KV_WIDTH), lambda b: (j, b, 0, 0))
    pos = PAST_LEN + jnp.arange(L, dtype=F32)
    tabs = jnp.stack(_rope_tables(pos))
    kcol = SWA_WIDTH // SWA_KV_WIDTH
    return pl.pallas_call(
        functools.partial(_swa_kernel, lq=L, prompt=False),
        grid=(B,),
        in_specs=[pl.BlockSpec(memory_space=pltpu.SMEM),
                  pl.BlockSpec((L, SWA_WIDTH), lambda b: (b, 0)),
                  pl.BlockSpec((L, SWA_KV_WIDTH), lambda b: (b, kcol)),
                  pl.BlockSpec((L, SWA_KV_WIDTH), lambda b: (b, kcol + 1)),
                  cache, cache,
                  *_swa_gate_specs(L, lambda col, b: (b, col)),
                  pl.BlockSpec((3, L, LANES), lambda b: (0, 0, 0)),
                  pl.BlockSpec((3, L, LANES), lambda b: (0, 0, 0))],
        out_specs=[pl.BlockSpec((L, SWA_WIDTH), lambda b: (b, 0)),
                   pl.BlockSpec((1, WINDOW, SWA_KV_WIDTH), lambda b: (b, 0, 0)),
                   pl.BlockSpec((1, WINDOW, SWA_KV_WIDTH), lambda b: (b, 0, 0))],
        out_shape=[jax.ShapeDtypeStruct((B * L, SWA_WIDTH), BF16),
                   jax.ShapeDtypeStruct((B, WINDOW, SWA_KV_WIDTH), F32),
                   jax.ShapeDtypeStruct((B, WINDOW, SWA_KV_WIDTH), F32)],
        compiler_params=_cparams(("parallel",)),
        name="swa_sample",
    )(sink, h, h, h, cache_k, cache_v, h, h, h, h, tabs, tabs)


def _swa_layer(x, cache, j, w_in, sink, w_out, ln_g, ln_b, B, L):
    h = _mm(x, w_in, 0, 2 * SWA_WIDTH + 2 * SWA_KV_WIDTH)
    if cache is None:
        o, nk, nv = _swa_prompt(h, sink, B, L)
    else:
        ck = cache[0].reshape(-1, B, WINDOW, SWA_KV_WIDTH)
        cv = cache[1].reshape(-1, B, WINDOW, SWA_KV_WIDTH)
        o, nk, nv = _swa_sample(h, sink, ck, cv, j, B, L)
    shape = (B, WINDOW, SWA_KV_HEADS, SWA_HEAD_DIM)
    return _mm_ln(o, w_out, x, ln_g, ln_b), nk.reshape(shape), nv.reshape(shape)


def _gla_gate_kernel(x_ref, wl_ref, wa_ref, ba_ref, g_ref):
    a_low = jnp.dot(x_ref[...].astype(BF16), wl_ref[...], preferred_element_type=F32)
    z = jnp.dot(a_low.astype(BF16), wa_ref[...], preferred_element_type=F32) + ba_ref[...]
    g_ref[...] = _log_sigmoid(z) * (1.0 / GLA_GATE_TEMP)


def _gla_gate(x, w_low, w_a2, b_a):
    M = x.shape[0]
    tm = min(M, PROJ_ROWS)
    return pl.pallas_call(
        _gla_gate_kernel,
        grid=(M // tm,),
        in_specs=[pl.BlockSpec((tm, D_MODEL), lambda i: (i, 0)),
                  pl.BlockSpec((D_MODEL, LANES), lambda i: (0, 0)),
                  pl.BlockSpec((LANES, GLA_KEY_DIM), lambda i: (0, 0)),
                  pl.BlockSpec((1, GLA_KEY_DIM), lambda i: (0, 0))],
        out_specs=pl.BlockSpec((tm, GLA_KEY_DIM), lambda i: (i, 0)),
        out_shape=jax.ShapeDtypeStruct((M, GLA_KEY_DIM), F32),
        compiler_params=_cparams(("parallel",)),
        name="gla_gate",
    )(x, w_low, w_a2, b_a.reshape(1, GLA_KEY_DIM))


def _tril3(C):
    return (lax.broadcasted_iota(jnp.int32, (C, 3 * C), 1) % C
            <= lax.broadcasted_iota(jnp.int32, (C, 3 * C), 0)).astype(BF16)


def _cumsum_rows(x, tril3):
    h1 = x.astype(BF16)
    r1 = x - h1.astype(F32)
    h2 = r1.astype(BF16)
    h3 = (r1 - h2.astype(F32)).astype(BF16)
    return jnp.dot(tril3, jnp.concatenate([h1, h2, h3], axis=0), preferred_element_type=F32)


def _gla_chunk_kernel(q_ref, k_ref, v_ref, gate_ref, g_ref, ng_ref, s0_ref, o_ref, so_ref, st_ref, *, C, SB):
    c_idx = pl.program_id(1)
    H, DK, DV = GLA_HEADS, GLA_DK, GLA_DV

    @pl.when(c_idx == 0)
    def _():
        for h in range(H):
            st_ref[h] = s0_ref[0, h].T

    b = _cumsum_rows(g_ref[...], _tril3(C))
    q = q_ref[...] * (DK ** -0.5)
    k = k_ref[...]
    v16 = v_ref[...].astype(BF16)
    b_last = b[C - 1:C, :]
    e_last = jnp.exp(b_last)
    qe = (q * jnp.exp(b)).astype(BF16)
    kd = (k * jnp.exp(b_last - b)).astype(BF16)
    row = lax.broadcasted_iota(jnp.int32, (C, H * DK), 0)
    qis, kjs = [], []
    for i in range(C // SB):
        r0, r1 = i * SB, (i + 1) * SB
        bn = b[r0:r0 + 1, :]
        qis.append((q[r0:r1] * jnp.exp(b[r0:r1] - bn)).astype(BF16))
        kjs.append((k * jnp.exp(jnp.where(row < r1, bn - b, 0.0))).astype(BF16))
    causal = lax.broadcasted_iota(jnp.int32, (C, C), 1) <= lax.broadcasted_iota(jnp.int32, (C, C), 0)

    results = []
    for h in range(H):
        ks, vs = slice(h * DK, (h + 1) * DK), slice(h * DV, (h + 1) * DV)
        st = st_ref[h]
        a_parts = [_nt(qi[:, ks], kj[:, ks]) for qi, kj in zip(qis, kjs)]
        a = jnp.concatenate(a_parts, axis=0) if len(a_parts) > 1 else a_parts[0]
        a = jnp.where(causal, a, 0.0).astype(BF16)
        o = _nt(qe[:, ks], st.astype(BF16)) + jnp.dot(a, v16[:, vs], preferred_element_type=F32)
        st_new = st * e_last[:, ks] + lax.dot_general(v16[:, vs], kd[:, ks], (((0,), (0,)), ((), ())),
                                                      preferred_element_type=F32)
        on = o * lax.rsqrt(jnp.mean(o * o, axis=-1, keepdims=True) + GLA_NORM_EPS) * ng_ref[...]
        results.append((st_new, (on * _silu(gate_ref[:, vs])).astype(o_ref.dtype)))
    for h, (st_new, out) in enumerate(results):
        st_ref[h] = st_new
        o_ref[:, h * DV:(h + 1) * DV] = out

    @pl.when(c_idx == pl.num_programs(1) - 1)
    def _():
        for h in range(H):
            so_ref[0, h] = st_ref[h].T


def _gla_chunk(h, g, norm_g, states, j, B, L):
    C = min(GLA_CHUNK, L)
    SB = min(GLA_SUB, C)
    nc = L // C
    row = lambda b, c: b * nc + c
    st0 = pl.BlockSpec((None, 1, GLA_HEADS, GLA_DK, GLA_DV), lambda b, c: (j, b, 0, 0, 0))
    st = pl.BlockSpec((1, GLA_HEADS, GLA_DK, GLA_DV), lambda b, c: (b, 0, 0, 0))
    return pl.pallas_call(
        functools.partial(_gla_chunk_kernel, C=C, SB=SB),
        grid=(B, nc),
        in_specs=[pl.BlockSpec((C, GLA_KEY_DIM), lambda b, c: (row(b, c), 0)),
                  pl.BlockSpec((C, GLA_KEY_DIM), lambda b, c: (row(b, c), 1)),
                  pl.BlockSpec((C, GLA_VAL_DIM), lambda b, c: (row(b, c), 1)),
                  pl.BlockSpec((C, GLA_VAL_DIM), lambda b, c: (row(b, c), 2)),
                  pl.BlockSpec((C, GLA_KEY_DIM), lambda b, c: (row(b, c), 0)),
                  pl.BlockSpec((1, GLA_DV), lambda b, c: (0, 0)),
                  st0],
        out_specs=[pl.BlockSpec((C, GLA_VAL_DIM), lambda b, c: (row(b, c), 0)), st],
        out_shape=[jax.ShapeDtypeStruct((B * L, GLA_VAL_DIM), BF16),
                   jax.ShapeDtypeStruct((B, GLA_HEADS, GLA_DK, GLA_DV), F32)],
        scratch_shapes=[pltpu.VMEM((GLA_HEADS, GLA_DV, GLA_DK), F32)],
        compiler_params=_cparams(("parallel", "arbitrary")),
        name="gla_chunk",
    )(h, h, h, h, g, norm_g.reshape(1, GLA_DV), states)


def _gla_layer(x, states, j, w_in, w_low, w_a2, b_a, norm_g, w_out, ln_g, ln_b, B, L):
    h = _mm(x, w_in, 0, 2 * GLA_KEY_DIM + 2 * GLA_VAL_DIM)
    g = _gla_gate(x, w_low, w_a2, b_a)
    if states is None:
        states, j = jnp.zeros((1, B, GLA_HEADS, GLA_DK, GLA_DV), F32), 0
    o, st = _gla_chunk(h, g, norm_g, states, j, B, L)
    return _mm_ln(o, w_out, x, ln_g, ln_b), st


SUBLANES = 8


def _prev_rows(x, pa_ref, pb_ref, seq_len):
    if seq_len is None:
        return pa_ref[...]
    tm = x.shape[0]
    starts_seq = (pl.program_id(0) * tm) % seq_len == 0
    first = jnp.where(starts_seq, pb_ref[0], pa_ref[SUBLANES - 1:SUBLANES, :])
    rows = lax.broadcasted_iota(jnp.int32, x.shape, 0)
    return jnp.where(rows == 0, first, pltpu.roll(x, 1, 0))


def _prev_specs(x, xprev, shift, tm, seq_len, grid_rank):
    pad = (0,) * (grid_rank - 1)
    sh = shift.reshape(shift.shape[0], 1, D_MODEL)
    if xprev is not None:
        return (xprev, sh), [pl.BlockSpec((tm, D_MODEL), lambda i, *_: (i, 0)),
                             pl.BlockSpec((1, 1, D_MODEL), lambda i, *_: (0, 0, 0))], None
    assert seq_len % tm == 0
    per = tm // SUBLANES
    return (x, sh), [pl.BlockSpec((SUBLANES, D_MODEL), lambda i, *_: (jnp.maximum(i * per - 1, 0), 0)),
                     pl.BlockSpec((1, 1, D_MODEL), lambda i, *_: ((i * tm) // seq_len, 0, 0))], seq_len


def _rwkv_proj_kernel(x_ref, pa_ref, pb_ref, mu_ref, w_ref, o_ref, xm_ref, *, seq_len):
    @pl.when(pl.program_id(2) == 0)
    def _():
        x = x_ref[...]
        xm = (x + (_prev_rows(x, pa_ref, pb_ref, seq_len) - x) * mu_ref[0]).astype(BF16)
        xm_ref[...] = xm
        o_ref[0] = jnp.dot(xm, w_ref[0], preferred_element_type=F32)

    @pl.when(pl.program_id(2) != 0)
    def _():
        o_ref[0] = jnp.dot(xm_ref[...], w_ref[0], preferred_element_type=F32)


def _rwkv_proj(x, xprev, shift, seq_len, mu4, w4, tn=1024):
    M = x.shape[0]
    tm = min(M, PROJ_ROWS)
    prev_ops, prev_specs, inline_len = _prev_specs(x, xprev, shift, tm, seq_len, 3)
    return pl.pallas_call(
        functools.partial(_rwkv_proj_kernel, seq_len=inline_len),
        grid=(M // tm, 4, D_MODEL // tn),
        in_specs=[pl.BlockSpec((tm, D_MODEL), lambda i, m, j: (i, 0)),
                  *prev_specs,
                  pl.BlockSpec((1, 1, D_MODEL), lambda i, m, j: (m, 0, 0)),
                  pl.BlockSpec((1, D_MODEL, tn), lambda i, m, j: (m, 0, j))],
        out_specs=pl.BlockSpec((1, tm, tn), lambda i, m, j: (m, i, j)),
        out_shape=jax.ShapeDtypeStruct((4, M, D_MODEL), F32),
        scratch_shapes=[pltpu.VMEM((tm, D_MODEL), BF16)],
        compiler_params=_cparams(("parallel", "arbitrary", "arbitrary")),
        name="rwkv_proj",
    )(x, *prev_ops, mu4, w4)


def _rwkv_lora_kernel(x_ref, pa_ref, pb_ref, mu_ref, w1_ref, w2_ref, w0_ref, a1_ref, a2_ref, a0_ref, lw_ref, a_ref,
                      *, seq_len):
    x = x_ref[...]
    xx = _prev_rows(x, pa_ref, pb_ref, seq_len) - x
    xw = (x + xx * mu_ref[0]).astype(BF16)
    xa = (x + xx * mu_ref[1]).astype(BF16)
    t = jnp.tanh(jnp.dot(xw, w1_ref[...], preferred_element_type=F32))
    wl = w0_ref[...] + jnp.dot(t.astype(BF16), w2_ref[...], preferred_element_type=F32)
    w_log = _log_sigmoid(wl) - 0.5
    lw_ref[...] = -jnp.exp(w_log)
    al = jnp.dot(xa, a1_ref[...], preferred_element_type=F32)
    az = a0_ref[...] + jnp.dot(al.astype(BF16), a2_ref[...], preferred_element_type=F32)
    a_ref[...] = 1.0 / (1.0 + jnp.exp(-az))


def _rwkv_lora(x, xprev, shift, seq_len, mu2, w1, w2, w0, a1, a2, a0):
    M = x.shape[0]
    tm = min(M, LORA_ROWS)
    R = w1.shape[1]
    full = lambda shape: pl.BlockSpec(shape, lambda i: tuple(0 for _ in shape))
    prev_ops, prev_specs, inline_len = _prev_specs(x, xprev, shift, tm, seq_len, 1)
    return pl.pallas_call(
        functools.partial(_rwkv_lora_kernel, seq_len=inline_len),
        grid=(M // tm,),
        in_specs=[pl.BlockSpec((tm, D_MODEL), lambda i: (i, 0)),
                  *prev_specs,
                  full((2, 1, D_MODEL)), full((D_MODEL, R)), full((R, D_MODEL)), full((1, D_MODEL)),
                  full((D_MODEL, R)), full((R, D_MODEL)), full((1, D_MODEL))],
        out_specs=[pl.BlockSpec((tm, D_MODEL), lambda i: (i, 0)),
                   pl.BlockSpec((tm, D_MODEL), lambda i: (i, 0))],
        out_shape=[jax.ShapeDtypeStruct((M, D_MODEL), F32), jax.ShapeDtypeStruct((M, D_MODEL), F32)],
        compiler_params=_cparams(("parallel",)),
        name="rwkv_lora",
    )(x, *prev_ops, mu2, w1, w2, w0.reshape(1, D_MODEL), a1, a2, a0.reshape(1, D_MODEL))


def _head_ones():
    r = lax.broadcasted_iota(jnp.int32, (LANES, LANES), 0) // RWKV_HEAD_DIM
    c = lax.broadcasted_iota(jnp.int32, (LANES, LANES), 1) // RWKV_HEAD_DIM
    e = (r == c).astype(BF16)
    return jnp.concatenate([e, e], axis=0)


def _segsum(x, e2):
    hi = x.astype(BF16)
    lo = (x - hi.astype(F32)).astype(BF16)
    return jnp.dot(jnp.concatenate([hi, lo], axis=1), e2, preferred_element_type=F32)


def _segsum_wide(x, e2):
    rows, n = x.shape[0], x.shape[1] // LANES
    s = _segsum(jnp.concatenate([x[:, i * LANES:(i + 1) * LANES] for i in range(n)], axis=0), e2)
    return jnp.concatenate([s[i * rows:(i + 1) * rows] for i in range(n)], axis=1)


RWKV_PAIRS = 16
RWKV_CHUNK = 64


def _rwkv_chunk_kernel(r_ref, k_ref, v_ref, gt_ref, lw_ref, a_ref, kk_ref, ka_ref, rk_ref, gng_ref, gnb_ref, s0_ref,
                       o_ref, so_ref, st_ref, *, lreal):
    C, N = RWKV_CHUNK, RWKV_HEAD_DIM
    t_idx = pl.program_id(2)
    e2 = _head_ones()
    m0 = lax.broadcasted_iota(jnp.int32, (C, LANES), 1) < N
    ti = lax.broadcasted_iota(jnp.int32, (C, 2 * C), 0)
    si2 = lax.broadcasted_iota(jnp.int32, (C, 2 * C), 1)
    si = si2 % C
    strict, incl, left = si < ti, si <= ti, si2 < C
    bd = (lax.broadcasted_iota(jnp.int32, (LANES, LANES), 0) // N
          == lax.broadcasted_iota(jnp.int32, (LANES, LANES), 1) // N)
    zs = jnp.zeros((N, N), F32)

    @pl.when(t_idx == 0)
    def _():
        for p in range(RWKV_PAIRS):
            top = jnp.concatenate([s0_ref[0, 2 * p], zs], axis=1)
            bot = jnp.concatenate([zs, s0_ref[0, 2 * p + 1]], axis=1)
            st_ref[p] = jnp.concatenate([top, bot], axis=0)

    def both(x, y):
        parts = [x] if y is None else [x, y]
        return jnp.concatenate([jnp.where(m0, z, 0.0) for z in parts] + [jnp.where(m0, 0.0, z) for z in parts],
                               axis=0).astype(BF16)

    def load(x):
        if lreal < C:
            x = jnp.concatenate([x, jnp.zeros((C - lreal, x.shape[1]), F32)], axis=0)
        return x

    inv_n = 1.0 / N
    n_iter = C.bit_length() - 1
    lanes = [slice(p * LANES, (p + 1) * LANES) for p in range(RWKV_PAIRS)]

    r, k, v = load(r_ref[0]), load(k_ref[0]), load(v_ref[0])
    lw, a = load(lw_ref[...]), load(a_ref[...])
    kk = k * kk_ref[...]
    kk = kk / jnp.maximum(jnp.sqrt(_segsum_wide(kk * kk, e2)), 1e-12)
    kp = k * (1.0 + (a - 1.0) * ka_ref[...])
    c = _cumsum_rows(lw, _tril3(C))
    e_c, e_nc = jnp.exp(c), jnp.exp(-c)
    at_w, rt_w = -kk * jnp.exp(c - lw), r * e_c
    bt_w, kt_w = (kk * a * e_nc).astype(BF16), (kp * e_nc).astype(BF16)
    bonus = _segsum_wide((r * kp * rk_ref[...])[:lreal], e2)

    def scores(p):
        ls = lanes[p]
        at, rt = at_w[:, ls], rt_w[:, ls]
        bk = jnp.concatenate([bt_w[:, ls], kt_w[:, ls]], axis=0)
        g = _nt(both(at, rt), bk)
        s_bd = st_ref[p]
        pq = _nt(jnp.concatenate([at, rt], axis=0).astype(BF16), s_bd.astype(BF16))
        return dict(bk=bk, s_bd=s_bd, g=g, pq=pq)

    def setup(p, d):
        g, pq = d.pop("g"), d.pop("pq")
        aa0, rr0 = jnp.where(strict, g[0:C], 0.0), jnp.where(incl, g[C:2 * C], 0.0)
        aa1, rr1 = jnp.where(strict, g[2 * C:3 * C], 0.0), jnp.where(incl, g[3 * C:], 0.0)
        a_ab = jnp.where(left, aa0, pltpu.roll(aa1, C, 1))
        a_ak = jnp.where(left, pltpu.roll(aa0, C, 1), aa1)
        x = pq[:C] + jnp.dot(a_ak.astype(BF16), both(v[:, lanes[p]], None), preferred_element_type=F32)
        d.update(y0=pq[C:], rr=jnp.concatenate([rr0, rr1], axis=1).astype(BF16), x=x, ac=a_ab)

    def neumann(d, it):
        ac = d["ac"]
        ac16 = ac.astype(BF16)
        rhs = both(d["x"], None)
        if it < n_iter - 1:
            a_bd = jnp.concatenate([jnp.where(left, ac, 0.0), jnp.where(left, 0.0, ac)], axis=0).astype(BF16)
            res = jnp.dot(ac16, jnp.concatenate([rhs, a_bd], axis=1), preferred_element_type=F32)
            d["x"] = d["x"] + res[:, :LANES]
            d["ac"] = res[:, LANES:]
        else:
            d["x"] = d["x"] + jnp.dot(ac16, rhs, preferred_element_type=F32)

    def finish(p, d):
        ls = lanes[p]
        u, vp = d["x"], v[:, ls]
        y = d["y0"] + jnp.dot(d["rr"], both(u, vp), preferred_element_type=F32)
        uv = jnp.concatenate([u, vp], axis=0).astype(BF16)
        ds = lax.dot_general(uv, d["bk"], (((0,), (0,)), ((), ())), preferred_element_type=F32)
        return (d["s_bd"] + jnp.where(bd, ds, 0.0)) * e_c[C - 1:C, ls], y[:lreal]

    work = [scores(p) for p in range(RWKV_PAIRS)]
    for p, d in enumerate(work):
        setup(p, d)
    for it in range(n_iter):
        for d in work:
            neumann(d, it)
    done = [finish(p, d) for p, d in enumerate(work)]
    for p, (s_new, _) in enumerate(done):
        st_ref[p] = s_new

    y = jnp.concatenate([yp for _, yp in done], axis=1)
    yc = y - _segsum_wide(y, e2) * inv_n
    yv = _segsum_wide(yc * yc, e2) * inv_n
    yn = yc * lax.rsqrt(yv + RWKV_GN_EPS) * gng_ref[...] + gnb_ref[...]
    o_ref[...] = ((yn + bonus * v[:lreal]) * _silu(gt_ref[0])).astype(o_ref.dtype)

    @pl.when(t_idx == pl.num_programs(2) - 1)
    def _():
        for p in range(RWKV_PAIRS):
            s = st_ref[p]
            so_ref[0, 2 * p] = s[:N, :N]
            so_ref[0, 2 * p + 1] = s[N:, N:]


def _rwkv_scan(proj, lw, a, k_k, k_a, r_k, gn_g, gn_b, states, j, B, L):
    tc = min(L, RWKV_CHUNK)
    nt = L // tc
    W = RWKV_PAIRS * LANES
    ng = D_MODEL // W
    hpg = 2 * RWKV_PAIRS
    row = lambda b, g, t: b * nt + t
    tok = lambda m: pl.BlockSpec((1, tc, W), lambda b, g, t: (m, row(b, g, t), g))
    vec = pl.BlockSpec((tc, W), lambda b, g, t: (row(b, g, t), g))
    par = pl.BlockSpec((1, W), lambda b, g, t: (0, g))
    st = pl.BlockSpec((1, hpg, RWKV_HEAD_DIM, RWKV_HEAD_DIM), lambda b, g, t: (b, g, 0, 0))
    st0 = pl.BlockSpec((None, 1, hpg, RWKV_HEAD_DIM, RWKV_HEAD_DIM), lambda b, g, t: (j, b, g, 0, 0))
    return pl.pallas_call(
        functools.partial(_rwkv_chunk_kernel, lreal=tc),
        grid=(B, ng, nt),
        in_specs=[tok(0), tok(1), tok(2), tok(3), vec, vec, par, par, par, par, par, st0],
        out_specs=[vec, st],
        out_shape=[jax.ShapeDtypeStruct((B * L, D_MODEL), BF16),
                   jax.ShapeDtypeStruct((B, RWKV_HEADS, RWKV_HEAD_DIM, RWKV_HEAD_DIM), F32)],
        scratch_shapes=[pltpu.VMEM((RWKV_PAIRS, LANES, LANES), F32)],
        compiler_params=_cparams(("parallel", "parallel", "arbitrary")),
        name="rwkv_scan",
    )(proj, proj, proj, proj, lw, a, k_k.reshape(1, D_MODEL), k_a.reshape(1, D_MODEL), r_k.reshape(1, D_MODEL),
      gn_g.reshape(1, D_MODEL), gn_b.reshape(1, D_MODEL), states)


def _rwkv_layer(x, shift, states, j, p, ln_g, ln_b, B, L):
    x3 = x.reshape(B, L, D_MODEL)
    if L % PROJ_ROWS == 0 and L % LORA_ROWS == 0:
        xprev = None
    else:
        xprev = jnp.concatenate([shift[:, None, :], x3[:, :-1]], axis=1).reshape(B * L, D_MODEL)
    proj = _rwkv_proj(x, xprev, shift, L, p["mu4"], p["w4"])
    lw, a = _rwkv_lora(x, xprev, shift, L, p["mu2"], p["w1"], p["w2"], p["w0"], p["a1"], p["a2"], p["a0"])
    o, st = _rwkv_scan(proj, lw, a, p["k_k"], p["k_a"], p["r_k"], p["gn_g"], p["gn_b"], states, j, B, L)
    return _mm_ln(o, p["w_out"], x, ln_g, ln_b), st, x3[:, -1]


def _pad_rank(w, axis):
    pad = [(0, 0), (0, 0)]
    pad[axis] = (0, LANES - w.shape[axis])
    return jnp.pad(w, pad).astype(BF16)


def _trunk(x3, cache, w):
    B, L, _ = x3.shape
    prompt = cache is None
    x = x3.reshape(B * L, D_MODEL)
    new_k, new_v, new_gla, new_wkv, new_shift = [], [], [], [], []
    for layer in range(DEPTH):
        kind, j = layer % 3, layer // 3
        g, b = w["ln_g"][layer], w["ln_b"][layer]
        if kind == 0:
            c = None if prompt else (cache["k"], cache["v"])
            x, nk, nv = _swa_layer(x, c, j, w["swa_w_in"][j], w["swa_sink"][j], w["swa_w_out"][j], g, b, B, L)
            new_k.append(nk)
            new_v.append(nv)
        elif kind == 1:
            st = None if prompt else cache["gla"]
            x, st = _gla_layer(x, st, j, w["gla_w_in"][j], w["gla_w_low"][j], w["gla_w_a2"][j], w["gla_b_a"][j],
                               w["gla_norm_g"][j], w["gla_w_out"][j], g, b, B, L)
            new_gla.append(st)
        else:
            if prompt:
                shift0 = jnp.zeros((B, D_MODEL), F32)
                s0, js = jnp.zeros((1, B, RWKV_HEADS, RWKV_HEAD_DIM, RWKV_HEAD_DIM), F32), 0
            else:
                shift0, s0, js = cache["shift"][j], cache["wkv"], j
            x, st, sh = _rwkv_layer(x, shift0, s0, js, w["rwkv"][j], g, b, B, L)
            new_wkv.append(st)
            new_shift.append(sh)
    return (x.reshape(B, L, D_MODEL), jnp.stack(new_k), jnp.stack(new_v), jnp.stack(new_gla), jnp.stack(new_wkv),
            jnp.stack(new_shift))


def kernel(x_prompt, x_sample, cache_swa_k, cache_swa_v, state_gla, state_rwkv, state_rwkv_shift, ln_g, ln_b, swa_w_in, swa_sink, swa_w_out, gla_w_in, gla_w_a2, gla_b_a, gla_norm_g, gla_w_out, rwkv_mu, rwkv_w_rkvg, rwkv_w0, rwkv_w1, rwkv_w2, rwkv_a0, rwkv_a1, rwkv_a2, rwkv_k_k, rwkv_k_a, rwkv_r_k, rwkv_gn_g, rwkv_gn_b, rwkv_w_out):
    n_rwkv = rwkv_mu.shape[0]
    gla_main = 2 * GLA_KEY_DIM + 2 * GLA_VAL_DIM
    rwkv = []
    for j in range(n_rwkv):
        rwkv.append(dict(
            mu4=rwkv_mu[j][jnp.array([0, 2, 3, 5])].reshape(4, 1, D_MODEL),
            mu2=rwkv_mu[j][jnp.array([1, 4])].reshape(2, 1, D_MODEL),
            w4=rwkv_w_rkvg[j].astype(BF16),
            w1=_pad_rank(rwkv_w1[j], 1), w2=_pad_rank(rwkv_w2[j], 0), w0=rwkv_w0[j],
            a1=_pad_rank(rwkv_a1[j], 1), a2=_pad_rank(rwkv_a2[j], 0), a0=rwkv_a0[j],
            k_k=rwkv_k_k[j], k_a=rwkv_k_a[j], r_k=rwkv_r_k[j], gn_g=rwkv_gn_g[j], gn_b=rwkv_gn_b[j],
            w_out=rwkv_w_out[j].astype(BF16)))
    w = dict(ln_g=ln_g, ln_b=ln_b,
             swa_w_in=swa_w_in.astype(BF16), swa_sink=swa_sink, swa_w_out=swa_w_out.astype(BF16),
             gla_w_in=gla_w_in[:, :, :gla_main].astype(BF16),
             gla_w_low=[_pad_rank(gla_w_in[j][:, gla_main:], 1) for j in range(gla_w_in.shape[0])],
             gla_w_a2=[_pad_rank(gla_w_a2[j], 0) for j in range(gla_w_a2.shape[0])],
             gla_b_a=gla_b_a, gla_norm_g=gla_norm_g, gla_w_out=gla_w_out.astype(BF16), rwkv=rwkv)
    y_p, p_k, p_v, p_gla, p_wkv, p_shift = _trunk(x_prompt, None, w)
    cache = dict(k=cache_swa_k, v=cache_swa_v, gla=state_gla, wkv=state_rwkv, shift=state_rwkv_shift)
    y_s, s_k, s_v, s_gla, s_wkv, s_shift = _trunk(x_sample, cache, w)
    return (y_p, y_s, p_k, p_v, p_gla, p_wkv, p_shift, s_k, s_v, s_gla, s_wkv, s_shift)
```

```python
import functools
import math

import jax
import jax.numpy as jnp
from jax import lax
from jax.experimental import pallas as pl
from jax.experimental.pallas import tpu as pltpu

F32 = jnp.float32
BF16 = jnp.bfloat16

D_MODEL = 2048
DEPTH = 4
PAST_LEN = 16384
ALPHA = (2 * DEPTH) ** 0.25
LN_EPS = 1e-5

SWA_HEADS = 32
SWA_KV_HEADS = 4
SWA_GROUP = SWA_HEADS // SWA_KV_HEADS
SWA_HEAD_DIM = 64
SWA_WIDTH = SWA_HEADS * SWA_HEAD_DIM
SWA_KV_WIDTH = SWA_KV_HEADS * SWA_HEAD_DIM
WINDOW = 128
ROT_DIM = SWA_HEAD_DIM // 4
ROPE_THETA = 500000.0

GLA_HEADS = 4
GLA_KEY_DIM = D_MODEL // 2
GLA_VAL_DIM = D_MODEL
GLA_DK = GLA_KEY_DIM // GLA_HEADS
GLA_DV = GLA_VAL_DIM // GLA_HEADS
GLA_GATE_RANK = 16
GLA_GATE_TEMP = 16.0
GLA_CHUNK = 64
GLA_SUB = 16
GLA_NORM_EPS = 1e-5

RWKV_HEAD_DIM = 64
RWKV_HEADS = D_MODEL // RWKV_HEAD_DIM
RWKV_GN_EPS = 64e-5

LANES = 128
VMEM_LIMIT = 56 * 1024 * 1024
PROJ_ROWS = 1024
PROJ_COLS = 1536
LORA_ROWS = 512
OUT_ROWS = 512
LN_SUB_ROWS = 256
NEG_BIG = -1e30


def _cparams(sem):
    return pltpu.CompilerParams(dimension_semantics=sem, vmem_limit_bytes=VMEM_LIMIT)


def _silu(x):
    return x * (1.0 / (1.0 + jnp.exp(-x)))


def _log_sigmoid(z):
    return jnp.minimum(z, 0.0) - jnp.log(1.0 + jnp.exp(-jnp.abs(z)))


def _nt(a, b):
    return lax.dot_general(a, b, (((1,), (1,)), ((), ())), preferred_element_type=F32)


def _mm_kernel(x_ref, w_ref, *refs, plans):
    o_ref = refs[-1]

    def run(plan):
        acc = jnp.dot(x_ref[...].astype(BF16), w_ref[...], preferred_element_type=F32)
        if all(op is None for op in plan):
            o_ref[...] = acc
            return
        for c, op in enumerate(plan):
            chunk = acc[:, c * LANES:(c + 1) * LANES]
            if op == "silu":
                chunk = _silu(chunk)
            elif op in ("rope", "rope_q"):
                chunk = _rope128(chunk, refs[0][0], refs[0][1], refs[0][2])
                if op == "rope_q":
                    chunk = chunk * SWA_HEAD_DIM ** -0.5
            o_ref[:, c * LANES:(c + 1) * LANES] = chunk

    if len(set(plans)) == 1:
        run(plans[0])
    else:
        for t, plan in enumerate(plans):
            pl.when(pl.program_id(1) == t)(functools.partial(run, plan))


def _mm(x, w, ncols, tn=PROJ_COLS, rope_q_cols=0, rope_cols=0, silu_from=None, rope_tabs=None):
    M, K = x.shape
    tm = min(M, PROJ_ROWS)
    assert M % tm == 0 and ncols % tn == 0

    def op_of(col):
        if col < rope_q_cols:
            return "rope_q"
        if col < rope_cols:
            return "rope"
        return "silu" if silu_from is not None and col >= silu_from else None

    plans = tuple(tuple(op_of(t * tn + c * LANES) for c in range(tn // LANES)) for t in range(ncols // tn))
    operands, specs = [x, w], [pl.BlockSpec((tm, K), lambda i, j: (i, 0)), pl.BlockSpec((K, tn), lambda i, j: (0, j))]
    if rope_cols:
        period = rope_tabs.shape[1] // tm
        operands.append(rope_tabs)
        specs.append(pl.BlockSpec((3, tm, LANES), lambda i, j: (0, i % period, 0)))
    return pl.pallas_call(
        functools.partial(_mm_kernel, plans=plans),
        grid=(M // tm, ncols // tn),
        in_specs=specs,
        out_specs=pl.BlockSpec((tm, tn), lambda i, j: (i, j)),
        out_shape=jax.ShapeDtypeStruct((M, ncols), F32),
        compiler_params=_cparams(("parallel", "arbitrary")),
        name="proj_mm",
    )(*operands)


def _mm_ln_kernel(a_ref, w_ref, x_ref, g_ref, b_ref, o_ref):
    for r0 in range(0, a_ref.shape[0], LN_SUB_ROWS):
        rows = slice(r0, min(r0 + LN_SUB_ROWS, a_ref.shape[0]))
        h = jnp.dot(a_ref[rows, :], w_ref[...], preferred_element_type=F32)
        z = ALPHA * x_ref[rows, :] + h
        mu = jnp.mean(z, axis=-1, keepdims=True)
        zc = z - mu
        var = jnp.mean(zc * zc, axis=-1, keepdims=True)
        o_ref[rows, :] = zc * lax.rsqrt(var + LN_EPS) * g_ref[...] + b_ref[...]


def _mm_ln(a, w, x, g, b):
    M, K = a.shape
    tm = min(M, OUT_ROWS)
    D = w.shape[1]
    return pl.pallas_call(
        _mm_ln_kernel,
        grid=(M // tm,),
        in_specs=[pl.BlockSpec((tm, K), lambda i: (i, 0)),
                  pl.BlockSpec((K, D), lambda i: (0, 0)),
                  pl.BlockSpec((tm, D), lambda i: (i, 0)),
                  pl.BlockSpec((1, D), lambda i: (0, 0)),
                  pl.BlockSpec((1, D), lambda i: (0, 0))],
        out_specs=pl.BlockSpec((tm, D), lambda i: (i, 0)),
        out_shape=jax.ShapeDtypeStruct((M, D), F32),
        compiler_params=_cparams(("parallel",)),
        name="out_proj_ln",
    )(a, w, x, g.reshape(1, D), b.reshape(1, D))


def _rope_tables(pos):
    half = ROT_DIM // 2
    inv = ROPE_THETA ** (-(jnp.arange(half, dtype=F32) * 2.0 / ROT_DIM))
    ang = pos[:, None] * inv[None, :]
    cos, sin = jnp.cos(ang), jnp.sin(ang)
    L = pos.shape[0]
    ones = jnp.ones((L, SWA_HEAD_DIM - ROT_DIM), F32)
    zeros_r = jnp.zeros((L, SWA_HEAD_DIM - ROT_DIM), F32)
    zeros_h = jnp.zeros((L, half), F32)
    c = jnp.concatenate([cos, cos, ones], axis=1)
    s1 = jnp.concatenate([-sin, zeros_h, zeros_r], axis=1)
    s2 = jnp.concatenate([zeros_h, sin, zeros_r], axis=1)
    rep = LANES // SWA_HEAD_DIM
    return jnp.tile(c, (1, rep)), jnp.tile(s1, (1, rep)), jnp.tile(s2, (1, rep))


def _rope128(x, c, s1, s2):
    return x * c + pltpu.roll(x, LANES - ROT_DIM // 2, 1) * s1 + pltpu.roll(x, ROT_DIM // 2, 1) * s2


def _swa_kernel(sink_ref, q_ref, kc_ref, vc_ref, kp_ref, vp_ref, g0_ref, g1_ref, g2_ref, g3_ref,
                o_ref, ko_ref, vo_ref, *, lq, prompt):
    gate_refs = (g0_ref, g1_ref, g2_ref, g3_ref)
    ppt = SWA_GATE_TILE // LANES
    k_cur = kc_ref[...]
    v_cur = vc_ref[...]
    if prompt:
        k_prev = kp_ref[...]
        v_prev = vp_ref[...]
        ko_ref[0] = k_cur
        vo_ref[0] = v_cur
    else:
        k_prev = kp_ref[0]
        v_prev = vp_ref[0]
        ko_ref[0] = jnp.concatenate([k_prev[lq:], k_cur], axis=0)
        vo_ref[0] = jnp.concatenate([v_prev[lq:], v_cur], axis=0)
        zpad = jnp.zeros((WINDOW - lq, SWA_KV_WIDTH), F32)
        k_cur = jnp.concatenate([k_cur, zpad], axis=0)
        v_cur = jnp.concatenate([v_cur, zpad], axis=0)
    k_all = jnp.concatenate([k_prev, k_cur], axis=0)
    v_all = jnp.concatenate([v_prev, v_cur], axis=0)
    nk = 2 * WINDOW
    nh = SWA_GROUP
    qi = lax.broadcasted_iota(jnp.int32, (lq, nk), 0)
    sj = lax.broadcasted_iota(jnp.int32, (lq, nk), 1)
    rel = qi + WINDOW - sj
    mask = jnp.logical_and(rel >= 0, rel <= WINDOW)
    if prompt:
        lo = jnp.where(pl.program_id(1) == 0, WINDOW, 0)
        mask = jnp.logical_and(mask, sj >= lo)
    m0 = lax.broadcasted_iota(jnp.int32, (lq, LANES), 1) < SWA_HEAD_DIM
    hd = SWA_HEAD_DIM
    ppd = nh // 2
    for kh in range(SWA_KV_HEADS):
        kg = k_all[:, kh * hd:(kh + 1) * hd]
        vg = v_all[:, kh * hd:(kh + 1) * hd]
        kk2 = jnp.concatenate([kg, kg], axis=1).astype(BF16)
        vv2 = jnp.concatenate([vg, vg], axis=1).astype(BF16)
        for p0 in range(kh * (nh // 2), (kh + 1) * (nh // 2), ppd):
            pairs = range(p0, p0 + ppd)
            rows = []
            for p in pairs:
                q2 = q_ref[:, p * LANES:(p + 1) * LANES]
                rows += [jnp.where(m0, q2, 0.0), jnp.where(m0, 0.0, q2)]
            qs = jnp.concatenate(rows, axis=0).astype(BF16)
            s_all = _nt(qs, kk2)
            es, dens = [], []
            for i in range(2 * ppd):
                snk = sink_ref[2 * p0 + i]
                s = jnp.where(mask, s_all[i * lq:(i + 1) * lq], NEG_BIG)
                m = jnp.maximum(jnp.max(s, axis=-1, keepdims=True), snk)
                e = jnp.exp(s - m)
                dens.append(jnp.sum(e, axis=-1, keepdims=True) + jnp.exp(snk - m))
                es.append(e.astype(BF16))
            o_all = jnp.dot(jnp.concatenate(es, axis=0), vv2, preferred_element_type=F32)
            for i, p in enumerate(pairs):
                oa = o_all[2 * i * lq:(2 * i + 1) * lq] / dens[2 * i]
                ob = o_all[(2 * i + 1) * lq:(2 * i + 2) * lq] / dens[2 * i + 1]
                g2 = gate_refs[p // ppt][:, (p % ppt) * LANES:(p % ppt + 1) * LANES]
                o_ref[:, p * LANES:(p + 1) * LANES] = (jnp.where(m0, oa, ob) * g2).astype(o_ref.dtype)


SWA_GATE_TILE = 512


def _swa_gate_specs(rows, index):
    g0 = (SWA_WIDTH + 2 * SWA_KV_WIDTH) // SWA_GATE_TILE
    return [pl.BlockSpec((rows, SWA_GATE_TILE), functools.partial(index, g0 + t))
            for t in range(SWA_WIDTH // SWA_GATE_TILE)]


def _swa_prompt(h, sink, B, S):
    nb = S // WINDOW
    kcol = SWA_WIDTH // SWA_KV_WIDTH
    prev = lambda b, n: b * nb + jnp.maximum(n - 1, 0)
    return pl.pallas_call(
        functools.partial(_swa_kernel, lq=WINDOW, prompt=True),
        grid=(B, nb),
        in_specs=[pl.BlockSpec(memory_space=pltpu.SMEM),
                  pl.BlockSpec((WINDOW, SWA_WIDTH), lambda b, n: (b * nb + n, 0)),
                  pl.BlockSpec((WINDOW, SWA_KV_WIDTH), lambda b, n: (b * nb + n, kcol)),
                  pl.BlockSpec((WINDOW, SWA_KV_WIDTH), lambda b, n: (b * nb + n, kcol + 1)),
                  pl.BlockSpec((WINDOW, SWA_KV_WIDTH), lambda b, n: (prev(b, n), kcol)),
                  pl.BlockSpec((WINDOW, SWA_KV_WIDTH), lambda b, n: (prev(b, n), kcol + 1)),
                  *_swa_gate_specs(WINDOW, lambda col, b, n: (b * nb + n, col))],
        out_specs=[pl.BlockSpec((WINDOW, SWA_WIDTH), lambda b, n: (b * nb + n, 0)),
                   pl.BlockSpec((1, WINDOW, SWA_KV_WIDTH), lambda b, n: (b, 0, 0)),
                   pl.BlockSpec((1, WINDOW, SWA_KV_WIDTH), lambda b, n: (b, 0, 0))],
        out_shape=[jax.ShapeDtypeStruct((B * S, SWA_WIDTH), BF16),
                   jax.ShapeDtypeStruct((B, WINDOW, SWA_KV_WIDTH), F32),
                   jax.ShapeDtypeStruct((B, WINDOW, SWA_KV_WIDTH), F32)],
        compiler_params=_cparams(("parallel", "arbitrary")),
        name="swa_prompt",
    )(sink, h, h, h, h, h, h, h, h, h)


def _swa_sample(h, sink, cache_k, cache_v, j, B, L):
    cache = pl.BlockSpec((None, 1, WINDOW, SWA_KV_WIDTH), lambda b: (j, b, 0, 0))
    kcol = SWA_WIDTH // SWA_KV_WIDTH
    return pl.pallas_call(
        functools.partial(_swa_kernel, lq=L, prompt=False),
        grid=(B,),
        in_specs=[pl.BlockSpec(memory_space=pltpu.SMEM),
                  pl.BlockSpec((L, SWA_WIDTH), lambda b: (b, 0)),
                  pl.BlockSpec((L, SWA_KV_WIDTH), lambda b: (b, kcol)),
                  pl.BlockSpec((L, SWA_KV_WIDTH), lambda b: (b, kcol + 1)),
                  cache, cache,
                  *_swa_gate_specs(L, lambda col, b: (b, col))],
        out_specs=[pl.BlockSpec((L, SWA_WIDTH), lambda b: (b, 0)),
                   pl.BlockSpec((1, WINDOW, SWA_KV_WIDTH), lambda b: (b, 0, 0)),
                   pl.BlockSpec((1, WINDOW, SWA_KV_WIDTH), lambda b: (b, 0, 0))],
        out_shape=[jax.ShapeDtypeStruct((B * L, SWA_WIDTH), BF16),
                   jax.ShapeDtypeStruct((B, WINDOW, SWA_KV_WIDTH), F32),
                   jax.ShapeDtypeStruct((B, WINDOW, SWA_KV_WIDTH), F32)],
        compiler_params=_cparams(("parallel",)),
        name="swa_sample",
    )(sink, h, h, h, cache_k, cache_v, h, h, h, h)


def _swa_layer(x, cache, j, w_in, sink, w_out, ln_g, ln_b, B, L):
    first = 0 if cache is None else PAST_LEN
    rows = max(L, min(B * L, PROJ_ROWS))
    tabs = jnp.stack(_rope_tables(first + (jnp.arange(rows) % L).astype(F32)))
    h = _mm(x, w_in, 2 * SWA_WIDTH + 2 * SWA_KV_WIDTH, rope_q_cols=SWA_WIDTH, rope_cols=SWA_WIDTH + SWA_KV_WIDTH,
            silu_from=SWA_WIDTH + 2 * SWA_KV_WIDTH, rope_tabs=tabs)
    if cache is None:
        o, nk, nv = _swa_prompt(h, sink, B, L)
    else:
        ck = cache[0].reshape(-1, B, WINDOW, SWA_KV_WIDTH)
        cv = cache[1].reshape(-1, B, WINDOW, SWA_KV_WIDTH)
        o, nk, nv = _swa_sample(h, sink, ck, cv, j, B, L)
    shape = (B, WINDOW, SWA_KV_HEADS, SWA_HEAD_DIM)
    return _mm_ln(o, w_out, x, ln_g, ln_b), nk.reshape(shape), nv.reshape(shape)


def _gla_gate_kernel(x_ref, wl_ref, wa_ref, ba_ref, g_ref):
    a_low = jnp.dot(x_ref[...].astype(BF16), wl_ref[...], preferred_element_type=F32)
    z = jnp.dot(a_low.astype(BF16), wa_ref[...], preferred_element_type=F32) + ba_ref[...]
    g_ref[...] = _log_sigmoid(z) * (1.0 / GLA_GATE_TEMP)


def _gla_gate(x, w_low, w_a2, b_a):
    M = x.shape[0]
    tm = min(M, PROJ_ROWS)
    return pl.pallas_call(
        _gla_gate_kernel,
        grid=(M // tm,),
        in_specs=[pl.BlockSpec((tm, D_MODEL), lambda i: (i, 0)),
                  pl.BlockSpec((D_MODEL, LANES), lambda i: (0, 0)),
                  pl.BlockSpec((LANES, GLA_KEY_DIM), lambda i: (0, 0)),
                  pl.BlockSpec((1, GLA_KEY_DIM), lambda i: (0, 0))],
        out_specs=pl.BlockSpec((tm, GLA_KEY_DIM), lambda i: (i, 0)),
        out_shape=jax.ShapeDtypeStruct((M, GLA_KEY_DIM), F32),
        compiler_params=_cparams(("parallel",)),
        name="gla_gate",
    )(x, w_low, w_a2, b_a.reshape(1, GLA_KEY_DIM))


def _tril3(C):
    return (lax.broadcasted_iota(jnp.int32, (C, 3 * C), 1) % C
            <= lax.broadcasted_iota(jnp.int32, (C, 3 * C), 0)).astype(BF16)


def _cumsum_rows(x, tril3):
    h1 = x.astype(BF16)
    r1 = x - h1.astype(F32)
    h2 = r1.astype(BF16)
    h3 = (r1 - h2.astype(F32)).astype(BF16)
    return jnp.dot(tril3, jnp.concatenate([h1, h2, h3], axis=0), preferred_element_type=F32)


def _gla_chunk_kernel(q_ref, k_ref, v_ref, gate_ref, g_ref, ng_ref, s0_ref, o_ref, so_ref, st_ref, *, C, SB):
    c_idx = pl.program_id(1)
    H, DK, DV = GLA_HEADS, GLA_DK, GLA_DV

    @pl.when(c_idx == 0)
    def _():
        for h in range(H):
            st_ref[h] = s0_ref[0, h].T

    b = _cumsum_rows(g_ref[...], _tril3(C))
    q = q_ref[...] * (DK ** -0.5)
    k = k_ref[...]
    v16 = v_ref[...].astype(BF16)
    b_last = b[C - 1:C, :]
    e_last = jnp.exp(b_last)
    qe = (q * jnp.exp(b)).astype(BF16)
    kd = (k * jnp.exp(b_last - b)).astype(BF16)
    row = lax.broadcasted_iota(jnp.int32, (C, H * DK), 0)
    qis, kjs = [], []
    for i in range(C // SB):
        r0, r1 = i * SB, (i + 1) * SB
        bn = b[r0:r0 + 1, :]
        qis.append((q[r0:r1] * jnp.exp(b[r0:r1] - bn)).astype(BF16))
        kjs.append((k * jnp.exp(jnp.where(row < r1, bn - b, 0.0))).astype(BF16))
    causal = lax.broadcasted_iota(jnp.int32, (C, C), 1) <= lax.broadcasted_iota(jnp.int32, (C, C), 0)

    results = []
    for h in range(H):
        ks, vs = slice(h * DK, (h + 1) * DK), slice(h * DV, (h + 1) * DV)
        st = st_ref[h]
        a_parts = [_nt(qi[:, ks], kj[:, ks]) for qi, kj in zip(qis, kjs)]
        a = jnp.concatenate(a_parts, axis=0) if len(a_parts) > 1 else a_parts[0]
        a = jnp.where(causal, a, 0.0).astype(BF16)
        o = _nt(qe[:, ks], st.astype(BF16)) + jnp.dot(a, v16[:, vs], preferred_element_type=F32)
        st_new = st * e_last[:, ks] + lax.dot_general(v16[:, vs], kd[:, ks], (((0,), (0,)), ((), ())),
                                                      preferred_element_type=F32)
        on = o * lax.rsqrt(jnp.mean(o * o, axis=-1, keepdims=True) + GLA_NORM_EPS) * ng_ref[...]
        results.append((st_new, (on * gate_ref[:, vs]).astype(o_ref.dtype)))
    for h, (st_new, out) in enumerate(results):
        st_ref[h] = st_new
        o_ref[:, h * DV:(h + 1) * DV] = out

    @pl.when(c_idx == pl.num_programs(1) - 1)
    def _():
        for h in range(H):
            so_ref[0, h] = st_ref[h].T


def _gla_chunk(h, g, norm_g, states, j, B, L):
    C = min(GLA_CHUNK, L)
    SB = min(GLA_SUB, C)
    nc = L // C
    row = lambda b, c: b * nc + c
    st0 = pl.BlockSpec((None, 1, GLA_HEADS, GLA_DK, GLA_DV), lambda b, c: (j, b, 0, 0, 0))
    st = pl.BlockSpec((1, GLA_HEADS, GLA_DK, GLA_DV), lambda b, c: (b, 0, 0, 0))
    return pl.pallas_call(
        functools.partial(_gla_chunk_kernel, C=C, SB=SB),
        grid=(B, nc),
        in_specs=[pl.BlockSpec((C, GLA_KEY_DIM), lambda b, c: (row(b, c), 0)),
                  pl.BlockSpec((C, GLA_KEY_DIM), lambda b, c: (row(b, c), 1)),
                  pl.BlockSpec((C, GLA_VAL_DIM), lambda b, c: (row(b, c), 1)),
                  pl.BlockSpec((C, GLA_VAL_DIM), lambda b, c: (row(b, c), 2)),
                  pl.BlockSpec((C, GLA_KEY_DIM), lambda b, c: (row(b, c), 0)),
                  pl.BlockSpec((1, GLA_DV), lambda b, c: (0, 0)),
                  st0],
        out_specs=[pl.BlockSpec((C, GLA_VAL_DIM), lambda b, c: (row(b, c), 0)), st],
        out_shape=[jax.ShapeDtypeStruct((B * L, GLA_VAL_DIM), BF16),
                   jax.ShapeDtypeStruct((B, GLA_HEADS, GLA_DK, GLA_DV), F32)],
        scratch_shapes=[pltpu.VMEM((GLA_HEADS, GLA_DV, GLA_DK), F32)],
        compiler_params=_cparams(("parallel", "arbitrary")),
        name="gla_chunk",
    )(h, h, h, h, g, norm_g.reshape(1, GLA_DV), states)


def _gla_layer(x, states, j, w_in, w_low, w_a2, b_a, norm_g, w_out, ln_g, ln_b, B, L):
    h = _mm(x, w_in, 2 * GLA_KEY_DIM + 2 * GLA_VAL_DIM, silu_from=2 * GLA_KEY_DIM + GLA_VAL_DIM)
    g = _gla_gate(x, w_low, w_a2, b_a)
    if states is None:
        states, j = jnp.zeros((1, B, GLA_HEADS, GLA_DK, GLA_DV), F32), 0
    o, st = _gla_chunk(h, g, norm_g, states, j, B, L)
    return _mm_ln(o, w_out, x, ln_g, ln_b), st


SUBLANES = 8
RWKV_GATE_INDEX = 3


def _prev_rows(x, pa_ref, pb_ref, seq_len):
    if seq_len is None:
        return pa_ref[...]
    tm = x.shape[0]
    starts_seq = (pl.program_id(0) * tm) % seq_len == 0
    first = jnp.where(starts_seq, pb_ref[0], pa_ref[SUBLANES - 1:SUBLANES, :])
    rows = lax.broadcasted_iota(jnp.int32, x.shape, 0)
    return jnp.where(rows == 0, first, pltpu.roll(x, 1, 0))


def _prev_specs(x, xprev, shift, tm, seq_len, grid_rank):
    pad = (0,) * (grid_rank - 1)
    sh = shift.reshape(shift.shape[0], 1, D_MODEL)
    if xprev is not None:
        return (xprev, sh), [pl.BlockSpec((tm, D_MODEL), lambda i, *_: (i, 0)),
                             pl.BlockSpec((1, 1, D_MODEL), lambda i, *_: (0, 0, 0))], None
    assert seq_len % tm == 0
    per = tm // SUBLANES
    return (x, sh), [pl.BlockSpec((SUBLANES, D_MODEL), lambda i, *_: (jnp.maximum(i * per - 1, 0), 0)),
                     pl.BlockSpec((1, 1, D_MODEL), lambda i, *_: ((i * tm) // seq_len, 0, 0))], seq_len


def _rwkv_proj_kernel(x_ref, pa_ref, pb_ref, mu_ref, w_ref, o_ref, xm_ref, *, seq_len):
    is_gate = pl.program_id(1) == RWKV_GATE_INDEX

    def emit(xm):
        acc = jnp.dot(xm, w_ref[0], preferred_element_type=F32)
        o_ref[0] = jnp.where(is_gate, _silu(acc), acc)

    @pl.when(pl.program_id(2) == 0)
    def _():
        x = x_ref[...]
        xm = (x + (_prev_rows(x, pa_ref, pb_ref, seq_len) - x) * mu_ref[0]).astype(BF16)
        xm_ref[...] = xm
        emit(xm)

    @pl.when(pl.program_id(2) != 0)
    def _():
        emit(xm_ref[...])


def _rwkv_proj(x, xprev, shift, seq_len, mu4, w4, tn=1024):
    M = x.shape[0]
    tm = min(M, PROJ_ROWS)
    prev_ops, prev_specs, inline_len = _prev_specs(x, xprev, shift, tm, seq_len, 3)
    return pl.pallas_call(
        functools.partial(_rwkv_proj_kernel, seq_len=inline_len),
        grid=(M // tm, 4, D_MODEL // tn),
        in_specs=[pl.BlockSpec((tm, D_MODEL), lambda i, m, j: (i, 0)),
                  *prev_specs,
                  pl.BlockSpec((1, 1, D_MODEL), lambda i, m, j: (m, 0, 0)),
                  pl.BlockSpec((1, D_MODEL, tn), lambda i, m, j: (m, 0, j))],
        out_specs=pl.BlockSpec((1, tm, tn), lambda i, m, j: (m, i, j)),
        out_shape=jax.ShapeDtypeStruct((4, M, D_MODEL), F32),
        scratch_shapes=[pltpu.VMEM((tm, D_MODEL), BF16)],
        compiler_params=_cparams(("parallel", "arbitrary", "arbitrary")),
        name="rwkv_proj",
    )(x, *prev_ops, mu4, w4)


def _rwkv_lora_kernel(x_ref, pa_ref, pb_ref, mu_ref, w1_ref, w2_ref, w0_ref, a1_ref, a2_ref, a0_ref, lw_ref, a_ref,
                      *, seq_len):
    x = x_ref[...]
    xx = _prev_rows(x, pa_ref, pb_ref, seq_len) - x
    xw = (x + xx * mu_ref[0]).astype(BF16)
    xa = (x + xx * mu_ref[1]).astype(BF16)
    t = jnp.tanh(jnp.dot(xw, w1_ref[...], preferred_element_type=F32))
    wl = w0_ref[...] + jnp.dot(t.astype(BF16), w2_ref[...], preferred_element_type=F32)
    w_log = _log_sigmoid(wl) - 0.5
    lw_ref[...] = -jnp.exp(w_log)
    al = jnp.dot(xa, a1_ref[...], preferred_element_type=F32)
    az = a0_ref[...] + jnp.dot(al.astype(BF16), a2_ref[...], preferred_element_type=F32)
    a_ref[...] = 1.0 / (1.0 + jnp.exp(-az))


def _rwkv_lora(x, xprev, shift, seq_len, mu2, w1, w2, w0, a1, a2, a0):
    M = x.shape[0]
    tm = min(M, LORA_ROWS)
    R = w1.shape[1]
    full = lambda shape: pl.BlockSpec(shape, lambda i: tuple(0 for _ in shape))
    prev_ops, prev_specs, inline_len = _prev_specs(x, xprev, shift, tm, seq_len, 1)
    return pl.pallas_call(
        functools.partial(_rwkv_lora_kernel, seq_len=inline_len),
        grid=(M // tm,),
        in_specs=[pl.BlockSpec((tm, D_MODEL), lambda i: (i, 0)),
                  *prev_specs,
                  full((2, 1, D_MODEL)), full((D_MODEL, R)), full((R, D_MODEL)), full((1, D_MODEL)),
                  full((D_MODEL, R)), full((R, D_MODEL)), full((1, D_MODEL))],
        out_specs=[pl.BlockSpec((tm, D_MODEL), lambda i: (i, 0)),
                   pl.BlockSpec((tm, D_MODEL), lambda i: (i, 0))],
        out_shape=[jax.ShapeDtypeStruct((M, D_MODEL), F32), jax.ShapeDtypeStruct((M, D_MODEL), F32)],
        compiler_params=_cparams(("parallel",)),
        name="rwkv_lora",
    )(x, *prev_ops, mu2, w1, w2, w0.reshape(1, D_MODEL), a1, a2, a0.reshape(1, D_MODEL))


def _head_ones():
    r = lax.broadcasted_iota(jnp.int32, (LANES, LANES), 0) // RWKV_HEAD_DIM
    c = lax.broadcasted_iota(jnp.int32, (LANES, LANES), 1) // RWKV_HEAD_DIM
    e = (r == c).astype(BF16)
    return jnp.concatenate([e, e], axis=0)


def _segsum(x, e2):
    hi = x.astype(BF16)
    lo = (x - hi.astype(F32)).astype(BF16)
    return jnp.dot(jnp.concatenate([hi, lo], axis=1), e2, preferred_element_type=F32)


def _segsum_wide(x, e2):
    rows, n = x.shape[0], x.shape[1] // LANES
    s = _segsum(jnp.concatenate([x[:, i * LANES:(i + 1) * LANES] for i in range(n)], axis=0), e2)
    return jnp.concatenate([s[i * rows:(i + 1) * rows] for i in range(n)], axis=1)


RWKV_PAIRS = 16
RWKV_CHUNK = 64


def _rwkv_chunk_kernel(r_ref, k_ref, v_ref, gt_ref, lw_ref, a_ref, kk_ref, ka_ref, rk_ref, gng_ref, gnb_ref, s0_ref,
                       o_ref, so_ref, st_ref, *, lreal):
    C, N = RWKV_CHUNK, RWKV_HEAD_DIM
    t_idx = pl.program_id(2)
    e2 = _head_ones()
    m0 = lax.broadcasted_iota(jnp.int32, (C, LANES), 1) < N
    ti = lax.broadcasted_iota(jnp.int32, (C, 2 * C), 0)
    si2 = lax.broadcasted_iota(jnp.int32, (C, 2 * C), 1)
    si = si2 % C
    strict, incl, left = si < ti, si <= ti, si2 < C
    bd = (lax.broadcasted_iota(jnp.int32, (LANES, LANES), 0) // N
          == lax.broadcasted_iota(jnp.int32, (LANES, LANES), 1) // N)
    zs = jnp.zeros((N, N), F32)

    @pl.when(t_idx == 0)
    def _():
        for p in range(RWKV_PAIRS):
            top = jnp.concatenate([s0_ref[0, 2 * p], zs], axis=1)
            bot = jnp.concatenate([zs, s0_ref[0, 2 * p + 1]], axis=1)
            st_ref[p] = jnp.concatenate([top, bot], axis=0)

    def both(x, y):
        parts = [x] if y is None else [x, y]
        return jnp.concatenate([jnp.where(m0, z, 0.0) for z in parts] + [jnp.where(m0, 0.0, z) for z in parts],
                               axis=0).astype(BF16)

    def load(x):
        if lreal < C:
            x = jnp.concatenate([x, jnp.zeros((C - lreal, x.shape[1]), F32)], axis=0)
        return x

    inv_n = 1.0 / N
    n_iter = max(1, (lreal - 1).bit_length())
    lanes = [slice(p * LANES, (p + 1) * LANES) for p in range(RWKV_PAIRS)]

    r, k, v = load(r_ref[0]), load(k_ref[0]), load(v_ref[0])
    lw, a = load(lw_ref[...]), load(a_ref[...])
    kk = k * kk_ref[...]
    kk = kk / jnp.maximum(jnp.sqrt(_segsum_wide(kk * kk, e2)), 1e-12)
    kp = k * (1.0 + (a - 1.0) * ka_ref[...])
    c = _cumsum_rows(lw, _tril3(C))
    e_c, e_nc = jnp.exp(c), jnp.exp(-c)
    at_w, rt_w = -kk * jnp.exp(c - lw), r * e_c
    bt_w, kt_w = (kk * a * e_nc).astype(BF16), (kp * e_nc).astype(BF16)
    bonus = _segsum_wide((r * kp * rk_ref[...])[:lreal], e2)

    def scores(p):
        ls = lanes[p]
        at, rt = at_w[:, ls], rt_w[:, ls]
        bk = jnp.concatenate([bt_w[:, ls], kt_w[:, ls]], axis=0)
        g = _nt(both(at, rt), bk)
        s_bd = st_ref[p]
        pq = _nt(jnp.concatenate([at, rt], axis=0).astype(BF16), s_bd.astype(BF16))
        return dict(bk=bk, s_bd=s_bd, g=g, pq=pq)

    def setup(p, d):
        g, pq = d.pop("g"), d.pop("pq")
        aa0, rr0 = jnp.where(strict, g[0:C], 0.0), jnp.where(incl, g[C:2 * C], 0.0)
        aa1, rr1 = jnp.where(strict, g[2 * C:3 * C], 0.0), jnp.where(incl, g[3 * C:], 0.0)
        a_ab = jnp.where(left, aa0, pltpu.roll(aa1, C, 1))
        a_ak = jnp.where(left, pltpu.roll(aa0, C, 1), aa1)
        x = pq[:C] + jnp.dot(a_ak.astype(BF16), both(v[:, lanes[p]], None), preferred_element_type=F32)
        d.update(y0=pq[C:], rr=jnp.concatenate([rr0, rr1], axis=1).astype(BF16), x=x, ac=a_ab)

    def neumann(d, it):
        ac = d["ac"]
        ac16 = ac.astype(BF16)
        rhs = both(d["x"], None)
        if it < n_iter - 1:
            a_bd = jnp.concatenate([jnp.where(left, ac, 0.0), jnp.where(left, 0.0, ac)], axis=0).astype(BF16)
            res = jnp.dot(ac16, jnp.concatenate([rhs, a_bd], axis=1), preferred_element_type=F32)
            d["x"] = d["x"] + res[:, :LANES]
            d["ac"] = res[:, LANES:]
        else:
            d["x"] = d["x"] + jnp.dot(ac16, rhs, preferred_element_type=F32)

    def finish(p, d):
        ls = lanes[p]
        u, vp = d["x"], v[:, ls]
        y = d["y0"] + jnp.dot(d["rr"], both(u, vp), preferred_element_type=F32)
        uv = jnp.concatenate([u, vp], axis=0).astype(BF16)
        ds = lax.dot_general(uv, d["bk"], (((0,), (0,)), ((), ())), preferred_element_type=F32)
        return (d["s_bd"] + jnp.where(bd, ds, 0.0)) * e_c[C - 1:C, ls], y[:lreal]

    work = [scores(p) for p in range(RWKV_PAIRS)]
    for p, d in enumerate(work):
        setup(p, d)
    for it in range(n_iter):
        for d in work:
            neumann(d, it)
    done = [finish(p, d) for p, d in enumerate(work)]
    for p, (s_new, _) in enumerate(done):
        st_ref[p] = s_new

    y = jnp.concatenate([yp for _, yp in done], axis=1)
    yc = y - _segsum_wide(y, e2) * inv_n
    yv = _segsum_wide(yc * yc, e2) * inv_n
    yn = yc * lax.rsqrt(yv + RWKV_GN_EPS) * gng_ref[...] + gnb_ref[...]
    o_ref[...] = ((yn + bonus * v[:lreal]) * gt_ref[0]).astype(o_ref.dtype)

    @pl.when(t_idx == pl.num_programs(2) - 1)
    def _():
        for p in range(RWKV_PAIRS):
            s = st_ref[p]
            so_ref[0, 2 * p] = s[:N, :N]
            so_ref[0, 2 * p + 1] = s[N:, N:]


def _rwkv_scan(proj, lw, a, k_k, k_a, r_k, gn_g, gn_b, states, j, B, L):
    tc = min(L, RWKV_CHUNK)
    nt = L // tc
    W = RWKV_PAIRS * LANES
    ng = D_MODEL // W
    hpg = 2 * RWKV_PAIRS
    row = lambda b, g, t: b * nt + t
    tok = lambda m: pl.BlockSpec((1, tc, W), lambda b, g, t: (m, row(b, g, t), g))
    vec = pl.BlockSpec((tc, W), lambda b, g, t: (row(b, g, t), g))
    par = pl.BlockSpec((1, W), lambda b, g, t: (0, g))
    st = pl.BlockSpec((1, hpg, RWKV_HEAD_DIM, RWKV_HEAD_DIM), lambda b, g, t: (b, g, 0, 0))
    st0 = pl.BlockSpec((None, 1, hpg, RWKV_HEAD_DIM, RWKV_HEAD_DIM), lambda b, g, t: (j, b, g, 0, 0))
    return pl.pallas_call(
        functools.partial(_rwkv_chunk_kernel, lreal=tc),
        grid=(B, ng, nt),
        in_specs=[tok(0), tok(1), tok(2), tok(3), vec, vec, par, par, par, par, par, st0],
        out_specs=[vec, st],
        out_shape=[jax.ShapeDtypeStruct((B * L, D_MODEL), BF16),
                   jax.ShapeDtypeStruct((B, RWKV_HEADS, RWKV_HEAD_DIM, RWKV_HEAD_DIM), F32)],
        scratch_shapes=[pltpu.VMEM((RWKV_PAIRS, LANES, LANES), F32)],
        compiler_params=_cparams(("parallel", "parallel", "arbitrary")),
        name="rwkv_scan",
    )(proj, proj, proj, proj, lw, a, k_k.reshape(1, D_MODEL), k_a.reshape(1, D_MODEL), r_k.reshape(1, D_MODEL),
      gn_g.reshape(1, D_MODEL), gn_b.reshape(1, D_MODEL), states)


def _rwkv_layer(x, shift, states, j, p, ln_g, ln_b, B, L):
    x3 = x.reshape(B, L, D_MODEL)
    if L % PROJ_ROWS == 0 and L % LORA_ROWS == 0:
        xprev = None
    else:
        xprev = jnp.concatenate([shift[:, None, :], x3[:, :-1]], axis=1).reshape(B * L, D_MODEL)
    proj = _rwkv_proj(x, xprev, shift, L, p["mu4"], p["w4"])
    lw, a = _rwkv_lora(x, xprev, shift, L, p["mu2"], p["w1"], p["w2"], p["w0"], p["a1"], p["a2"], p["a0"])
    o, st = _rwkv_scan(proj, lw, a, p["k_k"], p["k_a"], p["r_k"], p["gn_g"], p["gn_b"], states, j, B, L)
    return _mm_ln(o, p["w_out"], x, ln_g, ln_b), st, x3[:, -1]


def _pad_rank(w, axis):
    pad = [(0, 0), (0, 0)]
    pad[axis] = (0, LANES - w.shape[axis])
    return jnp.pad(w, pad).astype(BF16)


def _trunk(x3, cache, w):
    B, L, _ = x3.shape
    prompt = cache is None
    x = x3.reshape(B * L, D_MODEL)
    new_k, new_v, new_gla, new_wkv, new_shift = [], [], [], [], []
    for layer in range(DEPTH):
        kind, j = layer % 3, layer // 3
        g, b = w["ln_g"][layer], w["ln_b"][layer]
        if kind == 0:
            c = None if prompt else (cache["k"], cache["v"])
            x, nk, nv = _swa_layer(x, c, j, w["swa_w_in"][j], w["swa_sink"][j], w["swa_w_out"][j], g, b, B, L)
            new_k.append(nk)
            new_v.append(nv)
        elif kind == 1:
            st = None if prompt else cache["gla"]
            x, st = _gla_layer(x, st, j, w["gla_w_in"][j], w["gla_w_low"][j], w["gla_w_a2"][j], w["gla_b_a"][j],
                               w["gla_norm_g"][j], w["gla_w_out"][j], g, b, B, L)
            new_gla.append(st)
        else:
            if prompt:
                shift0 = jnp.zeros((B, D_MODEL), F32)
                s0, js = jnp.zeros((1, B, RWKV_HEADS, RWKV_HEAD_DIM, RWKV_HEAD_DIM), F32), 0
            else:
                shift0, s0, js = cache["shift"][j], cache["wkv"], j
            x, st, sh = _rwkv_layer(x, shift0, s0, js, w["rwkv"][j], g, b, B, L)
            new_wkv.append(st)
            new_shift.append(sh)
    stack = lambda parts: parts[0][None] if len(parts) == 1 else jnp.stack(parts)
    return (x.reshape(B, L, D_MODEL), stack(new_k), stack(new_v), stack(new_gla), stack(new_wkv), stack(new_shift))


def kernel(x_prompt, x_sample, cache_swa_k, cache_swa_v, state_gla, state_rwkv, state_rwkv_shift, ln_g, ln_b, swa_w_in, swa_sink, swa_w_out, gla_w_in, gla_w_a2, gla_b_a, gla_norm_g, gla_w_out, rwkv_mu, rwkv_w_rkvg, rwkv_w0, rwkv_w1, rwkv_w2, rwkv_a0, rwkv_a1, rwkv_a2, rwkv_k_k, rwkv_k_a, rwkv_r_k, rwkv_gn_g, rwkv_gn_b, rwkv_w_out):
    n_rwkv = rwkv_mu.shape[0]
    gla_main = 2 * GLA_KEY_DIM + 2 * GLA_VAL_DIM
    rwkv = []
    for j in range(n_rwkv):
        rwkv.append(dict(
            mu4=rwkv_mu[j][jnp.array([0, 2, 3, 5])].reshape(4, 1, D_MODEL),
            mu2=rwkv_mu[j][jnp.array([1, 4])].reshape(2, 1, D_MODEL),
            w4=rwkv_w_rkvg[j].astype(BF16),
            w1=_pad_rank(rwkv_w1[j], 1), w2=_pad_rank(rwkv_w2[j], 0), w0=rwkv_w0[j],
            a1=_pad_rank(rwkv_a1[j], 1), a2=_pad_rank(rwkv_a2[j], 0), a0=rwkv_a0[j],
            k_k=rwkv_k_k[j], k_a=rwkv_k_a[j], r_k=rwkv_r_k[j], gn_g=rwkv_gn_g[j], gn_b=rwkv_gn_b[j],
            w_out=rwkv_w_out[j].astype(BF16)))
    w = dict(ln_g=ln_g, ln_b=ln_b,
             swa_w_in=swa_w_in.astype(BF16), swa_sink=swa_sink, swa_w_out=swa_w_out.astype(BF16),
             gla_w_in=gla_w_in[:, :, :gla_main].astype(BF16),
             gla_w_low=[_pad_rank(gla_w_in[j][:, gla_main:], 1) for j in range(gla_w_in.shape[0])],
             gla_w_a2=[_pad_rank(gla_w_a2[j], 0) for j in range(gla_w_a2.shape[0])],
             gla_b_a=gla_b_a, gla_norm_g=gla_norm_g, gla_w_out=gla_w_out.astype(BF16), rwkv=rwkv)
    y_p, p_k, p_v, p_gla, p_wkv, p_shift = _trunk(x_prompt, None, w)
    cache = dict(k=cache_swa_k, v=cache_swa_v, gla=state_gla, wkv=state_rwkv, shift=state_rwkv_shift)
    y_s, s_k, s_v, s_gla, s_wkv, s_shift = _trunk(x_sample, cache, w)
    return (y_p, y_s, p_k, p_v, p_gla, p_wkv, p_shift, s_k, s_v, s_gla, s_wkv, s_shift)
```

```python
import functools
import math

import jax
import jax.numpy as jnp
from jax import lax
from jax.experimental import pallas as pl
from jax.experimental.pallas import tpu as pltpu

F32 = jnp.float32
BF16 = jnp.bfloat16

D_MODEL = 2048
DEPTH = 4
PAST_LEN = 16384
ALPHA = (2 * DEPTH) ** 0.25
LN_EPS = 1e-5

SWA_HEADS = 32
SWA_KV_HEADS = 4
SWA_GROUP = SWA_HEADS // SWA_KV_HEADS
SWA_HEAD_DIM = 64
SWA_WIDTH = SWA_HEADS * SWA_HEAD_DIM
SWA_KV_WIDTH = SWA_KV_HEADS * SWA_HEAD_DIM
WINDOW = 128
ROT_DIM = SWA_HEAD_DIM // 4
ROPE_THETA = 500000.0

GLA_HEADS = 4
GLA_KEY_DIM = D_MODEL // 2
GLA_VAL_DIM = D_MODEL
GLA_DK = GLA_KEY_DIM // GLA_HEADS
GLA_DV = GLA_VAL_DIM // GLA_HEADS
GLA_GATE_RANK = 16
GLA_GATE_TEMP = 16.0
GLA_CHUNK = 64
GLA_SUB = 16
GLA_NORM_EPS = 1e-5

RWKV_HEAD_DIM = 64
RWKV_HEADS = D_MODEL // RWKV_HEAD_DIM
RWKV_GN_EPS = 64e-5

LANES = 128
VMEM_LIMIT = 56 * 1024 * 1024
PROJ_ROWS = 1024
PROJ_COLS = 1536
LORA_ROWS = 512
OUT_ROWS = 512
LN_SUB_ROWS = 256
NEG_BIG = -1e30


def _cparams(sem):
    return pltpu.CompilerParams(dimension_semantics=sem, vmem_limit_bytes=VMEM_LIMIT)


def _silu(x):
    return x * (1.0 / (1.0 + jnp.exp(-x)))


def _log_sigmoid(z):
    return jnp.minimum(z, 0.0) - jnp.log(1.0 + jnp.exp(-jnp.abs(z)))


def _nt(a, b):
    return lax.dot_general(a, b, (((1,), (1,)), ((), ())), preferred_element_type=F32)


def _mm_kernel(x_ref, w_ref, *refs, plans):
    o_ref = refs[-1]

    def run(plan):
        acc = jnp.dot(x_ref[...].astype(BF16), w_ref[...], preferred_element_type=F32)
        if all(op is None for op in plan):
            o_ref[...] = acc
            return
        for c, op in enumerate(plan):
            chunk = acc[:, c * LANES:(c + 1) * LANES]
            if op == "silu":
                chunk = _silu(chunk)
            elif op in ("rope", "rope_q"):
                chunk = _rope128(chunk, refs[0][0], refs[0][1], refs[0][2])
                if op == "rope_q":
                    chunk = chunk * SWA_HEAD_DIM ** -0.5
            o_ref[:, c * LANES:(c + 1) * LANES] = chunk

    if len(set(plans)) == 1:
        run(plans[0])
    else:
        for t, plan in enumerate(plans):
            pl.when(pl.program_id(1) == t)(functools.partial(run, plan))


def _mm(x, w, ncols, tn=PROJ_COLS, rope_q_cols=0, rope_cols=0, silu_from=None, rope_tabs=None):
    M, K = x.shape
    tm = min(M, PROJ_ROWS)
    assert M % tm == 0 and ncols % tn == 0

    def op_of(col):
        if col < rope_q_cols:
            return "rope_q"
        if col < rope_cols:
            return "rope"
        return "silu" if silu_from is not None and col >= silu_from else None

    plans = tuple(tuple(op_of(t * tn + c * LANES) for c in range(tn // LANES)) for t in range(ncols // tn))
    operands, specs = [x, w], [pl.BlockSpec((tm, K), lambda i, j: (i, 0)), pl.BlockSpec((K, tn), lambda i, j: (0, j))]
    if rope_cols:
        period = rope_tabs.shape[1] // tm
        operands.append(rope_tabs)
        specs.append(pl.BlockSpec((3, tm, LANES), lambda i, j: (0, i % period, 0)))
    return pl.pallas_call(
        functools.partial(_mm_kernel, plans=plans),
        grid=(M // tm, ncols // tn),
        in_specs=specs,
        out_specs=pl.BlockSpec((tm, tn), lambda i, j: (i, j)),
        out_shape=jax.ShapeDtypeStruct((M, ncols), F32),
        compiler_params=_cparams(("parallel", "arbitrary")),
        name="proj_mm",
    )(*operands)


def _mm_ln_kernel(a_ref, w_ref, x_ref, g_ref, b_ref, o_ref):
    for r0 in range(0, a_ref.shape[0], LN_SUB_ROWS):
        rows = slice(r0, min(r0 + LN_SUB_ROWS, a_ref.shape[0]))
        h = jnp.dot(a_ref[rows, :], w_ref[...], preferred_element_type=F32)
        z = ALPHA * x_ref[rows, :] + h
        mu = jnp.mean(z, axis=-1, keepdims=True)
        zc = z - mu
        var = jnp.mean(zc * zc, axis=-1, keepdims=True)
        o_ref[rows, :] = zc * lax.rsqrt(var + LN_EPS) * g_ref[...] + b_ref[...]


def _mm_ln(a, w, x, g, b):
    M, K = a.shape
    tm = min(M, OUT_ROWS)
    D = w.shape[1]
    return pl.pallas_call(
        _mm_ln_kernel,
        grid=(M // tm,),
        in_specs=[pl.BlockSpec((tm, K), lambda i: (i, 0)),
                  pl.BlockSpec((K, D), lambda i: (0, 0)),
                  pl.BlockSpec((tm, D), lambda i: (i, 0)),
                  pl.BlockSpec((1, D), lambda i: (0, 0)),
                  pl.BlockSpec((1, D), lambda i: (0, 0))],
        out_specs=pl.BlockSpec((tm, D), lambda i: (i, 0)),
        out_shape=jax.ShapeDtypeStruct((M, D), F32),
        compiler_params=_cparams(("parallel",)),
        name="out_proj_ln",
    )(a, w, x, g.reshape(1, D), b.reshape(1, D))


def _rope_tables(pos):
    half = ROT_DIM // 2
    inv = ROPE_THETA ** (-(jnp.arange(half, dtype=F32) * 2.0 / ROT_DIM))
    ang = pos[:, None] * inv[None, :]
    cos, sin = jnp.cos(ang), jnp.sin(ang)
    L = pos.shape[0]
    ones = jnp.ones((L, SWA_HEAD_DIM - ROT_DIM), F32)
    zeros_r = jnp.zeros((L, SWA_HEAD_DIM - ROT_DIM), F32)
    zeros_h = jnp.zeros((L, half), F32)
    c = jnp.concatenate([cos, cos, ones], axis=1)
    s1 = jnp.concatenate([-sin, zeros_h, zeros_r], axis=1)
    s2 = jnp.concatenate([zeros_h, sin, zeros_r], axis=1)
    rep = LANES // SWA_HEAD_DIM
    return jnp.tile(c, (1, rep)), jnp.tile(s1, (1, rep)), jnp.tile(s2, (1, rep))


def _rope128(x, c, s1, s2):
    return x * c + pltpu.roll(x, LANES - ROT_DIM // 2, 1) * s1 + pltpu.roll(x, ROT_DIM // 2, 1) * s2


def _swa_kernel(sink_ref, q_ref, kc_ref, vc_ref, kp_ref, vp_ref, g0_ref, g1_ref, g2_ref, g3_ref,
                o_ref, ko_ref, vo_ref, *, lq, prompt):
    gate_refs = (g0_ref, g1_ref, g2_ref, g3_ref)
    ppt = SWA_GATE_TILE // LANES
    k_cur = kc_ref[...]
    v_cur = vc_ref[...]
    if prompt:
        k_prev = kp_ref[...]
        v_prev = vp_ref[...]
        ko_ref[0] = k_cur
        vo_ref[0] = v_cur
    else:
        k_prev = kp_ref[0]
        v_prev = vp_ref[0]
        ko_ref[0] = jnp.concatenate([k_prev[lq:], k_cur], axis=0)
        vo_ref[0] = jnp.concatenate([v_prev[lq:], v_cur], axis=0)
        zpad = jnp.zeros((WINDOW - lq, SWA_KV_WIDTH), F32)
        k_cur = jnp.concatenate([k_cur, zpad], axis=0)
        v_cur = jnp.concatenate([v_cur, zpad], axis=0)
    k_all = jnp.concatenate([k_prev, k_cur], axis=0)
    v_all = jnp.concatenate([v_prev, v_cur], axis=0)
    nk = 2 * WINDOW
    nh = SWA_GROUP
    qi = lax.broadcasted_iota(jnp.int32, (lq, nk), 0)
    sj = lax.broadcasted_iota(jnp.int32, (lq, nk), 1)
    rel = qi + WINDOW - sj
    mask = jnp.logical_and(rel >= 0, rel <= WINDOW)
    if prompt:
        lo = jnp.where(pl.program_id(1) == 0, WINDOW, 0)
        mask = jnp.logical_and(mask, sj >= lo)
    m0 = lax.broadcasted_iota(jnp.int32, (lq, LANES), 1) < SWA_HEAD_DIM
    hd = SWA_HEAD_DIM
    ppd = nh // 2
    for kh in range(SWA_KV_HEADS):
        kg = k_all[:, kh * hd:(kh + 1) * hd]
        vg = v_all[:, kh * hd:(kh + 1) * hd]
        kk2 = jnp.concatenate([kg, kg], axis=1).astype(BF16)
        vv2 = jnp.concatenate([vg, vg], axis=1).astype(BF16)
        for p0 in range(kh * (nh // 2), (kh + 1) * (nh // 2), ppd):
            pairs = range(p0, p0 + ppd)
            rows = []
            for p in pairs:
                q2 = q_ref[:, p * LANES:(p + 1) * LANES]
                rows += [jnp.where(m0, q2, 0.0), jnp.where(m0, 0.0, q2)]
            qs = jnp.concatenate(rows, axis=0).astype(BF16)
            s_all = _nt(qs, kk2)
            es, dens = [], []
            for i in range(2 * ppd):
                snk = sink_ref[2 * p0 + i]
                s = jnp.where(mask, s_all[i * lq:(i + 1) * lq], NEG_BIG)
                m = jnp.maximum(jnp.max(s, axis=-1, keepdims=True), snk)
                e = jnp.exp(s - m)
                dens.append(jnp.sum(e, axis=-1, keepdims=True) + jnp.exp(snk - m))
                es.append(e.astype(BF16))
            o_all = jnp.dot(jnp.concatenate(es, axis=0), vv2, preferred_element_type=F32)
            for i, p in enumerate(pairs):
                oa = o_all[2 * i * lq:(2 * i + 1) * lq] / dens[2 * i]
                ob = o_all[(2 * i + 1) * lq:(2 * i + 2) * lq] / dens[2 * i + 1]
                g2 = gate_refs[p // ppt][:, (p % ppt) * LANES:(p % ppt + 1) * LANES]
                o_ref[:, p * LANES:(p + 1) * LANES] = (jnp.where(m0, oa, ob) * g2).astype(o_ref.dtype)


SWA_GATE_TILE = 512


def _swa_gate_specs(rows, index):
    g0 = (SWA_WIDTH + 2 * SWA_KV_WIDTH) // SWA_GATE_TILE
    return [pl.BlockSpec((rows, SWA_GATE_TILE), functools.partial(index, g0 + t))
            for t in range(SWA_WIDTH // SWA_GATE_TILE)]


def _swa_prompt(h, sink, B, S):
    nb = S // WINDOW
    kcol = SWA_WIDTH // SWA_KV_WIDTH
    prev = lambda b, n: b * nb + jnp.maximum(n - 1, 0)
    return pl.pallas_call(
        functools.partial(_swa_kernel, lq=WINDOW, prompt=True),
        grid=(B, nb),
        in_specs=[pl.BlockSpec(memory_space=pltpu.SMEM),
                  pl.BlockSpec((WINDOW, SWA_WIDTH), lambda b, n: (b * nb + n, 0)),
                  pl.BlockSpec((WINDOW, SWA_KV_WIDTH), lambda b, n: (b * nb + n, kcol)),
                  pl.BlockSpec((WINDOW, SWA_KV_WIDTH), lambda b, n: (b * nb + n, kcol + 1)),
                  pl.BlockSpec((WINDOW, SWA_KV_WIDTH), lambda b, n: (prev(b, n), kcol)),
                  pl.BlockSpec((WINDOW, SWA_KV_WIDTH), lambda b, n: (prev(b, n), kcol + 1)),
                  *_swa_gate_specs(WINDOW, lambda col, b, n: (b * nb + n, col))],
        out_specs=[pl.BlockSpec((WINDOW, SWA_WIDTH), lambda b, n: (b * nb + n, 0)),
                   pl.BlockSpec((1, WINDOW, SWA_KV_WIDTH), lambda b, n: (b, 0, 0)),
                   pl.BlockSpec((1, WINDOW, SWA_KV_WIDTH), lambda b, n: (b, 0, 0))],
        out_shape=[jax.ShapeDtypeStruct((B * S, SWA_WIDTH), BF16),
                   jax.ShapeDtypeStruct((B, WINDOW, SWA_KV_WIDTH), F32),
                   jax.ShapeDtypeStruct((B, WINDOW, SWA_KV_WIDTH), F32)],
        compiler_params=_cparams(("parallel", "arbitrary")),
        name="swa_prompt",
    )(sink, h, h, h, h, h, h, h, h, h)


def _swa_sample(h, sink, cache_k, cache_v, j, B, L):
    cache = pl.BlockSpec((None, 1, WINDOW, SWA_KV_WIDTH), lambda b: (j, b, 0, 0))
    kcol = SWA_WIDTH // SWA_KV_WIDTH
    return pl.pallas_call(
        functools.partial(_swa_kernel, lq=L, prompt=False),
        grid=(B,),
        in_specs=[pl.BlockSpec(memory_space=pltpu.SMEM),
                  pl.BlockSpec((L, SWA_WIDTH), lambda b: (b, 0)),
                  pl.BlockSpec((L, SWA_KV_WIDTH), lambda b: (b, kcol)),
                  pl.BlockSpec((L, SWA_KV_WIDTH), lambda b: (b, kcol + 1)),
                  cache, cache,
                  *_swa_gate_specs(L, lambda col, b: (b, col))],
        out_specs=[pl.BlockSpec((L, SWA_WIDTH), lambda b: (b, 0)),
                   pl.BlockSpec((1, WINDOW, SWA_KV_WIDTH), lambda b: (b, 0, 0)),
                   pl.BlockSpec((1, WINDOW, SWA_KV_WIDTH), lambda b: (b, 0, 0))],
        out_shape=[jax.ShapeDtypeStruct((B * L, SWA_WIDTH), BF16),
                   jax.ShapeDtypeStruct((B, WINDOW, SWA_KV_WIDTH), F32),
                   jax.ShapeDtypeStruct((B, WINDOW, SWA_KV_WIDTH), F32)],
        compiler_params=_cparams(("parallel",)),
        name="swa_sample",
    )(sink, h, h, h, cache_k, cache_v, h, h, h, h)


def _swa_layer(x, cache, j, w_in, sink, w_out, ln_g, ln_b, B, L):
    first = 0 if cache is None else PAST_LEN
    rows = max(L, min(B * L, PROJ_ROWS))
    tabs = jnp.stack(_rope_tables(first + (jnp.arange(rows) % L).astype(F32)))
    h = _mm(x, w_in, 2 * SWA_WIDTH + 2 * SWA_KV_WIDTH, rope_q_cols=SWA_WIDTH, rope_cols=SWA_WIDTH + SWA_KV_WIDTH,
            silu_from=SWA_WIDTH + 2 * SWA_KV_WIDTH, rope_tabs=tabs)
    if cache is None:
        o, nk, nv = _swa_prompt(h, sink, B, L)
    else:
        ck = cache[0].reshape(-1, B, WINDOW, SWA_KV_WIDTH)
        cv = cache[1].reshape(-1, B, WINDOW, SWA_KV_WIDTH)
        o, nk, nv = _swa_sample(h, sink, ck, cv, j, B, L)
    shape = (B, WINDOW, SWA_KV_HEADS, SWA_HEAD_DIM)
    return _mm_ln(o, w_out, x, ln_g, ln_b), nk.reshape(shape), nv.reshape(shape)


def _gla_gate_kernel(x_ref, wl_ref, wa_ref, ba_ref, g_ref):
    a_low = jnp.dot(x_ref[...].astype(BF16), wl_ref[...], preferred_element_type=F32)
    z = jnp.dot(a_low.astype(BF16), wa_ref[...], preferred_element_type=F32) + ba_ref[...]
    g_ref[...] = _log_sigmoid(z) * (1.0 / GLA_GATE_TEMP)


def _gla_gate(x, w_low, w_a2, b_a):
    M = x.shape[0]
    tm = min(M, PROJ_ROWS)
    return pl.pallas_call(
        _gla_gate_kernel,
        grid=(M // tm,),
        in_specs=[pl.BlockSpec((tm, D_MODEL), lambda i: (i, 0)),
                  pl.BlockSpec((D_MODEL, LANES), lambda i: (0, 0)),
                  pl.BlockSpec((LANES, GLA_KEY_DIM), lambda i: (0, 0)),
                  pl.BlockSpec((1, GLA_KEY_DIM), lambda i: (0, 0))],
        out_specs=pl.BlockSpec((tm, GLA_KEY_DIM), lambda i: (i, 0)),
        out_shape=jax.ShapeDtypeStruct((M, GLA_KEY_DIM), F32),
        compiler_params=_cparams(("parallel",)),
        name="gla_gate",
    )(x, w_low, w_a2, b_a.reshape(1, GLA_KEY_DIM))


def _tril3(C):
    return (lax.broadcasted_iota(jnp.int32, (C, 3 * C), 1) % C
            <= lax.broadcasted_iota(jnp.int32, (C, 3 * C), 0)).astype(BF16)


def _cumsum_rows(x, tril3):
    h1 = x.astype(BF16)
    r1 = x - h1.astype(F32)
    h2 = r1.astype(BF16)
    h3 = (r1 - h2.astype(F32)).astype(BF16)
    return jnp.dot(tril3, jnp.concatenate([h1, h2, h3], axis=0), preferred_element_type=F32)


def _gla_chunk_kernel(q_ref, k_ref, v_ref, gate_ref, g_ref, ng_ref, s0_ref, o_ref, so_ref, st_ref, *, C, SB):
    c_idx = pl.program_id(1)
    H, DK, DV = GLA_HEADS, GLA_DK, GLA_DV

    @pl.when(c_idx == 0)
    def _():
        for h in range(H):
            st_ref[h] = s0_ref[0, h].T

    b = _cumsum_rows(g_ref[...], _tril3(C))
    q = q_ref[...] * (DK ** -0.5)
    k = k_ref[...]
    v16 = v_ref[...].astype(BF16)
    b_last = b[C - 1:C, :]
    e_last = jnp.exp(b_last)
    qe = (q * jnp.exp(b)).astype(BF16)
    kd = (k * jnp.exp(b_last - b)).astype(BF16)
    row = lax.broadcasted_iota(jnp.int32, (C, H * DK), 0)
    qis, kjs = [], []
    for i in range(C // SB):
        r0, r1 = i * SB, (i + 1) * SB
        bn = b[r0:r0 + 1, :]
        qis.append((q[r0:r1] * jnp.exp(b[r0:r1] - bn)).astype(BF16))
        kjs.append((k * jnp.exp(jnp.where(row < r1, bn - b, 0.0))).astype(BF16))
    causal = lax.broadcasted_iota(jnp.int32, (C, C), 1) <= lax.broadcasted_iota(jnp.int32, (C, C), 0)

    results = []
    for h in range(H):
        ks, vs = slice(h * DK, (h + 1) * DK), slice(h * DV, (h + 1) * DV)
        st = st_ref[h]
        a_parts = [_nt(qi[:, ks], kj[:, ks]) for qi, kj in zip(qis, kjs)]
        a = jnp.concatenate(a_parts, axis=0) if len(a_parts) > 1 else a_parts[0]
        a = jnp.where(causal, a, 0.0).astype(BF16)
        o = _nt(qe[:, ks], st.astype(BF16)) + jnp.dot(a, v16[:, vs], preferred_element_type=F32)
        st_new = st * e_last[:, ks] + lax.dot_general(v16[:, vs], kd[:, ks], (((0,), (0,)), ((), ())),
                                                      preferred_element_type=F32)
        on = o * lax.rsqrt(jnp.mean(o * o, axis=-1, keepdims=True) + GLA_NORM_EPS) * ng_ref[...]
        results.append((st_new, (on * gate_ref[:, vs]).astype(o_ref.dtype)))
    for h, (st_new, out) in enumerate(results):
        st_ref[h] = st_new
        o_ref[:, h * DV:(h + 1) * DV] = out

    @pl.when(c_idx == pl.num_programs(1) - 1)
    def _():
        for h in range(H):
            so_ref[0, h] = st_ref[h].T


def _gla_chunk(h, g, norm_g, states, j, B, L):
    C = min(GLA_CHUNK, L)
    SB = min(GLA_SUB, C)
    nc = L // C
    row = lambda b, c: b * nc + c
    st0 = pl.BlockSpec((None, 1, GLA_HEADS, GLA_DK, GLA_DV), lambda b, c: (j, b, 0, 0, 0))
    st = pl.BlockSpec((1, GLA_HEADS, GLA_DK, GLA_DV), lambda b, c: (b, 0, 0, 0))
    return pl.pallas_call(
        functools.partial(_gla_chunk_kernel, C=C, SB=SB),
        grid=(B, nc),
        in_specs=[pl.BlockSpec((C, GLA_KEY_DIM), lambda b, c: (row(b, c), 0)),
                  pl.BlockSpec((C, GLA_KEY_DIM), lambda b, c: (row(b, c), 1)),
                  pl.BlockSpec((C, GLA_VAL_DIM), lambda b, c: (row(b, c), 1)),
                  pl.BlockSpec((C, GLA_VAL_DIM), lambda b, c: (row(b, c), 2)),
                  pl.BlockSpec((C, GLA_KEY_DIM), lambda b, c: (row(b, c), 0)),
                  pl.BlockSpec((1, GLA_DV), lambda b, c: (0, 0)),
                  st0],
        out_specs=[pl.BlockSpec((C, GLA_VAL_DIM), lambda b, c: (row(b, c), 0)), st],
        out_shape=[jax.ShapeDtypeStruct((B * L, GLA_VAL_DIM), BF16),
                   jax.ShapeDtypeStruct((B, GLA_HEADS, GLA_DK, GLA_DV), F32)],
        scratch_shapes=[pltpu.VMEM((GLA_HEADS, GLA_DV, GLA_DK), F32)],
        compiler_params=_cparams(("parallel", "arbitrary")),
        name="gla_chunk",
    )(h, h, h, h, g, norm_g.reshape(1, GLA_DV), states)


def _gla_layer(x, states, j, w_in, w_low, w_a2, b_a, norm_g, w_out, ln_g, ln_b, B, L):
    h = _mm(x, w_in, 2 * GLA_KEY_DIM + 2 * GLA_VAL_DIM, silu_from=2 * GLA_KEY_DIM + GLA_VAL_DIM)
    g = _gla_gate(x, w_low, w_a2, b_a)
    if states is None:
        states, j = jnp.zeros((1, B, GLA_HEADS, GLA_DK, GLA_DV), F32), 0
    o, st = _gla_chunk(h, g, norm_g, states, j, B, L)
    return _mm_ln(o, w_out, x, ln_g, ln_b), st


SUBLANES = 8
RWKV_GATE_INDEX = 3


def _prev_rows(x, pa_ref, pb_ref, seq_len):
    if seq_len is None:
        return pa_ref[...]
    tm = x.shape[0]
    starts_seq = (pl.program_id(0) * tm) % seq_len == 0
    first = jnp.where(starts_seq, pb_ref[0], pa_ref[SUBLANES - 1:SUBLANES, :])
    rows = lax.broadcasted_iota(jnp.int32, x.shape, 0)
    return jnp.where(rows == 0, first, pltpu.roll(x, 1, 0))


def _prev_specs(x, xprev, shift, tm, seq_len, grid_rank):
    pad = (0,) * (grid_rank - 1)
    sh = shift.reshape(shift.shape[0], 1, D_MODEL)
    if xprev is not None:
        return (xprev, sh), [pl.BlockSpec((tm, D_MODEL), lambda i, *_: (i, 0)),
                             pl.BlockSpec((1, 1, D_MODEL), lambda i, *_: (0, 0, 0))], None
    assert seq_len % tm == 0
    per = tm // SUBLANES
    return (x, sh), [pl.BlockSpec((SUBLANES, D_MODEL), lambda i, *_: (jnp.maximum(i * per - 1, 0), 0)),
                     pl.BlockSpec((1, 1, D_MODEL), lambda i, *_: ((i * tm) // seq_len, 0, 0))], seq_len


def _rwkv_proj_kernel(x_ref, pa_ref, pb_ref, mu_ref, w_ref, o_ref, xm_ref, *, seq_len):
    def step(first_col, gate):
        if first_col:
            x = x_ref[...]
            xm = (x + (_prev_rows(x, pa_ref, pb_ref, seq_len) - x) * mu_ref[0]).astype(BF16)
            xm_ref[...] = xm
        else:
            xm = xm_ref[...]
        acc = jnp.dot(xm, w_ref[0], preferred_element_type=F32)
        o_ref[0] = _silu(acc) if gate else acc

    is_first = pl.program_id(2) == 0
    is_gate = pl.program_id(1) == RWKV_GATE_INDEX
    for first_col in (True, False):
        for gate in (True, False):
            here = jnp.logical_and(is_first == first_col, is_gate == gate)
            pl.when(here)(functools.partial(step, first_col, gate))


def _rwkv_proj(x, xprev, shift, seq_len, mu4, w4, tn=1024):
    M = x.shape[0]
    tm = min(M, PROJ_ROWS)
    prev_ops, prev_specs, inline_len = _prev_specs(x, xprev, shift, tm, seq_len, 3)
    return pl.pallas_call(
        functools.partial(_rwkv_proj_kernel, seq_len=inline_len),
        grid=(M // tm, 4, D_MODEL // tn),
        in_specs=[pl.BlockSpec((tm, D_MODEL), lambda i, m, j: (i, 0)),
                  *prev_specs,
                  pl.BlockSpec((1, 1, D_MODEL), lambda i, m, j: (m, 0, 0)),
                  pl.BlockSpec((1, D_MODEL, tn), lambda i, m, j: (m, 0, j))],
        out_specs=pl.BlockSpec((1, tm, tn), lambda i, m, j: (m, i, j)),
        out_shape=jax.ShapeDtypeStruct((4, M, D_MODEL), F32),
        scratch_shapes=[pltpu.VMEM((tm, D_MODEL), BF16)],
        compiler_params=_cparams(("parallel", "arbitrary", "arbitrary")),
        name="rwkv_proj",
    )(x, *prev_ops, mu4, w4)


def _rwkv_lora_kernel(x_ref, pa_ref, pb_ref, mu_ref, w1_ref, w2_ref, w0_ref, a1_ref, a2_ref, a0_ref, lw_ref, a_ref,
                      *, seq_len):
    x = x_ref[...]
    xx = _prev_rows(x, pa_ref, pb_ref, seq_len) - x
    xw = (x + xx * mu_ref[0]).astype(BF16)
    xa = (x + xx * mu_ref[1]).astype(BF16)
    t = jnp.tanh(jnp.dot(xw, w1_ref[...], preferred_element_type=F32))
    wl = w0_ref[...] + jnp.dot(t.astype(BF16), w2_ref[...], preferred_element_type=F32)
    lw_ref[...] = (-math.exp(-0.5)) / (1.0 + jnp.exp(-wl))
    al = jnp.dot(xa, a1_ref[...], preferred_element_type=F32)
    az = a0_ref[...] + jnp.dot(al.astype(BF16), a2_ref[...], preferred_element_type=F32)
    a_ref[...] = 1.0 / (1.0 + jnp.exp(-az))


def _rwkv_lora(x, xprev, shift, seq_len, mu2, w1, w2, w0, a1, a2, a0):
    M = x.shape[0]
    tm = min(M, LORA_ROWS)
    R = w1.shape[1]
    full = lambda shape: pl.BlockSpec(shape, lambda i: tuple(0 for _ in shape))
    prev_ops, prev_specs, inline_len = _prev_specs(x, xprev, shift, tm, seq_len, 1)
    return pl.pallas_call(
        functools.partial(_rwkv_lora_kernel, seq_len=inline_len),
        grid=(M // tm,),
        in_specs=[pl.BlockSpec((tm, D_MODEL), lambda i: (i, 0)),
                  *prev_specs,
                  full((2, 1, D_MODEL)), full((D_MODEL, R)), full((R, D_MODEL)), full((1, D_MODEL)),
                  full((D_MODEL, R)), full((R, D_MODEL)), full((1, D_MODEL))],
        out_specs=[pl.BlockSpec((tm, D_MODEL), lambda i: (i, 0)),
                   pl.BlockSpec((tm, D_MODEL), lambda i: (i, 0))],
        out_shape=[jax.ShapeDtypeStruct((M, D_MODEL), F32), jax.ShapeDtypeStruct((M, D_MODEL), F32)],
        compiler_params=_cparams(("parallel",)),
        name="rwkv_lora",
    )(x, *prev_ops, mu2, w1, w2, w0.reshape(1, D_MODEL), a1, a2, a0.reshape(1, D_MODEL))


def _head_ones():
    r = lax.broadcasted_iota(jnp.int32, (LANES, LANES), 0) // RWKV_HEAD_DIM
    c = lax.broadcasted_iota(jnp.int32, (LANES, LANES), 1) // RWKV_HEAD_DIM
    e = (r == c).astype(BF16)
    return jnp.concatenate([e, e], axis=0)


def _segsum(x, e2):
    hi = x.astype(BF16)
    lo = (x - hi.astype(F32)).astype(BF16)
    return jnp.dot(jnp.concatenate([hi, lo], axis=1), e2, preferred_element_type=F32)


def _segsum_wide(x, e2):
    rows, n = x.shape[0], x.shape[1] // LANES
    s = _segsum(jnp.concatenate([x[:, i * LANES:(i + 1) * LANES] for i in range(n)], axis=0), e2)
    return jnp.concatenate([s[i * rows:(i + 1) * rows] for i in range(n)], axis=1)


RWKV_PAIRS = 16
RWKV_SEQS = 2
RWKV_CHUNK = 64


def _rwkv_chunk_kernel(r_ref, k_ref, v_ref, gt_ref, lw_ref, a_ref, kk_ref, ka_ref, rk_ref, gng_ref, gnb_ref, s0_ref,
                       o_ref, so_ref, st_ref, *, lreal):
    C, N = RWKV_CHUNK, RWKV_HEAD_DIM
    t_idx = pl.program_id(2)
    e2 = _head_ones()
    m0 = lax.broadcasted_iota(jnp.int32, (C, LANES), 1) < N
    ti = lax.broadcasted_iota(jnp.int32, (C, 2 * C), 0)
    si2 = lax.broadcasted_iota(jnp.int32, (C, 2 * C), 1)
    si = si2 % C
    strict, incl, left = si < ti, si <= ti, si2 < C
    bd = (lax.broadcasted_iota(jnp.int32, (LANES, LANES), 0) // N
          == lax.broadcasted_iota(jnp.int32, (LANES, LANES), 1) // N)
    zs = jnp.zeros((N, N), F32)

    nseq = o_ref.shape[0]

    @pl.when(t_idx == 0)
    def _():
        for bi in range(nseq):
            for p in range(RWKV_PAIRS):
                top = jnp.concatenate([s0_ref[bi, 2 * p], zs], axis=1)
                bot = jnp.concatenate([zs, s0_ref[bi, 2 * p + 1]], axis=1)
                st_ref[bi * RWKV_PAIRS + p] = jnp.concatenate([top, bot], axis=0)

    def both(x, y):
        parts = [x] if y is None else [x, y]
        return jnp.concatenate([jnp.where(m0, z, 0.0) for z in parts] + [jnp.where(m0, 0.0, z) for z in parts],
                               axis=0).astype(BF16)

    def load(x):
        if lreal < C:
            x = jnp.concatenate([x, jnp.zeros((C - lreal, x.shape[1]), F32)], axis=0)
        return x

    inv_n = 1.0 / N
    n_iter = max(1, (lreal - 1).bit_length())
    lanes = [slice(p * LANES, (p + 1) * LANES) for p in range(RWKV_PAIRS)]

    def prep(bi):
        r, k, v = load(r_ref[0, bi]), load(k_ref[0, bi]), load(v_ref[0, bi])
        lw, a = load(lw_ref[bi]), load(a_ref[bi])
        kk = k * kk_ref[...]
        kk = kk / jnp.maximum(jnp.sqrt(_segsum_wide(kk * kk, e2)), 1e-12)
        kp = k * (1.0 + (a - 1.0) * ka_ref[...])
        c = _cumsum_rows(lw, _tril3(C))
        e_c, e_nc = jnp.exp(c), jnp.exp(-c)
        return dict(v=v, e_c=e_c, at_w=-kk * jnp.exp(c - lw), rt_w=r * e_c,
                    bt_w=(kk * a * e_nc).astype(BF16), kt_w=(kp * e_nc).astype(BF16),
                    bonus=_segsum_wide((r * kp * rk_ref[...])[:lreal], e2))

    def scores(bi, q, p):
        ls = lanes[p]
        at, rt = q["at_w"][:, ls], q["rt_w"][:, ls]
        bk = jnp.concatenate([q["bt_w"][:, ls], q["kt_w"][:, ls]], axis=0)
        g = _nt(both(at, rt), bk)
        s_bd = st_ref[bi * RWKV_PAIRS + p]
        pq = _nt(jnp.concatenate([at, rt], axis=0).astype(BF16), s_bd.astype(BF16))
        return dict(bk=bk, s_bd=s_bd, g=g, pq=pq)

    def setup(q, p, d):
        g, pq = d.pop("g"), d.pop("pq")
        aa0, rr0 = jnp.where(strict, g[0:C], 0.0), jnp.where(incl, g[C:2 * C], 0.0)
        aa1, rr1 = jnp.where(strict, g[2 * C:3 * C], 0.0), jnp.where(incl, g[3 * C:], 0.0)
        a_ab = jnp.where(left, aa0, pltpu.roll(aa1, C, 1))
        a_ak = jnp.where(left, pltpu.roll(aa0, C, 1), aa1)
        x = pq[:C] + jnp.dot(a_ak.astype(BF16), both(q["v"][:, lanes[p]], None), preferred_element_type=F32)
        d.update(y0=pq[C:], rr=jnp.concatenate([rr0, rr1], axis=1).astype(BF16), x=x, ac=a_ab)

    def neumann(d, it):
        ac = d["ac"]
        ac16 = ac.astype(BF16)
        rhs = both(d["x"], None)
        if it < n_iter - 1:
            a_bd = jnp.concatenate([jnp.where(left, ac, 0.0), jnp.where(left, 0.0, ac)], axis=0).astype(BF16)
            res = jnp.dot(ac16, jnp.concatenate([rhs, a_bd], axis=1), preferred_element_type=F32)
            d["x"] = d["x"] + res[:, :LANES]
            d["ac"] = res[:, LANES:]
        else:
            d["x"] = d["x"] + jnp.dot(ac16, rhs, preferred_element_type=F32)

    def finish(q, p, d):
        ls = lanes[p]
        u, vp = d["x"], q["v"][:, ls]
        y = d["y0"] + jnp.dot(d["rr"], both(u, vp), preferred_element_type=F32)
        uv = jnp.concatenate([u, vp], axis=0).astype(BF16)
        ds = lax.dot_general(uv, d["bk"], (((0,), (0,)), ((), ())), preferred_element_type=F32)
        return (d["s_bd"] + jnp.where(bd, ds, 0.0)) * q["e_c"][C - 1:C, ls], y[:lreal]

    def output(bi, q, done):
        y = jnp.concatenate([yp for _, yp in done], axis=1)
        yc = y - _segsum_wide(y, e2) * inv_n
        yv = _segsum_wide(yc * yc, e2) * inv_n
        yn = yc * lax.rsqrt(yv + RWKV_GN_EPS) * gng_ref[...] + gnb_ref[...]
        return ((yn + q["bonus"] * q["v"][:lreal]) * gt_ref[0, bi]).astype(o_ref.dtype)

    pairs = range(RWKV_PAIRS)
    stages = [
        lambda bi, s: s.update(q=prep(bi)),
        lambda bi, s: s.update(work=[scores(bi, s["q"], p) for p in pairs]),
        lambda bi, s: [setup(s["q"], p, d) for p, d in enumerate(s["work"])],
        lambda bi, s: [neumann(d, it) for it in range(n_iter) for d in s["work"]],
        lambda bi, s: s.update(done=[finish(s["q"], p, d) for p, d in enumerate(s["work"])]),
        lambda bi, s: s.update(out=output(bi, s["q"], s["done"])),
    ]
    seqs = [dict() for _ in range(nseq)]
    lag = 1
    for tick in range(len(stages) + lag * (nseq - 1)):
        for bi in range(nseq):
            if 0 <= tick - lag * bi < len(stages):
                stages[tick - lag * bi](bi, seqs[bi])
    for bi, s in enumerate(seqs):
        for p, (s_new, _) in enumerate(s["done"]):
            st_ref[bi * RWKV_PAIRS + p] = s_new
        o_ref[bi] = s["out"]

    @pl.when(t_idx == pl.num_programs(2) - 1)
    def _():
        for bi in range(nseq):
            for p in range(RWKV_PAIRS):
                s = st_ref[bi * RWKV_PAIRS + p]
                so_ref[bi, 2 * p] = s[:N, :N]
                so_ref[bi, 2 * p + 1] = s[N:, N:]


def _rwkv_scan(proj, lw, a, k_k, k_a, r_k, gn_g, gn_b, states, j, B, L):
    tc = min(L, RWKV_CHUNK)
    nt = L // tc
    W = RWKV_PAIRS * LANES
    ng = D_MODEL // W
    hpg = 2 * RWKV_PAIRS
    ns = RWKV_SEQS
    assert B % ns == 0
    hd = RWKV_HEAD_DIM
    tok = lambda m: pl.BlockSpec((1, ns, tc, W), lambda b, g, t: (m, b, t, g))
    vec = pl.BlockSpec((ns, tc, W), lambda b, g, t: (b, t, g))
    par = pl.BlockSpec((1, W), lambda b, g, t: (0, g))
    st = pl.BlockSpec((ns, hpg, hd, hd), lambda b, g, t: (b, g, 0, 0))
    st0 = pl.BlockSpec((None, ns, hpg, hd, hd), lambda b, g, t: (j, b, g, 0, 0))
    proj4 = proj.reshape(4, B, L, D_MODEL)
    o, st_out = pl.pallas_call(
        functools.partial(_rwkv_chunk_kernel, lreal=tc),
        grid=(B // ns, ng, nt),
        in_specs=[tok(0), tok(1), tok(2), tok(3), vec, vec, par, par, par, par, par, st0],
        out_specs=[vec, st],
        out_shape=[jax.ShapeDtypeStruct((B, L, D_MODEL), BF16),
                   jax.ShapeDtypeStruct((B, RWKV_HEADS, hd, hd), F32)],
        scratch_shapes=[pltpu.VMEM((ns * RWKV_PAIRS, LANES, LANES), F32)],
        compiler_params=_cparams(("parallel", "parallel", "arbitrary")),
        name="rwkv_scan",
    )(proj4, proj4, proj4, proj4, lw.reshape(B, L, D_MODEL), a.reshape(B, L, D_MODEL), k_k.reshape(1, D_MODEL),
      k_a.reshape(1, D_MODEL), r_k.reshape(1, D_MODEL), gn_g.reshape(1, D_MODEL), gn_b.reshape(1, D_MODEL), states)
    return o.reshape(B * L, D_MODEL), st_out


def _rwkv_layer(x, shift, states, j, p, ln_g, ln_b, B, L):
    x3 = x.reshape(B, L, D_MODEL)
    if L % PROJ_ROWS == 0 and L % LORA_ROWS == 0:
        xprev = None
    else:
        xprev = jnp.concatenate([shift[:, None, :], x3[:, :-1]], axis=1).reshape(B * L, D_MODEL)
    proj = _rwkv_proj(x, xprev, shift, L, p["mu4"], p["w4"])
    lw, a = _rwkv_lora(x, xprev, shift, L, p["mu2"], p["w1"], p["w2"], p["w0"], p["a1"], p["a2"], p["a0"])
    o, st = _rwkv_scan(proj, lw, a, p["k_k"], p["k_a"], p["r_k"], p["gn_g"], p["gn_b"], states, j, B, L)
    return _mm_ln(o, p["w_out"], x, ln_g, ln_b), st, x3[:, -1]


def _pad_rank(w, axis):
    pad = [(0, 0), (0, 0)]
    pad[axis] = (0, LANES - w.shape[axis])
    return jnp.pad(w, pad).astype(BF16)


def _trunk(x3, cache, w):
    B, L, _ = x3.shape
    prompt = cache is None
    x = x3.reshape(B * L, D_MODEL)
    new_k, new_v, new_gla, new_wkv, new_shift = [], [], [], [], []
    for layer in range(DEPTH):
        kind, j = layer % 3, layer // 3
        g, b = w["ln_g"][layer], w["ln_b"][layer]
        if kind == 0:
            c = None if prompt else (cache["k"], cache["v"])
            x, nk, nv = _swa_layer(x, c, j, w["swa_w_in"][j], w["swa_sink"][j], w["swa_w_out"][j], g, b, B, L)
            new_k.append(nk)
            new_v.append(nv)
        elif kind == 1:
            st = None if prompt else cache["gla"]
            x, st = _gla_layer(x, st, j, w["gla_w_in"][j], w["gla_w_low"][j], w["gla_w_a2"][j], w["gla_b_a"][j],
                               w["gla_norm_g"][j], w["gla_w_out"][j], g, b, B, L)
            new_gla.append(st)
        else:
            if prompt:
                shift0 = jnp.zeros((B, D_MODEL), F32)
                s0, js = jnp.zeros((1, B, RWKV_HEADS, RWKV_HEAD_DIM, RWKV_HEAD_DIM), F32), 0
            else:
                shift0, s0, js = cache["shift"][j], cache["wkv"], j
            x, st, sh = _rwkv_layer(x, shift0, s0, js, w["rwkv"][j], g, b, B, L)
            new_wkv.append(st)
            new_shift.append(sh)
    stack = lambda parts: parts[0][None] if len(parts) == 1 else jnp.stack(parts)
    return (x.reshape(B, L, D_MODEL), stack(new_k), stack(new_v), stack(new_gla), stack(new_wkv), stack(new_shift))


def kernel(x_prompt, x_sample, cache_swa_k, cache_swa_v, state_gla, state_rwkv, state_rwkv_shift, ln_g, ln_b, swa_w_in, swa_sink, swa_w_out, gla_w_in, gla_w_a2, gla_b_a, gla_norm_g, gla_w_out, rwkv_mu, rwkv_w_rkvg, rwkv_w0, rwkv_w1, rwkv_w2, rwkv_a0, rwkv_a1, rwkv_a2, rwkv_k_k, rwkv_k_a, rwkv_r_k, rwkv_gn_g, rwkv_gn_b, rwkv_w_out):
    n_rwkv = rwkv_mu.shape[0]
    gla_main = 2 * GLA_KEY_DIM + 2 * GLA_VAL_DIM
    rwkv = []
    for j in range(n_rwkv):
        rwkv.append(dict(
            mu4=rwkv_mu[j][jnp.array([0, 2, 3, 5])].reshape(4, 1, D_MODEL),
            mu2=rwkv_mu[j][jnp.array([1, 4])].reshape(2, 1, D_MODEL),
            w4=rwkv_w_rkvg[j].astype(BF16),
            w1=_pad_rank(rwkv_w1[j], 1), w2=_pad_rank(rwkv_w2[j], 0), w0=rwkv_w0[j],
            a1=_pad_rank(rwkv_a1[j], 1), a2=_pad_rank(rwkv_a2[j], 0), a0=rwkv_a0[j],
            k_k=rwkv_k_k[j], k_a=rwkv_k_a[j], r_k=rwkv_r_k[j], gn_g=rwkv_gn_g[j], gn_b=rwkv_gn_b[j],
            w_out=rwkv_w_out[j].astype(BF16)))
    per_layer = lambda a: [a[j].astype(BF16) for j in range(a.shape[0])]
    w = dict(ln_g=ln_g, ln_b=ln_b,
             swa_w_in=per_layer(swa_w_in), swa_sink=swa_sink, swa_w_out=per_layer(swa_w_out),
             gla_w_in=per_layer(gla_w_in[:, :, :gla_main]),
             gla_w_low=[_pad_rank(gla_w_in[j][:, gla_main:], 1) for j in range(gla_w_in.shape[0])],
             gla_w_a2=[_pad_rank(gla_w_a2[j], 0) for j in range(gla_w_a2.shape[0])],
             gla_b_a=gla_b_a, gla_norm_g=gla_norm_g, gla_w_out=per_layer(gla_w_out), rwkv=rwkv)
    y_p, p_k, p_v, p_gla, p_wkv, p_shift = _trunk(x_prompt, None, w)
    cache = dict(k=cache_swa_k, v=cache_swa_v, gla=state_gla, wkv=state_rwkv, shift=state_rwkv_shift)
    y_s, s_k, s_v, s_gla, s_wkv, s_shift = _trunk(x_sample, cache, w)
    return (y_p, y_s, p_k, p_v, p_gla, p_wkv, p_shift, s_k, s_v, s_gla, s_wkv, s_shift)
```

```python
import functools
import math

import jax
import jax.numpy as jnp
from jax import lax
from jax.experimental import pallas as pl
from jax.experimental.pallas import tpu as pltpu

F32 = jnp.float32
BF16 = jnp.bfloat16

D_MODEL = 2048
DEPTH = 4
PAST_LEN = 16384
ALPHA = (2 * DEPTH) ** 0.25
LN_EPS = 1e-5

SWA_HEADS = 32
SWA_KV_HEADS = 4
SWA_GROUP = SWA_HEADS // SWA_KV_HEADS
SWA_HEAD_DIM = 64
SWA_WIDTH = SWA_HEADS * SWA_HEAD_DIM
SWA_KV_WIDTH = SWA_KV_HEADS * SWA_HEAD_DIM
WINDOW = 128
ROT_DIM = SWA_HEAD_DIM // 4
ROPE_THETA = 500000.0

GLA_HEADS = 4
GLA_KEY_DIM = D_MODEL // 2
GLA_VAL_DIM = D_MODEL
GLA_DK = GLA_KEY_DIM // GLA_HEADS
GLA_DV = GLA_VAL_DIM // GLA_HEADS
GLA_GATE_RANK = 16
GLA_GATE_TEMP = 16.0
GLA_CHUNK = 64
GLA_SUB = 16
GLA_SEQS = 2
GLA_NORM_EPS = 1e-5

RWKV_HEAD_DIM = 64
RWKV_HEADS = D_MODEL // RWKV_HEAD_DIM
RWKV_GN_EPS = 64e-5

LANES = 128
VMEM_LIMIT = 56 * 1024 * 1024
PROJ_ROWS = 1024
PROJ_COLS = 1536
LORA_ROWS = 512
OUT_ROWS = 512
LN_SUB_ROWS = 256
NEG_BIG = -1e30


def _cparams(sem):
    return pltpu.CompilerParams(dimension_semantics=sem, vmem_limit_bytes=VMEM_LIMIT)


def _silu(x):
    return x * (1.0 / (1.0 + jnp.exp(-x)))


def _log_sigmoid(z):
    return jnp.minimum(z, 0.0) - jnp.log(1.0 + jnp.exp(-jnp.abs(z)))


def _nt(a, b):
    return lax.dot_general(a, b, (((1,), (1,)), ((), ())), preferred_element_type=F32)


def _mm_kernel(x_ref, w_ref, *refs, plans):
    o_ref = refs[-1]

    def run(plan):
        acc = jnp.dot(x_ref[...].astype(BF16), w_ref[...], preferred_element_type=F32)
        if all(op is None for op in plan):
            o_ref[...] = acc
            return
        for c, op in enumerate(plan):
            chunk = acc[:, c * LANES:(c + 1) * LANES]
            if op == "silu":
                chunk = _silu(chunk)
            elif op in ("rope", "rope_q"):
                chunk = _rope128(chunk, refs[0][0], refs[0][1], refs[0][2])
                if op == "rope_q":
                    chunk = chunk * SWA_HEAD_DIM ** -0.5
            o_ref[:, c * LANES:(c + 1) * LANES] = chunk

    if len(set(plans)) == 1:
        run(plans[0])
    else:
        for t, plan in enumerate(plans):
            pl.when(pl.program_id(1) == t)(functools.partial(run, plan))


def _mm(x, w, ncols, tn=PROJ_COLS, rope_q_cols=0, rope_cols=0, silu_from=None, rope_tabs=None):
    M, K = x.shape
    tm = min(M, PROJ_ROWS)
    assert M % tm == 0 and ncols % tn == 0

    def op_of(col):
        if col < rope_q_cols:
            return "rope_q"
        if col < rope_cols:
            return "rope"
        return "silu" if silu_from is not None and col >= silu_from else None

    plans = tuple(tuple(op_of(t * tn + c * LANES) for c in range(tn // LANES)) for t in range(ncols // tn))
    w, layer = (w, None) if not isinstance(w, tuple) else w
    w_spec = (pl.BlockSpec((K, tn), lambda i, j: (0, j)) if layer is None else
              pl.BlockSpec((None, K, tn), lambda i, j: (layer, 0, j)))
    operands, specs = [x, w], [pl.BlockSpec((tm, K), lambda i, j: (i, 0)), w_spec]
    if rope_cols:
        period = rope_tabs.shape[1] // tm
        operands.append(rope_tabs)
        specs.append(pl.BlockSpec((3, tm, LANES), lambda i, j: (0, i % period, 0)))
    return pl.pallas_call(
        functools.partial(_mm_kernel, plans=plans),
        grid=(M // tm, ncols // tn),
        in_specs=specs,
        out_specs=pl.BlockSpec((tm, tn), lambda i, j: (i, j)),
        out_shape=jax.ShapeDtypeStruct((M, ncols), F32),
        compiler_params=_cparams(("parallel", "arbitrary")),
        name="proj_mm",
    )(*operands)


def _mm_ln_kernel(a_ref, w_ref, x_ref, g_ref, b_ref, o_ref):
    for r0 in range(0, a_ref.shape[0], LN_SUB_ROWS):
        rows = slice(r0, min(r0 + LN_SUB_ROWS, a_ref.shape[0]))
        h = jnp.dot(a_ref[rows, :], w_ref[...], preferred_element_type=F32)
        z = ALPHA * x_ref[rows, :] + h
        mu = jnp.mean(z, axis=-1, keepdims=True)
        zc = z - mu
        var = jnp.mean(zc * zc, axis=-1, keepdims=True)
        o_ref[rows, :] = zc * lax.rsqrt(var + LN_EPS) * g_ref[...] + b_ref[...]


def _mm_ln(a, w, x, g, b):
    M, K = a.shape
    tm = min(M, OUT_ROWS)
    w, layer = (w, None) if not isinstance(w, tuple) else w
    D = w.shape[-1]
    return pl.pallas_call(
        _mm_ln_kernel,
        grid=(M // tm,),
        in_specs=[pl.BlockSpec((tm, K), lambda i: (i, 0)),
                  (pl.BlockSpec((K, D), lambda i: (0, 0)) if layer is None else
                   pl.BlockSpec((None, K, D), lambda i: (layer, 0, 0))),
                  pl.BlockSpec((tm, D), lambda i: (i, 0)),
                  pl.BlockSpec((1, D), lambda i: (0, 0)),
                  pl.BlockSpec((1, D), lambda i: (0, 0))],
        out_specs=pl.BlockSpec((tm, D), lambda i: (i, 0)),
        out_shape=jax.ShapeDtypeStruct((M, D), F32),
        compiler_params=_cparams(("parallel",)),
        name="out_proj_ln",
    )(a, w, x, g.reshape(1, D), b.reshape(1, D))


def _rope_tables(pos):
    half = ROT_DIM // 2
    inv = ROPE_THETA ** (-(jnp.arange(half, dtype=F32) * 2.0 / ROT_DIM))
    ang = pos[:, None] * inv[None, :]
    cos, sin = jnp.cos(ang), jnp.sin(ang)
    L = pos.shape[0]
    ones = jnp.ones((L, SWA_HEAD_DIM - ROT_DIM), F32)
    zeros_r = jnp.zeros((L, SWA_HEAD_DIM - ROT_DIM), F32)
    zeros_h = jnp.zeros((L, half), F32)
    c = jnp.concatenate([cos, cos, ones], axis=1)
    s1 = jnp.concatenate([-sin, zeros_h, zeros_r], axis=1)
    s2 = jnp.concatenate([zeros_h, sin, zeros_r], axis=1)
    rep = LANES // SWA_HEAD_DIM
    return jnp.tile(c, (1, rep)), jnp.tile(s1, (1, rep)), jnp.tile(s2, (1, rep))


def _rope128(x, c, s1, s2):
    return x * c + pltpu.roll(x, LANES - ROT_DIM // 2, 1) * s1 + pltpu.roll(x, ROT_DIM // 2, 1) * s2


def _swa_kernel(sink_ref, q_ref, kc_ref, vc_ref, kp_ref, vp_ref, g0_ref, g1_ref, g2_ref, g3_ref,
                o_ref, ko_ref, vo_ref, *, lq, prompt):
    gate_refs = (g0_ref, g1_ref, g2_ref, g3_ref)
    ppt = SWA_GATE_TILE // LANES
    nseq = ko_ref.shape[0]
    nk = 2 * WINDOW
    nh = SWA_GROUP
    qi = lax.broadcasted_iota(jnp.int32, (lq, nk), 0)
    sj = lax.broadcasted_iota(jnp.int32, (lq, nk), 1)
    rel = qi + WINDOW - sj
    mask = jnp.logical_and(rel >= 0, rel <= WINDOW)
    if prompt:
        lo = jnp.where(pl.program_id(1) == 0, WINDOW, 0)
        mask = jnp.logical_and(mask, sj >= lo)
    m0 = lax.broadcasted_iota(jnp.int32, (lq, LANES), 1) < SWA_HEAD_DIM
    hd = SWA_HEAD_DIM
    ppd = nh // 2

    def attend(bi):
        rs = slice(bi * lq, (bi + 1) * lq)
        k_cur, v_cur = kc_ref[rs, :], vc_ref[rs, :]
        if prompt:
            k_prev, v_prev = kp_ref[...], vp_ref[...]
            k_out, v_out = k_cur, v_cur
        else:
            k_prev, v_prev = kp_ref[bi], vp_ref[bi]
            k_out = jnp.concatenate([k_prev[lq:], k_cur], axis=0)
            v_out = jnp.concatenate([v_prev[lq:], v_cur], axis=0)
            zpad = jnp.zeros((WINDOW - lq, SWA_KV_WIDTH), F32)
            k_cur = jnp.concatenate([k_cur, zpad], axis=0)
            v_cur = jnp.concatenate([v_cur, zpad], axis=0)
        k_all = jnp.concatenate([k_prev, k_cur], axis=0)
        v_all = jnp.concatenate([v_prev, v_cur], axis=0)
        return k_out, v_out, [kv_group(rs, k_all, v_all, kh) for kh in range(SWA_KV_HEADS)]

    def kv_group(rs, k_all, v_all, kh):
        kg = k_all[:, kh * hd:(kh + 1) * hd]
        vg = v_all[:, kh * hd:(kh + 1) * hd]
        kk2 = jnp.concatenate([kg, kg], axis=1).astype(BF16)
        vv2 = jnp.concatenate([vg, vg], axis=1).astype(BF16)
        p0 = kh * ppd
        pairs = range(p0, p0 + ppd)
        rows = []
        for p in pairs:
            q2 = q_ref[rs, p * LANES:(p + 1) * LANES]
            rows += [jnp.where(m0, q2, 0.0), jnp.where(m0, 0.0, q2)]
        qs = jnp.concatenate(rows, axis=0).astype(BF16)
        s_all = _nt(qs, kk2)
        yield None
        es, dens = [], []
        for i in range(2 * ppd):
            snk = sink_ref[2 * p0 + i]
            s = jnp.where(mask, s_all[i * lq:(i + 1) * lq], NEG_BIG)
            m = jnp.maximum(jnp.max(s, axis=-1, keepdims=True), snk)
            e = jnp.exp(s - m)
            dens.append(jnp.sum(e, axis=-1, keepdims=True) + jnp.exp(snk - m))
            es.append(e.astype(BF16))
        yield None
        o_all = jnp.dot(jnp.concatenate(es, axis=0), vv2, preferred_element_type=F32)
        outs = []
        for i, p in enumerate(pairs):
            oa = o_all[2 * i * lq:(2 * i + 1) * lq] / dens[2 * i]
            ob = o_all[(2 * i + 1) * lq:(2 * i + 2) * lq] / dens[2 * i + 1]
            g2 = gate_refs[p // ppt][rs, (p % ppt) * LANES:(p % ppt + 1) * LANES]
            outs.append((p, (jnp.where(m0, oa, ob) * g2).astype(o_ref.dtype)))
        yield outs

    seqs = [attend(bi) for bi in range(nseq)]
    groups = [g for _, _, gs in seqs for g in gs]
    if nseq > 1:
        for _ in range(2):
            for g in groups:
                next(g)
        done = [next(g) for g in groups]
    else:
        done = [list(g)[-1] for g in groups]
    for bi, (k_out, v_out, gs) in enumerate(seqs):
        ko_ref[bi] = k_out
        vo_ref[bi] = v_out
        for outs in done[bi * len(gs):(bi + 1) * len(gs)]:
            for p, out in outs:
                o_ref[bi * lq:(bi + 1) * lq, p * LANES:(p + 1) * LANES] = out


SWA_SAMPLE_SEQS = 4
SWA_GATE_TILE = 512


def _swa_gate_specs(rows, index):
    g0 = (SWA_WIDTH + 2 * SWA_KV_WIDTH) // SWA_GATE_TILE
    return [pl.BlockSpec((rows, SWA_GATE_TILE), functools.partial(index, g0 + t))
            for t in range(SWA_WIDTH // SWA_GATE_TILE)]


def _swa_prompt(h, sink, B, S):
    nb = S // WINDOW
    kcol = SWA_WIDTH // SWA_KV_WIDTH
    prev = lambda b, n: b * nb + jnp.maximum(n - 1, 0)
    return pl.pallas_call(
        functools.partial(_swa_kernel, lq=WINDOW, prompt=True),
        grid=(B, nb),
        in_specs=[pl.BlockSpec(memory_space=pltpu.SMEM),
                  pl.BlockSpec((WINDOW, SWA_WIDTH), lambda b, n: (b * nb + n, 0)),
                  pl.BlockSpec((WINDOW, SWA_KV_WIDTH), lambda b, n: (b * nb + n, kcol)),
                  pl.BlockSpec((WINDOW, SWA_KV_WIDTH), lambda b, n: (b * nb + n, kcol + 1)),
                  pl.BlockSpec((WINDOW, SWA_KV_WIDTH), lambda b, n: (prev(b, n), kcol)),
                  pl.BlockSpec((WINDOW, SWA_KV_WIDTH), lambda b, n: (prev(b, n), kcol + 1)),
                  *_swa_gate_specs(WINDOW, lambda col, b, n: (b * nb + n, col))],
        out_specs=[pl.BlockSpec((WINDOW, SWA_WIDTH), lambda b, n: (b * nb + n, 0)),
                   pl.BlockSpec((1, WINDOW, SWA_KV_WIDTH), lambda b, n: (b, 0, 0)),
                   pl.BlockSpec((1, WINDOW, SWA_KV_WIDTH), lambda b, n: (b, 0, 0))],
        out_shape=[jax.ShapeDtypeStruct((B * S, SWA_WIDTH), BF16),
                   jax.ShapeDtypeStruct((B, WINDOW, SWA_KV_WIDTH), F32),
                   jax.ShapeDtypeStruct((B, WINDOW, SWA_KV_WIDTH), F32)],
        compiler_params=_cparams(("parallel", "arbitrary")),
        name="swa_prompt",
    )(sink, h, h, h, h, h, h, h, h, h)


def _swa_sample(h, sink, cache_k, cache_v, j, B, L):
    ns = SWA_SAMPLE_SEQS
    assert B % ns == 0
    cache = pl.BlockSpec((None, ns, WINDOW, SWA_KV_WIDTH), lambda b: (j, b, 0, 0))
    kcol = SWA_WIDTH // SWA_KV_WIDTH
    return pl.pallas_call(
        functools.partial(_swa_kernel, lq=L, prompt=False),
        grid=(B // ns,),
        in_specs=[pl.BlockSpec(memory_space=pltpu.SMEM),
                  pl.BlockSpec((ns * L, SWA_WIDTH), lambda b: (b, 0)),
                  pl.BlockSpec((ns * L, SWA_KV_WIDTH), lambda b: (b, kcol)),
                  pl.BlockSpec((ns * L, SWA_KV_WIDTH), lambda b: (b, kcol + 1)),
                  cache, cache,
                  *_swa_gate_specs(ns * L, lambda col, b: (b, col))],
        out_specs=[pl.BlockSpec((ns * L, SWA_WIDTH), lambda b: (b, 0)),
                   pl.BlockSpec((ns, WINDOW, SWA_KV_WIDTH), lambda b: (b, 0, 0)),
                   pl.BlockSpec((ns, WINDOW, SWA_KV_WIDTH), lambda b: (b, 0, 0))],
        out_shape=[jax.ShapeDtypeStruct((B * L, SWA_WIDTH), BF16),
                   jax.ShapeDtypeStruct((B, WINDOW, SWA_KV_WIDTH), F32),
                   jax.ShapeDtypeStruct((B, WINDOW, SWA_KV_WIDTH), F32)],
        compiler_params=_cparams(("parallel",)),
        name="swa_sample",
    )(sink, h, h, h, cache_k, cache_v, h, h, h, h)


def _swa_layer(x, cache, j, w_in, sink, w_out, ln_g, ln_b, B, L):
    first = 0 if cache is None else PAST_LEN
    rows = max(L, min(B * L, PROJ_ROWS))
    tabs = jnp.stack(_rope_tables(first + (jnp.arange(rows) % L).astype(F32)))
    h = _mm(x, w_in, 2 * SWA_WIDTH + 2 * SWA_KV_WIDTH, rope_q_cols=SWA_WIDTH, rope_cols=SWA_WIDTH + SWA_KV_WIDTH,
            silu_from=SWA_WIDTH + 2 * SWA_KV_WIDTH, rope_tabs=tabs)
    if cache is None:
        o, nk, nv = _swa_prompt(h, sink, B, L)
    else:
        ck = cache[0].reshape(-1, B, WINDOW, SWA_KV_WIDTH)
        cv = cache[1].reshape(-1, B, WINDOW, SWA_KV_WIDTH)
        o, nk, nv = _swa_sample(h, sink, ck, cv, j, B, L)
    shape = (B, WINDOW, SWA_KV_HEADS, SWA_HEAD_DIM)
    return _mm_ln(o, w_out, x, ln_g, ln_b), nk.reshape(shape), nv.reshape(shape)


def _gla_gate_kernel(x_ref, wl_ref, wa_ref, ba_ref, g_ref):
    a_low = jnp.dot(x_ref[...].astype(BF16), wl_ref[...], preferred_element_type=F32)
    z = jnp.dot(a_low.astype(BF16), wa_ref[...], preferred_element_type=F32) + ba_ref[...]
    g_ref[...] = _log_sigmoid(z) * (1.0 / GLA_GATE_TEMP)


def _gla_gate(x, w_low, w_a2, b_a):
    M = x.shape[0]
    tm = min(M, PROJ_ROWS)
    return pl.pallas_call(
        _gla_gate_kernel,
        grid=(M // tm,),
        in_specs=[pl.BlockSpec((tm, D_MODEL), lambda i: (i, 0)),
                  pl.BlockSpec((D_MODEL, LANES), lambda i: (0, 0)),
                  pl.BlockSpec((LANES, GLA_KEY_DIM), lambda i: (0, 0)),
                  pl.BlockSpec((1, GLA_KEY_DIM), lambda i: (0, 0))],
        out_specs=pl.BlockSpec((tm, GLA_KEY_DIM), lambda i: (i, 0)),
        out_shape=jax.ShapeDtypeStruct((M, GLA_KEY_DIM), F32),
        compiler_params=_cparams(("parallel",)),
        name="gla_gate",
    )(x, w_low, w_a2, b_a.reshape(1, GLA_KEY_DIM))


def _tril3(C):
    return (lax.broadcasted_iota(jnp.int32, (C, 3 * C), 1) % C
            <= lax.broadcasted_iota(jnp.int32, (C, 3 * C), 0)).astype(BF16)


def _cumsum_rows(x, tril3):
    h1 = x.astype(BF16)
    r1 = x - h1.astype(F32)
    h2 = r1.astype(BF16)
    h3 = (r1 - h2.astype(F32)).astype(BF16)
    return jnp.dot(tril3, jnp.concatenate([h1, h2, h3], axis=0), preferred_element_type=F32)


def _gla_chunk_kernel(q_ref, k_ref, v_ref, gate_ref, g_ref, ng_ref, s0_ref, o_ref, so_ref, st_ref, *, C, SB):
    c_idx = pl.program_id(1)
    H, DK, DV = GLA_HEADS, GLA_DK, GLA_DV
    nseq = o_ref.shape[0]

    @pl.when(c_idx == 0)
    def _():
        for bi in range(nseq):
            for h in range(H):
                st_ref[bi * H + h] = s0_ref[bi, h].T

    row = lax.broadcasted_iota(jnp.int32, (C, H * DK), 0)
    causal = lax.broadcasted_iota(jnp.int32, (C, C), 1) <= lax.broadcasted_iota(jnp.int32, (C, C), 0)
    tril3 = _tril3(C)

    def prep(bi):
        b = _cumsum_rows(g_ref[bi], tril3)
        q = q_ref[bi] * (DK ** -0.5)
        k = k_ref[bi]
        b_last = b[C - 1:C, :]
        qis, kjs = [], []
        for i in range(C // SB):
            r0, r1 = i * SB, (i + 1) * SB
            bn = b[r0:r0 + 1, :]
            qis.append((q[r0:r1] * jnp.exp(b[r0:r1] - bn)).astype(BF16))
            kjs.append((k * jnp.exp(jnp.where(row < r1, bn - b, 0.0))).astype(BF16))
        return dict(v16=v_ref[bi].astype(BF16), e_last=jnp.exp(b_last), qe=(q * jnp.exp(b)).astype(BF16),
                    kd=(k * jnp.exp(b_last - b)).astype(BF16), qis=qis, kjs=kjs)

    def head(bi, p, h):
        ks, vs = slice(h * DK, (h + 1) * DK), slice(h * DV, (h + 1) * DV)
        st = st_ref[bi * H + h]
        a_parts = [_nt(qi[:, ks], kj[:, ks]) for qi, kj in zip(p["qis"], p["kjs"])]
        v16 = p["v16"][:, vs]
        o_inter = _nt(p["qe"][:, ks], st.astype(BF16))
        ds = lax.dot_general(v16, p["kd"][:, ks], (((0,), (0,)), ((), ())), preferred_element_type=F32)
        yield None
        a = jnp.concatenate(a_parts, axis=0) if len(a_parts) > 1 else a_parts[0]
        a = jnp.where(causal, a, 0.0).astype(BF16)
        o = o_inter + jnp.dot(a, v16, preferred_element_type=F32)
        st_new = st * p["e_last"][:, ks] + ds
        yield None
        on = o * lax.rsqrt(jnp.mean(o * o, axis=-1, keepdims=True) + GLA_NORM_EPS) * ng_ref[...]
        yield st_new, (on * gate_ref[bi, :, vs]).astype(o_ref.dtype)

    preps = [prep(bi) for bi in range(nseq)]
    chains = [[head(bi, preps[bi], h) for h in range(H)] for bi in range(nseq)]
    if C == GLA_CHUNK:
        for _ in range(2):
            for per_seq in chains:
                for chain in per_seq:
                    next(chain)
        results = [[next(chain) for chain in per_seq] for per_seq in chains]
    else:
        results = [[list(chain)[-1] for chain in per_seq] for per_seq in chains]
    for bi in range(nseq):
        for h, (st_new, out) in enumerate(results[bi]):
            st_ref[bi * H + h] = st_new
            o_ref[bi, :, h * DV:(h + 1) * DV] = out

    @pl.when(c_idx == pl.num_programs(1) - 1)
    def _():
        for bi in range(nseq):
            for h in range(H):
                so_ref[bi, h] = st_ref[bi * H + h].T


def _gla_chunk(h, g, norm_g, states, j, B, L):
    C = min(GLA_CHUNK, L)
    SB = min(GLA_SUB, C)
    nc = L // C
    ns = GLA_SEQS
    assert B % ns == 0
    tok = lambda width, col: pl.BlockSpec((ns, C, width), lambda b, c: (b, c, col))
    st0 = pl.BlockSpec((None, ns, GLA_HEADS, GLA_DK, GLA_DV), lambda b, c: (j, b, 0, 0, 0))
    st = pl.BlockSpec((ns, GLA_HEADS, GLA_DK, GLA_DV), lambda b, c: (b, 0, 0, 0))
    h3 = h.reshape(B, L, h.shape[1])
    o, st_out = pl.pallas_call(
        functools.partial(_gla_chunk_kernel, C=C, SB=SB),
        grid=(B // ns, nc),
        in_specs=[tok(GLA_KEY_DIM, 0), tok(GLA_KEY_DIM, 1), tok(GLA_VAL_DIM, 1), tok(GLA_VAL_DIM, 2),
                  tok(GLA_KEY_DIM, 0),
                  pl.BlockSpec((1, GLA_DV), lambda b, c: (0, 0)),
                  st0],
        out_specs=[tok(GLA_VAL_DIM, 0), st],
        out_shape=[jax.ShapeDtypeStruct((B, L, GLA_VAL_DIM), BF16),
                   jax.ShapeDtypeStruct((B, GLA_HEADS, GLA_DK, GLA_DV), F32)],
        scratch_shapes=[pltpu.VMEM((ns * GLA_HEADS, GLA_DV, GLA_DK), F32)],
        compiler_params=_cparams(("parallel", "arbitrary")),
        name="gla_chunk",
    )(h3, h3, h3, h3, g.reshape(B, L, GLA_KEY_DIM), norm_g.reshape(1, GLA_DV), states)
    return o.reshape(B * L, GLA_VAL_DIM), st_out


def _gla_layer(x, states, j, w_in, w_low, w_a2, b_a, norm_g, w_out, ln_g, ln_b, B, L):
    h = _mm(x, w_in, 2 * GLA_KEY_DIM + 2 * GLA_VAL_DIM, silu_from=2 * GLA_KEY_DIM + GLA_VAL_DIM)
    g = _gla_gate(x, w_low, w_a2, b_a)
    if states is None:
        states, j = jnp.zeros((1, B, GLA_HEADS, GLA_DK, GLA_DV), F32), 0
    o, st = _gla_chunk(h, g, norm_g, states, j, B, L)
    return _mm_ln(o, w_out, x, ln_g, ln_b), st


SUBLANES = 8
RWKV_GATE_INDEX = 3


def _prev_rows(x, pa_ref, pb_ref, seq_len):
    if seq_len is None:
        return pa_ref[...]
    tm = x.shape[0]
    starts_seq = (pl.program_id(0) * tm) % seq_len == 0
    first = jnp.where(starts_seq, pb_ref[0], pa_ref[SUBLANES - 1:SUBLANES, :])
    rows = lax.broadcasted_iota(jnp.int32, x.shape, 0)
    return jnp.where(rows == 0, first, pltpu.roll(x, 1, 0))


def _prev_specs(x, xprev, shift, tm, seq_len, grid_rank):
    pad = (0,) * (grid_rank - 1)
    sh = shift.reshape(shift.shape[0], 1, D_MODEL)
    if xprev is not None:
        return (xprev, sh), [pl.BlockSpec((tm, D_MODEL), lambda i, *_: (i, 0)),
                             pl.BlockSpec((1, 1, D_MODEL), lambda i, *_: (0, 0, 0))], None
    assert seq_len % tm == 0
    per = tm // SUBLANES
    return (x, sh), [pl.BlockSpec((SUBLANES, D_MODEL), lambda i, *_: (jnp.maximum(i * per - 1, 0), 0)),
                     pl.BlockSpec((1, 1, D_MODEL), lambda i, *_: ((i * tm) // seq_len, 0, 0))], seq_len


def _rwkv_proj_kernel(x_ref, pa_ref, pb_ref, mu_ref, w_ref, o_ref, xm_ref, *, seq_len):
    def step(first_col, gate):
        if first_col:
            x = x_ref[...]
            xm = (x + (_prev_rows(x, pa_ref, pb_ref, seq_len) - x) * mu_ref[0]).astype(BF16)
            xm_ref[...] = xm
        else:
            xm = xm_ref[...]
        acc = jnp.dot(xm, w_ref[0], preferred_element_type=F32)
        o_ref[0] = _silu(acc) if gate else acc

    is_first = pl.program_id(2) == 0
    is_gate = pl.program_id(1) == RWKV_GATE_INDEX
    for first_col in (True, False):
        for gate in (True, False):
            here = jnp.logical_and(is_first == first_col, is_gate == gate)
            pl.when(here)(functools.partial(step, first_col, gate))


def _rwkv_proj(x, xprev, shift, seq_len, mu4, w4, tn=1024):
    M = x.shape[0]
    tm = min(M, PROJ_ROWS)
    prev_ops, prev_specs, inline_len = _prev_specs(x, xprev, shift, tm, seq_len, 3)
    return pl.pallas_call(
        functools.partial(_rwkv_proj_kernel, seq_len=inline_len),
        grid=(M // tm, 4, D_MODEL // tn),
        in_specs=[pl.BlockSpec((tm, D_MODEL), lambda i, m, j: (i, 0)),
                  *prev_specs,
                  pl.BlockSpec((1, 1, D_MODEL), lambda i, m, j: (m, 0, 0)),
                  pl.BlockSpec((1, D_MODEL, tn), lambda i, m, j: (m, 0, j))],
        out_specs=pl.BlockSpec((1, tm, tn), lambda i, m, j: (m, i, j)),
        out_shape=jax.ShapeDtypeStruct((4, M, D_MODEL), F32),
        scratch_shapes=[pltpu.VMEM((tm, D_MODEL), BF16)],
        compiler_params=_cparams(("parallel", "arbitrary", "arbitrary")),
        name="rwkv_proj",
    )(x, *prev_ops, mu4, w4)


def _rwkv_lora_kernel(x_ref, pa_ref, pb_ref, mu_ref, w1_ref, w2_ref, w0_ref, a1_ref, a2_ref, a0_ref, lw_ref, a_ref,
                      *, seq_len):
    x = x_ref[...]
    xx = _prev_rows(x, pa_ref, pb_ref, seq_len) - x
    xw = (x + xx * mu_ref[0]).astype(BF16)
    xa = (x + xx * mu_ref[1]).astype(BF16)
    t = jnp.tanh(jnp.dot(xw, w1_ref[...], preferred_element_type=F32))
    wl = w0_ref[...] + jnp.dot(t.astype(BF16), w2_ref[...], preferred_element_type=F32)
    lw_ref[...] = (-math.exp(-0.5)) / (1.0 + jnp.exp(-wl))
    al = jnp.dot(xa, a1_ref[...], preferred_element_type=F32)
    az = a0_ref[...] + jnp.dot(al.astype(BF16), a2_ref[...], preferred_element_type=F32)
    a_ref[...] = 1.0 / (1.0 + jnp.exp(-az))


def _rwkv_lora(x, xprev, shift, seq_len, mu2, w1, w2, w0, a1, a2, a0):
    M = x.shape[0]
    tm = min(M, LORA_ROWS)
    R = w1.shape[1]
    full = lambda shape: pl.BlockSpec(shape, lambda i: tuple(0 for _ in shape))
    prev_ops, prev_specs, inline_len = _prev_specs(x, xprev, shift, tm, seq_len, 1)
    return pl.pallas_call(
        functools.partial(_rwkv_lora_kernel, seq_len=inline_len),
        grid=(M // tm,),
        in_specs=[pl.BlockSpec((tm, D_MODEL), lambda i: (i, 0)),
                  *prev_specs,
                  full((2, 1, D_MODEL)), full((D_MODEL, R)), full((R, D_MODEL)), full((1, D_MODEL)),
                  full((D_MODEL, R)), full((R, D_MODEL)), full((1, D_MODEL))],
        out_specs=[pl.BlockSpec((tm, D_MODEL), lambda i: (i, 0)),
                   pl.BlockSpec((tm, D_MODEL), lambda i: (i, 0))],
        out_shape=[jax.ShapeDtypeStruct((M, D_MODEL), F32), jax.ShapeDtypeStruct((M, D_MODEL), F32)],
        compiler_params=_cparams(("parallel",)),
        name="rwkv_lora",
    )(x, *prev_ops, mu2, w1, w2, w0.reshape(1, D_MODEL), a1, a2, a0.reshape(1, D_MODEL))


def _head_ones():
    r = lax.broadcasted_iota(jnp.int32, (LANES, LANES), 0) // RWKV_HEAD_DIM
    c = lax.broadcasted_iota(jnp.int32, (LANES, LANES), 1) // RWKV_HEAD_DIM
    e = (r == c).astype(BF16)
    return jnp.concatenate([e, e], axis=0)


def _segsum(x, e2):
    hi = x.astype(BF16)
    lo = (x - hi.astype(F32)).astype(BF16)
    return jnp.dot(jnp.concatenate([hi, lo], axis=1), e2, preferred_element_type=F32)


def _segsum_wide(x, e2):
    rows, n = x.shape[0], x.shape[1] // LANES
    s = _segsum(jnp.concatenate([x[:, i * LANES:(i + 1) * LANES] for i in range(n)], axis=0), e2)
    return jnp.concatenate([s[i * rows:(i + 1) * rows] for i in range(n)], axis=1)


RWKV_PAIRS = 16
RWKV_SEQS = 2
RWKV_CHUNK = 64


def _rwkv_chunk_kernel(r_ref, k_ref, v_ref, gt_ref, lw_ref, a_ref, kk_ref, ka_ref, rk_ref, gng_ref, gnb_ref, s0_ref,
                       o_ref, so_ref, st_ref, *, lreal):
    C, N = RWKV_CHUNK, RWKV_HEAD_DIM
    t_idx = pl.program_id(2)
    e2 = _head_ones()
    m0 = lax.broadcasted_iota(jnp.int32, (C, LANES), 1) < N
    ti = lax.broadcasted_iota(jnp.int32, (C, 2 * C), 0)
    si2 = lax.broadcasted_iota(jnp.int32, (C, 2 * C), 1)
    si = si2 % C
    strict, incl, left = si < ti, si <= ti, si2 < C
    bd = (lax.broadcasted_iota(jnp.int32, (LANES, LANES), 0) // N
          == lax.broadcasted_iota(jnp.int32, (LANES, LANES), 1) // N)
    zs = jnp.zeros((N, N), F32)

    nseq = o_ref.shape[0]

    @pl.when(t_idx == 0)
    def _():
        for bi in range(nseq):
            for p in range(RWKV_PAIRS):
                top = jnp.concatenate([s0_ref[bi, 2 * p], zs], axis=1)
                bot = jnp.concatenate([zs, s0_ref[bi, 2 * p + 1]], axis=1)
                st_ref[bi * RWKV_PAIRS + p] = jnp.concatenate([top, bot], axis=0)

    def both(x, y):
        parts = [x] if y is None else [x, y]
        return jnp.concatenate([jnp.where(m0, z, 0.0) for z in parts] + [jnp.where(m0, 0.0, z) for z in parts],
                               axis=0).astype(BF16)

    def load(x):
        if lreal < C:
            x = jnp.concatenate([x, jnp.zeros((C - lreal, x.shape[1]), F32)], axis=0)
        return x

    inv_n = 1.0 / N
    n_iter = max(1, (lreal - 1).bit_length())
    lanes = [slice(p * LANES, (p + 1) * LANES) for p in range(RWKV_PAIRS)]

    def prep(bi):
        r, k, v = load(r_ref[0, bi]), load(k_ref[0, bi]), load(v_ref[0, bi])
        lw, a = load(lw_ref[bi]), load(a_ref[bi])
        kk = k * kk_ref[...]
        kk = kk / jnp.maximum(jnp.sqrt(_segsum_wide(kk * kk, e2)), 1e-12)
        kp = k * (1.0 + (a - 1.0) * ka_ref[...])
        c = _cumsum_rows(lw, _tril3(C))
        e_c, e_nc = jnp.exp(c), jnp.exp(-c)
        return dict(v=v, e_c=e_c, at_w=-kk * jnp.exp(c - lw), rt_w=r * e_c,
                    bt_w=(kk * a * e_nc).astype(BF16), kt_w=(kp * e_nc).astype(BF16),
                    bonus=_segsum_wide((r * kp * rk_ref[...])[:lreal], e2))

    def scores(bi, q, p):
        ls = lanes[p]
        at, rt = q["at_w"][:, ls], q["rt_w"][:, ls]
        bk = jnp.concatenate([q["bt_w"][:, ls], q["kt_w"][:, ls]], axis=0)
        g = _nt(both(at, rt), bk)
        s_bd = st_ref[bi * RWKV_PAIRS + p]
        pq = _nt(jnp.concatenate([at, rt], axis=0).astype(BF16), s_bd.astype(BF16))
        return dict(bk=bk, s_bd=s_bd, g=g, pq=pq)

    def setup(q, p, d):
        g, pq = d.pop("g"), d.pop("pq")
        aa0, rr0 = jnp.where(strict, g[0:C], 0.0), jnp.where(incl, g[C:2 * C], 0.0)
        aa1, rr1 = jnp.where(strict, g[2 * C:3 * C], 0.0), jnp.where(incl, g[3 * C:], 0.0)
        a_ab = jnp.where(left, aa0, pltpu.roll(aa1, C, 1))
        a_ak = jnp.where(left, pltpu.roll(aa0, C, 1), aa1)
        x = pq[:C] + jnp.dot(a_ak.astype(BF16), both(q["v"][:, lanes[p]], None), preferred_element_type=F32)
        d.update(y0=pq[C:], rr=jnp.concatenate([rr0, rr1], axis=1).astype(BF16), x=x, ac=a_ab)

    def neumann(d, it):
        ac = d["ac"]
        ac16 = ac.astype(BF16)
        rhs = both(d["x"], None)
        if it < n_iter - 1:
            a_bd = jnp.concatenate([jnp.where(left, ac, 0.0), jnp.where(left, 0.0, ac)], axis=0).astype(BF16)
            res = jnp.dot(ac16, jnp.concatenate([rhs, a_bd], axis=1), preferred_element_type=F32)
            d["x"] = d["x"] + res[:, :LANES]
            d["ac"] = res[:, LANES:]
        else:
            d["x"] = d["x"] + jnp.dot(ac16, rhs, preferred_element_type=F32)

    def finish(q, p, d):
        ls = lanes[p]
        u, vp = d["x"], q["v"][:, ls]
        y = d["y0"] + jnp.dot(d["rr"], both(u, vp), preferred_element_type=F32)
        uv = jnp.concatenate([u, vp], axis=0).astype(BF16)
        ds = lax.dot_general(uv, d["bk"], (((0,), (0,)), ((), ())), preferred_element_type=F32)
        return (d["s_bd"] + jnp.where(bd, ds, 0.0)) * q["e_c"][C - 1:C, ls], y[:lreal]

    def output(bi, q, done):
        y = jnp.concatenate([yp for _, yp in done], axis=1)
        yc = y - _segsum_wide(y, e2) * inv_n
        yv = _segsum_wide(yc * yc, e2) * inv_n
        yn = yc * lax.rsqrt(yv + RWKV_GN_EPS) * gng_ref[...] + gnb_ref[...]
        return ((yn + q["bonus"] * q["v"][:lreal]) * gt_ref[0, bi]).astype(o_ref.dtype)

    pairs = range(RWKV_PAIRS)
    stages = [
        lambda bi, s: s.update(q=prep(bi)),
        lambda bi, s: s.update(work=[scores(bi, s["q"], p) for p in pairs]),
        lambda bi, s: [setup(s["q"], p, d) for p, d in enumerate(s["work"])],
        lambda bi, s: [neumann(d, it) for it in range(n_iter) for d in s["work"]],
        lambda bi, s: s.update(done=[finish(s["q"], p, d) for p, d in enumerate(s["work"])]),
        lambda bi, s: s.update(out=output(bi, s["q"], s["done"])),
    ]
    seqs = [dict() for _ in range(nseq)]
    lag = 1
    for tick in range(len(stages) + lag * (nseq - 1)):
        for bi in range(nseq):
            if 0 <= tick - lag * bi < len(stages):
                stages[tick - lag * bi](bi, seqs[bi])
    for bi, s in enumerate(seqs):
        for p, (s_new, _) in enumerate(s["done"]):
            st_ref[bi * RWKV_PAIRS + p] = s_new
        o_ref[bi] = s["out"]

    @pl.when(t_idx == pl.num_programs(2) - 1)
    def _():
        for bi in range(nseq):
            for p in range(RWKV_PAIRS):
                s = st_ref[bi * RWKV_PAIRS + p]
                so_ref[bi, 2 * p] = s[:N, :N]
                so_ref[bi, 2 * p + 1] = s[N:, N:]


def _rwkv_scan(proj, lw, a, k_k, k_a, r_k, gn_g, gn_b, states, j, B, L):
    tc = min(L, RWKV_CHUNK)
    nt = L // tc
    W = RWKV_PAIRS * LANES
    ng = D_MODEL // W
    hpg = 2 * RWKV_PAIRS
    ns = RWKV_SEQS
    assert B % ns == 0
    hd = RWKV_HEAD_DIM
    tok = lambda m: pl.BlockSpec((1, ns, tc, W), lambda b, g, t: (m, b, t, g))
    vec = pl.BlockSpec((ns, tc, W), lambda b, g, t: (b, t, g))
    par = pl.BlockSpec((1, W), lambda b, g, t: (0, g))
    st = pl.BlockSpec((ns, hpg, hd, hd), lambda b, g, t: (b, g, 0, 0))
    st0 = pl.BlockSpec((None, ns, hpg, hd, hd), lambda b, g, t: (j, b, g, 0, 0))
    proj4 = proj.reshape(4, B, L, D_MODEL)
    o, st_out = pl.pallas_call(
        functools.partial(_rwkv_chunk_kernel, lreal=tc),
        grid=(B // ns, ng, nt),
        in_specs=[tok(0), tok(1), tok(2), tok(3), vec, vec, par, par, par, par, par, st0],
        out_specs=[vec, st],
        out_shape=[jax.ShapeDtypeStruct((B, L, D_MODEL), BF16),
                   jax.ShapeDtypeStruct((B, RWKV_HEADS, hd, hd), F32)],
        scratch_shapes=[pltpu.VMEM((ns * RWKV_PAIRS, LANES, LANES), F32)],
        compiler_params=_cparams(("parallel", "parallel", "arbitrary")),
        name="rwkv_scan",
    )(proj4, proj4, proj4, proj4, lw.reshape(B, L, D_MODEL), a.reshape(B, L, D_MODEL), k_k.reshape(1, D_MODEL),
      k_a.reshape(1, D_MODEL), r_k.reshape(1, D_MODEL), gn_g.reshape(1, D_MODEL), gn_b.reshape(1, D_MODEL), states)
    return o.reshape(B * L, D_MODEL), st_out


def _rwkv_layer(x, shift, states, j, p, ln_g, ln_b, B, L):
    x3 = x.reshape(B, L, D_MODEL)
    if L % PROJ_ROWS == 0 and L % LORA_ROWS == 0:
        xprev = None
    else:
        xprev = jnp.concatenate([shift[:, None, :], x3[:, :-1]], axis=1).reshape(B * L, D_MODEL)
    proj = _rwkv_proj(x, xprev, shift, L, p["mu4"], p["w4"])
    lw, a = _rwkv_lora(x, xprev, shift, L, p["mu2"], p["w1"], p["w2"], p["w0"], p["a1"], p["a2"], p["a0"])
    o, st = _rwkv_scan(proj, lw, a, p["k_k"], p["k_a"], p["r_k"], p["gn_g"], p["gn_b"], states, j, B, L)
    return _mm_ln(o, p["w_out"], x, ln_g, ln_b), st, x3[:, -1]


def _pad_rank(w, axis):
    pad = [(0, 0), (0, 0)]
    pad[axis] = (0, LANES - w.shape[axis])
    return jnp.pad(w, pad).astype(BF16)


def _trunk(x3, cache, w):
    B, L, _ = x3.shape
    prompt = cache is None
    x = x3.reshape(B * L, D_MODEL)
    new_k, new_v, new_gla, new_wkv, new_shift = [], [], [], [], []
    for layer in range(DEPTH):
        kind, j = layer % 3, layer // 3
        g, b = w["ln_g"][layer], w["ln_b"][layer]
        if kind == 0:
            c = None if prompt else (cache["k"], cache["v"])
            x, nk, nv = _swa_layer(x, c, j, (w["swa_w_in"], j), w["swa_sink"][j], (w["swa_w_out"], j), g, b, B, L)
            new_k.append(nk)
            new_v.append(nv)
        elif kind == 1:
            st = None if prompt else cache["gla"]
            x, st = _gla_layer(x, st, j, (w["gla_w_in"], j), w["gla_w_low"][j], w["gla_w_a2"][j], w["gla_b_a"][j],
                               w["gla_norm_g"][j], (w["gla_w_out"], j), g, b, B, L)
            new_gla.append(st)
        else:
            if prompt:
                shift0 = jnp.zeros((B, D_MODEL), F32)
                s0, js = jnp.zeros((1, B, RWKV_HEADS, RWKV_HEAD_DIM, RWKV_HEAD_DIM), F32), 0
            else:
                shift0, s0, js = cache["shift"][j], cache["wkv"], j
            x, st, sh = _rwkv_layer(x, shift0, s0, js, w["rwkv"][j], g, b, B, L)
            new_wkv.append(st)
            new_shift.append(sh)
    stack = lambda parts: parts[0][None] if len(parts) == 1 else jnp.stack(parts)
    return (x.reshape(B, L, D_MODEL), stack(new_k), stack(new_v), stack(new_gla), stack(new_wkv), stack(new_shift))


def kernel(x_prompt, x_sample, cache_swa_k, cache_swa_v, state_gla, state_rwkv, state_rwkv_shift, ln_g, ln_b, swa_w_in, swa_sink, swa_w_out, gla_w_in, gla_w_a2, gla_b_a, gla_norm_g, gla_w_out, rwkv_mu, rwkv_w_rkvg, rwkv_w0, rwkv_w1, rwkv_w2, rwkv_a0, rwkv_a1, rwkv_a2, rwkv_k_k, rwkv_k_a, rwkv_r_k, rwkv_gn_g, rwkv_gn_b, rwkv_w_out):
    n_rwkv = rwkv_mu.shape[0]
    gla_main = 2 * GLA_KEY_DIM + 2 * GLA_VAL_DIM
    rwkv = []
    for j in range(n_rwkv):
        rwkv.append(dict(
            mu4=rwkv_mu[j][jnp.array([0, 2, 3, 5])].reshape(4, 1, D_MODEL),
            mu2=rwkv_mu[j][jnp.array([1, 4])].reshape(2, 1, D_MODEL),
            w4=rwkv_w_rkvg[j].astype(BF16),
            w1=_pad_rank(rwkv_w1[j], 1), w2=_pad_rank(rwkv_w2[j], 0), w0=rwkv_w0[j],
            a1=_pad_rank(rwkv_a1[j], 1), a2=_pad_rank(rwkv_a2[j], 0), a0=rwkv_a0[j],
            k_k=rwkv_k_k[j], k_a=rwkv_k_a[j], r_k=rwkv_r_k[j], gn_g=rwkv_gn_g[j], gn_b=rwkv_gn_b[j],
            w_out=rwkv_w_out[j].astype(BF16)))
    w = dict(ln_g=ln_g, ln_b=ln_b,
             swa_w_in=swa_w_in.astype(BF16), swa_sink=swa_sink, swa_w_out=swa_w_out.astype(BF16),
             gla_w_in=gla_w_in.astype(BF16),
             gla_w_low=[_pad_rank(gla_w_in[j][:, gla_main:], 1) for j in range(gla_w_in.shape[0])],
             gla_w_a2=[_pad_rank(gla_w_a2[j], 0) for j in range(gla_w_a2.shape[0])],
             gla_b_a=gla_b_a, gla_norm_g=gla_norm_g, gla_w_out=gla_w_out.astype(BF16), rwkv=rwkv)
    y_p, p_k, p_v, p_gla, p_wkv, p_shift = _trunk(x_prompt, None, w)
    cache = dict(k=cache_swa_k, v=cache_swa_v, gla=state_gla, wkv=state_rwkv, shift=state_rwkv_shift)
    y_s, s_k, s_v, s_gla, s_wkv, s_shift = _trunk(x_sample, cache, w)
    return (y_p, y_s, p_k, p_v, p_gla, p_wkv, p_shift, s_k, s_v, s_gla, s_wkv, s_shift)
```

```python
import functools
import math

import jax
import jax.numpy as jnp
from jax import lax
from jax.experimental import pallas as pl
from jax.experimental.pallas import tpu as pltpu

F32 = jnp.float32
BF16 = jnp.bfloat16

D_MODEL = 2048
DEPTH = 4
PAST_LEN = 16384
ALPHA = (2 * DEPTH) ** 0.25
LN_EPS = 1e-5

SWA_HEADS = 32
SWA_KV_HEADS = 4
SWA_GROUP = SWA_HEADS // SWA_KV_HEADS
SWA_HEAD_DIM = 64
SWA_WIDTH = SWA_HEADS * SWA_HEAD_DIM
SWA_KV_WIDTH = SWA_KV_HEADS * SWA_HEAD_DIM
WINDOW = 128
ROT_DIM = SWA_HEAD_DIM // 4
ROPE_THETA = 500000.0

GLA_HEADS = 4
GLA_KEY_DIM = D_MODEL // 2
GLA_VAL_DIM = D_MODEL
GLA_DK = GLA_KEY_DIM // GLA_HEADS
GLA_DV = GLA_VAL_DIM // GLA_HEADS
GLA_GATE_RANK = 16
GLA_GATE_TEMP = 16.0
GLA_CHUNK = 64
GLA_SUB = 16
GLA_SEQS = 2
GLA_NORM_EPS = 1e-5

RWKV_HEAD_DIM = 64
RWKV_HEADS = D_MODEL // RWKV_HEAD_DIM
RWKV_GN_EPS = 64e-5

LANES = 128
VMEM_LIMIT = 56 * 1024 * 1024
PROJ_ROWS = 1024
PROJ_COLS = 1536
LORA_ROWS = 512
OUT_ROWS = 512
LN_SUB_ROWS = 256
NEG_BIG = -1e30


def _cparams(sem):
    return pltpu.CompilerParams(dimension_semantics=sem, vmem_limit_bytes=VMEM_LIMIT)


def _silu(x):
    return x * (1.0 / (1.0 + jnp.exp(-x)))


def _log_sigmoid(z):
    return jnp.minimum(z, 0.0) - jnp.log(1.0 + jnp.exp(-jnp.abs(z)))


def _nt(a, b):
    return lax.dot_general(a, b, (((1,), (1,)), ((), ())), preferred_element_type=F32)


def _mm_kernel(x_ref, w_ref, *refs, plans):
    o_ref = refs[-1]

    def run(plan):
        acc = jnp.dot(x_ref[...].astype(BF16), w_ref[...], preferred_element_type=F32)
        if all(op is None for op in plan):
            o_ref[...] = acc
            return
        for c, op in enumerate(plan):
            chunk = acc[:, c * LANES:(c + 1) * LANES]
            if op == "silu":
                chunk = _silu(chunk)
            elif op in ("rope", "rope_q"):
                chunk = _rope128(chunk, refs[0][0], refs[0][1], refs[0][2])
                if op == "rope_q":
                    chunk = chunk * SWA_HEAD_DIM ** -0.5
            o_ref[:, c * LANES:(c + 1) * LANES] = chunk

    if len(set(plans)) == 1:
        run(plans[0])
    else:
        for t, plan in enumerate(plans):
            pl.when(pl.program_id(1) == t)(functools.partial(run, plan))


def _mm(x, w, ncols, tn=PROJ_COLS, rope_q_cols=0, rope_cols=0, silu_from=None, rope_tabs=None):
    M, K = x.shape
    tm = min(M, PROJ_ROWS)
    assert M % tm == 0 and ncols % tn == 0

    def op_of(col):
        if col < rope_q_cols:
            return "rope_q"
        if col < rope_cols:
            return "rope"
        return "silu" if silu_from is not None and col >= silu_from else None

    plans = tuple(tuple(op_of(t * tn + c * LANES) for c in range(tn // LANES)) for t in range(ncols // tn))
    w, layer = (w, None) if not isinstance(w, tuple) else w
    w_spec = (pl.BlockSpec((K, tn), lambda i, j: (0, j)) if layer is None else
              pl.BlockSpec((None, K, tn), lambda i, j: (layer, 0, j)))
    operands, specs = [x, w], [pl.BlockSpec((tm, K), lambda i, j: (i, 0)), w_spec]
    if rope_cols:
        period = rope_tabs.shape[1] // tm
        operands.append(rope_tabs)
        specs.append(pl.BlockSpec((3, tm, LANES), lambda i, j: (0, i % period, 0)))
    return pl.pallas_call(
        functools.partial(_mm_kernel, plans=plans),
        grid=(M // tm, ncols // tn),
        in_specs=specs,
        out_specs=pl.BlockSpec((tm, tn), lambda i, j: (i, j)),
        out_shape=jax.ShapeDtypeStruct((M, ncols), F32),
        compiler_params=_cparams(("parallel", "arbitrary")),
        name="proj_mm",
    )(*operands)


def _mm_ln_kernel(a_ref, w_ref, x_ref, g_ref, b_ref, o_ref):
    for r0 in range(0, a_ref.shape[0], LN_SUB_ROWS):
        rows = slice(r0, min(r0 + LN_SUB_ROWS, a_ref.shape[0]))
        h = jnp.dot(a_ref[rows, :], w_ref[...], preferred_element_type=F32)
        z = ALPHA * x_ref[rows, :] + h
        mu = jnp.mean(z, axis=-1, keepdims=True)
        zc = z - mu
        var = jnp.mean(zc * zc, axis=-1, keepdims=True)
        o_ref[rows, :] = zc * lax.rsqrt(var + LN_EPS) * g_ref[...] + b_ref[...]


def _mm_ln(a, w, x, g, b):
    M, K = a.shape
    tm = min(M, OUT_ROWS)
    w, layer = (w, None) if not isinstance(w, tuple) else w
    D = w.shape[-1]
    return pl.pallas_call(
        _mm_ln_kernel,
        grid=(M // tm,),
        in_specs=[pl.BlockSpec((tm, K), lambda i: (i, 0)),
                  (pl.BlockSpec((K, D), lambda i: (0, 0)) if layer is None else
                   pl.BlockSpec((None, K, D), lambda i: (layer, 0, 0))),
                  pl.BlockSpec((tm, D), lambda i: (i, 0)),
                  pl.BlockSpec((1, D), lambda i: (0, 0)),
                  pl.BlockSpec((1, D), lambda i: (0, 0))],
        out_specs=pl.BlockSpec((tm, D), lambda i: (i, 0)),
        out_shape=jax.ShapeDtypeStruct((M, D), F32),
        compiler_params=_cparams(("parallel",)),
        name="out_proj_ln",
    )(a, w, x, g.reshape(1, D), b.reshape(1, D))


def _rope_tables(pos):
    half = ROT_DIM // 2
    inv = ROPE_THETA ** (-(jnp.arange(half, dtype=F32) * 2.0 / ROT_DIM))
    ang = pos[:, None] * inv[None, :]
    cos, sin = jnp.cos(ang), jnp.sin(ang)
    L = pos.shape[0]
    ones = jnp.ones((L, SWA_HEAD_DIM - ROT_DIM), F32)
    zeros_r = jnp.zeros((L, SWA_HEAD_DIM - ROT_DIM), F32)
    zeros_h = jnp.zeros((L, half), F32)
    c = jnp.concatenate([cos, cos, ones], axis=1)
    s1 = jnp.concatenate([-sin, zeros_h, zeros_r], axis=1)
    s2 = jnp.concatenate([zeros_h, sin, zeros_r], axis=1)
    rep = LANES // SWA_HEAD_DIM
    return jnp.tile(c, (1, rep)), jnp.tile(s1, (1, rep)), jnp.tile(s2, (1, rep))


def _rope128(x, c, s1, s2):
    return x * c + pltpu.roll(x, LANES - ROT_DIM // 2, 1) * s1 + pltpu.roll(x, ROT_DIM // 2, 1) * s2


def _swa_kernel(sink_ref, q_ref, kc_ref, vc_ref, kp_ref, vp_ref, g0_ref, g1_ref, g2_ref, g3_ref,
                o_ref, ko_ref, vo_ref, *, lq, nseq, prompt):
    gate_refs = (g0_ref, g1_ref, g2_ref, g3_ref)
    ppt = SWA_GATE_TILE // LANES
    nk = 2 * WINDOW
    nh = SWA_GROUP
    qi = lax.broadcasted_iota(jnp.int32, (lq, nk), 0)
    sj = lax.broadcasted_iota(jnp.int32, (lq, nk), 1)
    rel = qi + WINDOW - sj
    band = jnp.logical_and(rel >= 0, rel <= WINDOW)
    band_first = jnp.logical_and(band, sj >= jnp.where(pl.program_id(1) == 0, WINDOW, 0)) if prompt else band
    m0 = lax.broadcasted_iota(jnp.int32, (lq, LANES), 1) < SWA_HEAD_DIM
    hd = SWA_HEAD_DIM
    ppd = nh // 2

    def attend(bi):
        rs = slice(bi * lq, (bi + 1) * lq)
        mask = band_first if bi == 0 else band
        k_cur, v_cur = kc_ref[rs, :], vc_ref[rs, :]
        if prompt:
            if bi == 0:
                k_prev, v_prev = kp_ref[...], vp_ref[...]
            else:
                k_prev, v_prev = kc_ref[(bi - 1) * lq:bi * lq, :], vc_ref[(bi - 1) * lq:bi * lq, :]
            k_out, v_out = k_cur, v_cur
        else:
            k_prev, v_prev = kp_ref[bi], vp_ref[bi]
            k_out = jnp.concatenate([k_prev[lq:], k_cur], axis=0)
            v_out = jnp.concatenate([v_prev[lq:], v_cur], axis=0)
            zpad = jnp.zeros((WINDOW - lq, SWA_KV_WIDTH), F32)
            k_cur = jnp.concatenate([k_cur, zpad], axis=0)
            v_cur = jnp.concatenate([v_cur, zpad], axis=0)
        k_all = jnp.concatenate([k_prev, k_cur], axis=0)
        v_all = jnp.concatenate([v_prev, v_cur], axis=0)
        return k_out, v_out, [kv_group(rs, mask, k_all, v_all, kh) for kh in range(SWA_KV_HEADS)]

    def kv_group(rs, mask, k_all, v_all, kh):
        kg = k_all[:, kh * hd:(kh + 1) * hd]
        vg = v_all[:, kh * hd:(kh + 1) * hd]
        kk2 = jnp.concatenate([kg, kg], axis=1).astype(BF16)
        vv2 = jnp.concatenate([vg, vg], axis=1).astype(BF16)
        p0 = kh * ppd
        pairs = range(p0, p0 + ppd)
        rows = []
        for p in pairs:
            q2 = q_ref[rs, p * LANES:(p + 1) * LANES]
            rows += [jnp.where(m0, q2, 0.0), jnp.where(m0, 0.0, q2)]
        qs = jnp.concatenate(rows, axis=0).astype(BF16)
        s_all = _nt(qs, kk2)
        yield None
        es, dens = [], []
        for i in range(2 * ppd):
            snk = sink_ref[2 * p0 + i]
            s = jnp.where(mask, s_all[i * lq:(i + 1) * lq], NEG_BIG)
            m = jnp.maximum(jnp.max(s, axis=-1, keepdims=True), snk)
            e = jnp.exp(s - m)
            dens.append(jnp.sum(e, axis=-1, keepdims=True) + jnp.exp(snk - m))
            es.append(e.astype(BF16))
        yield None
        o_all = jnp.dot(jnp.concatenate(es, axis=0), vv2, preferred_element_type=F32)
        outs = []
        for i, p in enumerate(pairs):
            oa = o_all[2 * i * lq:(2 * i + 1) * lq] / dens[2 * i]
            ob = o_all[(2 * i + 1) * lq:(2 * i + 2) * lq] / dens[2 * i + 1]
            g2 = gate_refs[p // ppt][rs, (p % ppt) * LANES:(p % ppt + 1) * LANES]
            outs.append((p, (jnp.where(m0, oa, ob) * g2).astype(o_ref.dtype)))
        yield outs

    seqs = [attend(bi) for bi in range(nseq)]
    groups = [g for _, _, gs in seqs for g in gs]
    if not prompt:
        for _ in range(2):
            for g in groups:
                next(g)
        done = [next(g) for g in groups]
    else:
        done = [list(g)[-1] for g in groups]
    for bi, (k_out, v_out, gs) in enumerate(seqs):
        if not prompt:
            ko_ref[bi], vo_ref[bi] = k_out, v_out
        elif bi == nseq - 1:
            ko_ref[0], vo_ref[0] = k_out, v_out
        for outs in done[bi * len(gs):(bi + 1) * len(gs)]:
            for p, out in outs:
                o_ref[bi * lq:(bi + 1) * lq, p * LANES:(p + 1) * LANES] = out


SWA_SAMPLE_SEQS = 4
SWA_PROMPT_BLOCKS = 2
SWA_GATE_TILE = 512


def _swa_gate_specs(rows, index):
    g0 = (SWA_WIDTH + 2 * SWA_KV_WIDTH) // SWA_GATE_TILE
    return [pl.BlockSpec((rows, SWA_GATE_TILE), functools.partial(index, g0 + t))
            for t in range(SWA_WIDTH // SWA_GATE_TILE)]


def _swa_prompt(h, sink, B, S):
    nblk = SWA_PROMPT_BLOCKS
    nb = S // (nblk * WINDOW)
    rows = nblk * WINDOW
    kcol = SWA_WIDTH // SWA_KV_WIDTH
    prev = lambda b, n: (b * nb + n) * nblk - jnp.minimum(n, 1)
    return pl.pallas_call(
        functools.partial(_swa_kernel, lq=WINDOW, nseq=nblk, prompt=True),
        grid=(B, nb),
        in_specs=[pl.BlockSpec(memory_space=pltpu.SMEM),
                  pl.BlockSpec((rows, SWA_WIDTH), lambda b, n: (b * nb + n, 0)),
                  pl.BlockSpec((rows, SWA_KV_WIDTH), lambda b, n: (b * nb + n, kcol)),
                  pl.BlockSpec((rows, SWA_KV_WIDTH), lambda b, n: (b * nb + n, kcol + 1)),
                  pl.BlockSpec((WINDOW, SWA_KV_WIDTH), lambda b, n: (prev(b, n), kcol)),
                  pl.BlockSpec((WINDOW, SWA_KV_WIDTH), lambda b, n: (prev(b, n), kcol + 1)),
                  *_swa_gate_specs(rows, lambda col, b, n: (b * nb + n, col))],
        out_specs=[pl.BlockSpec((rows, SWA_WIDTH), lambda b, n: (b * nb + n, 0)),
                   pl.BlockSpec((1, WINDOW, SWA_KV_WIDTH), lambda b, n: (b, 0, 0)),
                   pl.BlockSpec((1, WINDOW, SWA_KV_WIDTH), lambda b, n: (b, 0, 0))],
        out_shape=[jax.ShapeDtypeStruct((B * S, SWA_WIDTH), BF16),
                   jax.ShapeDtypeStruct((B, WINDOW, SWA_KV_WIDTH), F32),
                   jax.ShapeDtypeStruct((B, WINDOW, SWA_KV_WIDTH), F32)],
        compiler_params=_cparams(("parallel", "arbitrary")),
        name="swa_prompt",
    )(sink, h, h, h, h, h, h, h, h, h)


def _swa_sample(h, sink, cache_k, cache_v, j, B, L):
    ns = SWA_SAMPLE_SEQS
    assert B % ns == 0
    cache = pl.BlockSpec((None, ns, WINDOW, SWA_KV_WIDTH), lambda b: (j, b, 0, 0))
    kcol = SWA_WIDTH // SWA_KV_WIDTH
    return pl.pallas_call(
        functools.partial(_swa_kernel, lq=L, nseq=ns, prompt=False),
        grid=(B // ns,),
        in_specs=[pl.BlockSpec(memory_space=pltpu.SMEM),
                  pl.BlockSpec((ns * L, SWA_WIDTH), lambda b: (b, 0)),
                  pl.BlockSpec((ns * L, SWA_KV_WIDTH), lambda b: (b, kcol)),
                  pl.BlockSpec((ns * L, SWA_KV_WIDTH), lambda b: (b, kcol + 1)),
                  cache, cache,
                  *_swa_gate_specs(ns * L, lambda col, b: (b, col))],
        out_specs=[pl.BlockSpec((ns * L, SWA_WIDTH), lambda b: (b, 0)),
                   pl.BlockSpec((ns, WINDOW, SWA_KV_WIDTH), lambda b: (b, 0, 0)),
                   pl.BlockSpec((ns, WINDOW, SWA_KV_WIDTH), lambda b: (b, 0, 0))],
        out_shape=[jax.ShapeDtypeStruct((B * L, SWA_WIDTH), BF16),
                   jax.ShapeDtypeStruct((B, WINDOW, SWA_KV_WIDTH), F32),
                   jax.ShapeDtypeStruct((B, WINDOW, SWA_KV_WIDTH), F32)],
        compiler_params=_cparams(("parallel",)),
        name="swa_sample",
    )(sink, h, h, h, cache_k, cache_v, h, h, h, h)


def _swa_layer(x, cache, j, w_in, sink, w_out, ln_g, ln_b, B, L):
    first = 0 if cache is None else PAST_LEN
    rows = max(L, min(B * L, PROJ_ROWS))
    tabs = jnp.stack(_rope_tables(first + (jnp.arange(rows) % L).astype(F32)))
    h = _mm(x, w_in, 2 * SWA_WIDTH + 2 * SWA_KV_WIDTH, rope_q_cols=SWA_WIDTH, rope_cols=SWA_WIDTH + SWA_KV_WIDTH,
            silu_from=SWA_WIDTH + 2 * SWA_KV_WIDTH, rope_tabs=tabs)
    if cache is None:
        o, nk, nv = _swa_prompt(h, sink, B, L)
    else:
        ck = cache[0].reshape(-1, B, WINDOW, SWA_KV_WIDTH)
        cv = cache[1].reshape(-1, B, WINDOW, SWA_KV_WIDTH)
        o, nk, nv = _swa_sample(h, sink, ck, cv, j, B, L)
    shape = (B, WINDOW, SWA_KV_HEADS, SWA_HEAD_DIM)
    return _mm_ln(o, w_out, x, ln_g, ln_b), nk.reshape(shape), nv.reshape(shape)


def _gla_gate_kernel(x_ref, wl_ref, wa_ref, ba_ref, g_ref):
    a_low = jnp.dot(x_ref[...].astype(BF16), wl_ref[...], preferred_element_type=F32)
    z = jnp.dot(a_low.astype(BF16), wa_ref[...], preferred_element_type=F32) + ba_ref[...]
    g_ref[...] = _log_sigmoid(z) * (1.0 / GLA_GATE_TEMP)


def _gla_gate(x, w_low, w_a2, b_a):
    M = x.shape[0]
    tm = min(M, PROJ_ROWS)
    return pl.pallas_call(
        _gla_gate_kernel,
        grid=(M // tm,),
        in_specs=[pl.BlockSpec((tm, D_MODEL), lambda i: (i, 0)),
                  pl.BlockSpec((D_MODEL, LANES), lambda i: (0, 0)),
                  pl.BlockSpec((LANES, GLA_KEY_DIM), lambda i: (0, 0)),
                  pl.BlockSpec((1, GLA_KEY_DIM), lambda i: (0, 0))],
        out_specs=pl.BlockSpec((tm, GLA_KEY_DIM), lambda i: (i, 0)),
        out_shape=jax.ShapeDtypeStruct((M, GLA_KEY_DIM), F32),
        compiler_params=_cparams(("parallel",)),
        name="gla_gate",
    )(x, w_low, w_a2, b_a.reshape(1, GLA_KEY_DIM))


def _tril3(C):
    return (lax.broadcasted_iota(jnp.int32, (C, 3 * C), 1) % C
            <= lax.broadcasted_iota(jnp.int32, (C, 3 * C), 0)).astype(BF16)


def _cumsum_rows(x, tril3):
    h1 = x.astype(BF16)
    r1 = x - h1.astype(F32)
    h2 = r1.astype(BF16)
    h3 = (r1 - h2.astype(F32)).astype(BF16)
    return jnp.dot(tril3, jnp.concatenate([h1, h2, h3], axis=0), preferred_element_type=F32)


def _gla_chunk_kernel(q_ref, k_ref, v_ref, gate_ref, g_ref, ng_ref, s0_ref, o_ref, so_ref, st_ref, *, C, SB):
    c_idx = pl.program_id(1)
    H, DK, DV = GLA_HEADS, GLA_DK, GLA_DV
    nseq = o_ref.shape[0]

    @pl.when(c_idx == 0)
    def _():
        for bi in range(nseq):
            for h in range(H):
                st_ref[bi * H + h] = s0_ref[bi, h].T

    row = lax.broadcasted_iota(jnp.int32, (C, H * DK), 0)
    causal = lax.broadcasted_iota(jnp.int32, (C, C), 1) <= lax.broadcasted_iota(jnp.int32, (C, C), 0)
    tril3 = _tril3(C)

    def prep(bi):
        b = _cumsum_rows(g_ref[bi], tril3)
        q = q_ref[bi] * (DK ** -0.5)
        k = k_ref[bi]
        b_last = b[C - 1:C, :]
        qis, kjs = [], []
        for i in range(C // SB):
            r0, r1 = i * SB, (i + 1) * SB
            bn = b[r0:r0 + 1, :]
            qis.append((q[r0:r1] * jnp.exp(b[r0:r1] - bn)).astype(BF16))
            kjs.append((k * jnp.exp(jnp.where(row < r1, bn - b, 0.0))).astype(BF16))
        return dict(v16=v_ref[bi].astype(BF16), e_last=jnp.exp(b_last), qe=(q * jnp.exp(b)).astype(BF16),
                    kd=(k * jnp.exp(b_last - b)).astype(BF16), qis=qis, kjs=kjs)

    def head(bi, p, h):
        ks, vs = slice(h * DK, (h + 1) * DK), slice(h * DV, (h + 1) * DV)
        st = st_ref[bi * H + h]
        a_parts = [_nt(qi[:, ks], kj[:, ks]) for qi, kj in zip(p["qis"], p["kjs"])]
        v16 = p["v16"][:, vs]
        o_inter = _nt(p["qe"][:, ks], st.astype(BF16))
        ds = lax.dot_general(v16, p["kd"][:, ks], (((0,), (0,)), ((), ())), preferred_element_type=F32)
        yield None
        a = jnp.concatenate(a_parts, axis=0) if len(a_parts) > 1 else a_parts[0]
        a = jnp.where(causal, a, 0.0).astype(BF16)
        o = o_inter + jnp.dot(a, v16, preferred_element_type=F32)
        st_new = st * p["e_last"][:, ks] + ds
        yield None
        on = o * lax.rsqrt(jnp.mean(o * o, axis=-1, keepdims=True) + GLA_NORM_EPS) * ng_ref[...]
        yield st_new, (on * gate_ref[bi, :, vs]).astype(o_ref.dtype)

    preps = [prep(bi) for bi in range(nseq)]
    chains = [[head(bi, preps[bi], h) for h in range(H)] for bi in range(nseq)]
    if C == GLA_CHUNK:
        for _ in range(2):
            for per_seq in chains:
                for chain in per_seq:
                    next(chain)
        results = [[next(chain) for chain in per_seq] for per_seq in chains]
    else:
        results = [[list(chain)[-1] for chain in per_seq] for per_seq in chains]
    for bi in range(nseq):
        for h, (st_new, out) in enumerate(results[bi]):
            st_ref[bi * H + h] = st_new
            o_ref[bi, :, h * DV:(h + 1) * DV] = out

    @pl.when(c_idx == pl.num_programs(1) - 1)
    def _():
        for bi in range(nseq):
            for h in range(H):
                so_ref[bi, h] = st_ref[bi * H + h].T


def _gla_chunk(h, g, norm_g, states, j, B, L):
    C = min(GLA_CHUNK, L)
    SB = min(GLA_SUB, C)
    nc = L // C
    ns = GLA_SEQS
    assert B % ns == 0
    tok = lambda width, col: pl.BlockSpec((ns, C, width), lambda b, c: (b, c, col))
    st0 = pl.BlockSpec((None, ns, GLA_HEADS, GLA_DK, GLA_DV), lambda b, c: (j, b, 0, 0, 0))
    st = pl.BlockSpec((ns, GLA_HEADS, GLA_DK, GLA_DV), lambda b, c: (b, 0, 0, 0))
    h3 = h.reshape(B, L, h.shape[1])
    o, st_out = pl.pallas_call(
        functools.partial(_gla_chunk_kernel, C=C, SB=SB),
        grid=(B // ns, nc),
        in_specs=[tok(GLA_KEY_DIM, 0), tok(GLA_KEY_DIM, 1), tok(GLA_VAL_DIM, 1), tok(GLA_VAL_DIM, 2),
                  tok(GLA_KEY_DIM, 0),
                  pl.BlockSpec((1, GLA_DV), lambda b, c: (0, 0)),
                  st0],
        out_specs=[tok(GLA_VAL_DIM, 0), st],
        out_shape=[jax.ShapeDtypeStruct((B, L, GLA_VAL_DIM), BF16),
                   jax.ShapeDtypeStruct((B, GLA_HEADS, GLA_DK, GLA_DV), F32)],
        scratch_shapes=[pltpu.VMEM((ns * GLA_HEADS, GLA_DV, GLA_DK), F32)],
        compiler_params=_cparams(("parallel", "arbitrary")),
        name="gla_chunk",
    )(h3, h3, h3, h3, g.reshape(B, L, GLA_KEY_DIM), norm_g.reshape(1, GLA_DV), states)
    return o.reshape(B * L, GLA_VAL_DIM), st_out


def _gla_layer(x, states, j, w_in, w_low, w_a2, b_a, norm_g, w_out, ln_g, ln_b, B, L):
    h = _mm(x, w_in, 2 * GLA_KEY_DIM + 2 * GLA_VAL_DIM, silu_from=2 * GLA_KEY_DIM + GLA_VAL_DIM)
    g = _gla_gate(x, w_low, w_a2, b_a)
    if states is None:
        states, j = jnp.zeros((1, B, GLA_HEADS, GLA_DK, GLA_DV), F32), 0
    o, st = _gla_chunk(h, g, norm_g, states, j, B, L)
    return _mm_ln(o, w_out, x, ln_g, ln_b), st


SUBLANES = 8
RWKV_GATE_INDEX = 3


def _prev_rows(x, pa_ref, pb_ref, seq_len):
    if seq_len is None:
        return pa_ref[...]
    tm = x.shape[0]
    starts_seq = (pl.program_id(0) * tm) % seq_len == 0
    first = jnp.where(starts_seq, pb_ref[0], pa_ref[SUBLANES - 1:SUBLANES, :])
    rows = lax.broadcasted_iota(jnp.int32, x.shape, 0)
    return jnp.where(rows == 0, first, pltpu.roll(x, 1, 0))


def _prev_specs(x, xprev, shift, tm, seq_len, grid_rank):
    pad = (0,) * (grid_rank - 1)
    sh = shift.reshape(shift.shape[0], 1, D_MODEL)
    if xprev is not None:
        return (xprev, sh), [pl.BlockSpec((tm, D_MODEL), lambda i, *_: (i, 0)),
                             pl.BlockSpec((1, 1, D_MODEL), lambda i, *_: (0, 0, 0))], None
    assert seq_len % tm == 0
    per = tm // SUBLANES
    return (x, sh), [pl.BlockSpec((SUBLANES, D_MODEL), lambda i, *_: (jnp.maximum(i * per - 1, 0), 0)),
                     pl.BlockSpec((1, 1, D_MODEL), lambda i, *_: ((i * tm) // seq_len, 0, 0))], seq_len


def _rwkv_proj_kernel(x_ref, pa_ref, pb_ref, mu_ref, w_ref, o_ref, xm_ref, *, seq_len):
    def step(first_col, gate):
        if first_col:
            x = x_ref[...]
            xm = (x + (_prev_rows(x, pa_ref, pb_ref, seq_len) - x) * mu_ref[0]).astype(BF16)
            xm_ref[...] = xm
        else:
            xm = xm_ref[...]
        acc = jnp.dot(xm, w_ref[0], preferred_element_type=F32)
        o_ref[0] = _silu(acc) if gate else acc

    is_first = pl.program_id(2) == 0
    is_gate = pl.program_id(1) == RWKV_GATE_INDEX
    for first_col in (True, False):
        for gate in (True, False):
            here = jnp.logical_and(is_first == first_col, is_gate == gate)
            pl.when(here)(functools.partial(step, first_col, gate))


def _rwkv_proj(x, xprev, shift, seq_len, mu4, w4, tn=1024):
    M = x.shape[0]
    tm = min(M, PROJ_ROWS)
    prev_ops, prev_specs, inline_len = _prev_specs(x, xprev, shift, tm, seq_len, 3)
    return pl.pallas_call(
        functools.partial(_rwkv_proj_kernel, seq_len=inline_len),
        grid=(M // tm, 4, D_MODEL // tn),
        in_specs=[pl.BlockSpec((tm, D_MODEL), lambda i, m, j: (i, 0)),
                  *prev_specs,
                  pl.BlockSpec((1, 1, D_MODEL), lambda i, m, j: (m, 0, 0)),
                  pl.BlockSpec((1, D_MODEL, tn), lambda i, m, j: (m, 0, j))],
        out_specs=pl.BlockSpec((1, tm, tn), lambda i, m, j: (m, i, j)),
        out_shape=jax.ShapeDtypeStruct((4, M, D_MODEL), F32),
        scratch_shapes=[pltpu.VMEM((tm, D_MODEL), BF16)],
        compiler_params=_cparams(("parallel", "arbitrary", "arbitrary")),
        name="rwkv_proj",
    )(x, *prev_ops, mu4, w4)


def _rwkv_lora_kernel(x_ref, pa_ref, pb_ref, mu_ref, w1_ref, w2_ref, w0_ref, a1_ref, a2_ref, a0_ref, lw_ref, a_ref,
                      *, seq_len):
    x = x_ref[...]
    xx = _prev_rows(x, pa_ref, pb_ref, seq_len) - x
    xw = (x + xx * mu_ref[0]).astype(BF16)
    xa = (x + xx * mu_ref[1]).astype(BF16)
    t = jnp.tanh(jnp.dot(xw, w1_ref[...], preferred_element_type=F32))
    wl = w0_ref[...] + jnp.dot(t.astype(BF16), w2_ref[...], preferred_element_type=F32)
    lw_ref[...] = (-math.exp(-0.5)) / (1.0 + jnp.exp(-wl))
    al = jnp.dot(xa, a1_ref[...], preferred_element_type=F32)
    az = a0_ref[...] + jnp.dot(al.astype(BF16), a2_ref[...], preferred_element_type=F32)
    a_ref[...] = 1.0 / (1.0 + jnp.exp(-az))


def _rwkv_lora(x, xprev, shift, seq_len, mu2, w1, w2, w0, a1, a2, a0):
    M = x.shape[0]
    tm = min(M, LORA_ROWS)
    R = w1.shape[1]
    full = lambda shape: pl.BlockSpec(shape, lambda i: tuple(0 for _ in shape))
    prev_ops, prev_specs, inline_len = _prev_specs(x, xprev, shift, tm, seq_len, 1)
    return pl.pallas_call(
        functools.partial(_rwkv_lora_kernel, seq_len=inline_len),
        grid=(M // tm,),
        in_specs=[pl.BlockSpec((tm, D_MODEL), lambda i: (i, 0)),
                  *prev_specs,
                  full((2, 1, D_MODEL)), full((D_MODEL, R)), full((R, D_MODEL)), full((1, D_MODEL)),
                  full((D_MODEL, R)), full((R, D_MODEL)), full((1, D_MODEL))],
        out_specs=[pl.BlockSpec((tm, D_MODEL), lambda i: (i, 0)),
                   pl.BlockSpec((tm, D_MODEL), lambda i: (i, 0))],
        out_shape=[jax.ShapeDtypeStruct((M, D_MODEL), F32), jax.ShapeDtypeStruct((M, D_MODEL), F32)],
        compiler_params=_cparams(("parallel",)),
        name="rwkv_lora",
    )(x, *prev_ops, mu2, w1, w2, w0.reshape(1, D_MODEL), a1, a2, a0.reshape(1, D_MODEL))


def _head_ones():
    r = lax.broadcasted_iota(jnp.int32, (LANES, LANES), 0) // RWKV_HEAD_DIM
    c = lax.broadcasted_iota(jnp.int32, (LANES, LANES), 1) // RWKV_HEAD_DIM
    e = (r == c).astype(BF16)
    return jnp.concatenate([e, e], axis=0)


def _segsum(x, e2):
    hi = x.astype(BF16)
    lo = (x - hi.astype(F32)).astype(BF16)
    return jnp.dot(jnp.concatenate([hi, lo], axis=1), e2, preferred_element_type=F32)


def _segsum_wide(x, e2):
    rows, n = x.shape[0], x.shape[1] // LANES
    s = _segsum(jnp.concatenate([x[:, i * LANES:(i + 1) * LANES] for i in range(n)], axis=0), e2)
    return jnp.concatenate([s[i * rows:(i + 1) * rows] for i in range(n)], axis=1)


RWKV_PAIRS = 16
RWKV_SEQS = 2
RWKV_CHUNK = 64


def _rwkv_chunk_kernel(r_ref, k_ref, v_ref, gt_ref, lw_ref, a_ref, kk_ref, ka_ref, rk_ref, gng_ref, gnb_ref, s0_ref,
                       o_ref, so_ref, st_ref, *, lreal):
    C, N = RWKV_CHUNK, RWKV_HEAD_DIM
    t_idx = pl.program_id(2)
    e2 = _head_ones()
    m0 = lax.broadcasted_iota(jnp.int32, (C, LANES), 1) < N
    ti = lax.broadcasted_iota(jnp.int32, (C, 2 * C), 0)
    si2 = lax.broadcasted_iota(jnp.int32, (C, 2 * C), 1)
    si = si2 % C
    strict, incl, left = si < ti, si <= ti, si2 < C
    bd = (lax.broadcasted_iota(jnp.int32, (LANES, LANES), 0) // N
          == lax.broadcasted_iota(jnp.int32, (LANES, LANES), 1) // N)
    zs = jnp.zeros((N, N), F32)

    nseq = o_ref.shape[0]

    @pl.when(t_idx == 0)
    def _():
        for bi in range(nseq):
            for p in range(RWKV_PAIRS):
                top = jnp.concatenate([s0_ref[bi, 2 * p], zs], axis=1)
                bot = jnp.concatenate([zs, s0_ref[bi, 2 * p + 1]], axis=1)
                st_ref[bi * RWKV_PAIRS + p] = jnp.concatenate([top, bot], axis=0)

    def both(x, y):
        parts = [x] if y is None else [x, y]
        return jnp.concatenate([jnp.where(m0, z, 0.0) for z in parts] + [jnp.where(m0, 0.0, z) for z in parts],
                               axis=0).astype(BF16)

    def load(x):
        if lreal < C:
            x = jnp.concatenate([x, jnp.zeros((C - lreal, x.shape[1]), F32)], axis=0)
        return x

    inv_n = 1.0 / N
    n_iter = max(1, (lreal - 1).bit_length())
    lanes = [slice(p * LANES, (p + 1) * LANES) for p in range(RWKV_PAIRS)]

    def prep(bi):
        r, k, v = load(r_ref[0, bi]), load(k_ref[0, bi]), load(v_ref[0, bi])
        lw, a = load(lw_ref[bi]), load(a_ref[bi])
        kk = k * kk_ref[...]
        kk = kk / jnp.maximum(jnp.sqrt(_segsum_wide(kk * kk, e2)), 1e-12)
        kp = k * (1.0 + (a - 1.0) * ka_ref[...])
        c = _cumsum_rows(lw, _tril3(C))
        e_c, e_nc = jnp.exp(c), jnp.exp(-c)
        return dict(v=v, e_c=e_c, at_w=-kk * jnp.exp(c - lw), rt_w=r * e_c,
                    bt_w=(kk * a * e_nc).astype(BF16), kt_w=(kp * e_nc).astype(BF16),
                    bonus=_segsum_wide((r * kp * rk_ref[...])[:lreal], e2))

    def scores(bi, q, p):
        ls = lanes[p]
        at, rt = q["at_w"][:, ls], q["rt_w"][:, ls]
        bk = jnp.concatenate([q["bt_w"][:, ls], q["kt_w"][:, ls]], axis=0)
        g = _nt(both(at, rt), bk)
        s_bd = st_ref[bi * RWKV_PAIRS + p]
        pq = _nt(jnp.concatenate([at, rt], axis=0).astype(BF16), s_bd.astype(BF16))
        return dict(bk=bk, s_bd=s_bd, g=g, pq=pq)

    def setup(q, p, d):
        g, pq = d.pop("g"), d.pop("pq")
        aa0, rr0 = jnp.where(strict, g[0:C], 0.0), jnp.where(incl, g[C:2 * C], 0.0)
        aa1, rr1 = jnp.where(strict, g[2 * C:3 * C], 0.0), jnp.where(incl, g[3 * C:], 0.0)
        a_ab = jnp.where(left, aa0, pltpu.roll(aa1, C, 1))
        a_ak = jnp.where(left, pltpu.roll(aa0, C, 1), aa1)
        x = pq[:C] + jnp.dot(a_ak.astype(BF16), both(q["v"][:, lanes[p]], None), preferred_element_type=F32)
        d.update(y0=pq[C:], rr=jnp.concatenate([rr0, rr1], axis=1).astype(BF16), x=x, ac=a_ab)

    def neumann(d, it):
        ac = d["ac"]
        ac16 = ac.astype(BF16)
        rhs = both(d["x"], None)
        if it < n_iter - 1:
            a_bd = jnp.concatenate([jnp.where(left, ac, 0.0), jnp.where(left, 0.0, ac)], axis=0).astype(BF16)
            res = jnp.dot(ac16, jnp.concatenate([rhs, a_bd], axis=1), preferred_element_type=F32)
            d["x"] = d["x"] + res[:, :LANES]
            d["ac"] = res[:, LANES:]
        else:
            d["x"] = d["x"] + jnp.dot(ac16, rhs, preferred_element_type=F32)

    def finish(q, p, d):
        ls = lanes[p]
        u, vp = d["x"], q["v"][:, ls]
        y = d["y0"] + jnp.dot(d["rr"], both(u, vp), preferred_element_type=F32)
        uv = jnp.concatenate([u, vp], axis=0).astype(BF16)
        ds = lax.dot_general(uv, d["bk"], (((0,), (0,)), ((), ())), preferred_element_type=F32)
        return (d["s_bd"] + jnp.where(bd, ds, 0.0)) * q["e_c"][C - 1:C, ls], y[:lreal]

    def output(bi, q, done):
        y = jnp.concatenate([yp for _, yp in done], axis=1)
        yc = y - _segsum_wide(y, e2) * inv_n
        yv = _segsum_wide(yc * yc, e2) * inv_n
        yn = yc * lax.rsqrt(yv + RWKV_GN_EPS) * gng_ref[...] + gnb_ref[...]
        return ((yn + q["bonus"] * q["v"][:lreal]) * gt_ref[0, bi]).astype(o_ref.dtype)

    pairs = range(RWKV_PAIRS)
    stages = [
        lambda bi, s: s.update(q=prep(bi)),
        lambda bi, s: s.update(work=[scores(bi, s["q"], p) for p in pairs]),
        lambda bi, s: [setup(s["q"], p, d) for p, d in enumerate(s["work"])],
        lambda bi, s: [neumann(d, it) for it in range(n_iter) for d in s["work"]],
        lambda bi, s: s.update(done=[finish(s["q"], p, d) for p, d in enumerate(s["work"])]),
        lambda bi, s: s.update(out=output(bi, s["q"], s["done"])),
    ]
    seqs = [dict() for _ in range(nseq)]
    lag = 1
    for tick in range(len(stages) + lag * (nseq - 1)):
        for bi in range(nseq):
            if 0 <= tick - lag * bi < len(stages):
                stages[tick - lag * bi](bi, seqs[bi])
    for bi, s in enumerate(seqs):
        for p, (s_new, _) in enumerate(s["done"]):
            st_ref[bi * RWKV_PAIRS + p] = s_new
        o_ref[bi] = s["out"]

    @pl.when(t_idx == pl.num_programs(2) - 1)
    def _():
        for bi in range(nseq):
            for p in range(RWKV_PAIRS):
                s = st_ref[bi * RWKV_PAIRS + p]
                so_ref[bi, 2 * p] = s[:N, :N]
                so_ref[bi, 2 * p + 1] = s[N:, N:]


def _rwkv_scan(proj, lw, a, k_k, k_a, r_k, gn_g, gn_b, states, j, B, L):
    tc = min(L, RWKV_CHUNK)
    nt = L // tc
    W = RWKV_PAIRS * LANES
    ng = D_MODEL // W
    hpg = 2 * RWKV_PAIRS
    ns = RWKV_SEQS
    assert B % ns == 0
    hd = RWKV_HEAD_DIM
    tok = lambda m: pl.BlockSpec((1, ns, tc, W), lambda b, g, t: (m, b, t, g))
    vec = pl.BlockSpec((ns, tc, W), lambda b, g, t: (b, t, g))
    par = pl.BlockSpec((1, W), lambda b, g, t: (0, g))
    st = pl.BlockSpec((ns, hpg, hd, hd), lambda b, g, t: (b, g, 0, 0))
    st0 = pl.BlockSpec((None, ns, hpg, hd, hd), lambda b, g, t: (j, b, g, 0, 0))
    proj4 = proj.reshape(4, B, L, D_MODEL)
    o, st_out = pl.pallas_call(
        functools.partial(_rwkv_chunk_kernel, lreal=tc),
        grid=(B // ns, ng, nt),
        in_specs=[tok(0), tok(1), tok(2), tok(3), vec, vec, par, par, par, par, par, st0],
        out_specs=[vec, st],
        out_shape=[jax.ShapeDtypeStruct((B, L, D_MODEL), BF16),
                   jax.ShapeDtypeStruct((B, RWKV_HEADS, hd, hd), F32)],
        scratch_shapes=[pltpu.VMEM((ns * RWKV_PAIRS, LANES, LANES), F32)],
        compiler_params=_cparams(("parallel", "parallel", "arbitrary")),
        name="rwkv_scan",
    )(proj4, proj4, proj4, proj4, lw.reshape(B, L, D_MODEL), a.reshape(B, L, D_MODEL), k_k.reshape(1, D_MODEL),
      k_a.reshape(1, D_MODEL), r_k.reshape(1, D_MODEL), gn_g.reshape(1, D_MODEL), gn_b.reshape(1, D_MODEL), states)
    return o.reshape(B * L, D_MODEL), st_out


def _rwkv_layer(x, shift, states, j, p, ln_g, ln_b, B, L):
    x3 = x.reshape(B, L, D_MODEL)
    if L % PROJ_ROWS == 0 and L % LORA_ROWS == 0:
        xprev = None
    else:
        xprev = jnp.concatenate([shift[:, None, :], x3[:, :-1]], axis=1).reshape(B * L, D_MODEL)
    proj = _rwkv_proj(x, xprev, shift, L, p["mu4"], p["w4"])
    lw, a = _rwkv_lora(x, xprev, shift, L, p["mu2"], p["w1"], p["w2"], p["w0"], p["a1"], p["a2"], p["a0"])
    o, st = _rwkv_scan(proj, lw, a, p["k_k"], p["k_a"], p["r_k"], p["gn_g"], p["gn_b"], states, j, B, L)
    return _mm_ln(o, p["w_out"], x, ln_g, ln_b), st, x3[:, -1]


def _pad_rank(w, axis):
    pad = [(0, 0), (0, 0)]
    pad[axis] = (0, LANES - w.shape[axis])
    return jnp.pad(w, pad).astype(BF16)


def _trunk(x3, cache, w):
    B, L, _ = x3.shape
    prompt = cache is None
    x = x3.reshape(B * L, D_MODEL)
    new_k, new_v, new_gla, new_wkv, new_shift = [], [], [], [], []
    for layer in range(DEPTH):
        kind, j = layer % 3, layer // 3
        g, b = w["ln_g"][layer], w["ln_b"][layer]
        if kind == 0:
            c = None if prompt else (cache["k"], cache["v"])
            x, nk, nv = _swa_layer(x, c, j, (w["swa_w_in"], j), w["swa_sink"][j], (w["swa_w_out"], j), g, b, B, L)
            new_k.append(nk)
            new_v.append(nv)
        elif kind == 1:
            st = None if prompt else cache["gla"]
            x, st = _gla_layer(x, st, j, (w["gla_w_in"], j), w["gla_w_low"][j], w["gla_w_a2"][j], w["gla_b_a"][j],
                               w["gla_norm_g"][j], (w["gla_w_out"], j), g, b, B, L)
            new_gla.append(st)
        else:
            if prompt:
                shift0 = jnp.zeros((B, D_MODEL), F32)
                s0, js = jnp.zeros((1, B, RWKV_HEADS, RWKV_HEAD_DIM, RWKV_HEAD_DIM), F32), 0
            else:
                shift0, s0, js = cache["shift"][j], cache["wkv"], j
            x, st, sh = _rwkv_layer(x, shift0, s0, js, w["rwkv"][j], g, b, B, L)
            new_wkv.append(st)
            new_shift.append(sh)
    stack = lambda parts: parts[0][None] if len(parts) == 1 else jnp.stack(parts)
    return (x.reshape(B, L, D_MODEL), stack(new_k), stack(new_v), stack(new_gla), stack(new_wkv), stack(new_shift))


def kernel(x_prompt, x_sample, cache_swa_k, cache_swa_v, state_gla, state_rwkv, state_rwkv_shift, ln_g, ln_b, swa_w_in, swa_sink, swa_w_out, gla_w_in, gla_w_a2, gla_b_a, gla_norm_g, gla_w_out, rwkv_mu, rwkv_w_rkvg, rwkv_w0, rwkv_w1, rwkv_w2, rwkv_a0, rwkv_a1, rwkv_a2, rwkv_k_k, rwkv_k_a, rwkv_r_k, rwkv_gn_g, rwkv_gn_b, rwkv_w_out):
    n_rwkv = rwkv_mu.shape[0]
    gla_main = 2 * GLA_KEY_DIM + 2 * GLA_VAL_DIM
    rwkv = []
    for j in range(n_rwkv):
        rwkv.append(dict(
            mu4=rwkv_mu[j][jnp.array([0, 2, 3, 5])].reshape(4, 1, D_MODEL),
            mu2=rwkv_mu[j][jnp.array([1, 4])].reshape(2, 1, D_MODEL),
            w4=rwkv_w_rkvg[j].astype(BF16),
            w1=_pad_rank(rwkv_w1[j], 1), w2=_pad_rank(rwkv_w2[j], 0), w0=rwkv_w0[j],
            a1=_pad_rank(rwkv_a1[j], 1), a2=_pad_rank(rwkv_a2[j], 0), a0=rwkv_a0[j],
            k_k=rwkv_k_k[j], k_a=rwkv_k_a[j], r_k=rwkv_r_k[j], gn_g=rwkv_gn_g[j], gn_b=rwkv_gn_b[j],
            w_out=rwkv_w_out[j].astype(BF16)))
    w = dict(ln_g=ln_g, ln_b=ln_b,
             swa_w_in=swa_w_in.astype(BF16), swa_sink=swa_sink, swa_w_out=swa_w_out.astype(BF16),
             gla_w_in=gla_w_in.astype(BF16),
             gla_w_low=[_pad_rank(gla_w_in[j][:, gla_main:], 1) for j in range(gla_w_in.shape[0])],
             gla_w_a2=[_pad_rank(gla_w_a2[j], 0) for j in range(gla_w_a2.shape[0])],
             gla_b_a=gla_b_a, gla_norm_g=gla_norm_g, gla_w_out=gla_w_out.astype(BF16), rwkv=rwkv)
    y_p, p_k, p_v, p_gla, p_wkv, p_shift = _trunk(x_prompt, None, w)
    cache = dict(k=cache_swa_k, v=cache_swa_v, gla=state_gla, wkv=state_rwkv, shift=state_rwkv_shift)
    y_s, s_k, s_v, s_gla, s_wkv, s_shift = _trunk(x_sample, cache, w)
    return (y_p, y_s, p_k, p_v, p_gla, p_wkv, p_shift, s_k, s_v, s_gla, s_wkv, s_shift)
```

```python
import functools
import math

import jax
import jax.numpy as jnp
from jax import lax
from jax.experimental import pallas as pl
from jax.experimental.pallas import tpu as pltpu

F32 = jnp.float32
BF16 = jnp.bfloat16

D_MODEL = 2048
DEPTH = 4
PAST_LEN = 16384
ALPHA = (2 * DEPTH) ** 0.25
LN_EPS = 1e-5

SWA_HEADS = 32
SWA_KV_HEADS = 4
SWA_GROUP = SWA_HEADS // SWA_KV_HEADS
SWA_HEAD_DIM = 64
SWA_WIDTH = SWA_HEADS * SWA_HEAD_DIM
SWA_KV_WIDTH = SWA_KV_HEADS * SWA_HEAD_DIM
WINDOW = 128
ROT_DIM = SWA_HEAD_DIM // 4
ROPE_THETA = 500000.0

GLA_HEADS = 4
GLA_KEY_DIM = D_MODEL // 2
GLA_VAL_DIM = D_MODEL
GLA_DK = GLA_KEY_DIM // GLA_HEADS
GLA_DV = GLA_VAL_DIM // GLA_HEADS
GLA_GATE_RANK = 16
GLA_GATE_TEMP = 16.0
GLA_CHUNK = 64
GLA_SUB = 16
GLA_SEQS = 4
GLA_NORM_EPS = 1e-5

RWKV_HEAD_DIM = 64
RWKV_HEADS = D_MODEL // RWKV_HEAD_DIM
RWKV_GN_EPS = 64e-5

LANES = 128
VMEM_LIMIT = 56 * 1024 * 1024
PROJ_ROWS = 1024
PROJ_COLS = 1536
LORA_ROWS = 512
OUT_ROWS = 512
LN_SUB_ROWS = 256
NEG_BIG = -1e30


def _cparams(sem):
    return pltpu.CompilerParams(dimension_semantics=sem, vmem_limit_bytes=VMEM_LIMIT)


def _silu(x):
    return x * (1.0 / (1.0 + jnp.exp(-x)))


def _log_sigmoid(z):
    return jnp.minimum(z, 0.0) - jnp.log(1.0 + jnp.exp(-jnp.abs(z)))


def _nt(a, b):
    return lax.dot_general(a, b, (((1,), (1,)), ((), ())), preferred_element_type=F32)


def _mm_kernel(x_ref, w_ref, *refs, plans, has_tabs, has_side):
    refs = list(refs)
    tab_ref = refs.pop(0) if has_tabs else None
    side_w_ref = refs.pop(0) if has_side else None
    o_ref = refs.pop(0)
    side_o_ref = refs.pop(0) if has_side else None

    def run(t, plan):
        x16 = x_ref[...].astype(BF16)
        acc = jnp.dot(x16, w_ref[...], preferred_element_type=F32)
        if has_side and t == 0:
            side_o_ref[...] = jnp.dot(x16, side_w_ref[...], preferred_element_type=F32)
        if all(op is None for op in plan):
            o_ref[...] = acc
            return
        for c, op in enumerate(plan):
            chunk = acc[:, c * LANES:(c + 1) * LANES]
            if op == "silu":
                chunk = _silu(chunk)
            elif op in ("rope", "rope_q"):
                chunk = _rope128(chunk, tab_ref[0], tab_ref[1], tab_ref[2])
                if op == "rope_q":
                    chunk = chunk * SWA_HEAD_DIM ** -0.5
            o_ref[:, c * LANES:(c + 1) * LANES] = chunk

    if len(set(plans)) == 1 and not has_side:
        run(0, plans[0])
    else:
        for t, plan in enumerate(plans):
            pl.when(pl.program_id(1) == t)(functools.partial(run, t, plan))


def _mm(x, w, ncols, tn=PROJ_COLS, rope_q_cols=0, rope_cols=0, silu_from=None, rope_tabs=None, side_w=None):
    M, K = x.shape
    tm = min(M, PROJ_ROWS)
    assert M % tm == 0 and ncols % tn == 0

    def op_of(col):
        if col < rope_q_cols:
            return "rope_q"
        if col < rope_cols:
            return "rope"
        return "silu" if silu_from is not None and col >= silu_from else None

    plans = tuple(tuple(op_of(t * tn + c * LANES) for c in range(tn // LANES)) for t in range(ncols // tn))
    w, layer = (w, None) if not isinstance(w, tuple) else w
    w_spec = (pl.BlockSpec((K, tn), lambda i, j: (0, j)) if layer is None else
              pl.BlockSpec((None, K, tn), lambda i, j: (layer, 0, j)))
    operands, specs = [x, w], [pl.BlockSpec((tm, K), lambda i, j: (i, 0)), w_spec]
    if rope_cols:
        period = rope_tabs.shape[1] // tm
        operands.append(rope_tabs)
        specs.append(pl.BlockSpec((3, tm, LANES), lambda i, j: (0, i % period, 0)))
    out_specs = [pl.BlockSpec((tm, tn), lambda i, j: (i, j))]
    out_shape = [jax.ShapeDtypeStruct((M, ncols), F32)]
    if side_w is not None:
        operands.append(side_w)
        specs.append(pl.BlockSpec((K, LANES), lambda i, j: (0, 0)))
        out_specs.append(pl.BlockSpec((tm, LANES), lambda i, j: (i, 0)))
        out_shape.append(jax.ShapeDtypeStruct((M, LANES), F32))
    outs = pl.pallas_call(
        functools.partial(_mm_kernel, plans=plans, has_tabs=bool(rope_cols), has_side=side_w is not None),
        grid=(M // tm, ncols // tn),
        in_specs=specs,
        out_specs=out_specs,
        out_shape=out_shape,
        compiler_params=_cparams(("parallel", "arbitrary")),
        name="proj_mm",
    )(*operands)
    return outs[0] if side_w is None else tuple(outs)


def _mm_ln_kernel(a_ref, w_ref, x_ref, g_ref, b_ref, o_ref):
    for r0 in range(0, a_ref.shape[0], LN_SUB_ROWS):
        rows = slice(r0, min(r0 + LN_SUB_ROWS, a_ref.shape[0]))
        h = jnp.dot(a_ref[rows, :], w_ref[...], preferred_element_type=F32)
        z = ALPHA * x_ref[rows, :] + h
        mu = jnp.mean(z, axis=-1, keepdims=True)
        zc = z - mu
        var = jnp.mean(zc * zc, axis=-1, keepdims=True)
        o_ref[rows, :] = zc * lax.rsqrt(var + LN_EPS) * g_ref[...] + b_ref[...]


def _mm_ln(a, w, x, g, b):
    M, K = a.shape
    tm = min(M, OUT_ROWS)
    w, layer = (w, None) if not isinstance(w, tuple) else w
    D = w.shape[-1]
    return pl.pallas_call(
        _mm_ln_kernel,
        grid=(M // tm,),
        in_specs=[pl.BlockSpec((tm, K), lambda i: (i, 0)),
                  (pl.BlockSpec((K, D), lambda i: (0, 0)) if layer is None else
                   pl.BlockSpec((None, K, D), lambda i: (layer, 0, 0))),
                  pl.BlockSpec((tm, D), lambda i: (i, 0)),
                  pl.BlockSpec((1, D), lambda i: (0, 0)),
                  pl.BlockSpec((1, D), lambda i: (0, 0))],
        out_specs=pl.BlockSpec((tm, D), lambda i: (i, 0)),
        out_shape=jax.ShapeDtypeStruct((M, D), F32),
        compiler_params=_cparams(("parallel",)),
        name="out_proj_ln",
    )(a, w, x, g.reshape(1, D), b.reshape(1, D))


def _rope_tables(pos):
    half = ROT_DIM // 2
    inv = ROPE_THETA ** (-(jnp.arange(half, dtype=F32) * 2.0 / ROT_DIM))
    ang = pos[:, None] * inv[None, :]
    cos, sin = jnp.cos(ang), jnp.sin(ang)
    L = pos.shape[0]
    ones = jnp.ones((L, SWA_HEAD_DIM - ROT_DIM), F32)
    zeros_r = jnp.zeros((L, SWA_HEAD_DIM - ROT_DIM), F32)
    zeros_h = jnp.zeros((L, half), F32)
    c = jnp.concatenate([cos, cos, ones], axis=1)
    s1 = jnp.concatenate([-sin, zeros_h, zeros_r], axis=1)
    s2 = jnp.concatenate([zeros_h, sin, zeros_r], axis=1)
    rep = LANES // SWA_HEAD_DIM
    return jnp.tile(c, (1, rep)), jnp.tile(s1, (1, rep)), jnp.tile(s2, (1, rep))


def _rope128(x, c, s1, s2):
    return x * c + pltpu.roll(x, LANES - ROT_DIM // 2, 1) * s1 + pltpu.roll(x, ROT_DIM // 2, 1) * s2


def _swa_kernel(sink_ref, q_ref, kc_ref, vc_ref, kp_ref, vp_ref, g0_ref, g1_ref, g2_ref, g3_ref,
                o_ref, ko_ref, vo_ref, *, lq, nseq, prompt):
    gate_refs = (g0_ref, g1_ref, g2_ref, g3_ref)
    ppt = SWA_GATE_TILE // LANES
    nk = 2 * WINDOW
    nh = SWA_GROUP
    qi = lax.broadcasted_iota(jnp.int32, (lq, nk), 0)
    sj = lax.broadcasted_iota(jnp.int32, (lq, nk), 1)
    rel = qi + WINDOW - sj
    band = jnp.logical_and(rel >= 0, rel <= WINDOW)
    band_first = jnp.logical_and(band, sj >= jnp.where(pl.program_id(1) == 0, WINDOW, 0)) if prompt else band
    m0 = lax.broadcasted_iota(jnp.int32, (lq, LANES), 1) < SWA_HEAD_DIM
    hd = SWA_HEAD_DIM
    ppd = nh // 2

    def attend(bi):
        rs = slice(bi * lq, (bi + 1) * lq)
        mask = band_first if bi == 0 else band
        k_cur, v_cur = kc_ref[rs, :], vc_ref[rs, :]
        if prompt:
            if bi == 0:
                k_prev, v_prev = kp_ref[...], vp_ref[...]
            else:
                k_prev, v_prev = kc_ref[(bi - 1) * lq:bi * lq, :], vc_ref[(bi - 1) * lq:bi * lq, :]
            k_out, v_out = k_cur, v_cur
        else:
            k_prev, v_prev = kp_ref[bi], vp_ref[bi]
            k_out = jnp.concatenate([k_prev[lq:], k_cur], axis=0)
            v_out = jnp.concatenate([v_prev[lq:], v_cur], axis=0)
            zpad = jnp.zeros((WINDOW - lq, SWA_KV_WIDTH), F32)
            k_cur = jnp.concatenate([k_cur, zpad], axis=0)
            v_cur = jnp.concatenate([v_cur, zpad], axis=0)
        k_all = jnp.concatenate([k_prev, k_cur], axis=0)
        v_all = jnp.concatenate([v_prev, v_cur], axis=0)
        return k_out, v_out, [kv_group(rs, mask, k_all, v_all, kh) for kh in range(SWA_KV_HEADS)]

    def kv_group(rs, mask, k_all, v_all, kh):
        kg = k_all[:, kh * hd:(kh + 1) * hd]
        vg = v_all[:, kh * hd:(kh + 1) * hd]
        kk2 = jnp.concatenate([kg, kg], axis=1).astype(BF16)
        vv2 = jnp.concatenate([vg, vg], axis=1).astype(BF16)
        p0 = kh * ppd
        pairs = range(p0, p0 + ppd)
        rows = []
        for p in pairs:
            q2 = q_ref[rs, p * LANES:(p + 1) * LANES]
            rows += [jnp.where(m0, q2, 0.0), jnp.where(m0, 0.0, q2)]
        qs = jnp.concatenate(rows, axis=0).astype(BF16)
        s_all = _nt(qs, kk2)
        yield None
        es, dens = [], []
        for i in range(2 * ppd):
            snk = sink_ref[2 * p0 + i]
            s = jnp.where(mask, s_all[i * lq:(i + 1) * lq], NEG_BIG)
            m = jnp.maximum(jnp.max(s, axis=-1, keepdims=True), snk)
            e = jnp.exp(s - m)
            dens.append(jnp.sum(e, axis=-1, keepdims=True) + jnp.exp(snk - m))
            es.append(e.astype(BF16))
        yield None
        o_all = jnp.dot(jnp.concatenate(es, axis=0), vv2, preferred_element_type=F32)
        outs = []
        for i, p in enumerate(pairs):
            oa = o_all[2 * i * lq:(2 * i + 1) * lq] / dens[2 * i]
            ob = o_all[(2 * i + 1) * lq:(2 * i + 2) * lq] / dens[2 * i + 1]
            g2 = gate_refs[p // ppt][rs, (p % ppt) * LANES:(p % ppt + 1) * LANES]
            outs.append((p, (jnp.where(m0, oa, ob) * g2).astype(o_ref.dtype)))
        yield outs

    seqs = [attend(bi) for bi in range(nseq)]
    groups = [g for _, _, gs in seqs for g in gs]
    if not prompt:
        for _ in range(2):
            for g in groups:
                next(g)
        done = [next(g) for g in groups]
    else:
        done = [list(g)[-1] for g in groups]
    for bi, (k_out, v_out, gs) in enumerate(seqs):
        if not prompt:
            ko_ref[bi], vo_ref[bi] = k_out, v_out
        elif bi == nseq - 1:
            ko_ref[0], vo_ref[0] = k_out, v_out
        for outs in done[bi * len(gs):(bi + 1) * len(gs)]:
            for p, out in outs:
                o_ref[bi * lq:(bi + 1) * lq, p * LANES:(p + 1) * LANES] = out


SWA_SAMPLE_SEQS = 4
SWA_PROMPT_BLOCKS = 4
SWA_GATE_TILE = 512


def _swa_gate_specs(rows, index):
    g0 = (SWA_WIDTH + 2 * SWA_KV_WIDTH) // SWA_GATE_TILE
    return [pl.BlockSpec((rows, SWA_GATE_TILE), functools.partial(index, g0 + t))
            for t in range(SWA_WIDTH // SWA_GATE_TILE)]


def _swa_prompt(h, sink, B, S):
    nblk = SWA_PROMPT_BLOCKS
    nb = S // (nblk * WINDOW)
    rows = nblk * WINDOW
    kcol = SWA_WIDTH // SWA_KV_WIDTH
    prev = lambda b, n: (b * nb + n) * nblk - jnp.minimum(n, 1)
    return pl.pallas_call(
        functools.partial(_swa_kernel, lq=WINDOW, nseq=nblk, prompt=True),
        grid=(B, nb),
        in_specs=[pl.BlockSpec(memory_space=pltpu.SMEM),
                  pl.BlockSpec((rows, SWA_WIDTH), lambda b, n: (b * nb + n, 0)),
                  pl.BlockSpec((rows, SWA_KV_WIDTH), lambda b, n: (b * nb + n, kcol)),
                  pl.BlockSpec((rows, SWA_KV_WIDTH), lambda b, n: (b * nb + n, kcol + 1)),
                  pl.BlockSpec((WINDOW, SWA_KV_WIDTH), lambda b, n: (prev(b, n), kcol)),
                  pl.BlockSpec((WINDOW, SWA_KV_WIDTH), lambda b, n: (prev(b, n), kcol + 1)),
                  *_swa_gate_specs(rows, lambda col, b, n: (b * nb + n, col))],
        out_specs=[pl.BlockSpec((rows, SWA_WIDTH), lambda b, n: (b * nb + n, 0)),
                   pl.BlockSpec((1, WINDOW, SWA_KV_WIDTH), lambda b, n: (b, 0, 0)),
                   pl.BlockSpec((1, WINDOW, SWA_KV_WIDTH), lambda b, n: (b, 0, 0))],
        out_shape=[jax.ShapeDtypeStruct((B * S, SWA_WIDTH), BF16),
                   jax.ShapeDtypeStruct((B, WINDOW, SWA_KV_WIDTH), F32),
                   jax.ShapeDtypeStruct((B, WINDOW, SWA_KV_WIDTH), F32)],
        compiler_params=_cparams(("parallel", "arbitrary")),
        name="swa_prompt",
    )(sink, h, h, h, h, h, h, h, h, h)


def _swa_sample(h, sink, cache_k, cache_v, j, B, L):
    ns = SWA_SAMPLE_SEQS
    assert B % ns == 0
    cache = pl.BlockSpec((None, ns, WINDOW, SWA_KV_WIDTH), lambda b: (j, b, 0, 0))
    kcol = SWA_WIDTH // SWA_KV_WIDTH
    return pl.pallas_call(
        functools.partial(_swa_kernel, lq=L, nseq=ns, prompt=False),
        grid=(B // ns,),
        in_specs=[pl.BlockSpec(memory_space=pltpu.SMEM),
                  pl.BlockSpec((ns * L, SWA_WIDTH), lambda b: (b, 0)),
                  pl.BlockSpec((ns * L, SWA_KV_WIDTH), lambda b: (b, kcol)),
                  pl.BlockSpec((ns * L, SWA_KV_WIDTH), lambda b: (b, kcol + 1)),
                  cache, cache,
                  *_swa_gate_specs(ns * L, lambda col, b: (b, col))],
        out_specs=[pl.BlockSpec((ns * L, SWA_WIDTH), lambda b: (b, 0)),
                   pl.BlockSpec((ns, WINDOW, SWA_KV_WIDTH), lambda b: (b, 0, 0)),
                   pl.BlockSpec((ns, WINDOW, SWA_KV_WIDTH), lambda b: (b, 0, 0))],
        out_shape=[jax.ShapeDtypeStruct((B * L, SWA_WIDTH), BF16),
                   jax.ShapeDtypeStruct((B, WINDOW, SWA_KV_WIDTH), F32),
                   jax.ShapeDtypeStruct((B, WINDOW, SWA_KV_WIDTH), F32)],
        compiler_params=_cparams(("parallel",)),
        name="swa_sample",
    )(sink, h, h, h, cache_k, cache_v, h, h, h, h)


def _swa_layer(x, cache, j, w_in, sink, w_out, ln_g, ln_b, B, L):
    first = 0 if cache is None else PAST_LEN
    rows = max(L, min(B * L, PROJ_ROWS))
    tabs = jnp.stack(_rope_tables(first + (jnp.arange(rows) % L).astype(F32)))
    h = _mm(x, w_in, 2 * SWA_WIDTH + 2 * SWA_KV_WIDTH, rope_q_cols=SWA_WIDTH, rope_cols=SWA_WIDTH + SWA_KV_WIDTH,
            silu_from=SWA_WIDTH + 2 * SWA_KV_WIDTH, rope_tabs=tabs)
    if cache is None:
        o, nk, nv = _swa_prompt(h, sink, B, L)
    else:
        ck = cache[0].reshape(-1, B, WINDOW, SWA_KV_WIDTH)
        cv = cache[1].reshape(-1, B, WINDOW, SWA_KV_WIDTH)
        o, nk, nv = _swa_sample(h, sink, ck, cv, j, B, L)
    shape = (B, WINDOW, SWA_KV_HEADS, SWA_HEAD_DIM)
    return _mm_ln(o, w_out, x, ln_g, ln_b), nk.reshape(shape), nv.reshape(shape)


def _tril3(C):
    return (lax.broadcasted_iota(jnp.int32, (C, 3 * C), 1) % C
            <= lax.broadcasted_iota(jnp.int32, (C, 3 * C), 0)).astype(BF16)


def _cumsum_rows(x, tril3):
    h1 = x.astype(BF16)
    r1 = x - h1.astype(F32)
    h2 = r1.astype(BF16)
    h3 = (r1 - h2.astype(F32)).astype(BF16)
    return jnp.dot(tril3, jnp.concatenate([h1, h2, h3], axis=0), preferred_element_type=F32)


def _gla_chunk_kernel(q_ref, k_ref, v_ref, gate_ref, al_ref, wa_ref, ba_ref, ng_ref, s0_ref, o_ref, so_ref, st_ref,
                      *, C, SB):
    c_idx = pl.program_id(1)
    H, DK, DV = GLA_HEADS, GLA_DK, GLA_DV
    nseq = o_ref.shape[0]

    @pl.when(c_idx == 0)
    def _():
        for bi in range(nseq):
            for h in range(H):
                st_ref[bi * H + h] = s0_ref[bi, h].T

    row = lax.broadcasted_iota(jnp.int32, (C, H * DK), 0)
    causal = lax.broadcasted_iota(jnp.int32, (C, C), 1) <= lax.broadcasted_iota(jnp.int32, (C, C), 0)
    tril3 = _tril3(C)

    def prep(bi):
        z = jnp.dot(al_ref[bi].astype(BF16), wa_ref[...], preferred_element_type=F32) + ba_ref[...]
        b = _cumsum_rows(_log_sigmoid(z) * (1.0 / GLA_GATE_TEMP), tril3)
        q = q_ref[bi] * (DK ** -0.5)
        k = k_ref[bi]
        b_last = b[C - 1:C, :]
        qis, kjs = [], []
        for i in range(C // SB):
            r0, r1 = i * SB, (i + 1) * SB
            bn = b[r0:r0 + 1, :]
            qis.append((q[r0:r1] * jnp.exp(b[r0:r1] - bn)).astype(BF16))
            kjs.append((k * jnp.exp(jnp.where(row < r1, bn - b, 0.0))).astype(BF16))
        return dict(v16=v_ref[bi].astype(BF16), e_last=jnp.exp(b_last), qe=(q * jnp.exp(b)).astype(BF16),
                    kd=(k * jnp.exp(b_last - b)).astype(BF16), qis=qis, kjs=kjs)

    def head(bi, p, h):
        ks, vs = slice(h * DK, (h + 1) * DK), slice(h * DV, (h + 1) * DV)
        st = st_ref[bi * H + h]
        a_parts = [_nt(qi[:, ks], kj[:, ks]) for qi, kj in zip(p["qis"], p["kjs"])]
        v16 = p["v16"][:, vs]
        o_inter = _nt(p["qe"][:, ks], st.astype(BF16))
        ds = lax.dot_general(v16, p["kd"][:, ks], (((0,), (0,)), ((), ())), preferred_element_type=F32)
        yield None
        a = jnp.concatenate(a_parts, axis=0) if len(a_parts) > 1 else a_parts[0]
        a = jnp.where(causal, a, 0.0).astype(BF16)
        o = o_inter + jnp.dot(a, v16, preferred_element_type=F32)
        st_new = st * p["e_last"][:, ks] + ds
        yield None
        on = o * lax.rsqrt(jnp.mean(o * o, axis=-1, keepdims=True) + GLA_NORM_EPS) * ng_ref[...]
        yield st_new, (on * gate_ref[bi, :, vs]).astype(o_ref.dtype)

    preps = [prep(bi) for bi in range(nseq)]
    chains = [[head(bi, preps[bi], h) for h in range(H)] for bi in range(nseq)]
    if C == GLA_CHUNK:
        for _ in range(2):
            for per_seq in chains:
                for chain in per_seq:
                    next(chain)
        results = [[next(chain) for chain in per_seq] for per_seq in chains]
    else:
        results = [[list(chain)[-1] for chain in per_seq] for per_seq in chains]
    for bi in range(nseq):
        for h, (st_new, out) in enumerate(results[bi]):
            st_ref[bi * H + h] = st_new
            o_ref[bi, :, h * DV:(h + 1) * DV] = out

    @pl.when(c_idx == pl.num_programs(1) - 1)
    def _():
        for bi in range(nseq):
            for h in range(H):
                so_ref[bi, h] = st_ref[bi * H + h].T


def _gla_chunk(h, a_low, w_a2, b_a, norm_g, states, j, B, L):
    C = min(GLA_CHUNK, L)
    SB = min(GLA_SUB, C)
    nc = L // C
    ns = GLA_SEQS
    assert B % ns == 0
    tok = lambda width, col: pl.BlockSpec((ns, C, width), lambda b, c: (b, c, col))
    st0 = pl.BlockSpec((None, ns, GLA_HEADS, GLA_DK, GLA_DV), lambda b, c: (j, b, 0, 0, 0))
    st = pl.BlockSpec((ns, GLA_HEADS, GLA_DK, GLA_DV), lambda b, c: (b, 0, 0, 0))
    h3 = h.reshape(B, L, h.shape[1])
    o, st_out = pl.pallas_call(
        functools.partial(_gla_chunk_kernel, C=C, SB=SB),
        grid=(B // ns, nc),
        in_specs=[tok(GLA_KEY_DIM, 0), tok(GLA_KEY_DIM, 1), tok(GLA_VAL_DIM, 1), tok(GLA_VAL_DIM, 2),
                  tok(LANES, 0),
                  pl.BlockSpec((LANES, GLA_KEY_DIM), lambda b, c: (0, 0)),
                  pl.BlockSpec((1, GLA_KEY_DIM), lambda b, c: (0, 0)),
                  pl.BlockSpec((1, GLA_DV), lambda b, c: (0, 0)),
                  st0],
        out_specs=[tok(GLA_VAL_DIM, 0), st],
        out_shape=[jax.ShapeDtypeStruct((B, L, GLA_VAL_DIM), BF16),
                   jax.ShapeDtypeStruct((B, GLA_HEADS, GLA_DK, GLA_DV), F32)],
        scratch_shapes=[pltpu.VMEM((ns * GLA_HEADS, GLA_DV, GLA_DK), F32)],
        compiler_params=_cparams(("parallel", "arbitrary")),
        name="gla_chunk",
    )(h3, h3, h3, h3, a_low.reshape(B, L, LANES), w_a2, b_a.reshape(1, GLA_KEY_DIM), norm_g.reshape(1, GLA_DV),
      states)
    return o.reshape(B * L, GLA_VAL_DIM), st_out


def _gla_layer(x, states, j, w_in, w_low, w_a2, b_a, norm_g, w_out, ln_g, ln_b, B, L):
    h, a_low = _mm(x, w_in, 2 * GLA_KEY_DIM + 2 * GLA_VAL_DIM, silu_from=2 * GLA_KEY_DIM + GLA_VAL_DIM, side_w=w_low)
    if states is None:
        states, j = jnp.zeros((1, B, GLA_HEADS, GLA_DK, GLA_DV), F32), 0
    o, st = _gla_chunk(h, a_low, w_a2, b_a, norm_g, states, j, B, L)
    return _mm_ln(o, w_out, x, ln_g, ln_b), st


SUBLANES = 8
RWKV_GATE_INDEX = 3


def _prev_rows(x, pa_ref, pb_ref, seq_len):
    if seq_len is None:
        return pa_ref[...]
    tm = x.shape[0]
    starts_seq = (pl.program_id(0) * tm) % seq_len == 0
    first = jnp.where(starts_seq, pb_ref[0], pa_ref[SUBLANES - 1:SUBLANES, :])
    rows = lax.broadcasted_iota(jnp.int32, x.shape, 0)
    return jnp.where(rows == 0, first, pltpu.roll(x, 1, 0))


def _prev_specs(x, xprev, shift, tm, seq_len, grid_rank):
    pad = (0,) * (grid_rank - 1)
    sh = shift.reshape(shift.shape[0], 1, D_MODEL)
    if xprev is not None:
        return (xprev, sh), [pl.BlockSpec((tm, D_MODEL), lambda i, *_: (i, 0)),
                             pl.BlockSpec((1, 1, D_MODEL), lambda i, *_: (0, 0, 0))], None
    assert seq_len % tm == 0
    per = tm // SUBLANES
    return (x, sh), [pl.BlockSpec((SUBLANES, D_MODEL), lambda i, *_: (jnp.maximum(i * per - 1, 0), 0)),
                     pl.BlockSpec((1, 1, D_MODEL), lambda i, *_: ((i * tm) // seq_len, 0, 0))], seq_len


def _rwkv_proj_kernel(x_ref, pa_ref, pb_ref, mu_ref, w_ref, o_ref, xm_ref, *, seq_len):
    def step(first_col, gate):
        if first_col:
            x = x_ref[...]
            xm = (x + (_prev_rows(x, pa_ref, pb_ref, seq_len) - x) * mu_ref[0]).astype(BF16)
            xm_ref[...] = xm
        else:
            xm = xm_ref[...]
        acc = jnp.dot(xm, w_ref[0], preferred_element_type=F32)
        o_ref[0] = _silu(acc) if gate else acc

    is_first = pl.program_id(2) == 0
    is_gate = pl.program_id(1) == RWKV_GATE_INDEX
    for first_col in (True, False):
        for gate in (True, False):
            here = jnp.logical_and(is_first == first_col, is_gate == gate)
            pl.when(here)(functools.partial(step, first_col, gate))


def _rwkv_proj(x, xprev, shift, seq_len, mu4, w4, tn=1024):
    M = x.shape[0]
    tm = min(M, PROJ_ROWS)
    prev_ops, prev_specs, inline_len = _prev_specs(x, xprev, shift, tm, seq_len, 3)
    return pl.pallas_call(
        functools.partial(_rwkv_proj_kernel, seq_len=inline_len),
        grid=(M // tm, 4, D_MODEL // tn),
        in_specs=[pl.BlockSpec((tm, D_MODEL), lambda i, m, j: (i, 0)),
                  *prev_specs,
                  pl.BlockSpec((1, 1, D_MODEL), lambda i, m, j: (m, 0, 0)),
                  pl.BlockSpec((1, D_MODEL, tn), lambda i, m, j: (m, 0, j))],
        out_specs=pl.BlockSpec((1, tm, tn), lambda i, m, j: (m, i, j)),
        out_shape=jax.ShapeDtypeStruct((4, M, D_MODEL), F32),
        scratch_shapes=[pltpu.VMEM((tm, D_MODEL), BF16)],
        compiler_params=_cparams(("parallel", "arbitrary", "arbitrary")),
        name="rwkv_proj",
    )(x, *prev_ops, mu4, w4)


def _rwkv_lora_kernel(x_ref, pa_ref, pb_ref, mu_ref, w1_ref, w2_ref, w0_ref, a1_ref, a2_ref, a0_ref, lw_ref, a_ref,
                      *, seq_len):
    x = x_ref[...]
    xx = _prev_rows(x, pa_ref, pb_ref, seq_len) - x
    xw = (x + xx * mu_ref[0]).astype(BF16)
    xa = (x + xx * mu_ref[1]).astype(BF16)
    t = jnp.tanh(jnp.dot(xw, w1_ref[...], preferred_element_type=F32))
    wl = w0_ref[...] + jnp.dot(t.astype(BF16), w2_ref[...], preferred_element_type=F32)
    lw_ref[...] = (-math.exp(-0.5)) / (1.0 + jnp.exp(-wl))
    al = jnp.dot(xa, a1_ref[...], preferred_element_type=F32)
    az = a0_ref[...] + jnp.dot(al.astype(BF16), a2_ref[...], preferred_element_type=F32)
    a_ref[...] = 1.0 / (1.0 + jnp.exp(-az))


def _rwkv_lora(x, xprev, shift, seq_len, mu2, w1, w2, w0, a1, a2, a0):
    M = x.shape[0]
    tm = min(M, LORA_ROWS)
    R = w1.shape[1]
    full = lambda shape: pl.BlockSpec(shape, lambda i: tuple(0 for _ in shape))
    prev_ops, prev_specs, inline_len = _prev_specs(x, xprev, shift, tm, seq_len, 1)
    return pl.pallas_call(
        functools.partial(_rwkv_lora_kernel, seq_len=inline_len),
        grid=(M // tm,),
        in_specs=[pl.BlockSpec((tm, D_MODEL), lambda i: (i, 0)),
                  *prev_specs,
                  full((2, 1, D_MODEL)), full((D_MODEL, R)), full((R, D_MODEL)), full((1, D_MODEL)),
                  full((D_MODEL, R)), full((R, D_MODEL)), full((1, D_MODEL))],
        out_specs=[pl.BlockSpec((tm, D_MODEL), lambda i: (i, 0)),
                   pl.BlockSpec((tm, D_MODEL), lambda i: (i, 0))],
        out_shape=[jax.ShapeDtypeStruct((M, D_MODEL), F32), jax.ShapeDtypeStruct((M, D_MODEL), F32)],
        compiler_params=_cparams(("parallel",)),
        name="rwkv_lora",
    )(x, *prev_ops, mu2, w1, w2, w0.reshape(1, D_MODEL), a1, a2, a0.reshape(1, D_MODEL))


def _head_ones():
    r = lax.broadcasted_iota(jnp.int32, (LANES, LANES), 0) // RWKV_HEAD_DIM
    c = lax.broadcasted_iota(jnp.int32, (LANES, LANES), 1) // RWKV_HEAD_DIM
    e = (r == c).astype(BF16)
    return jnp.concatenate([e, e], axis=0)


def _segsum(x, e2):
    hi = x.astype(BF16)
    lo = (x - hi.astype(F32)).astype(BF16)
    return jnp.dot(jnp.concatenate([hi, lo], axis=1), e2, preferred_element_type=F32)


def _segsum_wide(x, e2):
    rows, n = x.shape[0], x.shape[1] // LANES
    s = _segsum(jnp.concatenate([x[:, i * LANES:(i + 1) * LANES] for i in range(n)], axis=0), e2)
    return jnp.concatenate([s[i * rows:(i + 1) * rows] for i in range(n)], axis=1)


RWKV_PAIRS = 16
RWKV_SEQS = 2
RWKV_CHUNK = 64


def _rwkv_chunk_kernel(r_ref, k_ref, v_ref, gt_ref, lw_ref, a_ref, kk_ref, ka_ref, rk_ref, gng_ref, gnb_ref, s0_ref,
                       o_ref, so_ref, st_ref, *, lreal):
    C, N = RWKV_CHUNK, RWKV_HEAD_DIM
    t_idx = pl.program_id(2)
    e2 = _head_ones()
    m0 = lax.broadcasted_iota(jnp.int32, (C, LANES), 1) < N
    ti = lax.broadcasted_iota(jnp.int32, (C, 2 * C), 0)
    si2 = lax.broadcasted_iota(jnp.int32, (C, 2 * C), 1)
    si = si2 % C
    strict, incl, left = si < ti, si <= ti, si2 < C
    bd = (lax.broadcasted_iota(jnp.int32, (LANES, LANES), 0) // N
          == lax.broadcasted_iota(jnp.int32, (LANES, LANES), 1) // N)
    zs = jnp.zeros((N, N), F32)

    nseq = o_ref.shape[0]

    @pl.when(t_idx == 0)
    def _():
        for bi in range(nseq):
            for p in range(RWKV_PAIRS):
                top = jnp.concatenate([s0_ref[bi, 2 * p], zs], axis=1)
                bot = jnp.concatenate([zs, s0_ref[bi, 2 * p + 1]], axis=1)
                st_ref[bi * RWKV_PAIRS + p] = jnp.concatenate([top, bot], axis=0)

    def both(x, y):
        parts = [x] if y is None else [x, y]
        return jnp.concatenate([jnp.where(m0, z, 0.0) for z in parts] + [jnp.where(m0, 0.0, z) for z in parts],
                               axis=0).astype(BF16)

    def load(x):
        if lreal < C:
            x = jnp.concatenate([x, jnp.zeros((C - lreal, x.shape[1]), F32)], axis=0)
        return x

    inv_n = 1.0 / N
    n_iter = max(1, (lreal - 1).bit_length())
    lanes = [slice(p * LANES, (p + 1) * LANES) for p in range(RWKV_PAIRS)]

    def prep(bi):
        r, k, v = load(r_ref[0, bi]), load(k_ref[0, bi]), load(v_ref[0, bi])
        lw, a = load(lw_ref[bi]), load(a_ref[bi])
        kk = k * kk_ref[...]
        kk = kk / jnp.maximum(jnp.sqrt(_segsum_wide(kk * kk, e2)), 1e-12)
        kp = k * (1.0 + (a - 1.0) * ka_ref[...])
        c = _cumsum_rows(lw, _tril3(C))
        e_c, e_nc = jnp.exp(c), jnp.exp(-c)
        return dict(v=v, e_c=e_c, at_w=-kk * jnp.exp(c - lw), rt_w=r * e_c,
                    bt_w=(kk * a * e_nc).astype(BF16), kt_w=(kp * e_nc).astype(BF16),
                    bonus=_segsum_wide((r * kp * rk_ref[...])[:lreal], e2))

    def scores(bi, q, p):
        ls = lanes[p]
        at, rt = q["at_w"][:, ls], q["rt_w"][:, ls]
        bk = jnp.concatenate([q["bt_w"][:, ls], q["kt_w"][:, ls]], axis=0)
        g = _nt(both(at, rt), bk)
        s_bd = st_ref[bi * RWKV_PAIRS + p]
        pq = _nt(jnp.concatenate([at, rt], axis=0).astype(BF16), s_bd.astype(BF16))
        return dict(bk=bk, s_bd=s_bd, g=g, pq=pq)

    def setup(q, p, d):
        g, pq = d.pop("g"), d.pop("pq")
        aa0, rr0 = jnp.where(strict, g[0:C], 0.0), jnp.where(incl, g[C:2 * C], 0.0)
        aa1, rr1 = jnp.where(strict, g[2 * C:3 * C], 0.0), jnp.where(incl, g[3 * C:], 0.0)
        a_ab = jnp.where(left, aa0, pltpu.roll(aa1, C, 1))
        a_ak = jnp.where(left, pltpu.roll(aa0, C, 1), aa1)
        x = pq[:C] + jnp.dot(a_ak.astype(BF16), both(q["v"][:, lanes[p]], None), preferred_element_type=F32)
        d.update(y0=pq[C:], rr=jnp.concatenate([rr0, rr1], axis=1).astype(BF16), x=x, ac=a_ab)

    def neumann(d, it):
        ac = d["ac"]
        ac16 = ac.astype(BF16)
        rhs = both(d["x"], None)
        if it < n_iter - 1:
            a_bd = jnp.concatenate([jnp.where(left, ac, 0.0), jnp.where(left, 0.0, ac)], axis=0).astype(BF16)
            res = jnp.dot(ac16, jnp.concatenate([rhs, a_bd], axis=1), preferred_element_type=F32)
            d["x"] = d["x"] + res[:, :LANES]
            d["ac"] = res[:, LANES:]
        else:
            d["x"] = d["x"] + jnp.dot(ac16, rhs, preferred_element_type=F32)

    def finish(q, p, d):
        ls = lanes[p]
        u, vp = d["x"], q["v"][:, ls]
        y = d["y0"] + jnp.dot(d["rr"], both(u, vp), preferred_element_type=F32)
        uv = jnp.concatenate([u, vp], axis=0).astype(BF16)
        ds = lax.dot_general(uv, d["bk"], (((0,), (0,)), ((), ())), preferred_element_type=F32)
        return (d["s_bd"] + jnp.where(bd, ds, 0.0)) * q["e_c"][C - 1:C, ls], y[:lreal]

    def output(bi, q, done):
        y = jnp.concatenate([yp for _, yp in done], axis=1)
        yc = y - _segsum_wide(y, e2) * inv_n
        yv = _segsum_wide(yc * yc, e2) * inv_n
        yn = yc * lax.rsqrt(yv + RWKV_GN_EPS) * gng_ref[...] + gnb_ref[...]
        return ((yn + q["bonus"] * q["v"][:lreal]) * gt_ref[0, bi]).astype(o_ref.dtype)

    pairs = range(RWKV_PAIRS)
    stages = [
        lambda bi, s: s.update(q=prep(bi)),
        lambda bi, s: s.update(work=[scores(bi, s["q"], p) for p in pairs]),
        lambda bi, s: [setup(s["q"], p, d) for p, d in enumerate(s["work"])],
        lambda bi, s: [neumann(d, it) for it in range(n_iter) for d in s["work"]],
        lambda bi, s: s.update(done=[finish(s["q"], p, d) for p, d in enumerate(s["work"])]),
        lambda bi, s: s.update(out=output(bi, s["q"], s["done"])),
    ]
    seqs = [dict() for _ in range(nseq)]
    lag = 1
    for tick in range(len(stages) + lag * (nseq - 1)):
        for bi in range(nseq):
            if 0 <= tick - lag * bi < len(stages):
                stages[tick - lag * bi](bi, seqs[bi])
    for bi, s in enumerate(seqs):
        for p, (s_new, _) in enumerate(s["done"]):
            st_ref[bi * RWKV_PAIRS + p] = s_new
        o_ref[bi] = s["out"]

    @pl.when(t_idx == pl.num_programs(2) - 1)
    def _():
        for bi in range(nseq):
            for p in range(RWKV_PAIRS):
                s = st_ref[bi * RWKV_PAIRS + p]
                so_ref[bi, 2 * p] = s[:N, :N]
                so_ref[bi, 2 * p + 1] = s[N:, N:]


def _rwkv_scan(proj, lw, a, k_k, k_a, r_k, gn_g, gn_b, states, j, B, L):
    tc = min(L, RWKV_CHUNK)
    nt = L // tc
    W = RWKV_PAIRS * LANES
    ng = D_MODEL // W
    hpg = 2 * RWKV_PAIRS
    ns = RWKV_SEQS
    assert B % ns == 0
    hd = RWKV_HEAD_DIM
    tok = lambda m: pl.BlockSpec((1, ns, tc, W), lambda b, g, t: (m, b, t, g))
    vec = pl.BlockSpec((ns, tc, W), lambda b, g, t: (b, t, g))
    par = pl.BlockSpec((1, W), lambda b, g, t: (0, g))
    st = pl.BlockSpec((ns, hpg, hd, hd), lambda b, g, t: (b, g, 0, 0))
    st0 = pl.BlockSpec((None, ns, hpg, hd, hd), lambda b, g, t: (j, b, g, 0, 0))
    proj4 = proj.reshape(4, B, L, D_MODEL)
    o, st_out = pl.pallas_call(
        functools.partial(_rwkv_chunk_kernel, lreal=tc),
        grid=(B // ns, ng, nt),
        in_specs=[tok(0), tok(1), tok(2), tok(3), vec, vec, par, par, par, par, par, st0],
        out_specs=[vec, st],
        out_shape=[jax.ShapeDtypeStruct((B, L, D_MODEL), BF16),
                   jax.ShapeDtypeStruct((B, RWKV_HEADS, hd, hd), F32)],
        scratch_shapes=[pltpu.VMEM((ns * RWKV_PAIRS, LANES, LANES), F32)],
        compiler_params=_cparams(("parallel", "parallel", "arbitrary")),
        name="rwkv_scan",
    )(proj4, proj4, proj4, proj4, lw.reshape(B, L, D_MODEL), a.reshape(B, L, D_MODEL), k_k.reshape(1, D_MODEL),
      k_a.reshape(1, D_MODEL), r_k.reshape(1, D_MODEL), gn_g.reshape(1, D_MODEL), gn_b.reshape(1, D_MODEL), states)
    return o.reshape(B * L, D_MODEL), st_out


def _rwkv_layer(x, shift, states, j, p, ln_g, ln_b, B, L):
    x3 = x.reshape(B, L, D_MODEL)
    if L % PROJ_ROWS == 0 and L % LORA_ROWS == 0:
        xprev = None
    else:
        xprev = jnp.concatenate([shift[:, None, :], x3[:, :-1]], axis=1).reshape(B * L, D_MODEL)
    proj = _rwkv_proj(x, xprev, shift, L, p["mu4"], p["w4"])
    lw, a = _rwkv_lora(x, xprev, shift, L, p["mu2"], p["w1"], p["w2"], p["w0"], p["a1"], p["a2"], p["a0"])
    o, st = _rwkv_scan(proj, lw, a, p["k_k"], p["k_a"], p["r_k"], p["gn_g"], p["gn_b"], states, j, B, L)
    return _mm_ln(o, p["w_out"], x, ln_g, ln_b), st, x3[:, -1]


def _pad_rank(w, axis):
    pad = [(0, 0), (0, 0)]
    pad[axis] = (0, LANES - w.shape[axis])
    return jnp.pad(w, pad).astype(BF16)


def _trunk(x3, cache, w):
    B, L, _ = x3.shape
    prompt = cache is None
    x = x3.reshape(B * L, D_MODEL)
    new_k, new_v, new_gla, new_wkv, new_shift = [], [], [], [], []
    for layer in range(DEPTH):
        kind, j = layer % 3, layer // 3
        g, b = w["ln_g"][layer], w["ln_b"][layer]
        if kind == 0:
            c = None if prompt else (cache["k"], cache["v"])
            x, nk, nv = _swa_layer(x, c, j, (w["swa_w_in"], j), w["swa_sink"][j], (w["swa_w_out"], j), g, b, B, L)
            new_k.append(nk)
            new_v.append(nv)
        elif kind == 1:
            st = None if prompt else cache["gla"]
            x, st = _gla_layer(x, st, j, (w["gla_w_in"], j), w["gla_w_low"][j], w["gla_w_a2"][j], w["gla_b_a"][j],
                               w["gla_norm_g"][j], (w["gla_w_out"], j), g, b, B, L)
            new_gla.append(st)
        else:
            if prompt:
                shift0 = jnp.zeros((B, D_MODEL), F32)
                s0, js = jnp.zeros((1, B, RWKV_HEADS, RWKV_HEAD_DIM, RWKV_HEAD_DIM), F32), 0
            else:
                shift0, s0, js = cache["shift"][j], cache["wkv"], j
            x, st, sh = _rwkv_layer(x, shift0, s0, js, w["rwkv"][j], g, b, B, L)
            new_wkv.append(st)
            new_shift.append(sh)
    stack = lambda parts: parts[0][None] if len(parts) == 1 else jnp.stack(parts)
    return (x.reshape(B, L, D_MODEL), stack(new_k), stack(new_v), stack(new_gla), stack(new_wkv), stack(new_shift))


def kernel(x_prompt, x_sample, cache_swa_k, cache_swa_v, state_gla, state_rwkv, state_rwkv_shift, ln_g, ln_b, swa_w_in, swa_sink, swa_w_out, gla_w_in, gla_w_a2, gla_b_a, gla_norm_g, gla_w_out, rwkv_mu, rwkv_w_rkvg, rwkv_w0, rwkv_w1, rwkv_w2, rwkv_a0, rwkv_a1, rwkv_a2, rwkv_k_k, rwkv_k_a, rwkv_r_k, rwkv_gn_g, rwkv_gn_b, rwkv_w_out):
    n_rwkv = rwkv_mu.shape[0]
    gla_main = 2 * GLA_KEY_DIM + 2 * GLA_VAL_DIM
    rwkv = []
    for j in range(n_rwkv):
        rwkv.append(dict(
            mu4=rwkv_mu[j][jnp.array([0, 2, 3, 5])].reshape(4, 1, D_MODEL),
            mu2=rwkv_mu[j][jnp.array([1, 4])].reshape(2, 1, D_MODEL),
            w4=rwkv_w_rkvg[j].astype(BF16),
            w1=_pad_rank(rwkv_w1[j], 1), w2=_pad_rank(rwkv_w2[j], 0), w0=rwkv_w0[j],
            a1=_pad_rank(rwkv_a1[j], 1), a2=_pad_rank(rwkv_a2[j], 0), a0=rwkv_a0[j],
            k_k=rwkv_k_k[j], k_a=rwkv_k_a[j], r_k=rwkv_r_k[j], gn_g=rwkv_gn_g[j], gn_b=rwkv_gn_b[j],
            w_out=rwkv_w_out[j].astype(BF16)))
    w = dict(ln_g=ln_g, ln_b=ln_b,
             swa_w_in=swa_w_in.astype(BF16), swa_sink=swa_sink, swa_w_out=swa_w_out.astype(BF16),
             gla_w_in=gla_w_in.astype(BF16),
             gla_w_low=[_pad_rank(gla_w_in[j][:, gla_main:], 1) for j in range(gla_w_in.shape[0])],
             gla_w_a2=[_pad_rank(gla_w_a2[j], 0) for j in range(gla_w_a2.shape[0])],
             gla_b_a=gla_b_a, gla_norm_g=gla_norm_g, gla_w_out=gla_w_out.astype(BF16), rwkv=rwkv)
    y_p, p_k, p_v, p_gla, p_wkv, p_shift = _trunk(x_prompt, None, w)
    cache = dict(k=cache_swa_k, v=cache_swa_v, gla=state_gla, wkv=state_rwkv, shift=state_rwkv_shift)
    y_s, s_k, s_v, s_gla, s_wkv, s_shift = _trunk(x_sample, cache, w)
    return (y_p, y_s, p_k, p_v, p_gla, p_wkv, p_shift, s_k, s_v, s_gla, s_wkv, s_shift)
```

```python
import functools
import math

import jax
import jax.numpy as jnp
from jax import lax
from jax.experimental import pallas as pl
from jax.experimental.pallas import tpu as pltpu

F32 = jnp.float32
BF16 = jnp.bfloat16

D_MODEL = 2048
DEPTH = 4
PAST_LEN = 16384
ALPHA = (2 * DEPTH) ** 0.25
LN_EPS = 1e-5

SWA_HEADS = 32
SWA_KV_HEADS = 4
SWA_GROUP = SWA_HEADS // SWA_KV_HEADS
SWA_HEAD_DIM = 64
SWA_WIDTH = SWA_HEADS * SWA_HEAD_DIM
SWA_KV_WIDTH = SWA_KV_HEADS * SWA_HEAD_DIM
WINDOW = 128
ROT_DIM = SWA_HEAD_DIM // 4
ROPE_THETA = 500000.0

GLA_HEADS = 4
GLA_KEY_DIM = D_MODEL // 2
GLA_VAL_DIM = D_MODEL
GLA_DK = GLA_KEY_DIM // GLA_HEADS
GLA_DV = GLA_VAL_DIM // GLA_HEADS
GLA_GATE_TEMP = 16.0
GLA_CHUNK = 64
GLA_SUB = 16
GLA_SEQS = 4
GLA_NORM_EPS = 1e-5

RWKV_HEAD_DIM = 64
RWKV_HEADS = D_MODEL // RWKV_HEAD_DIM
RWKV_GN_EPS = 64e-5

LANES = 128
VMEM_LIMIT = 56 * 1024 * 1024
PROJ_ROWS = 1024
PROJ_COLS = 1536
LORA_ROWS = 512
OUT_ROWS = 512
LN_SUB_ROWS = 256
NEG_BIG = -1e30


def _cparams(sem):
    return pltpu.CompilerParams(dimension_semantics=sem, vmem_limit_bytes=VMEM_LIMIT)


def _silu(x):
    return x * (1.0 / (1.0 + jnp.exp(-x)))


def _log_sigmoid(z):
    return jnp.minimum(z, 0.0) - jnp.log(1.0 + jnp.exp(-jnp.abs(z)))


def _nt(a, b):
    return lax.dot_general(a, b, (((1,), (1,)), ((), ())), preferred_element_type=F32)


def _mm_kernel(x_ref, w_ref, *refs, plans, has_tabs, has_side):
    refs = list(refs)
    tab_ref = refs.pop(0) if has_tabs else None
    side_w_ref = refs.pop(0) if has_side else None
    o_ref = refs.pop(0)
    side_o_ref = refs.pop(0) if has_side else None

    def run(t, plan):
        x16 = x_ref[...].astype(BF16)
        acc = jnp.dot(x16, w_ref[...], preferred_element_type=F32)
        if has_side and t == 0:
            side_o_ref[...] = jnp.dot(x16, side_w_ref[...], preferred_element_type=F32)
        if all(op is None for op in plan):
            o_ref[...] = acc
            return
        for c, op in enumerate(plan):
            chunk = acc[:, c * LANES:(c + 1) * LANES]
            if op == "silu":
                chunk = _silu(chunk)
            elif op in ("rope", "rope_q"):
                chunk = _rope128(chunk, tab_ref[0], tab_ref[1], tab_ref[2])
                if op == "rope_q":
                    chunk = chunk * SWA_HEAD_DIM ** -0.5
            o_ref[:, c * LANES:(c + 1) * LANES] = chunk

    if len(set(plans)) == 1 and not has_side:
        run(0, plans[0])
    else:
        for t, plan in enumerate(plans):
            pl.when(pl.program_id(1) == t)(functools.partial(run, t, plan))


def _mm(x, w, ncols, tn=PROJ_COLS, rope_q_cols=0, rope_cols=0, silu_from=None, rope_tabs=None, side_w=None):
    M, K = x.shape
    tm = min(M, PROJ_ROWS)
    assert M % tm == 0 and ncols % tn == 0

    def op_of(col):
        if col < rope_q_cols:
            return "rope_q"
        if col < rope_cols:
            return "rope"
        return "silu" if silu_from is not None and col >= silu_from else None

    plans = tuple(tuple(op_of(t * tn + c * LANES) for c in range(tn // LANES)) for t in range(ncols // tn))
    w, layer = (w, None) if not isinstance(w, tuple) else w
    w_spec = (pl.BlockSpec((K, tn), lambda i, j: (0, j)) if layer is None else
              pl.BlockSpec((None, K, tn), lambda i, j: (layer, 0, j)))
    operands, specs = [x, w], [pl.BlockSpec((tm, K), lambda i, j: (i, 0)), w_spec]
    if rope_cols:
        period = rope_tabs.shape[1] // tm
        operands.append(rope_tabs)
        specs.append(pl.BlockSpec((3, tm, LANES), lambda i, j: (0, i % period, 0)))
    out_specs = [pl.BlockSpec((tm, tn), lambda i, j: (i, j))]
    out_shape = [jax.ShapeDtypeStruct((M, ncols), F32)]
    if side_w is not None:
        operands.append(side_w)
        specs.append(pl.BlockSpec((K, LANES), lambda i, j: (0, 0)))
        out_specs.append(pl.BlockSpec((tm, LANES), lambda i, j: (i, 0)))
        out_shape.append(jax.ShapeDtypeStruct((M, LANES), F32))
    outs = pl.pallas_call(
        functools.partial(_mm_kernel, plans=plans, has_tabs=bool(rope_cols), has_side=side_w is not None),
        grid=(M // tm, ncols // tn),
        in_specs=specs,
        out_specs=out_specs,
        out_shape=out_shape,
        compiler_params=_cparams(("parallel", "arbitrary")),
        name="proj_mm",
    )(*operands)
    return outs[0] if side_w is None else tuple(outs)


def _mm_ln_kernel(a_ref, w_ref, x_ref, g_ref, b_ref, o_ref):
    for r0 in range(0, a_ref.shape[0], LN_SUB_ROWS):
        rows = slice(r0, min(r0 + LN_SUB_ROWS, a_ref.shape[0]))
        h = jnp.dot(a_ref[rows, :], w_ref[...], preferred_element_type=F32)
        z = ALPHA * x_ref[rows, :] + h
        mu = jnp.mean(z, axis=-1, keepdims=True)
        zc = z - mu
        var = jnp.mean(zc * zc, axis=-1, keepdims=True)
        o_ref[rows, :] = zc * lax.rsqrt(var + LN_EPS) * g_ref[...] + b_ref[...]


def _mm_ln(a, w, x, g, b):
    M, K = a.shape
    tm = min(M, OUT_ROWS)
    w, layer = (w, None) if not isinstance(w, tuple) else w
    D = w.shape[-1]
    return pl.pallas_call(
        _mm_ln_kernel,
        grid=(M // tm,),
        in_specs=[pl.BlockSpec((tm, K), lambda i: (i, 0)),
                  (pl.BlockSpec((K, D), lambda i: (0, 0)) if layer is None else
                   pl.BlockSpec((None, K, D), lambda i: (layer, 0, 0))),
                  pl.BlockSpec((tm, D), lambda i: (i, 0)),
                  pl.BlockSpec((1, D), lambda i: (0, 0)),
                  pl.BlockSpec((1, D), lambda i: (0, 0))],
        out_specs=pl.BlockSpec((tm, D), lambda i: (i, 0)),
        out_shape=jax.ShapeDtypeStruct((M, D), F32),
        compiler_params=_cparams(("parallel",)),
        name="out_proj_ln",
    )(a, w, x, g.reshape(1, D), b.reshape(1, D))


def _rope_tables(pos):
    half = ROT_DIM // 2
    inv = ROPE_THETA ** (-(jnp.arange(half, dtype=F32) * 2.0 / ROT_DIM))
    ang = pos[:, None] * inv[None, :]
    cos, sin = jnp.cos(ang), jnp.sin(ang)
    L = pos.shape[0]
    ones = jnp.ones((L, SWA_HEAD_DIM - ROT_DIM), F32)
    zeros_r = jnp.zeros((L, SWA_HEAD_DIM - ROT_DIM), F32)
    zeros_h = jnp.zeros((L, half), F32)
    c = jnp.concatenate([cos, cos, ones], axis=1)
    s1 = jnp.concatenate([-sin, zeros_h, zeros_r], axis=1)
    s2 = jnp.concatenate([zeros_h, sin, zeros_r], axis=1)
    rep = LANES // SWA_HEAD_DIM
    return jnp.tile(c, (1, rep)), jnp.tile(s1, (1, rep)), jnp.tile(s2, (1, rep))


def _rope128(x, c, s1, s2):
    return x * c + pltpu.roll(x, LANES - ROT_DIM // 2, 1) * s1 + pltpu.roll(x, ROT_DIM // 2, 1) * s2


def _swa_kernel(sink_ref, q_ref, kc_ref, vc_ref, kp_ref, vp_ref, g0_ref, g1_ref, g2_ref, g3_ref,
                o_ref, ko_ref, vo_ref, *, lq, nseq, prompt):
    gate_refs = (g0_ref, g1_ref, g2_ref, g3_ref)
    ppt = SWA_GATE_TILE // LANES
    nk = 2 * WINDOW
    nh = SWA_GROUP
    qi = lax.broadcasted_iota(jnp.int32, (lq, nk), 0)
    sj = lax.broadcasted_iota(jnp.int32, (lq, nk), 1)
    rel = qi + WINDOW - sj
    band = jnp.logical_and(rel >= 0, rel <= WINDOW)
    band_first = jnp.logical_and(band, sj >= jnp.where(pl.program_id(1) == 0, WINDOW, 0)) if prompt else band
    m0 = lax.broadcasted_iota(jnp.int32, (lq, LANES), 1) < SWA_HEAD_DIM
    hd = SWA_HEAD_DIM
    ppd = nh // 2

    def attend(bi):
        rs = slice(bi * lq, (bi + 1) * lq)
        mask = band_first if bi == 0 else band
        k_cur, v_cur = kc_ref[rs, :], vc_ref[rs, :]
        if prompt:
            if bi == 0:
                k_prev, v_prev = kp_ref[...], vp_ref[...]
            else:
                k_prev, v_prev = kc_ref[(bi - 1) * lq:bi * lq, :], vc_ref[(bi - 1) * lq:bi * lq, :]
            k_out, v_out = k_cur, v_cur
        else:
            k_prev, v_prev = kp_ref[bi], vp_ref[bi]
            k_out = jnp.concatenate([k_prev[lq:], k_cur], axis=0)
            v_out = jnp.concatenate([v_prev[lq:], v_cur], axis=0)
            zpad = jnp.zeros((WINDOW - lq, SWA_KV_WIDTH), F32)
            k_cur = jnp.concatenate([k_cur, zpad], axis=0)
            v_cur = jnp.concatenate([v_cur, zpad], axis=0)
        k_all = jnp.concatenate([k_prev, k_cur], axis=0)
        v_all = jnp.concatenate([v_prev, v_cur], axis=0)
        return k_out, v_out, [kv_group(rs, mask, k_all, v_all, kh) for kh in range(SWA_KV_HEADS)]

    def kv_group(rs, mask, k_all, v_all, kh):
        kg = k_all[:, kh * hd:(kh + 1) * hd]
        vg = v_all[:, kh * hd:(kh + 1) * hd]
        kk2 = jnp.concatenate([kg, kg], axis=1).astype(BF16)
        vv2 = jnp.concatenate([vg, vg], axis=1).astype(BF16)
        p0 = kh * ppd
        pairs = range(p0, p0 + ppd)
        rows = []
        for p in pairs:
            q2 = q_ref[rs, p * LANES:(p + 1) * LANES]
            rows += [jnp.where(m0, q2, 0.0), jnp.where(m0, 0.0, q2)]
        qs = jnp.concatenate(rows, axis=0).astype(BF16)
        s_all = _nt(qs, kk2)
        yield None
        es, dens = [], []
        for i in range(2 * ppd):
            snk = sink_ref[2 * p0 + i]
            s = jnp.where(mask, s_all[i * lq:(i + 1) * lq], NEG_BIG)
            m = jnp.maximum(jnp.max(s, axis=-1, keepdims=True), snk)
            e = jnp.exp(s - m)
            dens.append(jnp.sum(e, axis=-1, keepdims=True) + jnp.exp(snk - m))
            es.append(e.astype(BF16))
        yield None
        o_all = jnp.dot(jnp.concatenate(es, axis=0), vv2, preferred_element_type=F32)
        outs = []
        for i, p in enumerate(pairs):
            oa = o_all[2 * i * lq:(2 * i + 1) * lq] / dens[2 * i]
            ob = o_all[(2 * i + 1) * lq:(2 * i + 2) * lq] / dens[2 * i + 1]
            g2 = gate_refs[p // ppt][rs, (p % ppt) * LANES:(p % ppt + 1) * LANES]
            outs.append((p, (jnp.where(m0, oa, ob) * g2).astype(o_ref.dtype)))
        yield outs

    seqs = [attend(bi) for bi in range(nseq)]
    groups = [g for _, _, gs in seqs for g in gs]
    if not prompt:
        for _ in range(2):
            for g in groups:
                next(g)
        done = [next(g) for g in groups]
    else:
        done = [list(g)[-1] for g in groups]
    for bi, (k_out, v_out, gs) in enumerate(seqs):
        if not prompt:
            ko_ref[bi], vo_ref[bi] = k_out, v_out
        elif bi == nseq - 1:
            ko_ref[0], vo_ref[0] = k_out, v_out
        for outs in done[bi * len(gs):(bi + 1) * len(gs)]:
            for p, out in outs:
                o_ref[bi * lq:(bi + 1) * lq, p * LANES:(p + 1) * LANES] = out


SWA_SAMPLE_SEQS = 4
SWA_PROMPT_BLOCKS = 4
SWA_GATE_TILE = 512


def _swa_gate_specs(rows, index):
    g0 = (SWA_WIDTH + 2 * SWA_KV_WIDTH) // SWA_GATE_TILE
    return [pl.BlockSpec((rows, SWA_GATE_TILE), functools.partial(index, g0 + t))
            for t in range(SWA_WIDTH // SWA_GATE_TILE)]


def _swa_prompt(h, sink, B, S):
    nblk = SWA_PROMPT_BLOCKS
    nb = S // (nblk * WINDOW)
    rows = nblk * WINDOW
    kcol = SWA_WIDTH // SWA_KV_WIDTH
    prev = lambda b, n: (b * nb + n) * nblk - jnp.minimum(n, 1)
    return pl.pallas_call(
        functools.partial(_swa_kernel, lq=WINDOW, nseq=nblk, prompt=True),
        grid=(B, nb),
        in_specs=[pl.BlockSpec(memory_space=pltpu.SMEM),
                  pl.BlockSpec((rows, SWA_WIDTH), lambda b, n: (b * nb + n, 0)),
                  pl.BlockSpec((rows, SWA_KV_WIDTH), lambda b, n: (b * nb + n, kcol)),
                  pl.BlockSpec((rows, SWA_KV_WIDTH), lambda b, n: (b * nb + n, kcol + 1)),
                  pl.BlockSpec((WINDOW, SWA_KV_WIDTH), lambda b, n: (prev(b, n), kcol)),
                  pl.BlockSpec((WINDOW, SWA_KV_WIDTH), lambda b, n: (prev(b, n), kcol + 1)),
                  *_swa_gate_specs(rows, lambda col, b, n: (b * nb + n, col))],
        out_specs=[pl.BlockSpec((rows, SWA_WIDTH), lambda b, n: (b * nb + n, 0)),
                   pl.BlockSpec((1, WINDOW, SWA_KV_WIDTH), lambda b, n: (b, 0, 0)),
                   pl.BlockSpec((1, WINDOW, SWA_KV_WIDTH), lambda b, n: (b, 0, 0))],
        out_shape=[jax.ShapeDtypeStruct((B * S, SWA_WIDTH), BF16),
                   jax.ShapeDtypeStruct((B, WINDOW, SWA_KV_WIDTH), F32),
                   jax.ShapeDtypeStruct((B, WINDOW, SWA_KV_WIDTH), F32)],
        compiler_params=_cparams(("parallel", "arbitrary")),
        name="swa_prompt",
    )(sink, h, h, h, h, h, h, h, h, h)


def _swa_sample(h, sink, cache_k, cache_v, j, B, L):
    ns = SWA_SAMPLE_SEQS
    assert B % ns == 0
    cache = pl.BlockSpec((None, ns, WINDOW, SWA_KV_WIDTH), lambda b: (j, b, 0, 0))
    kcol = SWA_WIDTH // SWA_KV_WIDTH
    return pl.pallas_call(
        functools.partial(_swa_kernel, lq=L, nseq=ns, prompt=False),
        grid=(B // ns,),
        in_specs=[pl.BlockSpec(memory_space=pltpu.SMEM),
                  pl.BlockSpec((ns * L, SWA_WIDTH), lambda b: (b, 0)),
                  pl.BlockSpec((ns * L, SWA_KV_WIDTH), lambda b: (b, kcol)),
                  pl.BlockSpec((ns * L, SWA_KV_WIDTH), lambda b: (b, kcol + 1)),
                  cache, cache,
                  *_swa_gate_specs(ns * L, lambda col, b: (b, col))],
        out_specs=[pl.BlockSpec((ns * L, SWA_WIDTH), lambda b: (b, 0)),
                   pl.BlockSpec((ns, WINDOW, SWA_KV_WIDTH), lambda b: (b, 0, 0)),
                   pl.BlockSpec((ns, WINDOW, SWA_KV_WIDTH), lambda b: (b, 0, 0))],
        out_shape=[jax.ShapeDtypeStruct((B * L, SWA_WIDTH), BF16),
                   jax.ShapeDtypeStruct((B, WINDOW, SWA_KV_WIDTH), F32),
                   jax.ShapeDtypeStruct((B, WINDOW, SWA_KV_WIDTH), F32)],
        compiler_params=_cparams(("parallel",)),
        name="swa_sample",
    )(sink, h, h, h, cache_k, cache_v, h, h, h, h)


def _swa_layer(x, cache, j, w_in, sink, w_out, ln_g, ln_b, B, L):
    first = 0 if cache is None else PAST_LEN
    rows = max(L, min(B * L, PROJ_ROWS))
    tabs = jnp.stack(_rope_tables(first + (jnp.arange(rows) % L).astype(F32)))
    h = _mm(x, w_in, 2 * SWA_WIDTH + 2 * SWA_KV_WIDTH, rope_q_cols=SWA_WIDTH, rope_cols=SWA_WIDTH + SWA_KV_WIDTH,
            silu_from=SWA_WIDTH + 2 * SWA_KV_WIDTH, rope_tabs=tabs)
    if cache is None:
        o, nk, nv = _swa_prompt(h, sink, B, L)
    else:
        ck = cache[0].reshape(-1, B, WINDOW, SWA_KV_WIDTH)
        cv = cache[1].reshape(-1, B, WINDOW, SWA_KV_WIDTH)
        o, nk, nv = _swa_sample(h, sink, ck, cv, j, B, L)
    shape = (B, WINDOW, SWA_KV_HEADS, SWA_HEAD_DIM)
    return _mm_ln(o, w_out, x, ln_g, ln_b), nk.reshape(shape), nv.reshape(shape)


def _tril3(C):
    return (lax.broadcasted_iota(jnp.int32, (C, 3 * C), 1) % C
            <= lax.broadcasted_iota(jnp.int32, (C, 3 * C), 0)).astype(BF16)


def _cumsum_rows(x, tril3):
    h1 = x.astype(BF16)
    r1 = x - h1.astype(F32)
    h2 = r1.astype(BF16)
    h3 = (r1 - h2.astype(F32)).astype(BF16)
    return jnp.dot(tril3, jnp.concatenate([h1, h2, h3], axis=0), preferred_element_type=F32)


def _gla_chunk_kernel(q_ref, k_ref, v_ref, gate_ref, al_ref, wa_ref, ba_ref, ng_ref, s0_ref, o_ref, so_ref, st_ref,
                      *, C, SB):
    c_idx = pl.program_id(1)
    H, DK, DV = GLA_HEADS, GLA_DK, GLA_DV
    nseq = o_ref.shape[0]
    staged = C == GLA_CHUNK

    @pl.when(c_idx == 0)
    def _():
        for bi in range(nseq):
            for h in range(H):
                st_ref[bi * H + h] = s0_ref[bi, h].T

    row = lax.broadcasted_iota(jnp.int32, (C, H * DK), 0)
    causal = lax.broadcasted_iota(jnp.int32, (C, C), 1) <= lax.broadcasted_iota(jnp.int32, (C, C), 0)
    tril3 = _tril3(C)

    def prep(bi):
        z = jnp.dot(al_ref[bi].astype(BF16), wa_ref[...], preferred_element_type=F32) + ba_ref[...]
        b = _cumsum_rows(_log_sigmoid(z) * (1.0 / GLA_GATE_TEMP), tril3)
        q = q_ref[bi] * (DK ** -0.5)
        k = k_ref[bi]
        b_last = b[C - 1:C, :]
        qis, kjs = [], []
        for i in range(C // SB):
            r0, r1 = i * SB, (i + 1) * SB
            bn = b[r0:r0 + 1, :]
            qis.append((q[r0:r1] * jnp.exp(b[r0:r1] - bn)).astype(BF16))
            kjs.append((k * jnp.exp(jnp.where(row < r1, bn - b, 0.0))).astype(BF16))
        return dict(v16=v_ref[bi].astype(BF16), e_last=jnp.exp(b_last), qe=(q * jnp.exp(b)).astype(BF16),
                    kd=(k * jnp.exp(b_last - b)).astype(BF16), qis=qis, kjs=kjs)

    def head(bi, p, h):
        ks, vs = slice(h * DK, (h + 1) * DK), slice(h * DV, (h + 1) * DV)
        st = st_ref[bi * H + h]
        a_parts = [_nt(qi[:, ks], kj[:, ks]) for qi, kj in zip(p["qis"], p["kjs"])]
        v16 = p["v16"][:, vs]
        o_inter = _nt(p["qe"][:, ks], st.astype(BF16))
        ds = lax.dot_general(v16, p["kd"][:, ks], (((0,), (0,)), ((), ())), preferred_element_type=F32)
        yield None
        a = jnp.concatenate(a_parts, axis=0) if len(a_parts) > 1 else a_parts[0]
        a = jnp.where(causal, a, 0.0).astype(BF16)
        o = o_inter + jnp.dot(a, v16, preferred_element_type=F32)
        st_new = st * p["e_last"][:, ks] + ds
        yield None
        on = o * lax.rsqrt(jnp.mean(o * o, axis=-1, keepdims=True) + GLA_NORM_EPS) * ng_ref[...]
        yield st_new, (on * gate_ref[bi, :, vs]).astype(o_ref.dtype)

    preps = [prep(bi) for bi in range(nseq)]
    chains = [[head(bi, preps[bi], h) for h in range(H)] for bi in range(nseq)]
    if staged:
        for _ in range(2):
            for per_seq in chains:
                for chain in per_seq:
                    next(chain)
        results = [[next(chain) for chain in per_seq] for per_seq in chains]
    else:
        results = [[list(chain)[-1] for chain in per_seq] for per_seq in chains]
    for bi in range(nseq):
        for h, (st_new, out) in enumerate(results[bi]):
            st_ref[bi * H + h] = st_new
            o_ref[bi, :, h * DV:(h + 1) * DV] = out

    @pl.when(c_idx == pl.num_programs(1) - 1)
    def _():
        for bi in range(nseq):
            for h in range(H):
                so_ref[bi, h] = st_ref[bi * H + h].T


def _gla_chunk(h, a_low, w_a2, b_a, norm_g, states, j, B, L):
    C = min(GLA_CHUNK, L)
    SB = min(GLA_SUB, C)
    nc = L // C
    ns = GLA_SEQS
    assert B % ns == 0
    tok = lambda width, col: pl.BlockSpec((ns, C, width), lambda b, c: (b, c, col))
    st0 = pl.BlockSpec((None, ns, GLA_HEADS, GLA_DK, GLA_DV), lambda b, c: (j, b, 0, 0, 0))
    st = pl.BlockSpec((ns, GLA_HEADS, GLA_DK, GLA_DV), lambda b, c: (b, 0, 0, 0))
    h3 = h.reshape(B, L, h.shape[1])
    o, st_out = pl.pallas_call(
        functools.partial(_gla_chunk_kernel, C=C, SB=SB),
        grid=(B // ns, nc),
        in_specs=[tok(GLA_KEY_DIM, 0), tok(GLA_KEY_DIM, 1), tok(GLA_VAL_DIM, 1), tok(GLA_VAL_DIM, 2),
                  tok(LANES, 0),
                  pl.BlockSpec((LANES, GLA_KEY_DIM), lambda b, c: (0, 0)),
                  pl.BlockSpec((1, GLA_KEY_DIM), lambda b, c: (0, 0)),
                  pl.BlockSpec((1, GLA_DV), lambda b, c: (0, 0)),
                  st0],
        out_specs=[tok(GLA_VAL_DIM, 0), st],
        out_shape=[jax.ShapeDtypeStruct((B, L, GLA_VAL_DIM), BF16),
                   jax.ShapeDtypeStruct((B, GLA_HEADS, GLA_DK, GLA_DV), F32)],
        scratch_shapes=[pltpu.VMEM((ns * GLA_HEADS, GLA_DV, GLA_DK), F32)],
        compiler_params=_cparams(("parallel", "arbitrary")),
        name="gla_chunk",
    )(h3, h3, h3, h3, a_low.reshape(B, L, LANES), w_a2, b_a.reshape(1, GLA_KEY_DIM), norm_g.reshape(1, GLA_DV),
      states)
    return o.reshape(B * L, GLA_VAL_DIM), st_out


def _gla_layer(x, states, j, w_in, w_low, w_a2, b_a, norm_g, w_out, ln_g, ln_b, B, L):
    h, a_low = _mm(x, w_in, 2 * GLA_KEY_DIM + 2 * GLA_VAL_DIM, silu_from=2 * GLA_KEY_DIM + GLA_VAL_DIM, side_w=w_low)
    if states is None:
        states, j = jnp.zeros((1, B, GLA_HEADS, GLA_DK, GLA_DV), F32), 0
    o, st = _gla_chunk(h, a_low, w_a2, b_a, norm_g, states, j, B, L)
    return _mm_ln(o, w_out, x, ln_g, ln_b), st


SUBLANES = 8
RWKV_GATE_INDEX = 3


def _prev_rows(x, pa_ref, pb_ref, seq_len):
    if seq_len is None:
        return pa_ref[...]
    tm = x.shape[0]
    starts_seq = (pl.program_id(0) * tm) % seq_len == 0
    first = jnp.where(starts_seq, pb_ref[0], pa_ref[SUBLANES - 1:SUBLANES, :])
    rows = lax.broadcasted_iota(jnp.int32, x.shape, 0)
    return jnp.where(rows == 0, first, pltpu.roll(x, 1, 0))


def _prev_specs(x, xprev, shift, tm, seq_len, grid_rank):
    pad = (0,) * (grid_rank - 1)
    sh = shift.reshape(shift.shape[0], 1, D_MODEL)
    if xprev is not None:
        return (xprev, sh), [pl.BlockSpec((tm, D_MODEL), lambda i, *_: (i, 0)),
                             pl.BlockSpec((1, 1, D_MODEL), lambda i, *_: (0, 0, 0))], None
    assert seq_len % tm == 0
    per = tm // SUBLANES
    return (x, sh), [pl.BlockSpec((SUBLANES, D_MODEL), lambda i, *_: (jnp.maximum(i * per - 1, 0), 0)),
                     pl.BlockSpec((1, 1, D_MODEL), lambda i, *_: ((i * tm) // seq_len, 0, 0))], seq_len


def _rwkv_proj_kernel(x_ref, pa_ref, pb_ref, mu_ref, w_ref, o_ref, xm_ref, *, seq_len):
    def step(first_col, gate):
        if first_col:
            x = x_ref[...]
            xm = (x + (_prev_rows(x, pa_ref, pb_ref, seq_len) - x) * mu_ref[0]).astype(BF16)
            xm_ref[...] = xm
        else:
            xm = xm_ref[...]
        acc = jnp.dot(xm, w_ref[0], preferred_element_type=F32)
        o_ref[0] = _silu(acc) if gate else acc

    is_first = pl.program_id(2) == 0
    is_gate = pl.program_id(1) == RWKV_GATE_INDEX
    for first_col in (True, False):
        for gate in (True, False):
            here = jnp.logical_and(is_first == first_col, is_gate == gate)
            pl.when(here)(functools.partial(step, first_col, gate))


def _rwkv_proj(x, xprev, shift, seq_len, mu4, w4, tn=1024):
    M = x.shape[0]
    tm = min(M, PROJ_ROWS)
    prev_ops, prev_specs, inline_len = _prev_specs(x, xprev, shift, tm, seq_len, 3)
    return pl.pallas_call(
        functools.partial(_rwkv_proj_kernel, seq_len=inline_len),
        grid=(M // tm, 4, D_MODEL // tn),
        in_specs=[pl.BlockSpec((tm, D_MODEL), lambda i, m, j: (i, 0)),
                  *prev_specs,
                  pl.BlockSpec((1, 1, D_MODEL), lambda i, m, j: (m, 0, 0)),
                  pl.BlockSpec((1, D_MODEL, tn), lambda i, m, j: (m, 0, j))],
        out_specs=pl.BlockSpec((1, tm, tn), lambda i, m, j: (m, i, j)),
        out_shape=jax.ShapeDtypeStruct((4, M, D_MODEL), F32),
        scratch_shapes=[pltpu.VMEM((tm, D_MODEL), BF16)],
        compiler_params=_cparams(("parallel", "arbitrary", "arbitrary")),
        name="rwkv_proj",
    )(x, *prev_ops, mu4, w4)


def _rwkv_lora_kernel(x_ref, pa_ref, pb_ref, mu_ref, w1_ref, w2_ref, w0_ref, a1_ref, a2_ref, a0_ref, lw_ref, a_ref,
                      *, seq_len):
    x = x_ref[...]
    xx = _prev_rows(x, pa_ref, pb_ref, seq_len) - x
    xw = (x + xx * mu_ref[0]).astype(BF16)
    xa = (x + xx * mu_ref[1]).astype(BF16)
    t = jnp.tanh(jnp.dot(xw, w1_ref[...], preferred_element_type=F32))
    wl = w0_ref[...] + jnp.dot(t.astype(BF16), w2_ref[...], preferred_element_type=F32)
    lw_ref[...] = (-math.exp(-0.5)) / (1.0 + jnp.exp(-wl))
    al = jnp.dot(xa, a1_ref[...], preferred_element_type=F32)
    az = a0_ref[...] + jnp.dot(al.astype(BF16), a2_ref[...], preferred_element_type=F32)
    a_ref[...] = 1.0 / (1.0 + jnp.exp(-az))


def _rwkv_lora(x, xprev, shift, seq_len, mu2, w1, w2, w0, a1, a2, a0):
    M = x.shape[0]
    tm = min(M, LORA_ROWS)
    R = w1.shape[1]
    full = lambda shape: pl.BlockSpec(shape, lambda i: tuple(0 for _ in shape))
    prev_ops, prev_specs, inline_len = _prev_specs(x, xprev, shift, tm, seq_len, 1)
    return pl.pallas_call(
        functools.partial(_rwkv_lora_kernel, seq_len=inline_len),
        grid=(M // tm,),
        in_specs=[pl.BlockSpec((tm, D_MODEL), lambda i: (i, 0)),
                  *prev_specs,
                  full((2, 1, D_MODEL)), full((D_MODEL, R)), full((R, D_MODEL)), full((1, D_MODEL)),
                  full((D_MODEL, R)), full((R, D_MODEL)), full((1, D_MODEL))],
        out_specs=[pl.BlockSpec((tm, D_MODEL), lambda i: (i, 0)),
                   pl.BlockSpec((tm, D_MODEL), lambda i: (i, 0))],
        out_shape=[jax.ShapeDtypeStruct((M, D_MODEL), F32), jax.ShapeDtypeStruct((M, D_MODEL), F32)],
        compiler_params=_cparams(("parallel",)),
        name="rwkv_lora",
    )(x, *prev_ops, mu2, w1, w2, w0.reshape(1, D_MODEL), a1, a2, a0.reshape(1, D_MODEL))


def _head_ones():
    r = lax.broadcasted_iota(jnp.int32, (LANES, LANES), 0) // RWKV_HEAD_DIM
    c = lax.broadcasted_iota(jnp.int32, (LANES, LANES), 1) // RWKV_HEAD_DIM
    e = (r == c).astype(BF16)
    return jnp.concatenate([e, e], axis=0)


def _segsum(x, e2):
    hi = x.astype(BF16)
    lo = (x - hi.astype(F32)).astype(BF16)
    return jnp.dot(jnp.concatenate([hi, lo], axis=1), e2, preferred_element_type=F32)


def _segsum_wide(x, e2):
    rows, n = x.shape[0], x.shape[1] // LANES
    s = _segsum(jnp.concatenate([x[:, i * LANES:(i + 1) * LANES] for i in range(n)], axis=0), e2)
    return jnp.concatenate([s[i * rows:(i + 1) * rows] for i in range(n)], axis=1)


RWKV_PAIRS = 16
RWKV_SEQS = 2
RWKV_CHUNK = 64


def _rwkv_chunk_kernel(r_ref, k_ref, v_ref, gt_ref, lw_ref, a_ref, kk_ref, ka_ref, rk_ref, gng_ref, gnb_ref, s0_ref,
                       o_ref, so_ref, st_ref, *, lreal):
    C, N = RWKV_CHUNK, RWKV_HEAD_DIM
    t_idx = pl.program_id(2)
    e2 = _head_ones()
    m0 = lax.broadcasted_iota(jnp.int32, (C, LANES), 1) < N
    ti = lax.broadcasted_iota(jnp.int32, (C, 2 * C), 0)
    si2 = lax.broadcasted_iota(jnp.int32, (C, 2 * C), 1)
    si = si2 % C
    strict, incl, left = si < ti, si <= ti, si2 < C
    bd = (lax.broadcasted_iota(jnp.int32, (LANES, LANES), 0) // N
          == lax.broadcasted_iota(jnp.int32, (LANES, LANES), 1) // N)
    zs = jnp.zeros((N, N), F32)

    nseq = o_ref.shape[0]

    @pl.when(t_idx == 0)
    def _():
        for bi in range(nseq):
            for p in range(RWKV_PAIRS):
                top = jnp.concatenate([s0_ref[bi, 2 * p], zs], axis=1)
                bot = jnp.concatenate([zs, s0_ref[bi, 2 * p + 1]], axis=1)
                st_ref[bi * RWKV_PAIRS + p] = jnp.concatenate([top, bot], axis=0)

    def both(x, y):
        parts = [x] if y is None else [x, y]
        return jnp.concatenate([jnp.where(m0, z, 0.0) for z in parts] + [jnp.where(m0, 0.0, z) for z in parts],
                               axis=0).astype(BF16)

    def load(x):
        if lreal < C:
            x = jnp.concatenate([x, jnp.zeros((C - lreal, x.shape[1]), F32)], axis=0)
        return x

    inv_n = 1.0 / N
    n_iter = max(1, (lreal - 1).bit_length())
    lanes = [slice(p * LANES, (p + 1) * LANES) for p in range(RWKV_PAIRS)]

    def prep(bi):
        r, k, v = load(r_ref[0, bi]), load(k_ref[0, bi]), load(v_ref[0, bi])
        lw, a = load(lw_ref[bi]), load(a_ref[bi])
        kk = k * kk_ref[...]
        kk = kk / jnp.maximum(jnp.sqrt(_segsum_wide(kk * kk, e2)), 1e-12)
        kp = k * (1.0 + (a - 1.0) * ka_ref[...])
        c = _cumsum_rows(lw, _tril3(C))
        e_c, e_nc = jnp.exp(c), jnp.exp(-c)
        return dict(v=v, e_c=e_c, at_w=-kk * jnp.exp(c - lw), rt_w=r * e_c,
                    bt_w=(kk * a * e_nc).astype(BF16), kt_w=(kp * e_nc).astype(BF16),
                    bonus=_segsum_wide((r * kp * rk_ref[...])[:lreal], e2))

    def scores(bi, q, p):
        ls = lanes[p]
        at, rt = q["at_w"][:, ls], q["rt_w"][:, ls]
        bk = jnp.concatenate([q["bt_w"][:, ls], q["kt_w"][:, ls]], axis=0)
        g = _nt(both(at, rt), bk)
        s_bd = st_ref[bi * RWKV_PAIRS + p]
        pq = _nt(jnp.concatenate([at, rt], axis=0).astype(BF16), s_bd.astype(BF16))
        return dict(bk=bk, s_bd=s_bd, g=g, pq=pq)

    def setup(q, p, d):
        g, pq = d.pop("g"), d.pop("pq")
        aa0, rr0 = jnp.where(strict, g[0:C], 0.0), jnp.where(incl, g[C:2 * C], 0.0)
        aa1, rr1 = jnp.where(strict, g[2 * C:3 * C], 0.0), jnp.where(incl, g[3 * C:], 0.0)
        a_ab = jnp.where(left, aa0, pltpu.roll(aa1, C, 1))
        a_ak = jnp.where(left, pltpu.roll(aa0, C, 1), aa1)
        x = pq[:C] + jnp.dot(a_ak.astype(BF16), both(q["v"][:, lanes[p]], None), preferred_element_type=F32)
        d.update(y0=pq[C:], rr=jnp.concatenate([rr0, rr1], axis=1).astype(BF16), x=x, ac=a_ab)

    def neumann(d, it):
        ac = d["ac"]
        ac16 = ac.astype(BF16)
        rhs = both(d["x"], None)
        if it < n_iter - 1:
            a_bd = jnp.concatenate([jnp.where(left, ac, 0.0), jnp.where(left, 0.0, ac)], axis=0).astype(BF16)
            res = jnp.dot(ac16, jnp.concatenate([rhs, a_bd], axis=1), preferred_element_type=F32)
            d["x"] = d["x"] + res[:, :LANES]
            d["ac"] = res[:, LANES:]
        else:
            d["x"] = d["x"] + jnp.dot(ac16, rhs, preferred_element_type=F32)

    def finish(q, p, d):
        ls = lanes[p]
        u, vp = d["x"], q["v"][:, ls]
        y = d["y0"] + jnp.dot(d["rr"], both(u, vp), preferred_element_type=F32)
        uv = jnp.concatenate([u, vp], axis=0).astype(BF16)
        ds = lax.dot_general(uv, d["bk"], (((0,), (0,)), ((), ())), preferred_element_type=F32)
        return (d["s_bd"] + jnp.where(bd, ds, 0.0)) * q["e_c"][C - 1:C, ls], y[:lreal]

    def output(bi, q, done):
        y = jnp.concatenate([yp for _, yp in done], axis=1)
        yc = y - _segsum_wide(y, e2) * inv_n
        yv = _segsum_wide(yc * yc, e2) * inv_n
        yn = yc * lax.rsqrt(yv + RWKV_GN_EPS) * gng_ref[...] + gnb_ref[...]
        return ((yn + q["bonus"] * q["v"][:lreal]) * gt_ref[0, bi]).astype(o_ref.dtype)

    pairs = range(RWKV_PAIRS)
    stages = [
        lambda bi, s: s.update(q=prep(bi)),
        lambda bi, s: s.update(work=[scores(bi, s["q"], p) for p in pairs]),
        lambda bi, s: [setup(s["q"], p, d) for p, d in enumerate(s["work"])],
        lambda bi, s: [neumann(d, it) for it in range(n_iter) for d in s["work"]],
        lambda bi, s: s.update(done=[finish(s["q"], p, d) for p, d in enumerate(s["work"])]),
        lambda bi, s: s.update(out=output(bi, s["q"], s["done"])),
    ]
    seqs = [dict() for _ in range(nseq)]
    lag = 1
    for tick in range(len(stages) + lag * (nseq - 1)):
        for bi in range(nseq):
            if 0 <= tick - lag * bi < len(stages):
                stages[tick - lag * bi](bi, seqs[bi])
    for bi, s in enumerate(seqs):
        for p, (s_new, _) in enumerate(s["done"]):
            st_ref[bi * RWKV_PAIRS + p] = s_new
        o_ref[bi] = s["out"]

    @pl.when(t_idx == pl.num_programs(2) - 1)
    def _():
        for bi in range(nseq):
            for p in range(RWKV_PAIRS):
                s = st_ref[bi * RWKV_PAIRS + p]
                so_ref[bi, 2 * p] = s[:N, :N]
                so_ref[bi, 2 * p + 1] = s[N:, N:]


def _rwkv_scan(proj, lw, a, k_k, k_a, r_k, gn_g, gn_b, states, j, B, L):
    tc = min(L, RWKV_CHUNK)
    nt = L // tc
    W = RWKV_PAIRS * LANES
    ng = D_MODEL // W
    hpg = 2 * RWKV_PAIRS
    ns = RWKV_SEQS
    assert B % ns == 0
    hd = RWKV_HEAD_DIM
    tok = lambda m: pl.BlockSpec((1, ns, tc, W), lambda b, g, t: (m, b, t, g))
    vec = pl.BlockSpec((ns, tc, W), lambda b, g, t: (b, t, g))
    par = pl.BlockSpec((1, W), lambda b, g, t: (0, g))
    st = pl.BlockSpec((ns, hpg, hd, hd), lambda b, g, t: (b, g, 0, 0))
    st0 = pl.BlockSpec((None, ns, hpg, hd, hd), lambda b, g, t: (j, b, g, 0, 0))
    proj4 = proj.reshape(4, B, L, D_MODEL)
    o, st_out = pl.pallas_call(
        functools.partial(_rwkv_chunk_kernel, lreal=tc),
        grid=(B // ns, ng, nt),
        in_specs=[tok(0), tok(1), tok(2), tok(3), vec, vec, par, par, par, par, par, st0],
        out_specs=[vec, st],
        out_shape=[jax.ShapeDtypeStruct((B, L, D_MODEL), BF16),
                   jax.ShapeDtypeStruct((B, RWKV_HEADS, hd, hd), F32)],
        scratch_shapes=[pltpu.VMEM((ns * RWKV_PAIRS, LANES, LANES), F32)],
        compiler_params=_cparams(("parallel", "parallel", "arbitrary")),
        name="rwkv_scan",
    )(proj4, proj4, proj4, proj4, lw.reshape(B, L, D_MODEL), a.reshape(B, L, D_MODEL), k_k.reshape(1, D_MODEL),
      k_a.reshape(1, D_MODEL), r_k.reshape(1, D_MODEL), gn_g.reshape(1, D_MODEL), gn_b.reshape(1, D_MODEL), states)
    return o.reshape(B * L, D_MODEL), st_out


def _rwkv_layer(x, shift, states, j, p, ln_g, ln_b, B, L):
    x3 = x.reshape(B, L, D_MODEL)
    if L % PROJ_ROWS == 0 and L % LORA_ROWS == 0:
        xprev = None
    else:
        xprev = jnp.concatenate([shift[:, None, :], x3[:, :-1]], axis=1).reshape(B * L, D_MODEL)
    proj = _rwkv_proj(x, xprev, shift, L, p["mu4"], p["w4"])
    lw, a = _rwkv_lora(x, xprev, shift, L, p["mu2"], p["w1"], p["w2"], p["w0"], p["a1"], p["a2"], p["a0"])
    o, st = _rwkv_scan(proj, lw, a, p["k_k"], p["k_a"], p["r_k"], p["gn_g"], p["gn_b"], states, j, B, L)
    return _mm_ln(o, p["w_out"], x, ln_g, ln_b), st, x3[:, -1]


def _pad_rank(w, axis):
    pad = [(0, 0), (0, 0)]
    pad[axis] = (0, LANES - w.shape[axis])
    return jnp.pad(w, pad).astype(BF16)


def _trunk(x3, cache, w):
    B, L, _ = x3.shape
    prompt = cache is None
    x = x3.reshape(B * L, D_MODEL)
    new_k, new_v, new_gla, new_wkv, new_shift = [], [], [], [], []
    for layer in range(DEPTH):
        kind, j = layer % 3, layer // 3
        g, b = w["ln_g"][layer], w["ln_b"][layer]
        if kind == 0:
            c = None if prompt else (cache["k"], cache["v"])
            x, nk, nv = _swa_layer(x, c, j, (w["swa_w_in"], j), w["swa_sink"][j], (w["swa_w_out"], j), g, b, B, L)
            new_k.append(nk)
            new_v.append(nv)
        elif kind == 1:
            st = None if prompt else cache["gla"]
            x, st = _gla_layer(x, st, j, (w["gla_w_in"], j), w["gla_w_low"][j], w["gla_w_a2"][j], w["gla_b_a"][j],
                               w["gla_norm_g"][j], (w["gla_w_out"], j), g, b, B, L)
            new_gla.append(st)
        else:
            if prompt:
                shift0 = jnp.zeros((B, D_MODEL), F32)
                s0, js = jnp.zeros((1, B, RWKV_HEADS, RWKV_HEAD_DIM, RWKV_HEAD_DIM), F32), 0
            else:
                shift0, s0, js = cache["shift"][j], cache["wkv"], j
            x, st, sh = _rwkv_layer(x, shift0, s0, js, w["rwkv"][j], g, b, B, L)
            new_wkv.append(st)
            new_shift.append(sh)
    stack = lambda parts: parts[0][None] if len(parts) == 1 else jnp.stack(parts)
    return (x.reshape(B, L, D_MODEL), stack(new_k), stack(new_v), stack(new_gla), stack(new_wkv), stack(new_shift))


def kernel(x_prompt, x_sample, cache_swa_k, cache_swa_v, state_gla, state_rwkv, state_rwkv_shift, ln_g, ln_b, swa_w_in, swa_sink, swa_w_out, gla_w_in, gla_w_a2, gla_b_a, gla_norm_g, gla_w_out, rwkv_mu, rwkv_w_rkvg, rwkv_w0, rwkv_w1, rwkv_w2, rwkv_a0, rwkv_a1, rwkv_a2, rwkv_k_k, rwkv_k_a, rwkv_r_k, rwkv_gn_g, rwkv_gn_b, rwkv_w_out):
    n_rwkv = rwkv_mu.shape[0]
    gla_main = 2 * GLA_KEY_DIM + 2 * GLA_VAL_DIM
    rwkv = []
    for j in range(n_rwkv):
        rwkv.append(dict(
            mu4=rwkv_mu[j][jnp.array([0, 2, 3, 5])].reshape(4, 1, D_MODEL),
            mu2=rwkv_mu[j][jnp.array([1, 4])].reshape(2, 1, D_MODEL),
            w4=rwkv_w_rkvg[j].astype(BF16),
            w1=_pad_rank(rwkv_w1[j], 1), w2=_pad_rank(rwkv_w2[j], 0), w0=rwkv_w0[j],
            a1=_pad_rank(rwkv_a1[j], 1), a2=_pad_rank(rwkv_a2[j], 0), a0=rwkv_a0[j],
            k_k=rwkv_k_k[j], k_a=rwkv_k_a[j], r_k=rwkv_r_k[j], gn_g=rwkv_gn_g[j], gn_b=rwkv_gn_b[j],
            w_out=rwkv_w_out[j].astype(BF16)))
    w = dict(ln_g=ln_g, ln_b=ln_b,
             swa_w_in=swa_w_in.astype(BF16), swa_sink=swa_sink, swa_w_out=swa_w_out.astype(BF16),
             gla_w_in=gla_w_in.astype(BF16),
             gla_w_low=[_pad_rank(gla_w_in[j][:, gla_main:], 1) for j in range(gla_w_in.shape[0])],
             gla_w_a2=[_pad_rank(gla_w_a2[j], 0) for j in range(gla_w_a2.shape[0])],
             gla_b_a=gla_b_a, gla_norm_g=gla_norm_g, gla_w_out=gla_w_out.astype(BF16), rwkv=rwkv)
    y_p, p_k, p_v, p_gla, p_wkv, p_shift = _trunk(x_prompt, None, w)
    cache = dict(k=cache_swa_k, v=cache_swa_v, gla=state_gla, wkv=state_rwkv, shift=state_rwkv_shift)
    y_s, s_k, s_v, s_gla, s_wkv, s_shift = _trunk(x_sample, cache, w)
    return (y_p, y_s, p_k, p_v, p_gla, p_wkv, p_shift, s_k, s_v, s_gla, s_wkv, s_shift)
```

```python
import functools
import math

import jax
import jax.numpy as jnp
from jax import lax
from jax.experimental import pallas as pl
from jax.experimental.pallas import tpu as pltpu

F32 = jnp.float32
BF16 = jnp.bfloat16

D_MODEL = 2048
DEPTH = 4
PAST_LEN = 16384
ALPHA = (2 * DEPTH) ** 0.25
LN_EPS = 1e-5

SWA_HEADS = 32
SWA_KV_HEADS = 4
SWA_GROUP = SWA_HEADS // SWA_KV_HEADS
SWA_HEAD_DIM = 64
SWA_WIDTH = SWA_HEADS * SWA_HEAD_DIM
SWA_KV_WIDTH = SWA_KV_HEADS * SWA_HEAD_DIM
WINDOW = 128
ROT_DIM = SWA_HEAD_DIM // 4
ROPE_THETA = 500000.0

GLA_HEADS = 4
GLA_KEY_DIM = D_MODEL // 2
GLA_VAL_DIM = D_MODEL
GLA_DK = GLA_KEY_DIM // GLA_HEADS
GLA_DV = GLA_VAL_DIM // GLA_HEADS
GLA_GATE_TEMP = 16.0
GLA_CHUNK = 64
GLA_SUB = 16
GLA_SEQS = 4
GLA_NORM_EPS = 1e-5

RWKV_HEAD_DIM = 64
RWKV_HEADS = D_MODEL // RWKV_HEAD_DIM
RWKV_GN_EPS = 64e-5

LANES = 128
VMEM_LIMIT = 56 * 1024 * 1024
PROJ_ROWS = 1024
PROJ_COLS = 1536
SMALL_M_COLS = 512
LORA_ROWS = 512
OUT_ROWS = 512
LN_SUB_ROWS = 256
NEG_BIG = -1e30


def _cparams(sem):
    return pltpu.CompilerParams(dimension_semantics=sem, vmem_limit_bytes=VMEM_LIMIT)


def _silu(x):
    return x * (1.0 / (1.0 + jnp.exp(-x)))


def _log_sigmoid(z):
    return jnp.minimum(z, 0.0) - jnp.log(1.0 + jnp.exp(-jnp.abs(z)))


def _nt(a, b):
    return lax.dot_general(a, b, (((1,), (1,)), ((), ())), preferred_element_type=F32)


def _mm_kernel(x_ref, w_ref, *refs, plans, has_tabs, has_side):
    refs = list(refs)
    tab_ref = refs.pop(0) if has_tabs else None
    side_w_ref = refs.pop(0) if has_side else None
    o_ref = refs.pop(0)
    side_o_ref = refs.pop(0) if has_side else None

    def run(t, plan):
        x16 = x_ref[...].astype(BF16)
        acc = jnp.dot(x16, w_ref[...], preferred_element_type=F32)
        if has_side and t == 0:
            side_o_ref[...] = jnp.dot(x16, side_w_ref[...], preferred_element_type=F32)
        if all(op is None for op in plan):
            o_ref[...] = acc
            return
        for c, op in enumerate(plan):
            chunk = acc[:, c * LANES:(c + 1) * LANES]
            if op == "silu":
                chunk = _silu(chunk)
            elif op in ("rope", "rope_q"):
                chunk = _rope128(chunk, tab_ref[0], tab_ref[1], tab_ref[2])
                if op == "rope_q":
                    chunk = chunk * SWA_HEAD_DIM ** -0.5
            o_ref[:, c * LANES:(c + 1) * LANES] = chunk

    if len(set(plans)) == 1 and not has_side:
        run(0, plans[0])
    else:
        for t, plan in enumerate(plans):
            pl.when(pl.program_id(1) == t)(functools.partial(run, t, plan))


def _mm(x, w, ncols, tn=PROJ_COLS, rope_q_cols=0, rope_cols=0, silu_from=None, rope_tabs=None, side_w=None):
    M, K = x.shape
    tm = min(M, PROJ_ROWS)
    if M < PROJ_ROWS:
        tn = SMALL_M_COLS
    assert M % tm == 0 and ncols % tn == 0

    def op_of(col):
        if col < rope_q_cols:
            return "rope_q"
        if col < rope_cols:
            return "rope"
        return "silu" if silu_from is not None and col >= silu_from else None

    plans = tuple(tuple(op_of(t * tn + c * LANES) for c in range(tn // LANES)) for t in range(ncols // tn))
    w, layer = (w, None) if not isinstance(w, tuple) else w
    w_spec = (pl.BlockSpec((K, tn), lambda i, j: (0, j)) if layer is None else
              pl.BlockSpec((None, K, tn), lambda i, j: (layer, 0, j)))
    operands, specs = [x, w], [pl.BlockSpec((tm, K), lambda i, j: (i, 0)), w_spec]
    if rope_cols:
        period = rope_tabs.shape[1] // tm
        operands.append(rope_tabs)
        specs.append(pl.BlockSpec((3, tm, LANES), lambda i, j: (0, i % period, 0)))
    out_specs = [pl.BlockSpec((tm, tn), lambda i, j: (i, j))]
    out_shape = [jax.ShapeDtypeStruct((M, ncols), F32)]
    if side_w is not None:
        operands.append(side_w)
        specs.append(pl.BlockSpec((K, LANES), lambda i, j: (0, 0)))
        out_specs.append(pl.BlockSpec((tm, LANES), lambda i, j: (i, 0)))
        out_shape.append(jax.ShapeDtypeStruct((M, LANES), F32))
    outs = pl.pallas_call(
        functools.partial(_mm_kernel, plans=plans, has_tabs=bool(rope_cols), has_side=side_w is not None),
        grid=(M // tm, ncols // tn),
        in_specs=specs,
        out_specs=out_specs,
        out_shape=out_shape,
        compiler_params=_cparams(("parallel", "arbitrary")),
        name="proj_mm",
    )(*operands)
    return outs[0] if side_w is None else tuple(outs)


def _mm_ln_kernel(a_ref, w_ref, x_ref, g_ref, b_ref, o_ref):
    for r0 in range(0, a_ref.shape[0], LN_SUB_ROWS):
        rows = slice(r0, min(r0 + LN_SUB_ROWS, a_ref.shape[0]))
        h = jnp.dot(a_ref[rows, :], w_ref[...], preferred_element_type=F32)
        z = ALPHA * x_ref[rows, :] + h
        mu = jnp.mean(z, axis=-1, keepdims=True)
        zc = z - mu
        var = jnp.mean(zc * zc, axis=-1, keepdims=True)
        o_ref[rows, :] = zc * lax.rsqrt(var + LN_EPS) * g_ref[...] + b_ref[...]


def _mm_ln(a, w, x, g, b):
    M, K = a.shape
    tm = min(M, OUT_ROWS)
    w, layer = (w, None) if not isinstance(w, tuple) else w
    D = w.shape[-1]
    return pl.pallas_call(
        _mm_ln_kernel,
        grid=(M // tm,),
        in_specs=[pl.BlockSpec((tm, K), lambda i: (i, 0)),
                  (pl.BlockSpec((K, D), lambda i: (0, 0)) if layer is None else
                   pl.BlockSpec((None, K, D), lambda i: (layer, 0, 0))),
                  pl.BlockSpec((tm, D), lambda i: (i, 0)),
                  pl.BlockSpec((1, D), lambda i: (0, 0)),
                  pl.BlockSpec((1, D), lambda i: (0, 0))],
        out_specs=pl.BlockSpec((tm, D), lambda i: (i, 0)),
        out_shape=jax.ShapeDtypeStruct((M, D), F32),
        compiler_params=_cparams(("parallel",)),
        name="out_proj_ln",
    )(a, w, x, g.reshape(1, D), b.reshape(1, D))


def _rope_tables(pos):
    half = ROT_DIM // 2
    inv = ROPE_THETA ** (-(jnp.arange(half, dtype=F32) * 2.0 / ROT_DIM))
    ang = pos[:, None] * inv[None, :]
    cos, sin = jnp.cos(ang), jnp.sin(ang)
    L = pos.shape[0]
    ones = jnp.ones((L, SWA_HEAD_DIM - ROT_DIM), F32)
    zeros_r = jnp.zeros((L, SWA_HEAD_DIM - ROT_DIM), F32)
    zeros_h = jnp.zeros((L, half), F32)
    c = jnp.concatenate([cos, cos, ones], axis=1)
    s1 = jnp.concatenate([-sin, zeros_h, zeros_r], axis=1)
    s2 = jnp.concatenate([zeros_h, sin, zeros_r], axis=1)
    rep = LANES // SWA_HEAD_DIM
    return jnp.tile(c, (1, rep)), jnp.tile(s1, (1, rep)), jnp.tile(s2, (1, rep))


def _rope128(x, c, s1, s2):
    return x * c + pltpu.roll(x, LANES - ROT_DIM // 2, 1) * s1 + pltpu.roll(x, ROT_DIM // 2, 1) * s2


def _swa_kernel(sink_ref, q_ref, kc_ref, vc_ref, kp_ref, vp_ref, g0_ref, g1_ref, g2_ref, g3_ref,
                o_ref, ko_ref, vo_ref, *, lq, nseq, prompt):
    gate_refs = (g0_ref, g1_ref, g2_ref, g3_ref)
    ppt = SWA_GATE_TILE // LANES
    nk = 2 * WINDOW
    nh = SWA_GROUP
    qi = lax.broadcasted_iota(jnp.int32, (lq, nk), 0)
    sj = lax.broadcasted_iota(jnp.int32, (lq, nk), 1)
    rel = qi + WINDOW - sj
    band = jnp.logical_and(rel >= 0, rel <= WINDOW)
    band_first = jnp.logical_and(band, sj >= jnp.where(pl.program_id(1) == 0, WINDOW, 0)) if prompt else band
    m0 = lax.broadcasted_iota(jnp.int32, (lq, LANES), 1) < SWA_HEAD_DIM
    hd = SWA_HEAD_DIM
    ppd = nh // 2

    def attend(bi):
        rs = slice(bi * lq, (bi + 1) * lq)
        mask = band_first if bi == 0 else band
        k_cur, v_cur = kc_ref[rs, :], vc_ref[rs, :]
        if prompt:
            if bi == 0:
                k_prev, v_prev = kp_ref[...], vp_ref[...]
            else:
                k_prev, v_prev = kc_ref[(bi - 1) * lq:bi * lq, :], vc_ref[(bi - 1) * lq:bi * lq, :]
            k_out, v_out = k_cur, v_cur
        else:
            k_prev, v_prev = kp_ref[bi], vp_ref[bi]
            k_out = jnp.concatenate([k_prev[lq:], k_cur], axis=0)
            v_out = jnp.concatenate([v_prev[lq:], v_cur], axis=0)
            zpad = jnp.zeros((WINDOW - lq, SWA_KV_WIDTH), F32)
            k_cur = jnp.concatenate([k_cur, zpad], axis=0)
            v_cur = jnp.concatenate([v_cur, zpad], axis=0)
        k_all = jnp.concatenate([k_prev, k_cur], axis=0)
        v_all = jnp.concatenate([v_prev, v_cur], axis=0)
        return k_out, v_out, [kv_group(rs, mask, k_all, v_all, kh) for kh in range(SWA_KV_HEADS)]

    def kv_group(rs, mask, k_all, v_all, kh):
        kg = k_all[:, kh * hd:(kh + 1) * hd]
        vg = v_all[:, kh * hd:(kh + 1) * hd]
        kk2 = jnp.concatenate([kg, kg], axis=1).astype(BF16)
        vv2 = jnp.concatenate([vg, vg], axis=1).astype(BF16)
        p0 = kh * ppd
        pairs = range(p0, p0 + ppd)
        rows = []
        for p in pairs:
            q2 = q_ref[rs, p * LANES:(p + 1) * LANES]
            rows += [jnp.where(m0, q2, 0.0), jnp.where(m0, 0.0, q2)]
        qs = jnp.concatenate(rows, axis=0).astype(BF16)
        s_all = _nt(qs, kk2)
        yield None
        es, dens = [], []
        for i in range(2 * ppd):
            snk = sink_ref[2 * p0 + i]
            s = jnp.where(mask, s_all[i * lq:(i + 1) * lq], NEG_BIG)
            m = jnp.maximum(jnp.max(s, axis=-1, keepdims=True), snk)
            e = jnp.exp(s - m)
            dens.append(jnp.sum(e, axis=-1, keepdims=True) + jnp.exp(snk - m))
            es.append(e.astype(BF16))
        yield None
        o_all = jnp.dot(jnp.concatenate(es, axis=0), vv2, preferred_element_type=F32)
        outs = []
        for i, p in enumerate(pairs):
            oa = o_all[2 * i * lq:(2 * i + 1) * lq] / dens[2 * i]
            ob = o_all[(2 * i + 1) * lq:(2 * i + 2) * lq] / dens[2 * i + 1]
            g2 = gate_refs[p // ppt][rs, (p % ppt) * LANES:(p % ppt + 1) * LANES]
            outs.append((p, (jnp.where(m0, oa, ob) * g2).astype(o_ref.dtype)))
        yield outs

    seqs = [attend(bi) for bi in range(nseq)]
    groups = [g for _, _, gs in seqs for g in gs]
    if not prompt:
        for _ in range(2):
            for g in groups:
                next(g)
        done = [next(g) for g in groups]
    else:
        done = [list(g)[-1] for g in groups]
    for bi, (k_out, v_out, gs) in enumerate(seqs):
        if not prompt:
            ko_ref[bi], vo_ref[bi] = k_out, v_out
        elif bi == nseq - 1:
            ko_ref[0], vo_ref[0] = k_out, v_out
        for outs in done[bi * len(gs):(bi + 1) * len(gs)]:
            for p, out in outs:
                o_ref[bi * lq:(bi + 1) * lq, p * LANES:(p + 1) * LANES] = out


SWA_SAMPLE_SEQS = 4
SWA_PROMPT_BLOCKS = 4
SWA_GATE_TILE = 512


def _swa_gate_specs(rows, index):
    g0 = (SWA_WIDTH + 2 * SWA_KV_WIDTH) // SWA_GATE_TILE
    return [pl.BlockSpec((rows, SWA_GATE_TILE), functools.partial(index, g0 + t))
            for t in range(SWA_WIDTH // SWA_GATE_TILE)]


def _swa_prompt(h, sink, B, S):
    nblk = SWA_PROMPT_BLOCKS
    nb = S // (nblk * WINDOW)
    rows = nblk * WINDOW
    kcol = SWA_WIDTH // SWA_KV_WIDTH
    prev = lambda b, n: (b * nb + n) * nblk - jnp.minimum(n, 1)
    return pl.pallas_call(
        functools.partial(_swa_kernel, lq=WINDOW, nseq=nblk, prompt=True),
        grid=(B, nb),
        in_specs=[pl.BlockSpec(memory_space=pltpu.SMEM),
                  pl.BlockSpec((rows, SWA_WIDTH), lambda b, n: (b * nb + n, 0)),
                  pl.BlockSpec((rows, SWA_KV_WIDTH), lambda b, n: (b * nb + n, kcol)),
                  pl.BlockSpec((rows, SWA_KV_WIDTH), lambda b, n: (b * nb + n, kcol + 1)),
                  pl.BlockSpec((WINDOW, SWA_KV_WIDTH), lambda b, n: (prev(b, n), kcol)),
                  pl.BlockSpec((WINDOW, SWA_KV_WIDTH), lambda b, n: (prev(b, n), kcol + 1)),
                  *_swa_gate_specs(rows, lambda col, b, n: (b * nb + n, col))],
        out_specs=[pl.BlockSpec((rows, SWA_WIDTH), lambda b, n: (b * nb + n, 0)),
                   pl.BlockSpec((1, WINDOW, SWA_KV_WIDTH), lambda b, n: (b, 0, 0)),
                   pl.BlockSpec((1, WINDOW, SWA_KV_WIDTH), lambda b, n: (b, 0, 0))],
        out_shape=[jax.ShapeDtypeStruct((B * S, SWA_WIDTH), BF16),
                   jax.ShapeDtypeStruct((B, WINDOW, SWA_KV_WIDTH), F32),
                   jax.ShapeDtypeStruct((B, WINDOW, SWA_KV_WIDTH), F32)],
        compiler_params=_cparams(("parallel", "arbitrary")),
        name="swa_prompt",
    )(sink, h, h, h, h, h, h, h, h, h)


def _swa_sample(h, sink, cache_k, cache_v, j, B, L):
    ns = SWA_SAMPLE_SEQS
    assert B % ns == 0
    cache = pl.BlockSpec((None, ns, WINDOW, SWA_KV_WIDTH), lambda b: (j, b, 0, 0))
    kcol = SWA_WIDTH // SWA_KV_WIDTH
    return pl.pallas_call(
        functools.partial(_swa_kernel, lq=L, nseq=ns, prompt=False),
        grid=(B // ns,),
        in_specs=[pl.BlockSpec(memory_space=pltpu.SMEM),
                  pl.BlockSpec((ns * L, SWA_WIDTH), lambda b: (b, 0)),
                  pl.BlockSpec((ns * L, SWA_KV_WIDTH), lambda b: (b, kcol)),
                  pl.BlockSpec((ns * L, SWA_KV_WIDTH), lambda b: (b, kcol + 1)),
                  cache, cache,
                  *_swa_gate_specs(ns * L, lambda col, b: (b, col))],
        out_specs=[pl.BlockSpec((ns * L, SWA_WIDTH), lambda b: (b, 0)),
                   pl.BlockSpec((ns, WINDOW, SWA_KV_WIDTH), lambda b: (b, 0, 0)),
                   pl.BlockSpec((ns, WINDOW, SWA_KV_WIDTH), lambda b: (b, 0, 0))],
        out_shape=[jax.ShapeDtypeStruct((B * L, SWA_WIDTH), BF16),
                   jax.ShapeDtypeStruct((B, WINDOW, SWA_KV_WIDTH), F32),
                   jax.ShapeDtypeStruct((B, WINDOW, SWA_KV_WIDTH), F32)],
        compiler_params=_cparams(("parallel",)),
        name="swa_sample",
    )(sink, h, h, h, cache_k, cache_v, h, h, h, h)


def _swa_layer(x, cache, j, w_in, sink, w_out, ln_g, ln_b, B, L):
    first = 0 if cache is None else PAST_LEN
    rows = max(L, min(B * L, PROJ_ROWS))
    tabs = jnp.stack(_rope_tables(first + (jnp.arange(rows) % L).astype(F32)))
    h = _mm(x, w_in, 2 * SWA_WIDTH + 2 * SWA_KV_WIDTH, rope_q_cols=SWA_WIDTH, rope_cols=SWA_WIDTH + SWA_KV_WIDTH,
            silu_from=SWA_WIDTH + 2 * SWA_KV_WIDTH, rope_tabs=tabs)
    if cache is None:
        o, nk, nv = _swa_prompt(h, sink, B, L)
    else:
        ck = cache[0].reshape(-1, B, WINDOW, SWA_KV_WIDTH)
        cv = cache[1].reshape(-1, B, WINDOW, SWA_KV_WIDTH)
        o, nk, nv = _swa_sample(h, sink, ck, cv, j, B, L)
    shape = (B, WINDOW, SWA_KV_HEADS, SWA_HEAD_DIM)
    return _mm_ln(o, w_out, x, ln_g, ln_b), nk.reshape(shape), nv.reshape(shape)


def _tril3(C):
    return (lax.broadcasted_iota(jnp.int32, (C, 3 * C), 1) % C
            <= lax.broadcasted_iota(jnp.int32, (C, 3 * C), 0)).astype(BF16)


def _cumsum_rows(x, tril3):
    h1 = x.astype(BF16)
    r1 = x - h1.astype(F32)
    h2 = r1.astype(BF16)
    h3 = (r1 - h2.astype(F32)).astype(BF16)
    return jnp.dot(tril3, jnp.concatenate([h1, h2, h3], axis=0), preferred_element_type=F32)


def _gla_chunk_kernel(q_ref, k_ref, v_ref, gate_ref, al_ref, wa_ref, ba_ref, ng_ref, s0_ref, o_ref, so_ref, st_ref,
                      *, C, SB):
    c_idx = pl.program_id(1)
    H, DK, DV = GLA_HEADS, GLA_DK, GLA_DV
    nseq = o_ref.shape[0]
    staged = C == GLA_CHUNK

    @pl.when(c_idx == 0)
    def _():
        for bi in range(nseq):
            for h in range(H):
                st_ref[bi * H + h] = s0_ref[bi, h].T

    row = lax.broadcasted_iota(jnp.int32, (C, H * DK), 0)
    causal = lax.broadcasted_iota(jnp.int32, (C, C), 1) <= lax.broadcasted_iota(jnp.int32, (C, C), 0)
    tril3 = _tril3(C)

    def prep(bi):
        z = jnp.dot(al_ref[bi].astype(BF16), wa_ref[...], preferred_element_type=F32) + ba_ref[...]
        b = _cumsum_rows(_log_sigmoid(z) * (1.0 / GLA_GATE_TEMP), tril3)
        q = q_ref[bi] * (DK ** -0.5)
        k = k_ref[bi]
        b_last = b[C - 1:C, :]
        qis, kjs = [], []
        for i in range(C // SB):
            r0, r1 = i * SB, (i + 1) * SB
            bn = b[r0:r0 + 1, :]
            qis.append((q[r0:r1] * jnp.exp(b[r0:r1] - bn)).astype(BF16))
            kjs.append((k * jnp.exp(jnp.where(row < r1, bn - b, 0.0))).astype(BF16))
        return dict(v16=v_ref[bi].astype(BF16), e_last=jnp.exp(b_last), qe=(q * jnp.exp(b)).astype(BF16),
                    kd=(k * jnp.exp(b_last - b)).astype(BF16), qis=qis, kjs=kjs)

    def head(bi, p, h):
        ks, vs = slice(h * DK, (h + 1) * DK), slice(h * DV, (h + 1) * DV)
        st = st_ref[bi * H + h]
        a_parts = [_nt(qi[:, ks], kj[:, ks]) for qi, kj in zip(p["qis"], p["kjs"])]
        v16 = p["v16"][:, vs]
        o_inter = _nt(p["qe"][:, ks], st.astype(BF16))
        ds = lax.dot_general(v16, p["kd"][:, ks], (((0,), (0,)), ((), ())), preferred_element_type=F32)
        yield None
        a = jnp.concatenate(a_parts, axis=0) if len(a_parts) > 1 else a_parts[0]
        a = jnp.where(causal, a, 0.0).astype(BF16)
        o = o_inter + jnp.dot(a, v16, preferred_element_type=F32)
        st_new = st * p["e_last"][:, ks] + ds
        yield None
        on = o * lax.rsqrt(jnp.mean(o * o, axis=-1, keepdims=True) + GLA_NORM_EPS) * ng_ref[...]
        yield st_new, (on * gate_ref[bi, :, vs]).astype(o_ref.dtype)

    preps = [prep(bi) for bi in range(nseq)]
    chains = [[head(bi, preps[bi], h) for h in range(H)] for bi in range(nseq)]
    if staged:
        for _ in range(2):
            for per_seq in chains:
                for chain in per_seq:
                    next(chain)
        results = [[next(chain) for chain in per_seq] for per_seq in chains]
    else:
        results = [[list(chain)[-1] for chain in per_seq] for per_seq in chains]
    for bi in range(nseq):
        for h, (st_new, out) in enumerate(results[bi]):
            st_ref[bi * H + h] = st_new
            o_ref[bi, :, h * DV:(h + 1) * DV] = out

    @pl.when(c_idx == pl.num_programs(1) - 1)
    def _():
        for bi in range(nseq):
            for h in range(H):
                so_ref[bi, h] = st_ref[bi * H + h].T


def _gla_chunk(h, a_low, w_a2, b_a, norm_g, states, j, B, L):
    C = min(GLA_CHUNK, L)
    SB = min(GLA_SUB, C)
    nc = L // C
    ns = GLA_SEQS
    assert B % ns == 0
    tok = lambda width, col: pl.BlockSpec((ns, C, width), lambda b, c: (b, c, col))
    st0 = pl.BlockSpec((None, ns, GLA_HEADS, GLA_DK, GLA_DV), lambda b, c: (j, b, 0, 0, 0))
    st = pl.BlockSpec((ns, GLA_HEADS, GLA_DK, GLA_DV), lambda b, c: (b, 0, 0, 0))
    h3 = h.reshape(B, L, h.shape[1])
    o, st_out = pl.pallas_call(
        functools.partial(_gla_chunk_kernel, C=C, SB=SB),
        grid=(B // ns, nc),
        in_specs=[tok(GLA_KEY_DIM, 0), tok(GLA_KEY_DIM, 1), tok(GLA_VAL_DIM, 1), tok(GLA_VAL_DIM, 2),
                  tok(LANES, 0),
                  pl.BlockSpec((LANES, GLA_KEY_DIM), lambda b, c: (0, 0)),
                  pl.BlockSpec((1, GLA_KEY_DIM), lambda b, c: (0, 0)),
                  pl.BlockSpec((1, GLA_DV), lambda b, c: (0, 0)),
                  st0],
        out_specs=[tok(GLA_VAL_DIM, 0), st],
        out_shape=[jax.ShapeDtypeStruct((B, L, GLA_VAL_DIM), BF16),
                   jax.ShapeDtypeStruct((B, GLA_HEADS, GLA_DK, GLA_DV), F32)],
        scratch_shapes=[pltpu.VMEM((ns * GLA_HEADS, GLA_DV, GLA_DK), F32)],
        compiler_params=_cparams(("parallel", "arbitrary")),
        name="gla_chunk",
    )(h3, h3, h3, h3, a_low.reshape(B, L, LANES), w_a2, b_a.reshape(1, GLA_KEY_DIM), norm_g.reshape(1, GLA_DV),
      states)
    return o.reshape(B * L, GLA_VAL_DIM), st_out


def _gla_layer(x, states, j, w_in, w_low, w_a2, b_a, norm_g, w_out, ln_g, ln_b, B, L):
    h, a_low = _mm(x, w_in, 2 * GLA_KEY_DIM + 2 * GLA_VAL_DIM, silu_from=2 * GLA_KEY_DIM + GLA_VAL_DIM, side_w=w_low)
    if states is None:
        states, j = jnp.zeros((1, B, GLA_HEADS, GLA_DK, GLA_DV), F32), 0
    o, st = _gla_chunk(h, a_low, w_a2, b_a, norm_g, states, j, B, L)
    return _mm_ln(o, w_out, x, ln_g, ln_b), st


SUBLANES = 8
RWKV_GATE_INDEX = 3


def _prev_rows(x, pa_ref, pb_ref, seq_len):
    if seq_len is None:
        return pa_ref[...]
    tm = x.shape[0]
    starts_seq = (pl.program_id(0) * tm) % seq_len == 0
    first = jnp.where(starts_seq, pb_ref[0], pa_ref[SUBLANES - 1:SUBLANES, :])
    rows = lax.broadcasted_iota(jnp.int32, x.shape, 0)
    return jnp.where(rows == 0, first, pltpu.roll(x, 1, 0))


def _prev_specs(x, xprev, shift, tm, seq_len, grid_rank):
    pad = (0,) * (grid_rank - 1)
    sh = shift.reshape(shift.shape[0], 1, D_MODEL)
    if xprev is not None:
        return (xprev, sh), [pl.BlockSpec((tm, D_MODEL), lambda i, *_: (i, 0)),
                             pl.BlockSpec((1, 1, D_MODEL), lambda i, *_: (0, 0, 0))], None
    assert seq_len % tm == 0
    per = tm // SUBLANES
    return (x, sh), [pl.BlockSpec((SUBLANES, D_MODEL), lambda i, *_: (jnp.maximum(i * per - 1, 0), 0)),
                     pl.BlockSpec((1, 1, D_MODEL), lambda i, *_: ((i * tm) // seq_len, 0, 0))], seq_len


def _rwkv_proj_kernel(x_ref, pa_ref, pb_ref, mu_ref, w_ref, o_ref, xm_ref, *, seq_len):
    def step(first_col, gate):
        if first_col:
            x = x_ref[...]
            xm = (x + (_prev_rows(x, pa_ref, pb_ref, seq_len) - x) * mu_ref[0]).astype(BF16)
            xm_ref[...] = xm
        else:
            xm = xm_ref[...]
        acc = jnp.dot(xm, w_ref[0], preferred_element_type=F32)
        o_ref[0] = _silu(acc) if gate else acc

    is_first = pl.program_id(2) == 0
    is_gate = pl.program_id(1) == RWKV_GATE_INDEX
    for first_col in (True, False):
        for gate in (True, False):
            here = jnp.logical_and(is_first == first_col, is_gate == gate)
            pl.when(here)(functools.partial(step, first_col, gate))


def _rwkv_proj(x, xprev, shift, seq_len, mu4, w4, tn=1024):
    M = x.shape[0]
    tm = min(M, PROJ_ROWS)
    if M < PROJ_ROWS:
        tn = SMALL_M_COLS
    prev_ops, prev_specs, inline_len = _prev_specs(x, xprev, shift, tm, seq_len, 3)
    return pl.pallas_call(
        functools.partial(_rwkv_proj_kernel, seq_len=inline_len),
        grid=(M // tm, 4, D_MODEL // tn),
        in_specs=[pl.BlockSpec((tm, D_MODEL), lambda i, m, j: (i, 0)),
                  *prev_specs,
                  pl.BlockSpec((1, 1, D_MODEL), lambda i, m, j: (m, 0, 0)),
                  pl.BlockSpec((1, D_MODEL, tn), lambda i, m, j: (m, 0, j))],
        out_specs=pl.BlockSpec((1, tm, tn), lambda i, m, j: (m, i, j)),
        out_shape=jax.ShapeDtypeStruct((4, M, D_MODEL), F32),
        scratch_shapes=[pltpu.VMEM((tm, D_MODEL), BF16)],
        compiler_params=_cparams(("parallel", "arbitrary", "arbitrary")),
        name="rwkv_proj",
    )(x, *prev_ops, mu4, w4)


def _rwkv_lora_kernel(x_ref, pa_ref, pb_ref, mu_ref, w1_ref, w2_ref, w0_ref, a1_ref, a2_ref, a0_ref, lw_ref, a_ref,
                      *, seq_len):
    x = x_ref[...]
    xx = _prev_rows(x, pa_ref, pb_ref, seq_len) - x
    xw = (x + xx * mu_ref[0]).astype(BF16)
    xa = (x + xx * mu_ref[1]).astype(BF16)
    t = jnp.tanh(jnp.dot(xw, w1_ref[...], preferred_element_type=F32))
    wl = w0_ref[...] + jnp.dot(t.astype(BF16), w2_ref[...], preferred_element_type=F32)
    lw_ref[...] = (-math.exp(-0.5)) / (1.0 + jnp.exp(-wl))
    al = jnp.dot(xa, a1_ref[...], preferred_element_type=F32)
    az = a0_ref[...] + jnp.dot(al.astype(BF16), a2_ref[...], preferred_element_type=F32)
    a_ref[...] = 1.0 / (1.0 + jnp.exp(-az))


def _rwkv_lora(x, xprev, shift, seq_len, mu2, w1, w2, w0, a1, a2, a0):
    M = x.shape[0]
    tm = min(M, LORA_ROWS)
    R = w1.shape[1]
    full = lambda shape: pl.BlockSpec(shape, lambda i: tuple(0 for _ in shape))
    prev_ops, prev_specs, inline_len = _prev_specs(x, xprev, shift, tm, seq_len, 1)
    return pl.pallas_call(
        functools.partial(_rwkv_lora_kernel, seq_len=inline_len),
        grid=(M // tm,),
        in_specs=[pl.BlockSpec((tm, D_MODEL), lambda i: (i, 0)),
                  *prev_specs,
                  full((2, 1, D_MODEL)), full((D_MODEL, R)), full((R, D_MODEL)), full((1, D_MODEL)),
                  full((D_MODEL, R)), full((R, D_MODEL)), full((1, D_MODEL))],
        out_specs=[pl.BlockSpec((tm, D_MODEL), lambda i: (i, 0)),
                   pl.BlockSpec((tm, D_MODEL), lambda i: (i, 0))],
        out_shape=[jax.ShapeDtypeStruct((M, D_MODEL), F32), jax.ShapeDtypeStruct((M, D_MODEL), F32)],
        compiler_params=_cparams(("parallel",)),
        name="rwkv_lora",
    )(x, *prev_ops, mu2, w1, w2, w0.reshape(1, D_MODEL), a1, a2, a0.reshape(1, D_MODEL))


def _head_ones():
    r = lax.broadcasted_iota(jnp.int32, (LANES, LANES), 0) // RWKV_HEAD_DIM
    c = lax.broadcasted_iota(jnp.int32, (LANES, LANES), 1) // RWKV_HEAD_DIM
    e = (r == c).astype(BF16)
    return jnp.concatenate([e, e], axis=0)


def _segsum(x, e2):
    hi = x.astype(BF16)
    lo = (x - hi.astype(F32)).astype(BF16)
    return jnp.dot(jnp.concatenate([hi, lo], axis=1), e2, preferred_element_type=F32)


def _segsum_wide(x, e2):
    rows, n = x.shape[0], x.shape[1] // LANES
    s = _segsum(jnp.concatenate([x[:, i * LANES:(i + 1) * LANES] for i in range(n)], axis=0), e2)
    return jnp.concatenate([s[i * rows:(i + 1) * rows] for i in range(n)], axis=1)


RWKV_PAIRS = 16
RWKV_SEQS = 2
RWKV_CHUNK = 64


def _rwkv_chunk_kernel(r_ref, k_ref, v_ref, gt_ref, lw_ref, a_ref, kk_ref, ka_ref, rk_ref, gng_ref, gnb_ref, s0_ref,
                       o_ref, so_ref, st_ref, *, lreal):
    C, N = RWKV_CHUNK, RWKV_HEAD_DIM
    t_idx = pl.program_id(2)
    e2 = _head_ones()
    m0 = lax.broadcasted_iota(jnp.int32, (C, LANES), 1) < N
    ti = lax.broadcasted_iota(jnp.int32, (C, 2 * C), 0)
    si2 = lax.broadcasted_iota(jnp.int32, (C, 2 * C), 1)
    si = si2 % C
    strict, incl, left = si < ti, si <= ti, si2 < C
    bd = (lax.broadcasted_iota(jnp.int32, (LANES, LANES), 0) // N
          == lax.broadcasted_iota(jnp.int32, (LANES, LANES), 1) // N)
    zs = jnp.zeros((N, N), F32)

    nseq = o_ref.shape[0]

    @pl.when(t_idx == 0)
    def _():
        for bi in range(nseq):
            for p in range(RWKV_PAIRS):
                top = jnp.concatenate([s0_ref[bi, 2 * p], zs], axis=1)
                bot = jnp.concatenate([zs, s0_ref[bi, 2 * p + 1]], axis=1)
                st_ref[bi * RWKV_PAIRS + p] = jnp.concatenate([top, bot], axis=0)

    def both(x, y):
        parts = [x] if y is None else [x, y]
        return jnp.concatenate([jnp.where(m0, z, 0.0) for z in parts] + [jnp.where(m0, 0.0, z) for z in parts],
                               axis=0).astype(BF16)

    def load(x):
        if lreal < C:
            x = jnp.concatenate([x, jnp.zeros((C - lreal, x.shape[1]), F32)], axis=0)
        return x

    inv_n = 1.0 / N
    n_iter = max(1, (lreal - 1).bit_length())
    lanes = [slice(p * LANES, (p + 1) * LANES) for p in range(RWKV_PAIRS)]

    def prep(bi):
        r, k, v = load(r_ref[0, bi]), load(k_ref[0, bi]), load(v_ref[0, bi])
        lw, a = load(lw_ref[bi]), load(a_ref[bi])
        kk = k * kk_ref[...]
        kk = kk / jnp.maximum(jnp.sqrt(_segsum_wide(kk * kk, e2)), 1e-12)
        kp = k * (1.0 + (a - 1.0) * ka_ref[...])
        c = _cumsum_rows(lw, _tril3(C))
        e_c, e_nc = jnp.exp(c), jnp.exp(-c)
        return dict(v=v, e_c=e_c, at_w=-kk * jnp.exp(c - lw), rt_w=r * e_c,
                    bt_w=(kk * a * e_nc).astype(BF16), kt_w=(kp * e_nc).astype(BF16),
                    bonus=_segsum_wide((r * kp * rk_ref[...])[:lreal], e2))

    def scores(bi, q, p):
        ls = lanes[p]
        at, rt = q["at_w"][:, ls], q["rt_w"][:, ls]
        bk = jnp.concatenate([q["bt_w"][:, ls], q["kt_w"][:, ls]], axis=0)
        g = _nt(both(at, rt), bk)
        s_bd = st_ref[bi * RWKV_PAIRS + p]
        pq = _nt(jnp.concatenate([at, rt], axis=0).astype(BF16), s_bd.astype(BF16))
        return dict(bk=bk, s_bd=s_bd, g=g, pq=pq)

    def setup(q, p, d):
        g, pq = d.pop("g"), d.pop("pq")
        aa0, rr0 = jnp.where(strict, g[0:C], 0.0), jnp.where(incl, g[C:2 * C], 0.0)
        aa1, rr1 = jnp.where(strict, g[2 * C:3 * C], 0.0), jnp.where(incl, g[3 * C:], 0.0)
        a_ab = jnp.where(left, aa0, pltpu.roll(aa1, C, 1))
        a_ak = jnp.where(left, pltpu.roll(aa0, C, 1), aa1)
        x = pq[:C] + jnp.dot(a_ak.astype(BF16), both(q["v"][:, lanes[p]], None), preferred_element_type=F32)
        d.update(y0=pq[C:], rr=jnp.concatenate([rr0, rr1], axis=1).astype(BF16), x=x, ac=a_ab)

    def neumann(d, it):
        ac = d["ac"]
        ac16 = ac.astype(BF16)
        rhs = both(d["x"], None)
        if it < n_iter - 1:
            a_bd = jnp.concatenate([jnp.where(left, ac, 0.0), jnp.where(left, 0.0, ac)], axis=0).astype(BF16)
            res = jnp.dot(ac16, jnp.concatenate([rhs, a_bd], axis=1), preferred_element_type=F32)
            d["x"] = d["x"] + res[:, :LANES]
            d["ac"] = res[:, LANES:]
        else:
            d["x"] = d["x"] + jnp.dot(ac16, rhs, preferred_element_type=F32)

    def finish(q, p, d):
        ls = lanes[p]
        u, vp = d["x"], q["v"][:, ls]
        y = d["y0"] + jnp.dot(d["rr"], both(u, vp), preferred_element_type=F32)
        uv = jnp.concatenate([u, vp], axis=0).astype(BF16)
        ds = lax.dot_general(uv, d["bk"], (((0,), (0,)), ((), ())), preferred_element_type=F32)
        return (d["s_bd"] + jnp.where(bd, ds, 0.0)) * q["e_c"][C - 1:C, ls], y[:lreal]

    def output(bi, q, done):
        y = jnp.concatenate([yp for _, yp in done], axis=1)
        yc = y - _segsum_wide(y, e2) * inv_n
        yv = _segsum_wide(yc * yc, e2) * inv_n
        yn = yc * lax.rsqrt(yv + RWKV_GN_EPS) * gng_ref[...] + gnb_ref[...]
        return ((yn + q["bonus"] * q["v"][:lreal]) * gt_ref[0, bi]).astype(o_ref.dtype)

    pairs = range(RWKV_PAIRS)
    stages = [
        lambda bi, s: s.update(q=prep(bi)),
        lambda bi, s: s.update(work=[scores(bi, s["q"], p) for p in pairs]),
        lambda bi, s: [setup(s["q"], p, d) for p, d in enumerate(s["work"])],
        lambda bi, s: [neumann(d, it) for it in range(n_iter) for d in s["work"]],
        lambda bi, s: s.update(done=[finish(s["q"], p, d) for p, d in enumerate(s["work"])]),
        lambda bi, s: s.update(out=output(bi, s["q"], s["done"])),
    ]
    seqs = [dict() for _ in range(nseq)]
    lag = 1
    for tick in range(len(stages) + lag * (nseq - 1)):
        for bi in range(nseq):
            if 0 <= tick - lag * bi < len(stages):
                stages[tick - lag * bi](bi, seqs[bi])
    for bi, s in enumerate(seqs):
        for p, (s_new, _) in enumerate(s["done"]):
            st_ref[bi * RWKV_PAIRS + p] = s_new
        o_ref[bi] = s["out"]

    @pl.when(t_idx == pl.num_programs(2) - 1)
    def _():
        for bi in range(nseq):
            for p in range(RWKV_PAIRS):
                s = st_ref[bi * RWKV_PAIRS + p]
                so_ref[bi, 2 * p] = s[:N, :N]
                so_ref[bi, 2 * p + 1] = s[N:, N:]


def _rwkv_scan(proj, lw, a, k_k, k_a, r_k, gn_g, gn_b, states, j, B, L):
    tc = min(L, RWKV_CHUNK)
    nt = L // tc
    W = RWKV_PAIRS * LANES
    ng = D_MODEL // W
    hpg = 2 * RWKV_PAIRS
    ns = RWKV_SEQS
    assert B % ns == 0
    hd = RWKV_HEAD_DIM
    tok = lambda m: pl.BlockSpec((1, ns, tc, W), lambda b, g, t: (m, b, t, g))
    vec = pl.BlockSpec((ns, tc, W), lambda b, g, t: (b, t, g))
    par = pl.BlockSpec((1, W), lambda b, g, t: (0, g))
    st = pl.BlockSpec((ns, hpg, hd, hd), lambda b, g, t: (b, g, 0, 0))
    st0 = pl.BlockSpec((None, ns, hpg, hd, hd), lambda b, g, t: (j, b, g, 0, 0))
    proj4 = proj.reshape(4, B, L, D_MODEL)
    o, st_out = pl.pallas_call(
        functools.partial(_rwkv_chunk_kernel, lreal=tc),
        grid=(B // ns, ng, nt),
        in_specs=[tok(0), tok(1), tok(2), tok(3), vec, vec, par, par, par, par, par, st0],
        out_specs=[vec, st],
        out_shape=[jax.ShapeDtypeStruct((B, L, D_MODEL), BF16),
                   jax.ShapeDtypeStruct((B, RWKV_HEADS, hd, hd), F32)],
        scratch_shapes=[pltpu.VMEM((ns * RWKV_PAIRS, LANES, LANES), F32)],
        compiler_params=_cparams(("parallel", "parallel", "arbitrary")),
        name="rwkv_scan",
    )(proj4, proj4, proj4, proj4, lw.reshape(B, L, D_MODEL), a.reshape(B, L, D_MODEL), k_k.reshape(1, D_MODEL),
      k_a.reshape(1, D_MODEL), r_k.reshape(1, D_MODEL), gn_g.reshape(1, D_MODEL), gn_b.reshape(1, D_MODEL), states)
    return o.reshape(B * L, D_MODEL), st_out


def _rwkv_layer(x, shift, states, j, p, ln_g, ln_b, B, L):
    x3 = x.reshape(B, L, D_MODEL)
    if L % PROJ_ROWS == 0 and L % LORA_ROWS == 0:
        xprev = None
    else:
        xprev = jnp.concatenate([shift[:, None, :], x3[:, :-1]], axis=1).reshape(B * L, D_MODEL)
    proj = _rwkv_proj(x, xprev, shift, L, p["mu4"], p["w4"])
    lw, a = _rwkv_lora(x, xprev, shift, L, p["mu2"], p["w1"], p["w2"], p["w0"], p["a1"], p["a2"], p["a0"])
    o, st = _rwkv_scan(proj, lw, a, p["k_k"], p["k_a"], p["r_k"], p["gn_g"], p["gn_b"], states, j, B, L)
    return _mm_ln(o, p["w_out"], x, ln_g, ln_b), st, x3[:, -1]


def _pad_rank(w, axis):
    pad = [(0, 0), (0, 0)]
    pad[axis] = (0, LANES - w.shape[axis])
    return jnp.pad(w, pad).astype(BF16)


def _trunk(x3, cache, w):
    B, L, _ = x3.shape
    prompt = cache is None
    x = x3.reshape(B * L, D_MODEL)
    new_k, new_v, new_gla, new_wkv, new_shift = [], [], [], [], []
    for layer in range(DEPTH):
        kind, j = layer % 3, layer // 3
        g, b = w["ln_g"][layer], w["ln_b"][layer]
        if kind == 0:
            c = None if prompt else (cache["k"], cache["v"])
            x, nk, nv = _swa_layer(x, c, j, (w["swa_w_in"], j), w["swa_sink"][j], (w["swa_w_out"], j), g, b, B, L)
            new_k.append(nk)
            new_v.append(nv)
        elif kind == 1:
            st = None if prompt else cache["gla"]
            x, st = _gla_layer(x, st, j, (w["gla_w_in"], j), w["gla_w_low"][j], w["gla_w_a2"][j], w["gla_b_a"][j],
                               w["gla_norm_g"][j], (w["gla_w_out"], j), g, b, B, L)
            new_gla.append(st)
        else:
            if prompt:
                shift0 = jnp.zeros((B, D_MODEL), F32)
                s0, js = jnp.zeros((1, B, RWKV_HEADS, RWKV_HEAD_DIM, RWKV_HEAD_DIM), F32), 0
            else:
                shift0, s0, js = cache["shift"][j], cache["wkv"], j
            x, st, sh = _rwkv_layer(x, shift0, s0, js, w["rwkv"][j], g, b, B, L)
            new_wkv.append(st)
            new_shift.append(sh)
    stack = lambda parts: parts[0][None] if len(parts) == 1 else jnp.stack(parts)
    return (x.reshape(B, L, D_MODEL), stack(new_k), stack(new_v), stack(new_gla), stack(new_wkv), stack(new_shift))


def kernel(x_prompt, x_sample, cache_swa_k, cache_swa_v, state_gla, state_rwkv, state_rwkv_shift, ln_g, ln_b, swa_w_in, swa_sink, swa_w_out, gla_w_in, gla_w_a2, gla_b_a, gla_norm_g, gla_w_out, rwkv_mu, rwkv_w_rkvg, rwkv_w0, rwkv_w1, rwkv_w2, rwkv_a0, rwkv_a1, rwkv_a2, rwkv_k_k, rwkv_k_a, rwkv_r_k, rwkv_gn_g, rwkv_gn_b, rwkv_w_out):
    n_rwkv = rwkv_mu.shape[0]
    gla_main = 2 * GLA_KEY_DIM + 2 * GLA_VAL_DIM
    rwkv = []
    for j in range(n_rwkv):
        rwkv.append(dict(
            mu4=rwkv_mu[j][jnp.array([0, 2, 3, 5])].reshape(4, 1, D_MODEL),
            mu2=rwkv_mu[j][jnp.array([1, 4])].reshape(2, 1, D_MODEL),
            w4=rwkv_w_rkvg[j].astype(BF16),
            w1=_pad_rank(rwkv_w1[j], 1), w2=_pad_rank(rwkv_w2[j], 0), w0=rwkv_w0[j],
            a1=_pad_rank(rwkv_a1[j], 1), a2=_pad_rank(rwkv_a2[j], 0), a0=rwkv_a0[j],
            k_k=rwkv_k_k[j], k_a=rwkv_k_a[j], r_k=rwkv_r_k[j], gn_g=rwkv_gn_g[j], gn_b=rwkv_gn_b[j],
            w_out=rwkv_w_out[j].astype(BF16)))
    w = dict(ln_g=ln_g, ln_b=ln_b,
             swa_w_in=swa_w_in.astype(BF16), swa_sink=swa_sink, swa_w_out=swa_w_out.astype(BF16),
             gla_w_in=gla_w_in.astype(BF16),
             gla_w_low=[_pad_rank(gla_w_in[j][:, gla_main:], 1) for j in range(gla_w_in.shape[0])],
             gla_w_a2=[_pad_rank(gla_w_a2[j], 0) for j in range(gla_w_a2.shape[0])],
             gla_b_a=gla_b_a, gla_norm_g=gla_norm_g, gla_w_out=gla_w_out.astype(BF16), rwkv=rwkv)
    y_p, p_k, p_v, p_gla, p_wkv, p_shift = _trunk(x_prompt, None, w)
    cache = dict(k=cache_swa_k, v=cache_swa_v, gla=state_gla, wkv=state_rwkv, shift=state_rwkv_shift)
    y_s, s_k, s_v, s_gla, s_wkv, s_shift = _trunk(x_sample, cache, w)
    return (y_p, y_s, p_k, p_v, p_gla, p_wkv, p_shift, s_k, s_v, s_gla, s_wkv, s_shift)
```

```python
import functools
import math

import jax
import jax.numpy as jnp
from jax import lax
from jax.experimental import pallas as pl
from jax.experimental.pallas import tpu as pltpu

F32 = jnp.float32
BF16 = jnp.bfloat16

D_MODEL = 2048
DEPTH = 4
PAST_LEN = 16384
ALPHA = (2 * DEPTH) ** 0.25
LN_EPS = 1e-5

SWA_HEADS = 32
SWA_KV_HEADS = 4
SWA_GROUP = SWA_HEADS // SWA_KV_HEADS
SWA_HEAD_DIM = 64
SWA_WIDTH = SWA_HEADS * SWA_HEAD_DIM
SWA_KV_WIDTH = SWA_KV_HEADS * SWA_HEAD_DIM
WINDOW = 128
ROT_DIM = SWA_HEAD_DIM // 4
ROPE_THETA = 500000.0

GLA_HEADS = 4
GLA_KEY_DIM = D_MODEL // 2
GLA_VAL_DIM = D_MODEL
GLA_DK = GLA_KEY_DIM // GLA_HEADS
GLA_DV = GLA_VAL_DIM // GLA_HEADS
GLA_GATE_TEMP = 16.0
GLA_CHUNK = 64
GLA_SUB = 16
GLA_SEQS = 4
GLA_NORM_EPS = 1e-5

RWKV_HEAD_DIM = 64
RWKV_HEADS = D_MODEL // RWKV_HEAD_DIM
RWKV_GN_EPS = 64e-5

LANES = 128
VMEM_LIMIT = 56 * 1024 * 1024
PROJ_ROWS = 1024
PROJ_COLS = 1536
LORA_ROWS = 512
OUT_ROWS = 512
LN_SUB_ROWS = 256
NEG_BIG = -1e30


def _cparams(sem):
    return pltpu.CompilerParams(dimension_semantics=sem, vmem_limit_bytes=VMEM_LIMIT)


def _silu(x):
    return x * (1.0 / (1.0 + jnp.exp(-x)))


def _log_sigmoid(z):
    return jnp.minimum(z, 0.0) - jnp.log(1.0 + jnp.exp(-jnp.abs(z)))


def _nt(a, b):
    return lax.dot_general(a, b, (((1,), (1,)), ((), ())), preferred_element_type=F32)


def _mm_kernel(x_ref, w_ref, *refs, plans, has_tabs, has_side):
    refs = list(refs)
    tab_ref = refs.pop(0) if has_tabs else None
    side_w_ref = refs.pop(0) if has_side else None
    o_ref = refs.pop(0)
    side_o_ref = refs.pop(0) if has_side else None

    def run(t, plan):
        x16 = x_ref[...].astype(BF16)
        acc = jnp.dot(x16, w_ref[...], preferred_element_type=F32)
        if has_side and t == 0:
            side_o_ref[...] = jnp.dot(x16, side_w_ref[...], preferred_element_type=F32)
        if all(op is None for op in plan):
            o_ref[...] = acc
            return
        for c, op in enumerate(plan):
            chunk = acc[:, c * LANES:(c + 1) * LANES]
            if op == "silu":
                chunk = _silu(chunk)
            elif op in ("rope", "rope_q"):
                chunk = _rope128(chunk, tab_ref[0], tab_ref[1], tab_ref[2])
                if op == "rope_q":
                    chunk = chunk * SWA_HEAD_DIM ** -0.5
            o_ref[:, c * LANES:(c + 1) * LANES] = chunk

    if len(set(plans)) == 1 and not has_side:
        run(0, plans[0])
    else:
        for t, plan in enumerate(plans):
            pl.when(pl.program_id(1) == t)(functools.partial(run, t, plan))


def _mm(x, w, ncols, tn=PROJ_COLS, rope_q_cols=0, rope_cols=0, silu_from=None, rope_tabs=None, side_w=None):
    M, K = x.shape
    tm = min(M, PROJ_ROWS)
    assert M % tm == 0 and ncols % tn == 0

    def op_of(col):
        if col < rope_q_cols:
            return "rope_q"
        if col < rope_cols:
            return "rope"
        return "silu" if silu_from is not None and col >= silu_from else None

    plans = tuple(tuple(op_of(t * tn + c * LANES) for c in range(tn // LANES)) for t in range(ncols // tn))
    w, layer = (w, None) if not isinstance(w, tuple) else w
    w_spec = (pl.BlockSpec((K, tn), lambda i, j: (0, j)) if layer is None else
              pl.BlockSpec((None, K, tn), lambda i, j: (layer, 0, j)))
    operands, specs = [x, w], [pl.BlockSpec((tm, K), lambda i, j: (i, 0)), w_spec]
    if rope_cols:
        period = rope_tabs.shape[1] // tm
        operands.append(rope_tabs)
        specs.append(pl.BlockSpec((3, tm, LANES), lambda i, j: (0, i % period, 0)))
    out_specs = [pl.BlockSpec((tm, tn), lambda i, j: (i, j))]
    out_shape = [jax.ShapeDtypeStruct((M, ncols), F32)]
    if side_w is not None:
        operands.append(side_w)
        specs.append(pl.BlockSpec((K, LANES), lambda i, j: (0, 0)))
        out_specs.append(pl.BlockSpec((tm, LANES), lambda i, j: (i, 0)))
        out_shape.append(jax.ShapeDtypeStruct((M, LANES), F32))
    outs = pl.pallas_call(
        functools.partial(_mm_kernel, plans=plans, has_tabs=bool(rope_cols), has_side=side_w is not None),
        grid=(M // tm, ncols // tn),
        in_specs=specs,
        out_specs=out_specs,
        out_shape=out_shape,
        compiler_params=_cparams(("parallel", "arbitrary")),
        name="proj_mm",
    )(*operands)
    return outs[0] if side_w is None else tuple(outs)


def _mm_ln_kernel(a_ref, w_ref, x_ref, g_ref, b_ref, o_ref):
    for r0 in range(0, a_ref.shape[0], LN_SUB_ROWS):
        rows = slice(r0, min(r0 + LN_SUB_ROWS, a_ref.shape[0]))
        h = jnp.dot(a_ref[rows, :], w_ref[...], preferred_element_type=F32)
        z = ALPHA * x_ref[rows, :] + h
        mu = jnp.mean(z, axis=-1, keepdims=True)
        zc = z - mu
        var = jnp.mean(zc * zc, axis=-1, keepdims=True)
        o_ref[rows, :] = zc * lax.rsqrt(var + LN_EPS) * g_ref[...] + b_ref[...]


def _mm_ln(a, w, x, g, b):
    M, K = a.shape
    tm = min(M, OUT_ROWS)
    w, layer = (w, None) if not isinstance(w, tuple) else w
    D = w.shape[-1]
    return pl.pallas_call(
        _mm_ln_kernel,
        grid=(M // tm,),
        in_specs=[pl.BlockSpec((tm, K), lambda i: (i, 0)),
                  (pl.BlockSpec((K, D), lambda i: (0, 0)) if layer is None else
                   pl.BlockSpec((None, K, D), lambda i: (layer, 0, 0))),
                  pl.BlockSpec((tm, D), lambda i: (i, 0)),
                  pl.BlockSpec((1, D), lambda i: (0, 0)),
                  pl.BlockSpec((1, D), lambda i: (0, 0))],
        out_specs=pl.BlockSpec((tm, D), lambda i: (i, 0)),
        out_shape=jax.ShapeDtypeStruct((M, D), F32),
        compiler_params=_cparams(("parallel",)),
        name="out_proj_ln",
    )(a, w, x, g.reshape(1, D), b.reshape(1, D))


def _rope_tables(pos):
    half = ROT_DIM // 2
    inv = ROPE_THETA ** (-(jnp.arange(half, dtype=F32) * 2.0 / ROT_DIM))
    ang = pos[:, None] * inv[None, :]
    cos, sin = jnp.cos(ang), jnp.sin(ang)
    L = pos.shape[0]
    ones = jnp.ones((L, SWA_HEAD_DIM - ROT_DIM), F32)
    zeros_r = jnp.zeros((L, SWA_HEAD_DIM - ROT_DIM), F32)
    zeros_h = jnp.zeros((L, half), F32)
    c = jnp.concatenate([cos, cos, ones], axis=1)
    s1 = jnp.concatenate([-sin, zeros_h, zeros_r], axis=1)
    s2 = jnp.concatenate([zeros_h, sin, zeros_r], axis=1)
    rep = LANES // SWA_HEAD_DIM
    return jnp.tile(c, (1, rep)), jnp.tile(s1, (1, rep)), jnp.tile(s2, (1, rep))


def _rope128(x, c, s1, s2):
    return x * c + pltpu.roll(x, LANES - ROT_DIM // 2, 1) * s1 + pltpu.roll(x, ROT_DIM // 2, 1) * s2


def _swa_kernel(sink_ref, q_ref, kc_ref, vc_ref, kp_ref, vp_ref, g0_ref, g1_ref, g2_ref, g3_ref,
                o_ref, ko_ref, vo_ref, *, lq, nseq, prompt):
    gate_refs = (g0_ref, g1_ref, g2_ref, g3_ref)
    ppt = SWA_GATE_TILE // LANES
    nk = 2 * WINDOW
    nh = SWA_GROUP
    qi = lax.broadcasted_iota(jnp.int32, (lq, nk), 0)
    sj = lax.broadcasted_iota(jnp.int32, (lq, nk), 1)
    rel = qi + WINDOW - sj
    band = jnp.logical_and(rel >= 0, rel <= WINDOW)
    band_first = jnp.logical_and(band, sj >= jnp.where(pl.program_id(1) == 0, WINDOW, 0)) if prompt else band
    m0 = lax.broadcasted_iota(jnp.int32, (lq, LANES), 1) < SWA_HEAD_DIM
    hd = SWA_HEAD_DIM
    ppd = nh // 2

    def attend(bi):
        rs = slice(bi * lq, (bi + 1) * lq)
        mask = band_first if bi == 0 else band
        k_cur, v_cur = kc_ref[rs, :], vc_ref[rs, :]
        if prompt:
            if bi == 0:
                k_prev, v_prev = kp_ref[...], vp_ref[...]
            else:
                k_prev, v_prev = kc_ref[(bi - 1) * lq:bi * lq, :], vc_ref[(bi - 1) * lq:bi * lq, :]
            k_out, v_out = k_cur, v_cur
        else:
            k_prev, v_prev = kp_ref[bi], vp_ref[bi]
            k_out = jnp.concatenate([k_prev[lq:], k_cur], axis=0)
            v_out = jnp.concatenate([v_prev[lq:], v_cur], axis=0)
            zpad = jnp.zeros((WINDOW - lq, SWA_KV_WIDTH), F32)
            k_cur = jnp.concatenate([k_cur, zpad], axis=0)
            v_cur = jnp.concatenate([v_cur, zpad], axis=0)
        k_all = jnp.concatenate([k_prev, k_cur], axis=0)
        v_all = jnp.concatenate([v_prev, v_cur], axis=0)
        return k_out, v_out, [kv_group(rs, mask, k_all, v_all, kh) for kh in range(SWA_KV_HEADS)]

    def kv_group(rs, mask, k_all, v_all, kh):
        kg = k_all[:, kh * hd:(kh + 1) * hd]
        vg = v_all[:, kh * hd:(kh + 1) * hd]
        kk2 = jnp.concatenate([kg, kg], axis=1).astype(BF16)
        vv2 = jnp.concatenate([vg, vg], axis=1).astype(BF16)
        p0 = kh * ppd
        pairs = range(p0, p0 + ppd)
        rows = []
        for p in pairs:
            q2 = q_ref[rs, p * LANES:(p + 1) * LANES]
            rows += [jnp.where(m0, q2, 0.0), jnp.where(m0, 0.0, q2)]
        qs = jnp.concatenate(rows, axis=0).astype(BF16)
        s_all = _nt(qs, kk2)
        yield None
        es, dens = [], []
        for i in range(2 * ppd):
            snk = sink_ref[2 * p0 + i]
            s = jnp.where(mask, s_all[i * lq:(i + 1) * lq], NEG_BIG)
            m = jnp.maximum(jnp.max(s, axis=-1, keepdims=True), snk)
            e = jnp.exp(s - m)
            dens.append(jnp.sum(e, axis=-1, keepdims=True) + jnp.exp(snk - m))
            es.append(e.astype(BF16))
        yield None
        o_all = jnp.dot(jnp.concatenate(es, axis=0), vv2, preferred_element_type=F32)
        outs = []
        for i, p in enumerate(pairs):
            oa = o_all[2 * i * lq:(2 * i + 1) * lq] / dens[2 * i]
            ob = o_all[(2 * i + 1) * lq:(2 * i + 2) * lq] / dens[2 * i + 1]
            g2 = gate_refs[p // ppt][rs, (p % ppt) * LANES:(p % ppt + 1) * LANES]
            outs.append((p, (jnp.where(m0, oa, ob) * g2).astype(o_ref.dtype)))
        yield outs

    seqs = [attend(bi) for bi in range(nseq)]
    groups = [g for _, _, gs in seqs for g in gs]
    if not prompt:
        for _ in range(2):
            for g in groups:
                next(g)
        done = [next(g) for g in groups]
    else:
        done = [list(g)[-1] for g in groups]
    for bi, (k_out, v_out, gs) in enumerate(seqs):
        if not prompt:
            ko_ref[bi], vo_ref[bi] = k_out, v_out
        elif bi == nseq - 1:
            ko_ref[0], vo_ref[0] = k_out, v_out
        for outs in done[bi * len(gs):(bi + 1) * len(gs)]:
            for p, out in outs:
                o_ref[bi * lq:(bi + 1) * lq, p * LANES:(p + 1) * LANES] = out


SWA_SAMPLE_SEQS = 4
SWA_PROMPT_BLOCKS = 4
SWA_GATE_TILE = 512


def _swa_gate_specs(rows, index):
    g0 = (SWA_WIDTH + 2 * SWA_KV_WIDTH) // SWA_GATE_TILE
    return [pl.BlockSpec((rows, SWA_GATE_TILE), functools.partial(index, g0 + t))
            for t in range(SWA_WIDTH // SWA_GATE_TILE)]


def _swa_prompt(h, sink, B, S):
    nblk = SWA_PROMPT_BLOCKS
    nb = S // (nblk * WINDOW)
    rows = nblk * WINDOW
    kcol = SWA_WIDTH // SWA_KV_WIDTH
    prev = lambda b, n: (b * nb + n) * nblk - jnp.minimum(n, 1)
    return pl.pallas_call(
        functools.partial(_swa_kernel, lq=WINDOW, nseq=nblk, prompt=True),
        grid=(B, nb),
        in_specs=[pl.BlockSpec(memory_space=pltpu.SMEM),
                  pl.BlockSpec((rows, SWA_WIDTH), lambda b, n: (b * nb + n, 0)),
                  pl.BlockSpec((rows, SWA_KV_WIDTH), lambda b, n: (b * nb + n, kcol)),
                  pl.BlockSpec((rows, SWA_KV_WIDTH), lambda b, n: (b * nb + n, kcol + 1)),
                  pl.BlockSpec((WINDOW, SWA_KV_WIDTH), lambda b, n: (prev(b, n), kcol)),
                  pl.BlockSpec((WINDOW, SWA_KV_WIDTH), lambda b, n: (prev(b, n), kcol + 1)),
                  *_swa_gate_specs(rows, lambda col, b, n: (b * nb + n, col))],
        out_specs=[pl.BlockSpec((rows, SWA_WIDTH), lambda b, n: (b * nb + n, 0)),
                   pl.BlockSpec((1, WINDOW, SWA_KV_WIDTH), lambda b, n: (b, 0, 0)),
                   pl.BlockSpec((1, WINDOW, SWA_KV_WIDTH), lambda b, n: (b, 0, 0))],
        out_shape=[jax.ShapeDtypeStruct((B * S, SWA_WIDTH), BF16),
                   jax.ShapeDtypeStruct((B, WINDOW, SWA_KV_WIDTH), F32),
                   jax.ShapeDtypeStruct((B, WINDOW, SWA_KV_WIDTH), F32)],
        compiler_params=_cparams(("parallel", "arbitrary")),
        name="swa_prompt",
    )(sink, h, h, h, h, h, h, h, h, h)


def _swa_sample(h, sink, cache_k, cache_v, j, B, L):
    ns = SWA_SAMPLE_SEQS
    assert B % ns == 0
    cache = pl.BlockSpec((None, ns, WINDOW, SWA_KV_WIDTH), lambda b: (j, b, 0, 0))
    kcol = SWA_WIDTH // SWA_KV_WIDTH
    return pl.pallas_call(
        functools.partial(_swa_kernel, lq=L, nseq=ns, prompt=False),
        grid=(B // ns,),
        in_specs=[pl.BlockSpec(memory_space=pltpu.SMEM),
                  pl.BlockSpec((ns * L, SWA_WIDTH), lambda b: (b, 0)),
                  pl.BlockSpec((ns * L, SWA_KV_WIDTH), lambda b: (b, kcol)),
                  pl.BlockSpec((ns * L, SWA_KV_WIDTH), lambda b: (b, kcol + 1)),
                  cache, cache,
                  *_swa_gate_specs(ns * L, lambda col, b: (b, col))],
        out_specs=[pl.BlockSpec((ns * L, SWA_WIDTH), lambda b: (b, 0)),
                   pl.BlockSpec((ns, WINDOW, SWA_KV_WIDTH), lambda b: (b, 0, 0)),
                   pl.BlockSpec((ns, WINDOW, SWA_KV_WIDTH), lambda b: (b, 0, 0))],
        out_shape=[jax.ShapeDtypeStruct((B * L, SWA_WIDTH), BF16),
                   jax.ShapeDtypeStruct((B, WINDOW, SWA_KV_WIDTH), F32),
                   jax.ShapeDtypeStruct((B, WINDOW, SWA_KV_WIDTH), F32)],
        compiler_params=_cparams(("parallel",)),
        name="swa_sample",
    )(sink, h, h, h, cache_k, cache_v, h, h, h, h)


def _swa_layer(x, cache, j, w_in, sink, w_out, ln_g, ln_b, B, L):
    first = 0 if cache is None else PAST_LEN
    rows = max(L, min(B * L, PROJ_ROWS))
    tabs = jnp.stack(_rope_tables(first + (jnp.arange(rows) % L).astype(F32)))
    h = _mm(x, w_in, 2 * SWA_WIDTH + 2 * SWA_KV_WIDTH, rope_q_cols=SWA_WIDTH, rope_cols=SWA_WIDTH + SWA_KV_WIDTH,
            silu_from=SWA_WIDTH + 2 * SWA_KV_WIDTH, rope_tabs=tabs)
    if cache is None:
        o, nk, nv = _swa_prompt(h, sink, B, L)
    else:
        ck = cache[0].reshape(-1, B, WINDOW, SWA_KV_WIDTH)
        cv = cache[1].reshape(-1, B, WINDOW, SWA_KV_WIDTH)
        o, nk, nv = _swa_sample(h, sink, ck, cv, j, B, L)
    shape = (B, WINDOW, SWA_KV_HEADS, SWA_HEAD_DIM)
    return _mm_ln(o, w_out, x, ln_g, ln_b), nk.reshape(shape), nv.reshape(shape)


def _tril3(C):
    return (lax.broadcasted_iota(jnp.int32, (C, 3 * C), 1) % C
            <= lax.broadcasted_iota(jnp.int32, (C, 3 * C), 0)).astype(BF16)


def _cumsum_rows(x, tril3):
    h1 = x.astype(BF16)
    r1 = x - h1.astype(F32)
    h2 = r1.astype(BF16)
    h3 = (r1 - h2.astype(F32)).astype(BF16)
    return jnp.dot(tril3, jnp.concatenate([h1, h2, h3], axis=0), preferred_element_type=F32)


def _gla_chunk_kernel(q_ref, k_ref, v_ref, gate_ref, al_ref, wa_ref, ba_ref, ng_ref, s0_ref, o_ref, so_ref, st_ref,
                      *, C, SB):
    c_idx = pl.program_id(1)
    H, DK, DV = GLA_HEADS, GLA_DK, GLA_DV
    nseq = o_ref.shape[0]
    staged = C == GLA_CHUNK

    @pl.when(c_idx == 0)
    def _():
        for bi in range(nseq):
            for h in range(H):
                st_ref[bi * H + h] = s0_ref[bi, h].T

    row = lax.broadcasted_iota(jnp.int32, (C, H * DK), 0)
    causal = lax.broadcasted_iota(jnp.int32, (C, C), 1) <= lax.broadcasted_iota(jnp.int32, (C, C), 0)
    tril3 = _tril3(C)

    def prep(bi):
        z = jnp.dot(al_ref[bi].astype(BF16), wa_ref[...], preferred_element_type=F32) + ba_ref[...]
        b = _cumsum_rows(_log_sigmoid(z) * (1.0 / GLA_GATE_TEMP), tril3)
        q = q_ref[bi] * (DK ** -0.5)
        k = k_ref[bi]
        b_last = b[C - 1:C, :]
        qis, kjs = [], []
        for i in range(C // SB):
            r0, r1 = i * SB, (i + 1) * SB
            bn = b[r0:r0 + 1, :]
            qis.append((q[r0:r1] * jnp.exp(b[r0:r1] - bn)).astype(BF16))
            kjs.append((k * jnp.exp(jnp.where(row < r1, bn - b, 0.0))).astype(BF16))
        return dict(v16=v_ref[bi].astype(BF16), e_last=jnp.exp(b_last), qe=(q * jnp.exp(b)).astype(BF16),
                    kd=(k * jnp.exp(b_last - b)).astype(BF16), qis=qis, kjs=kjs)

    def head(bi, p, h):
        ks, vs = slice(h * DK, (h + 1) * DK), slice(h * DV, (h + 1) * DV)
        st = st_ref[bi * H + h]
        a_parts = [_nt(qi[:, ks], kj[:, ks]) for qi, kj in zip(p["qis"], p["kjs"])]
        v16 = p["v16"][:, vs]
        o_inter = _nt(p["qe"][:, ks], st.astype(BF16))
        ds = lax.dot_general(v16, p["kd"][:, ks], (((0,), (0,)), ((), ())), preferred_element_type=F32)
        yield None
        a = jnp.concatenate(a_parts, axis=0) if len(a_parts) > 1 else a_parts[0]
        a = jnp.where(causal, a, 0.0).astype(BF16)
        o = o_inter + jnp.dot(a, v16, preferred_element_type=F32)
        st_new = st * p["e_last"][:, ks] + ds
        yield None
        on = o * lax.rsqrt(jnp.mean(o * o, axis=-1, keepdims=True) + GLA_NORM_EPS) * ng_ref[...]
        yield st_new, (on * gate_ref[bi, :, vs]).astype(o_ref.dtype)

    preps = [prep(bi) for bi in range(nseq)]
    chains = [[head(bi, preps[bi], h) for h in range(H)] for bi in range(nseq)]
    if staged:
        for _ in range(2):
            for per_seq in chains:
                for chain in per_seq:
                    next(chain)
        results = [[next(chain) for chain in per_seq] for per_seq in chains]
    else:
        results = [[list(chain)[-1] for chain in per_seq] for per_seq in chains]
    for bi in range(nseq):
        for h, (st_new, out) in enumerate(results[bi]):
            st_ref[bi * H + h] = st_new
            o_ref[bi, :, h * DV:(h + 1) * DV] = out

    @pl.when(c_idx == pl.num_programs(1) - 1)
    def _():
        for bi in range(nseq):
            for h in range(H):
                so_ref[bi, h] = st_ref[bi * H + h].T


def _gla_chunk(h, a_low, w_a2, b_a, norm_g, states, j, B, L):
    C = min(GLA_CHUNK, L)
    SB = min(GLA_SUB, C)
    nc = L // C
    ns = GLA_SEQS
    assert B % ns == 0
    tok = lambda width, col: pl.BlockSpec((ns, C, width), lambda b, c: (b, c, col))
    st0 = pl.BlockSpec((None, ns, GLA_HEADS, GLA_DK, GLA_DV), lambda b, c: (j, b, 0, 0, 0))
    st = pl.BlockSpec((ns, GLA_HEADS, GLA_DK, GLA_DV), lambda b, c: (b, 0, 0, 0))
    h3 = h.reshape(B, L, h.shape[1])
    o, st_out = pl.pallas_call(
        functools.partial(_gla_chunk_kernel, C=C, SB=SB),
        grid=(B // ns, nc),
        in_specs=[tok(GLA_KEY_DIM, 0), tok(GLA_KEY_DIM, 1), tok(GLA_VAL_DIM, 1), tok(GLA_VAL_DIM, 2),
                  tok(LANES, 0),
                  pl.BlockSpec((LANES, GLA_KEY_DIM), lambda b, c: (0, 0)),
                  pl.BlockSpec((1, GLA_KEY_DIM), lambda b, c: (0, 0)),
                  pl.BlockSpec((1, GLA_DV), lambda b, c: (0, 0)),
                  st0],
        out_specs=[tok(GLA_VAL_DIM, 0), st],
        out_shape=[jax.ShapeDtypeStruct((B, L, GLA_VAL_DIM), BF16),
                   jax.ShapeDtypeStruct((B, GLA_HEADS, GLA_DK, GLA_DV), F32)],
        scratch_shapes=[pltpu.VMEM((ns * GLA_HEADS, GLA_DV, GLA_DK), F32)],
        compiler_params=_cparams(("parallel", "arbitrary")),
        name="gla_chunk",
    )(h3, h3, h3, h3, a_low.reshape(B, L, LANES), w_a2, b_a.reshape(1, GLA_KEY_DIM), norm_g.reshape(1, GLA_DV),
      states)
    return o.reshape(B * L, GLA_VAL_DIM), st_out


def _gla_layer(x, states, j, w_in, w_low, w_a2, b_a, norm_g, w_out, ln_g, ln_b, B, L):
    h, a_low = _mm(x, w_in, 2 * GLA_KEY_DIM + 2 * GLA_VAL_DIM, silu_from=2 * GLA_KEY_DIM + GLA_VAL_DIM, side_w=w_low)
    if states is None:
        states, j = jnp.zeros((1, B, GLA_HEADS, GLA_DK, GLA_DV), F32), 0
    o, st = _gla_chunk(h, a_low, w_a2, b_a, norm_g, states, j, B, L)
    return _mm_ln(o, w_out, x, ln_g, ln_b), st


SUBLANES = 8
RWKV_GATE_INDEX = 3


def _prev_rows(x, pa_ref, pb_ref, seq_len):
    if seq_len is None:
        return pa_ref[...]
    tm = x.shape[0]
    starts_seq = (pl.program_id(0) * tm) % seq_len == 0
    first = jnp.where(starts_seq, pb_ref[0], pa_ref[SUBLANES - 1:SUBLANES, :])
    rows = lax.broadcasted_iota(jnp.int32, x.shape, 0)
    return jnp.where(rows == 0, first, pltpu.roll(x, 1, 0))


def _prev_specs(x, xprev, shift, tm, seq_len, grid_rank):
    pad = (0,) * (grid_rank - 1)
    sh = shift.reshape(shift.shape[0], 1, D_MODEL)
    if xprev is not None:
        return (xprev, sh), [pl.BlockSpec((tm, D_MODEL), lambda i, *_: (i, 0)),
                             pl.BlockSpec((1, 1, D_MODEL), lambda i, *_: (0, 0, 0))], None
    assert seq_len % tm == 0
    per = tm // SUBLANES
    return (x, sh), [pl.BlockSpec((SUBLANES, D_MODEL), lambda i, *_: (jnp.maximum(i * per - 1, 0), 0)),
                     pl.BlockSpec((1, 1, D_MODEL), lambda i, *_: ((i * tm) // seq_len, 0, 0))], seq_len


def _rwkv_proj_kernel(x_ref, pa_ref, pb_ref, mu_ref, w_ref, o_ref, xm_ref, *, seq_len):
    def step(first_col, gate):
        if first_col:
            x = x_ref[...]
            xm = (x + (_prev_rows(x, pa_ref, pb_ref, seq_len) - x) * mu_ref[0]).astype(BF16)
            xm_ref[...] = xm
        else:
            xm = xm_ref[...]
        acc = jnp.dot(xm, w_ref[0], preferred_element_type=F32)
        o_ref[0] = _silu(acc) if gate else acc

    is_first = pl.program_id(2) == 0
    is_gate = pl.program_id(1) == RWKV_GATE_INDEX
    for first_col in (True, False):
        for gate in (True, False):
            here = jnp.logical_and(is_first == first_col, is_gate == gate)
            pl.when(here)(functools.partial(step, first_col, gate))


def _rwkv_proj(x, xprev, shift, seq_len, mu4, w4, tn=1024):
    M = x.shape[0]
    tm = min(M, PROJ_ROWS)
    prev_ops, prev_specs, inline_len = _prev_specs(x, xprev, shift, tm, seq_len, 3)
    return pl.pallas_call(
        functools.partial(_rwkv_proj_kernel, seq_len=inline_len),
        grid=(M // tm, 4, D_MODEL // tn),
        in_specs=[pl.BlockSpec((tm, D_MODEL), lambda i, m, j: (i, 0)),
                  *prev_specs,
                  pl.BlockSpec((1, 1, D_MODEL), lambda i, m, j: (m, 0, 0)),
                  pl.BlockSpec((1, D_MODEL, tn), lambda i, m, j: (m, 0, j))],
        out_specs=pl.BlockSpec((1, tm, tn), lambda i, m, j: (m, i, j)),
        out_shape=jax.ShapeDtypeStruct((4, M, D_MODEL), F32),
        scratch_shapes=[pltpu.VMEM((tm, D_MODEL), BF16)],
        compiler_params=_cparams(("parallel", "arbitrary", "arbitrary")),
        name="rwkv_proj",
    )(x, *prev_ops, mu4, w4)


def _rwkv_lora_kernel(x_ref, pa_ref, pb_ref, mu_ref, w1_ref, w2_ref, w0_ref, a1_ref, a2_ref, a0_ref, lw_ref, a_ref,
                      *, seq_len):
    x = x_ref[...]
    xx = _prev_rows(x, pa_ref, pb_ref, seq_len) - x
    xw = (x + xx * mu_ref[0]).astype(BF16)
    xa = (x + xx * mu_ref[1]).astype(BF16)
    t = jnp.tanh(jnp.dot(xw, w1_ref[...], preferred_element_type=F32))
    wl = w0_ref[...] + jnp.dot(t.astype(BF16), w2_ref[...], preferred_element_type=F32)
    lw_ref[...] = (-math.exp(-0.5)) / (1.0 + jnp.exp(-wl))
    al = jnp.dot(xa, a1_ref[...], preferred_element_type=F32)
    az = a0_ref[...] + jnp.dot(al.astype(BF16), a2_ref[...], preferred_element_type=F32)
    a_ref[...] = 1.0 / (1.0 + jnp.exp(-az))


def _rwkv_lora(x, xprev, shift, seq_len, mu2, w1, w2, w0, a1, a2, a0):
    M = x.shape[0]
    tm = min(M, LORA_ROWS)
    R = w1.shape[1]
    full = lambda shape: pl.BlockSpec(shape, lambda i: tuple(0 for _ in shape))
    prev_ops, prev_specs, inline_len = _prev_specs(x, xprev, shift, tm, seq_len, 1)
    return pl.pallas_call(
        functools.partial(_rwkv_lora_kernel, seq_len=inline_len),
        grid=(M // tm,),
        in_specs=[pl.BlockSpec((tm, D_MODEL), lambda i: (i, 0)),
                  *prev_specs,
                  full((2, 1, D_MODEL)), full((D_MODEL, R)), full((R, D_MODEL)), full((1, D_MODEL)),
                  full((D_MODEL, R)), full((R, D_MODEL)), full((1, D_MODEL))],
        out_specs=[pl.BlockSpec((tm, D_MODEL), lambda i: (i, 0)),
                   pl.BlockSpec((tm, D_MODEL), lambda i: (i, 0))],
        out_shape=[jax.ShapeDtypeStruct((M, D_MODEL), F32), jax.ShapeDtypeStruct((M, D_MODEL), F32)],
        compiler_params=_cparams(("parallel",)),
        name="rwkv_lora",
    )(x, *prev_ops, mu2, w1, w2, w0.reshape(1, D_MODEL), a1, a2, a0.reshape(1, D_MODEL))


def _head_ones():
    r = lax.broadcasted_iota(jnp.int32, (LANES, LANES), 0) // RWKV_HEAD_DIM
    c = lax.broadcasted_iota(jnp.int32, (LANES, LANES), 1) // RWKV_HEAD_DIM
    e = (r == c).astype(BF16)
    return jnp.concatenate([e, e], axis=0)


def _segsum(x, e2):
    hi = x.astype(BF16)
    lo = (x - hi.astype(F32)).astype(BF16)
    return jnp.dot(jnp.concatenate([hi, lo], axis=1), e2, preferred_element_type=F32)


def _segsum_wide(x, e2):
    rows, n = x.shape[0], x.shape[1] // LANES
    s = _segsum(jnp.concatenate([x[:, i * LANES:(i + 1) * LANES] for i in range(n)], axis=0), e2)
    return jnp.concatenate([s[i * rows:(i + 1) * rows] for i in range(n)], axis=1)


RWKV_PAIRS = 16
RWKV_SEQS = 4
RWKV_CHUNK = 64


def _rwkv_chunk_kernel(r_ref, k_ref, v_ref, gt_ref, lw_ref, a_ref, kk_ref, ka_ref, rk_ref, gng_ref, gnb_ref, s0_ref,
                       o_ref, so_ref, st_ref, *, lreal):
    C, N = RWKV_CHUNK, RWKV_HEAD_DIM
    t_idx = pl.program_id(2)
    e2 = _head_ones()
    m0 = lax.broadcasted_iota(jnp.int32, (C, LANES), 1) < N
    ti = lax.broadcasted_iota(jnp.int32, (C, 2 * C), 0)
    si2 = lax.broadcasted_iota(jnp.int32, (C, 2 * C), 1)
    si = si2 % C
    strict, incl, left = si < ti, si <= ti, si2 < C
    bd = (lax.broadcasted_iota(jnp.int32, (LANES, LANES), 0) // N
          == lax.broadcasted_iota(jnp.int32, (LANES, LANES), 1) // N)
    zs = jnp.zeros((N, N), F32)

    nseq = o_ref.shape[0]

    @pl.when(t_idx == 0)
    def _():
        for bi in range(nseq):
            for p in range(RWKV_PAIRS):
                top = jnp.concatenate([s0_ref[bi, 2 * p], zs], axis=1)
                bot = jnp.concatenate([zs, s0_ref[bi, 2 * p + 1]], axis=1)
                st_ref[bi * RWKV_PAIRS + p] = jnp.concatenate([top, bot], axis=0)

    def both(x, y):
        parts = [x] if y is None else [x, y]
        return jnp.concatenate([jnp.where(m0, z, 0.0) for z in parts] + [jnp.where(m0, 0.0, z) for z in parts],
                               axis=0).astype(BF16)

    def load(x):
        if lreal < C:
            x = jnp.concatenate([x, jnp.zeros((C - lreal, x.shape[1]), F32)], axis=0)
        return x

    inv_n = 1.0 / N
    n_iter = max(1, (lreal - 1).bit_length())
    lanes = [slice(p * LANES, (p + 1) * LANES) for p in range(RWKV_PAIRS)]

    def prep(bi):
        r, k, v = load(r_ref[0, bi]), load(k_ref[0, bi]), load(v_ref[0, bi])
        lw, a = load(lw_ref[bi]), load(a_ref[bi])
        kk = k * kk_ref[...]
        kk = kk / jnp.maximum(jnp.sqrt(_segsum_wide(kk * kk, e2)), 1e-12)
        kp = k * (1.0 + (a - 1.0) * ka_ref[...])
        c = _cumsum_rows(lw, _tril3(C))
        e_c, e_nc = jnp.exp(c), jnp.exp(-c)
        return dict(v=v, e_c=e_c, at_w=-kk * jnp.exp(c - lw), rt_w=r * e_c,
                    bt_w=(kk * a * e_nc).astype(BF16), kt_w=(kp * e_nc).astype(BF16),
                    bonus=_segsum_wide((r * kp * rk_ref[...])[:lreal], e2))

    def scores(bi, q, p):
        ls = lanes[p]
        at, rt = q["at_w"][:, ls], q["rt_w"][:, ls]
        bk = jnp.concatenate([q["bt_w"][:, ls], q["kt_w"][:, ls]], axis=0)
        g = _nt(both(at, rt), bk)
        s_bd = st_ref[bi * RWKV_PAIRS + p]
        pq = _nt(jnp.concatenate([at, rt], axis=0).astype(BF16), s_bd.astype(BF16))
        return dict(bk=bk, s_bd=s_bd, g=g, pq=pq)

    def setup(q, p, d):
        g, pq = d.pop("g"), d.pop("pq")
        aa0, rr0 = jnp.where(strict, g[0:C], 0.0), jnp.where(incl, g[C:2 * C], 0.0)
        aa1, rr1 = jnp.where(strict, g[2 * C:3 * C], 0.0), jnp.where(incl, g[3 * C:], 0.0)
        a_ab = jnp.where(left, aa0, pltpu.roll(aa1, C, 1))
        a_ak = jnp.where(left, pltpu.roll(aa0, C, 1), aa1)
        x = pq[:C] + jnp.dot(a_ak.astype(BF16), both(q["v"][:, lanes[p]], None), preferred_element_type=F32)
        d.update(y0=pq[C:], rr=jnp.concatenate([rr0, rr1], axis=1).astype(BF16), x=x, ac=a_ab)

    def neumann(d, it):
        ac = d["ac"]
        ac16 = ac.astype(BF16)
        rhs = both(d["x"], None)
        if it < n_iter - 1:
            a_bd = jnp.concatenate([jnp.where(left, ac, 0.0), jnp.where(left, 0.0, ac)], axis=0).astype(BF16)
            res = jnp.dot(ac16, jnp.concatenate([rhs, a_bd], axis=1), preferred_element_type=F32)
            d["x"] = d["x"] + res[:, :LANES]
            d["ac"] = res[:, LANES:]
        else:
            d["x"] = d["x"] + jnp.dot(ac16, rhs, preferred_element_type=F32)

    def finish(q, p, d):
        ls = lanes[p]
        u, vp = d["x"], q["v"][:, ls]
        y = d["y0"] + jnp.dot(d["rr"], both(u, vp), preferred_element_type=F32)
        uv = jnp.concatenate([u, vp], axis=0).astype(BF16)
        ds = lax.dot_general(uv, d["bk"], (((0,), (0,)), ((), ())), preferred_element_type=F32)
        return (d["s_bd"] + jnp.where(bd, ds, 0.0)) * q["e_c"][C - 1:C, ls], y[:lreal]

    def output(bi, q, done):
        y = jnp.concatenate([yp for _, yp in done], axis=1)
        yc = y - _segsum_wide(y, e2) * inv_n
        yv = _segsum_wide(yc * yc, e2) * inv_n
        yn = yc * lax.rsqrt(yv + RWKV_GN_EPS) * gng_ref[...] + gnb_ref[...]
        return ((yn + q["bonus"] * q["v"][:lreal]) * gt_ref[0, bi]).astype(o_ref.dtype)

    pairs = range(RWKV_PAIRS)
    stages = [
        lambda bi, s: s.update(q=prep(bi)),
        lambda bi, s: s.update(work=[scores(bi, s["q"], p) for p in pairs]),
        lambda bi, s: [setup(s["q"], p, d) for p, d in enumerate(s["work"])],
        lambda bi, s: [neumann(d, it) for it in range(n_iter) for d in s["work"]],
        lambda bi, s: s.update(done=[finish(s["q"], p, d) for p, d in enumerate(s["work"])]),
        lambda bi, s: s.update(out=output(bi, s["q"], s["done"])),
    ]
    seqs = [dict() for _ in range(nseq)]
    lag = 1
    for tick in range(len(stages) + lag * (nseq - 1)):
        for bi in range(nseq):
            if 0 <= tick - lag * bi < len(stages):
                stages[tick - lag * bi](bi, seqs[bi])
    for bi, s in enumerate(seqs):
        for p, (s_new, _) in enumerate(s["done"]):
            st_ref[bi * RWKV_PAIRS + p] = s_new
        o_ref[bi] = s["out"]

    @pl.when(t_idx == pl.num_programs(2) - 1)
    def _():
        for bi in range(nseq):
            for p in range(RWKV_PAIRS):
                s = st_ref[bi * RWKV_PAIRS + p]
                so_ref[bi, 2 * p] = s[:N, :N]
                so_ref[bi, 2 * p + 1] = s[N:, N:]


def _rwkv_scan(proj, lw, a, k_k, k_a, r_k, gn_g, gn_b, states, j, B, L):
    tc = min(L, RWKV_CHUNK)
    nt = L // tc
    W = RWKV_PAIRS * LANES
    ng = D_MODEL // W
    hpg = 2 * RWKV_PAIRS
    ns = RWKV_SEQS
    assert B % ns == 0
    hd = RWKV_HEAD_DIM
    tok = lambda m: pl.BlockSpec((1, ns, tc, W), lambda b, g, t: (m, b, t, g))
    vec = pl.BlockSpec((ns, tc, W), lambda b, g, t: (b, t, g))
    par = pl.BlockSpec((1, W), lambda b, g, t: (0, g))
    st = pl.BlockSpec((ns, hpg, hd, hd), lambda b, g, t: (b, g, 0, 0))
    st0 = pl.BlockSpec((None, ns, hpg, hd, hd), lambda b, g, t: (j, b, g, 0, 0))
    proj4 = proj.reshape(4, B, L, D_MODEL)
    o, st_out = pl.pallas_call(
        functools.partial(_rwkv_chunk_kernel, lreal=tc),
        grid=(B // ns, ng, nt),
        in_specs=[tok(0), tok(1), tok(2), tok(3), vec, vec, par, par, par, par, par, st0],
        out_specs=[vec, st],
        out_shape=[jax.ShapeDtypeStruct((B, L, D_MODEL), BF16),
                   jax.ShapeDtypeStruct((B, RWKV_HEADS, hd, hd), F32)],
        scratch_shapes=[pltpu.VMEM((ns * RWKV_PAIRS, LANES, LANES), F32)],
        compiler_params=_cparams(("parallel", "parallel", "arbitrary")),
        name="rwkv_scan",
    )(proj4, proj4, proj4, proj4, lw.reshape(B, L, D_MODEL), a.reshape(B, L, D_MODEL), k_k.reshape(1, D_MODEL),
      k_a.reshape(1, D_MODEL), r_k.reshape(1, D_MODEL), gn_g.reshape(1, D_MODEL), gn_b.reshape(1, D_MODEL), states)
    return o.reshape(B * L, D_MODEL), st_out


def _rwkv_layer(x, shift, states, j, p, ln_g, ln_b, B, L):
    x3 = x.reshape(B, L, D_MODEL)
    if L % PROJ_ROWS == 0 and L % LORA_ROWS == 0:
        xprev = None
    else:
        xprev = jnp.concatenate([shift[:, None, :], x3[:, :-1]], axis=1).reshape(B * L, D_MODEL)
    proj = _rwkv_proj(x, xprev, shift, L, p["mu4"], p["w4"])
    lw, a = _rwkv_lora(x, xprev, shift, L, p["mu2"], p["w1"], p["w2"], p["w0"], p["a1"], p["a2"], p["a0"])
    o, st = _rwkv_scan(proj, lw, a, p["k_k"], p["k_a"], p["r_k"], p["gn_g"], p["gn_b"], states, j, B, L)
    return _mm_ln(o, p["w_out"], x, ln_g, ln_b), st, x3[:, -1]


def _pad_rank(w, axis):
    pad = [(0, 0), (0, 0)]
    pad[axis] = (0, LANES - w.shape[axis])
    return jnp.pad(w, pad).astype(BF16)


def _trunk(x3, cache, w):
    B, L, _ = x3.shape
    prompt = cache is None
    x = x3.reshape(B * L, D_MODEL)
    new_k, new_v, new_gla, new_wkv, new_shift = [], [], [], [], []
    for layer in range(DEPTH):
        kind, j = layer % 3, layer // 3
        g, b = w["ln_g"][layer], w["ln_b"][layer]
        if kind == 0:
            c = None if prompt else (cache["k"], cache["v"])
            x, nk, nv = _swa_layer(x, c, j, (w["swa_w_in"], j), w["swa_sink"][j], (w["swa_w_out"], j), g, b, B, L)
            new_k.append(nk)
            new_v.append(nv)
        elif kind == 1:
            st = None if prompt else cache["gla"]
            x, st = _gla_layer(x, st, j, (w["gla_w_in"], j), w["gla_w_low"][j], w["gla_w_a2"][j], w["gla_b_a"][j],
                               w["gla_norm_g"][j], (w["gla_w_out"], j), g, b, B, L)
            new_gla.append(st)
        else:
            if prompt:
                shift0 = jnp.zeros((B, D_MODEL), F32)
                s0, js = jnp.zeros((1, B, RWKV_HEADS, RWKV_HEAD_DIM, RWKV_HEAD_DIM), F32), 0
            else:
                shift0, s0, js = cache["shift"][j], cache["wkv"], j
            x, st, sh = _rwkv_layer(x, shift0, s0, js, w["rwkv"][j], g, b, B, L)
            new_wkv.append(st)
            new_shift.append(sh)
    stack = lambda parts: parts[0][None] if len(parts) == 1 else jnp.stack(parts)
    return (x.reshape(B, L, D_MODEL), stack(new_k), stack(new_v), stack(new_gla), stack(new_wkv), stack(new_shift))


def kernel(x_prompt, x_sample, cache_swa_k, cache_swa_v, state_gla, state_rwkv, state_rwkv_shift, ln_g, ln_b, swa_w_in, swa_sink, swa_w_out, gla_w_in, gla_w_a2, gla_b_a, gla_norm_g, gla_w_out, rwkv_mu, rwkv_w_rkvg, rwkv_w0, rwkv_w1, rwkv_w2, rwkv_a0, rwkv_a1, rwkv_a2, rwkv_k_k, rwkv_k_a, rwkv_r_k, rwkv_gn_g, rwkv_gn_b, rwkv_w_out):
    n_rwkv = rwkv_mu.shape[0]
    gla_main = 2 * GLA_KEY_DIM + 2 * GLA_VAL_DIM
    rwkv = []
    for j in range(n_rwkv):
        rwkv.append(dict(
            mu4=rwkv_mu[j][jnp.array([0, 2, 3, 5])].reshape(4, 1, D_MODEL),
            mu2=rwkv_mu[j][jnp.array([1, 4])].reshape(2, 1, D_MODEL),
            w4=rwkv_w_rkvg[j].astype(BF16),
            w1=_pad_rank(rwkv_w1[j], 1), w2=_pad_rank(rwkv_w2[j], 0), w0=rwkv_w0[j],
            a1=_pad_rank(rwkv_a1[j], 1), a2=_pad_rank(rwkv_a2[j], 0), a0=rwkv_a0[j],
            k_k=rwkv_k_k[j], k_a=rwkv_k_a[j], r_k=rwkv_r_k[j], gn_g=rwkv_gn_g[j], gn_b=rwkv_gn_b[j],
            w_out=rwkv_w_out[j].astype(BF16)))
    w = dict(ln_g=ln_g, ln_b=ln_b,
             swa_w_in=swa_w_in.astype(BF16), swa_sink=swa_sink, swa_w_out=swa_w_out.astype(BF16),
             gla_w_in=gla_w_in.astype(BF16),
             gla_w_low=[_pad_rank(gla_w_in[j][:, gla_main:], 1) for j in range(gla_w_in.shape[0])],
             gla_w_a2=[_pad_rank(gla_w_a2[j], 0) for j in range(gla_w_a2.shape[0])],
             gla_b_a=gla_b_a, gla_norm_g=gla_norm_g, gla_w_out=gla_w_out.astype(BF16), rwkv=rwkv)
    y_p, p_k, p_v, p_gla, p_wkv, p_shift = _trunk(x_prompt, None, w)
    cache = dict(k=cache_swa_k, v=cache_swa_v, gla=state_gla, wkv=state_rwkv, shift=state_rwkv_shift)
    y_s, s_k, s_v, s_gla, s_wkv, s_shift = _trunk(x_sample, cache, w)
    return (y_p, y_s, p_k, p_v, p_gla, p_wkv, p_shift, s_k, s_v, s_gla, s_wkv, s_shift)
```

```python
import functools
import math

import jax
import jax.numpy as jnp
from jax import lax
from jax.experimental import pallas as pl
from jax.experimental.pallas import tpu as pltpu

F32 = jnp.float32
BF16 = jnp.bfloat16

D_MODEL = 2048
DEPTH = 4
PAST_LEN = 16384
ALPHA = (2 * DEPTH) ** 0.25
LN_EPS = 1e-5

SWA_HEADS = 32
SWA_KV_HEADS = 4
SWA_GROUP = SWA_HEADS // SWA_KV_HEADS
SWA_HEAD_DIM = 64
SWA_WIDTH = SWA_HEADS * SWA_HEAD_DIM
SWA_KV_WIDTH = SWA_KV_HEADS * SWA_HEAD_DIM
WINDOW = 128
ROT_DIM = SWA_HEAD_DIM // 4
ROPE_THETA = 500000.0

GLA_HEADS = 4
GLA_KEY_DIM = D_MODEL // 2
GLA_VAL_DIM = D_MODEL
GLA_DK = GLA_KEY_DIM // GLA_HEADS
GLA_DV = GLA_VAL_DIM // GLA_HEADS
GLA_GATE_TEMP = 16.0
GLA_CHUNK = 64
GLA_SUB = 16
GLA_SEQS = 4
GLA_NORM_EPS = 1e-5

RWKV_HEAD_DIM = 64
RWKV_HEADS = D_MODEL // RWKV_HEAD_DIM
RWKV_GN_EPS = 64e-5

LANES = 128
VMEM_LIMIT = 56 * 1024 * 1024
PROJ_ROWS = 1024
PROJ_COLS = 1536
LORA_ROWS = 512
OUT_ROWS = 1024
LN_SUB_ROWS = 256
NEG_BIG = -1e30


def _cparams(sem):
    return pltpu.CompilerParams(dimension_semantics=sem, vmem_limit_bytes=VMEM_LIMIT)


def _silu(x):
    return x * (1.0 / (1.0 + jnp.exp(-x)))


def _log_sigmoid(z):
    return jnp.minimum(z, 0.0) - jnp.log(1.0 + jnp.exp(-jnp.abs(z)))


def _nt(a, b):
    return lax.dot_general(a, b, (((1,), (1,)), ((), ())), preferred_element_type=F32)


def _mm_kernel(x_ref, w_ref, *refs, plans, has_tabs, has_side):
    refs = list(refs)
    tab_ref = refs.pop(0) if has_tabs else None
    side_w_ref = refs.pop(0) if has_side else None
    o_ref = refs.pop(0)
    side_o_ref = refs.pop(0) if has_side else None

    def run(t, plan):
        x16 = x_ref[...].astype(BF16)
        acc = jnp.dot(x16, w_ref[...], preferred_element_type=F32)
        if has_side and t == 0:
            side_o_ref[...] = jnp.dot(x16, side_w_ref[...], preferred_element_type=F32)
        if all(op is None for op in plan):
            o_ref[...] = acc
            return
        for c, op in enumerate(plan):
            chunk = acc[:, c * LANES:(c + 1) * LANES]
            if op == "silu":
                chunk = _silu(chunk)
            elif op in ("rope", "rope_q"):
                chunk = _rope128(chunk, tab_ref[0], tab_ref[1], tab_ref[2])
                if op == "rope_q":
                    chunk = chunk * SWA_HEAD_DIM ** -0.5
            o_ref[:, c * LANES:(c + 1) * LANES] = chunk

    if len(set(plans)) == 1 and not has_side:
        run(0, plans[0])
    else:
        for t, plan in enumerate(plans):
            pl.when(pl.program_id(1) == t)(functools.partial(run, t, plan))


def _mm(x, w, ncols, tn=PROJ_COLS, rope_q_cols=0, rope_cols=0, silu_from=None, rope_tabs=None, side_w=None):
    M, K = x.shape
    tm = min(M, PROJ_ROWS)
    assert M % tm == 0 and ncols % tn == 0

    def op_of(col):
        if col < rope_q_cols:
            return "rope_q"
        if col < rope_cols:
            return "rope"
        return "silu" if silu_from is not None and col >= silu_from else None

    plans = tuple(tuple(op_of(t * tn + c * LANES) for c in range(tn // LANES)) for t in range(ncols // tn))
    w, layer = (w, None) if not isinstance(w, tuple) else w
    w_spec = (pl.BlockSpec((K, tn), lambda i, j: (0, j)) if layer is None else
              pl.BlockSpec((None, K, tn), lambda i, j: (layer, 0, j)))
    operands, specs = [x, w], [pl.BlockSpec((tm, K), lambda i, j: (i, 0)), w_spec]
    if rope_cols:
        period = rope_tabs.shape[1] // tm
        operands.append(rope_tabs)
        specs.append(pl.BlockSpec((3, tm, LANES), lambda i, j: (0, i % period, 0)))
    out_specs = [pl.BlockSpec((tm, tn), lambda i, j: (i, j))]
    out_shape = [jax.ShapeDtypeStruct((M, ncols), F32)]
    if side_w is not None:
        operands.append(side_w)
        specs.append(pl.BlockSpec((K, LANES), lambda i, j: (0, 0)))
        out_specs.append(pl.BlockSpec((tm, LANES), lambda i, j: (i, 0)))
        out_shape.append(jax.ShapeDtypeStruct((M, LANES), F32))
    outs = pl.pallas_call(
        functools.partial(_mm_kernel, plans=plans, has_tabs=bool(rope_cols), has_side=side_w is not None),
        grid=(M // tm, ncols // tn),
        in_specs=specs,
        out_specs=out_specs,
        out_shape=out_shape,
        compiler_params=_cparams(("parallel", "arbitrary")),
        name="proj_mm",
    )(*operands)
    return outs[0] if side_w is None else tuple(outs)


def _mm_ln_kernel(a_ref, w_ref, x_ref, g_ref, b_ref, o_ref):
    for r0 in range(0, a_ref.shape[0], LN_SUB_ROWS):
        rows = slice(r0, min(r0 + LN_SUB_ROWS, a_ref.shape[0]))
        h = jnp.dot(a_ref[rows, :], w_ref[...], preferred_element_type=F32)
        z = ALPHA * x_ref[rows, :] + h
        mu = jnp.mean(z, axis=-1, keepdims=True)
        zc = z - mu
        var = jnp.mean(zc * zc, axis=-1, keepdims=True)
        o_ref[rows, :] = zc * lax.rsqrt(var + LN_EPS) * g_ref[...] + b_ref[...]


def _mm_ln(a, w, x, g, b):
    M, K = a.shape
    tm = min(M, OUT_ROWS)
    w, layer = (w, None) if not isinstance(w, tuple) else w
    D = w.shape[-1]
    return pl.pallas_call(
        _mm_ln_kernel,
        grid=(M // tm,),
        in_specs=[pl.BlockSpec((tm, K), lambda i: (i, 0)),
                  (pl.BlockSpec((K, D), lambda i: (0, 0), pipeline_mode=pl.Buffered(1)) if layer is None else
                   pl.BlockSpec((None, K, D), lambda i: (layer, 0, 0), pipeline_mode=pl.Buffered(1))),
                  pl.BlockSpec((tm, D), lambda i: (i, 0)),
                  pl.BlockSpec((1, D), lambda i: (0, 0)),
                  pl.BlockSpec((1, D), lambda i: (0, 0))],
        out_specs=pl.BlockSpec((tm, D), lambda i: (i, 0)),
        out_shape=jax.ShapeDtypeStruct((M, D), F32),
        compiler_params=_cparams(("parallel",)),
        name="out_proj_ln",
    )(a, w, x, g.reshape(1, D), b.reshape(1, D))


def _rope_tables(pos):
    half = ROT_DIM // 2
    inv = ROPE_THETA ** (-(jnp.arange(half, dtype=F32) * 2.0 / ROT_DIM))
    ang = pos[:, None] * inv[None, :]
    cos, sin = jnp.cos(ang), jnp.sin(ang)
    L = pos.shape[0]
    ones = jnp.ones((L, SWA_HEAD_DIM - ROT_DIM), F32)
    zeros_r = jnp.zeros((L, SWA_HEAD_DIM - ROT_DIM), F32)
    zeros_h = jnp.zeros((L, half), F32)
    c = jnp.concatenate([cos, cos, ones], axis=1)
    s1 = jnp.concatenate([-sin, zeros_h, zeros_r], axis=1)
    s2 = jnp.concatenate([zeros_h, sin, zeros_r], axis=1)
    rep = LANES // SWA_HEAD_DIM
    return jnp.tile(c, (1, rep)), jnp.tile(s1, (1, rep)), jnp.tile(s2, (1, rep))


def _rope128(x, c, s1, s2):
    return x * c + pltpu.roll(x, LANES - ROT_DIM // 2, 1) * s1 + pltpu.roll(x, ROT_DIM // 2, 1) * s2


def _swa_kernel(sink_ref, q_ref, kc_ref, vc_ref, kp_ref, vp_ref, g0_ref, g1_ref, g2_ref, g3_ref,
                o_ref, ko_ref, vo_ref, *, lq, nseq, prompt):
    gate_refs = (g0_ref, g1_ref, g2_ref, g3_ref)
    ppt = SWA_GATE_TILE // LANES
    nk = 2 * WINDOW
    nh = SWA_GROUP
    qi = lax.broadcasted_iota(jnp.int32, (lq, nk), 0)
    sj = lax.broadcasted_iota(jnp.int32, (lq, nk), 1)
    rel = qi + WINDOW - sj
    band = jnp.logical_and(rel >= 0, rel <= WINDOW)
    band_first = jnp.logical_and(band, sj >= jnp.where(pl.program_id(1) == 0, WINDOW, 0)) if prompt else band
    m0 = lax.broadcasted_iota(jnp.int32, (lq, LANES), 1) < SWA_HEAD_DIM
    hd = SWA_HEAD_DIM
    ppd = nh // 2

    def attend(bi):
        rs = slice(bi * lq, (bi + 1) * lq)
        mask = band_first if bi == 0 else band
        k_cur, v_cur = kc_ref[rs, :], vc_ref[rs, :]
        if prompt:
            if bi == 0:
                k_prev, v_prev = kp_ref[...], vp_ref[...]
            else:
                k_prev, v_prev = kc_ref[(bi - 1) * lq:bi * lq, :], vc_ref[(bi - 1) * lq:bi * lq, :]
            k_out, v_out = k_cur, v_cur
        else:
            k_prev, v_prev = kp_ref[bi], vp_ref[bi]
            k_out = jnp.concatenate([k_prev[lq:], k_cur], axis=0)
            v_out = jnp.concatenate([v_prev[lq:], v_cur], axis=0)
            zpad = jnp.zeros((WINDOW - lq, SWA_KV_WIDTH), F32)
            k_cur = jnp.concatenate([k_cur, zpad], axis=0)
            v_cur = jnp.concatenate([v_cur, zpad], axis=0)
        k_all = jnp.concatenate([k_prev, k_cur], axis=0)
        v_all = jnp.concatenate([v_prev, v_cur], axis=0)
        return k_out, v_out, [kv_group(rs, mask, k_all, v_all, kh) for kh in range(SWA_KV_HEADS)]

    def kv_group(rs, mask, k_all, v_all, kh):
        kg = k_all[:, kh * hd:(kh + 1) * hd]
        vg = v_all[:, kh * hd:(kh + 1) * hd]
        kk2 = jnp.concatenate([kg, kg], axis=1).astype(BF16)
        vv2 = jnp.concatenate([vg, vg], axis=1).astype(BF16)
        p0 = kh * ppd
        pairs = range(p0, p0 + ppd)
        rows = []
        for p in pairs:
            q2 = q_ref[rs, p * LANES:(p + 1) * LANES]
            rows += [jnp.where(m0, q2, 0.0), jnp.where(m0, 0.0, q2)]
        qs = jnp.concatenate(rows, axis=0).astype(BF16)
        s_all = _nt(qs, kk2)
        yield None
        es, dens = [], []
        for i in range(2 * ppd):
            snk = sink_ref[2 * p0 + i]
            s = jnp.where(mask, s_all[i * lq:(i + 1) * lq], NEG_BIG)
            m = jnp.maximum(jnp.max(s, axis=-1, keepdims=True), snk)
            e = jnp.exp(s - m)
            dens.append(jnp.sum(e, axis=-1, keepdims=True) + jnp.exp(snk - m))
            es.append(e.astype(BF16))
        yield None
        o_all = jnp.dot(jnp.concatenate(es, axis=0), vv2, preferred_element_type=F32)
        outs = []
        for i, p in enumerate(pairs):
            oa = o_all[2 * i * lq:(2 * i + 1) * lq] / dens[2 * i]
            ob = o_all[(2 * i + 1) * lq:(2 * i + 2) * lq] / dens[2 * i + 1]
            g2 = gate_refs[p // ppt][rs, (p % ppt) * LANES:(p % ppt + 1) * LANES]
            outs.append((p, (jnp.where(m0, oa, ob) * g2).astype(o_ref.dtype)))
        yield outs

    seqs = [attend(bi) for bi in range(nseq)]
    groups = [g for _, _, gs in seqs for g in gs]
    if not prompt:
        for _ in range(2):
            for g in groups:
                next(g)
        done = [next(g) for g in groups]
    else:
        done = [list(g)[-1] for g in groups]
    for bi, (k_out, v_out, gs) in enumerate(seqs):
        if not prompt:
            ko_ref[bi], vo_ref[bi] = k_out, v_out
        elif bi == nseq - 1:
            ko_ref[0], vo_ref[0] = k_out, v_out
        for outs in done[bi * len(gs):(bi + 1) * len(gs)]:
            for p, out in outs:
                o_ref[bi * lq:(bi + 1) * lq, p * LANES:(p + 1) * LANES] = out


SWA_SAMPLE_SEQS = 4
SWA_PROMPT_BLOCKS = 4
SWA_GATE_TILE = 512


def _swa_gate_specs(rows, index):
    g0 = (SWA_WIDTH + 2 * SWA_KV_WIDTH) // SWA_GATE_TILE
    return [pl.BlockSpec((rows, SWA_GATE_TILE), functools.partial(index, g0 + t))
            for t in range(SWA_WIDTH // SWA_GATE_TILE)]


def _swa_prompt(h, sink, B, S):
    nblk = SWA_PROMPT_BLOCKS
    nb = S // (nblk * WINDOW)
    rows = nblk * WINDOW
    kcol = SWA_WIDTH // SWA_KV_WIDTH
    prev = lambda b, n: (b * nb + n) * nblk - jnp.minimum(n, 1)
    return pl.pallas_call(
        functools.partial(_swa_kernel, lq=WINDOW, nseq=nblk, prompt=True),
        grid=(B, nb),
        in_specs=[pl.BlockSpec(memory_space=pltpu.SMEM),
                  pl.BlockSpec((rows, SWA_WIDTH), lambda b, n: (b * nb + n, 0)),
                  pl.BlockSpec((rows, SWA_KV_WIDTH), lambda b, n: (b * nb + n, kcol)),
                  pl.BlockSpec((rows, SWA_KV_WIDTH), lambda b, n: (b * nb + n, kcol + 1)),
                  pl.BlockSpec((WINDOW, SWA_KV_WIDTH), lambda b, n: (prev(b, n), kcol)),
                  pl.BlockSpec((WINDOW, SWA_KV_WIDTH), lambda b, n: (prev(b, n), kcol + 1)),
                  *_swa_gate_specs(rows, lambda col, b, n: (b * nb + n, col))],
        out_specs=[pl.BlockSpec((rows, SWA_WIDTH), lambda b, n: (b * nb + n, 0)),
                   pl.BlockSpec((1, WINDOW, SWA_KV_WIDTH), lambda b, n: (b, 0, 0)),
                   pl.BlockSpec((1, WINDOW, SWA_KV_WIDTH), lambda b, n: (b, 0, 0))],
        out_shape=[jax.ShapeDtypeStruct((B * S, SWA_WIDTH), BF16),
                   jax.ShapeDtypeStruct((B, WINDOW, SWA_KV_WIDTH), F32),
                   jax.ShapeDtypeStruct((B, WINDOW, SWA_KV_WIDTH), F32)],
        compiler_params=_cparams(("parallel", "arbitrary")),
        name="swa_prompt",
    )(sink, h, h, h, h, h, h, h, h, h)


def _swa_sample(h, sink, cache_k, cache_v, j, B, L):
    ns = SWA_SAMPLE_SEQS
    assert B % ns == 0
    cache = pl.BlockSpec((None, ns, WINDOW, SWA_KV_WIDTH), lambda b: (j, b, 0, 0))
    kcol = SWA_WIDTH // SWA_KV_WIDTH
    return pl.pallas_call(
        functools.partial(_swa_kernel, lq=L, nseq=ns, prompt=False),
        grid=(B // ns,),
        in_specs=[pl.BlockSpec(memory_space=pltpu.SMEM),
                  pl.BlockSpec((ns * L, SWA_WIDTH), lambda b: (b, 0)),
                  pl.BlockSpec((ns * L, SWA_KV_WIDTH), lambda b: (b, kcol)),
                  pl.BlockSpec((ns * L, SWA_KV_WIDTH), lambda b: (b, kcol + 1)),
                  cache, cache,
                  *_swa_gate_specs(ns * L, lambda col, b: (b, col))],
        out_specs=[pl.BlockSpec((ns * L, SWA_WIDTH), lambda b: (b, 0)),
                   pl.BlockSpec((ns, WINDOW, SWA_KV_WIDTH), lambda b: (b, 0, 0)),
                   pl.BlockSpec((ns, WINDOW, SWA_KV_WIDTH), lambda b: (b, 0, 0))],
        out_shape=[jax.ShapeDtypeStruct((B * L, SWA_WIDTH), BF16),
                   jax.ShapeDtypeStruct((B, WINDOW, SWA_KV_WIDTH), F32),
                   jax.ShapeDtypeStruct((B, WINDOW, SWA_KV_WIDTH), F32)],
        compiler_params=_cparams(("parallel",)),
        name="swa_sample",
    )(sink, h, h, h, cache_k, cache_v, h, h, h, h)


def _swa_layer(x, cache, j, w_in, sink, w_out, ln_g, ln_b, B, L):
    first = 0 if cache is None else PAST_LEN
    rows = max(L, min(B * L, PROJ_ROWS))
    tabs = jnp.stack(_rope_tables(first + (jnp.arange(rows) % L).astype(F32)))
    h = _mm(x, w_in, 2 * SWA_WIDTH + 2 * SWA_KV_WIDTH, rope_q_cols=SWA_WIDTH, rope_cols=SWA_WIDTH + SWA_KV_WIDTH,
            silu_from=SWA_WIDTH + 2 * SWA_KV_WIDTH, rope_tabs=tabs)
    if cache is None:
        o, nk, nv = _swa_prompt(h, sink, B, L)
    else:
        ck = cache[0].reshape(-1, B, WINDOW, SWA_KV_WIDTH)
        cv = cache[1].reshape(-1, B, WINDOW, SWA_KV_WIDTH)
        o, nk, nv = _swa_sample(h, sink, ck, cv, j, B, L)
    shape = (B, WINDOW, SWA_KV_HEADS, SWA_HEAD_DIM)
    return _mm_ln(o, w_out, x, ln_g, ln_b), nk.reshape(shape), nv.reshape(shape)


def _tril3(C):
    return (lax.broadcasted_iota(jnp.int32, (C, 3 * C), 1) % C
            <= lax.broadcasted_iota(jnp.int32, (C, 3 * C), 0)).astype(BF16)


def _cumsum_rows(x, tril3):
    h1 = x.astype(BF16)
    r1 = x - h1.astype(F32)
    h2 = r1.astype(BF16)
    h3 = (r1 - h2.astype(F32)).astype(BF16)
    return jnp.dot(tril3, jnp.concatenate([h1, h2, h3], axis=0), preferred_element_type=F32)


def _gla_chunk_kernel(q_ref, k_ref, v_ref, gate_ref, al_ref, wa_ref, ba_ref, ng_ref, s0_ref, o_ref, so_ref, st_ref,
                      *, C, SB):
    c_idx = pl.program_id(1)
    H, DK, DV = GLA_HEADS, GLA_DK, GLA_DV
    nseq = o_ref.shape[0]
    staged = C == GLA_CHUNK

    @pl.when(c_idx == 0)
    def _():
        for bi in range(nseq):
            for h in range(H):
                st_ref[bi * H + h] = s0_ref[bi, h].T

    row = lax.broadcasted_iota(jnp.int32, (C, H * DK), 0)
    causal = lax.broadcasted_iota(jnp.int32, (C, C), 1) <= lax.broadcasted_iota(jnp.int32, (C, C), 0)
    tril3 = _tril3(C)

    def prep(bi):
        z = jnp.dot(al_ref[bi].astype(BF16), wa_ref[...], preferred_element_type=F32) + ba_ref[...]
        b = _cumsum_rows(_log_sigmoid(z) * (1.0 / GLA_GATE_TEMP), tril3)
        q = q_ref[bi] * (DK ** -0.5)
        k = k_ref[bi]
        b_last = b[C - 1:C, :]
        qis, kjs = [], []
        for i in range(C // SB):
            r0, r1 = i * SB, (i + 1) * SB
            bn = b[r0:r0 + 1, :]
            qis.append((q[r0:r1] * jnp.exp(b[r0:r1] - bn)).astype(BF16))
            kjs.append((k * jnp.exp(jnp.where(row < r1, bn - b, 0.0))).astype(BF16))
        return dict(v16=v_ref[bi].astype(BF16), e_last=jnp.exp(b_last), qe=(q * jnp.exp(b)).astype(BF16),
                    kd=(k * jnp.exp(b_last - b)).astype(BF16), qis=qis, kjs=kjs)

    def head(bi, p, h):
        ks, vs = slice(h * DK, (h + 1) * DK), slice(h * DV, (h + 1) * DV)
        st = st_ref[bi * H + h]
        a_parts = [_nt(qi[:, ks], kj[:, ks]) for qi, kj in zip(p["qis"], p["kjs"])]
        v16 = p["v16"][:, vs]
        o_inter = _nt(p["qe"][:, ks], st.astype(BF16))
        ds = lax.dot_general(v16, p["kd"][:, ks], (((0,), (0,)), ((), ())), preferred_element_type=F32)
        yield None
        a = jnp.concatenate(a_parts, axis=0) if len(a_parts) > 1 else a_parts[0]
        a = jnp.where(causal, a, 0.0).astype(BF16)
        o = o_inter + jnp.dot(a, v16, preferred_element_type=F32)
        st_new = st * p["e_last"][:, ks] + ds
        yield None
        on = o * lax.rsqrt(jnp.mean(o * o, axis=-1, keepdims=True) + GLA_NORM_EPS) * ng_ref[...]
        yield st_new, (on * gate_ref[bi, :, vs]).astype(o_ref.dtype)

    preps = [prep(bi) for bi in range(nseq)]
    chains = [[head(bi, preps[bi], h) for h in range(H)] for bi in range(nseq)]
    if staged:
        for _ in range(2):
            for per_seq in chains:
                for chain in per_seq:
                    next(chain)
        results = [[next(chain) for chain in per_seq] for per_seq in chains]
    else:
        results = [[list(chain)[-1] for chain in per_seq] for per_seq in chains]
    for bi in range(nseq):
        for h, (st_new, out) in enumerate(results[bi]):
            st_ref[bi * H + h] = st_new
            o_ref[bi, :, h * DV:(h + 1) * DV] = out

    @pl.when(c_idx == pl.num_programs(1) - 1)
    def _():
        for bi in range(nseq):
            for h in range(H):
                so_ref[bi, h] = st_ref[bi * H + h].T


def _gla_chunk(h, a_low, w_a2, b_a, norm_g, states, j, B, L):
    C = min(GLA_CHUNK, L)
    SB = min(GLA_SUB, C)
    nc = L // C
    ns = GLA_SEQS
    assert B % ns == 0
    tok = lambda width, col: pl.BlockSpec((ns, C, width), lambda b, c: (b, c, col))
    st0 = pl.BlockSpec((None, ns, GLA_HEADS, GLA_DK, GLA_DV), lambda b, c: (j, b, 0, 0, 0))
    st = pl.BlockSpec((ns, GLA_HEADS, GLA_DK, GLA_DV), lambda b, c: (b, 0, 0, 0))
    h3 = h.reshape(B, L, h.shape[1])
    o, st_out = pl.pallas_call(
        functools.partial(_gla_chunk_kernel, C=C, SB=SB),
        grid=(B // ns, nc),
        in_specs=[tok(GLA_KEY_DIM, 0), tok(GLA_KEY_DIM, 1), tok(GLA_VAL_DIM, 1), tok(GLA_VAL_DIM, 2),
                  tok(LANES, 0),
                  pl.BlockSpec((LANES, GLA_KEY_DIM), lambda b, c: (0, 0)),
                  pl.BlockSpec((1, GLA_KEY_DIM), lambda b, c: (0, 0)),
                  pl.BlockSpec((1, GLA_DV), lambda b, c: (0, 0)),
                  st0],
        out_specs=[tok(GLA_VAL_DIM, 0), st],
        out_shape=[jax.ShapeDtypeStruct((B, L, GLA_VAL_DIM), BF16),
                   jax.ShapeDtypeStruct((B, GLA_HEADS, GLA_DK, GLA_DV), F32)],
        scratch_shapes=[pltpu.VMEM((ns * GLA_HEADS, GLA_DV, GLA_DK), F32)],
        compiler_params=_cparams(("parallel", "arbitrary")),
        name="gla_chunk",
    )(h3, h3, h3, h3, a_low.reshape(B, L, LANES), w_a2, b_a.reshape(1, GLA_KEY_DIM), norm_g.reshape(1, GLA_DV),
      states)
    return o.reshape(B * L, GLA_VAL_DIM), st_out


def _gla_layer(x, states, j, w_in, w_low, w_a2, b_a, norm_g, w_out, ln_g, ln_b, B, L):
    h, a_low = _mm(x, w_in, 2 * GLA_KEY_DIM + 2 * GLA_VAL_DIM, silu_from=2 * GLA_KEY_DIM + GLA_VAL_DIM, side_w=w_low)
    if states is None:
        states, j = jnp.zeros((1, B, GLA_HEADS, GLA_DK, GLA_DV), F32), 0
    o, st = _gla_chunk(h, a_low, w_a2, b_a, norm_g, states, j, B, L)
    return _mm_ln(o, w_out, x, ln_g, ln_b), st


SUBLANES = 8
RWKV_GATE_INDEX = 3


def _prev_rows(x, pa_ref, pb_ref, seq_len):
    if seq_len is None:
        return pa_ref[...]
    tm = x.shape[0]
    starts_seq = (pl.program_id(0) * tm) % seq_len == 0
    first = jnp.where(starts_seq, pb_ref[0], pa_ref[SUBLANES - 1:SUBLANES, :])
    rows = lax.broadcasted_iota(jnp.int32, x.shape, 0)
    return jnp.where(rows == 0, first, pltpu.roll(x, 1, 0))


def _prev_specs(x, xprev, shift, tm, seq_len, grid_rank):
    pad = (0,) * (grid_rank - 1)
    sh = shift.reshape(shift.shape[0], 1, D_MODEL)
    if xprev is not None:
        return (xprev, sh), [pl.BlockSpec((tm, D_MODEL), lambda i, *_: (i, 0)),
                             pl.BlockSpec((1, 1, D_MODEL), lambda i, *_: (0, 0, 0))], None
    assert seq_len % tm == 0
    per = tm // SUBLANES
    return (x, sh), [pl.BlockSpec((SUBLANES, D_MODEL), lambda i, *_: (jnp.maximum(i * per - 1, 0), 0)),
                     pl.BlockSpec((1, 1, D_MODEL), lambda i, *_: ((i * tm) // seq_len, 0, 0))], seq_len


def _rwkv_proj_kernel(x_ref, pa_ref, pb_ref, mu_ref, w_ref, o_ref, xm_ref, *, seq_len):
    def step(first_col, gate):
        if first_col:
            x = x_ref[...]
            xm = (x + (_prev_rows(x, pa_ref, pb_ref, seq_len) - x) * mu_ref[0]).astype(BF16)
            xm_ref[...] = xm
        else:
            xm = xm_ref[...]
        acc = jnp.dot(xm, w_ref[0], preferred_element_type=F32)
        o_ref[0] = _silu(acc) if gate else acc

    is_first = pl.program_id(2) == 0
    is_gate = pl.program_id(1) == RWKV_GATE_INDEX
    for first_col in (True, False):
        for gate in (True, False):
            here = jnp.logical_and(is_first == first_col, is_gate == gate)
            pl.when(here)(functools.partial(step, first_col, gate))


def _rwkv_proj(x, xprev, shift, seq_len, mu4, w4, tn=1024):
    M = x.shape[0]
    tm = min(M, PROJ_ROWS)
    prev_ops, prev_specs, inline_len = _prev_specs(x, xprev, shift, tm, seq_len, 3)
    return pl.pallas_call(
        functools.partial(_rwkv_proj_kernel, seq_len=inline_len),
        grid=(M // tm, 4, D_MODEL // tn),
        in_specs=[pl.BlockSpec((tm, D_MODEL), lambda i, m, j: (i, 0)),
                  *prev_specs,
                  pl.BlockSpec((1, 1, D_MODEL), lambda i, m, j: (m, 0, 0)),
                  pl.BlockSpec((1, D_MODEL, tn), lambda i, m, j: (m, 0, j))],
        out_specs=pl.BlockSpec((1, tm, tn), lambda i, m, j: (m, i, j)),
        out_shape=jax.ShapeDtypeStruct((4, M, D_MODEL), F32),
        scratch_shapes=[pltpu.VMEM((tm, D_MODEL), BF16)],
        compiler_params=_cparams(("parallel", "arbitrary", "arbitrary")),
        name="rwkv_proj",
    )(x, *prev_ops, mu4, w4)


def _rwkv_lora_kernel(x_ref, pa_ref, pb_ref, mu_ref, w1_ref, w2_ref, w0_ref, a1_ref, a2_ref, a0_ref, lw_ref, a_ref,
                      *, seq_len):
    x = x_ref[...]
    xx = _prev_rows(x, pa_ref, pb_ref, seq_len) - x
    xw = (x + xx * mu_ref[0]).astype(BF16)
    xa = (x + xx * mu_ref[1]).astype(BF16)
    t = jnp.tanh(jnp.dot(xw, w1_ref[...], preferred_element_type=F32))
    wl = w0_ref[...] + jnp.dot(t.astype(BF16), w2_ref[...], preferred_element_type=F32)
    lw_ref[...] = (-math.exp(-0.5)) / (1.0 + jnp.exp(-wl))
    al = jnp.dot(xa, a1_ref[...], preferred_element_type=F32)
    az = a0_ref[...] + jnp.dot(al.astype(BF16), a2_ref[...], preferred_element_type=F32)
    a_ref[...] = 1.0 / (1.0 + jnp.exp(-az))


def _rwkv_lora(x, xprev, shift, seq_len, mu2, w1, w2, w0, a1, a2, a0):
    M = x.shape[0]
    tm = min(M, LORA_ROWS)
    R = w1.shape[1]
    full = lambda shape: pl.BlockSpec(shape, lambda i: tuple(0 for _ in shape))
    prev_ops, prev_specs, inline_len = _prev_specs(x, xprev, shift, tm, seq_len, 1)
    return pl.pallas_call(
        functools.partial(_rwkv_lora_kernel, seq_len=inline_len),
        grid=(M // tm,),
        in_specs=[pl.BlockSpec((tm, D_MODEL), lambda i: (i, 0)),
                  *prev_specs,
                  full((2, 1, D_MODEL)), full((D_MODEL, R)), full((R, D_MODEL)), full((1, D_MODEL)),
                  full((D_MODEL, R)), full((R, D_MODEL)), full((1, D_MODEL))],
        out_specs=[pl.BlockSpec((tm, D_MODEL), lambda i: (i, 0)),
                   pl.BlockSpec((tm, D_MODEL), lambda i: (i, 0))],
        out_shape=[jax.ShapeDtypeStruct((M, D_MODEL), F32), jax.ShapeDtypeStruct((M, D_MODEL), F32)],
        compiler_params=_cparams(("parallel",)),
        name="rwkv_lora",
    )(x, *prev_ops, mu2, w1, w2, w0.reshape(1, D_MODEL), a1, a2, a0.reshape(1, D_MODEL))


def _head_ones():
    r = lax.broadcasted_iota(jnp.int32, (LANES, LANES), 0) // RWKV_HEAD_DIM
    c = lax.broadcasted_iota(jnp.int32, (LANES, LANES), 1) // RWKV_HEAD_DIM
    e = (r == c).astype(BF16)
    return jnp.concatenate([e, e], axis=0)


def _segsum(x, e2):
    hi = x.astype(BF16)
    lo = (x - hi.astype(F32)).astype(BF16)
    return jnp.dot(jnp.concatenate([hi, lo], axis=1), e2, preferred_element_type=F32)


def _segsum_wide(x, e2):
    rows, n = x.shape[0], x.shape[1] // LANES
    s = _segsum(jnp.concatenate([x[:, i * LANES:(i + 1) * LANES] for i in range(n)], axis=0), e2)
    return jnp.concatenate([s[i * rows:(i + 1) * rows] for i in range(n)], axis=1)


RWKV_PAIRS = 16
RWKV_SEQS = 4
RWKV_CHUNK = 64


def _rwkv_chunk_kernel(r_ref, k_ref, v_ref, gt_ref, lw_ref, a_ref, kk_ref, ka_ref, rk_ref, gng_ref, gnb_ref, s0_ref,
                       o_ref, so_ref, st_ref, *, lreal):
    C, N = RWKV_CHUNK, RWKV_HEAD_DIM
    t_idx = pl.program_id(2)
    e2 = _head_ones()
    m0 = lax.broadcasted_iota(jnp.int32, (C, LANES), 1) < N
    ti = lax.broadcasted_iota(jnp.int32, (C, 2 * C), 0)
    si2 = lax.broadcasted_iota(jnp.int32, (C, 2 * C), 1)
    si = si2 % C
    strict, incl, left = si < ti, si <= ti, si2 < C
    bd = (lax.broadcasted_iota(jnp.int32, (LANES, LANES), 0) // N
          == lax.broadcasted_iota(jnp.int32, (LANES, LANES), 1) // N)
    zs = jnp.zeros((N, N), F32)

    nseq = o_ref.shape[0]

    @pl.when(t_idx == 0)
    def _():
        for bi in range(nseq):
            for p in range(RWKV_PAIRS):
                top = jnp.concatenate([s0_ref[bi, 2 * p], zs], axis=1)
                bot = jnp.concatenate([zs, s0_ref[bi, 2 * p + 1]], axis=1)
                st_ref[bi * RWKV_PAIRS + p] = jnp.concatenate([top, bot], axis=0)

    def both(x, y):
        parts = [x] if y is None else [x, y]
        return jnp.concatenate([jnp.where(m0, z, 0.0) for z in parts] + [jnp.where(m0, 0.0, z) for z in parts],
                               axis=0).astype(BF16)

    def load(x):
        if lreal < C:
            x = jnp.concatenate([x, jnp.zeros((C - lreal, x.shape[1]), F32)], axis=0)
        return x

    inv_n = 1.0 / N
    n_iter = max(1, (lreal - 1).bit_length())
    lanes = [slice(p * LANES, (p + 1) * LANES) for p in range(RWKV_PAIRS)]

    def prep(bi):
        r, k, v = load(r_ref[0, bi]), load(k_ref[0, bi]), load(v_ref[0, bi])
        lw, a = load(lw_ref[bi]), load(a_ref[bi])
        kk = k * kk_ref[...]
        kk = kk / jnp.maximum(jnp.sqrt(_segsum_wide(kk * kk, e2)), 1e-12)
        kp = k * (1.0 + (a - 1.0) * ka_ref[...])
        c = _cumsum_rows(lw, _tril3(C))
        e_c, e_nc = jnp.exp(c), jnp.exp(-c)
        return dict(v=v, e_c=e_c, at_w=-kk * jnp.exp(c - lw), rt_w=r * e_c,
                    bt_w=(kk * a * e_nc).astype(BF16), kt_w=(kp * e_nc).astype(BF16),
                    bonus=_segsum_wide((r * kp * rk_ref[...])[:lreal], e2))

    def scores(bi, q, p):
        ls = lanes[p]
        at, rt = q["at_w"][:, ls], q["rt_w"][:, ls]
        bk = jnp.concatenate([q["bt_w"][:, ls], q["kt_w"][:, ls]], axis=0)
        g = _nt(both(at, rt), bk)
        s_bd = st_ref[bi * RWKV_PAIRS + p]
        pq = _nt(jnp.concatenate([at, rt], axis=0).astype(BF16), s_bd.astype(BF16))
        return dict(bk=bk, s_bd=s_bd, g=g, pq=pq)

    def setup(q, p, d):
        g, pq = d.pop("g"), d.pop("pq")
        aa0, rr0 = jnp.where(strict, g[0:C], 0.0), jnp.where(incl, g[C:2 * C], 0.0)
        aa1, rr1 = jnp.where(strict, g[2 * C:3 * C], 0.0), jnp.where(incl, g[3 * C:], 0.0)
        a_ab = jnp.where(left, aa0, pltpu.roll(aa1, C, 1))
        a_ak = jnp.where(left, pltpu.roll(aa0, C, 1), aa1)
        x = pq[:C] + jnp.dot(a_ak.astype(BF16), both(q["v"][:, lanes[p]], None), preferred_element_type=F32)
        d.update(y0=pq[C:], rr=jnp.concatenate([rr0, rr1], axis=1).astype(BF16), x=x, ac=a_ab)

    def neumann(d, it):
        ac = d["ac"]
        ac16 = ac.astype(BF16)
        rhs = both(d["x"], None)
        if it < n_iter - 1:
            a_bd = jnp.concatenate([jnp.where(left, ac, 0.0), jnp.where(left, 0.0, ac)], axis=0).astype(BF16)
            res = jnp.dot(ac16, jnp.concatenate([rhs, a_bd], axis=1), preferred_element_type=F32)
            d["x"] = d["x"] + res[:, :LANES]
            d["ac"] = res[:, LANES:]
        else:
            d["x"] = d["x"] + jnp.dot(ac16, rhs, preferred_element_type=F32)

    def finish(q, p, d):
        ls = lanes[p]
        u, vp = d["x"], q["v"][:, ls]
        y = d["y0"] + jnp.dot(d["rr"], both(u, vp), preferred_element_type=F32)
        uv = jnp.concatenate([u, vp], axis=0).astype(BF16)
        ds = lax.dot_general(uv, d["bk"], (((0,), (0,)), ((), ())), preferred_element_type=F32)
        return (d["s_bd"] + jnp.where(bd, ds, 0.0)) * q["e_c"][C - 1:C, ls], y[:lreal]

    def output(bi, q, done):
        y = jnp.concatenate([yp for _, yp in done], axis=1)
        yc = y - _segsum_wide(y, e2) * inv_n
        yv = _segsum_wide(yc * yc, e2) * inv_n
        yn = yc * lax.rsqrt(yv + RWKV_GN_EPS) * gng_ref[...] + gnb_ref[...]
        return ((yn + q["bonus"] * q["v"][:lreal]) * gt_ref[0, bi]).astype(o_ref.dtype)

    pairs = range(RWKV_PAIRS)
    stages = [
        lambda bi, s: s.update(q=prep(bi)),
        lambda bi, s: s.update(work=[scores(bi, s["q"], p) for p in pairs]),
        lambda bi, s: [setup(s["q"], p, d) for p, d in enumerate(s["work"])],
        lambda bi, s: [neumann(d, it) for it in range(n_iter) for d in s["work"]],
        lambda bi, s: s.update(done=[finish(s["q"], p, d) for p, d in enumerate(s["work"])]),
        lambda bi, s: s.update(out=output(bi, s["q"], s["done"])),
    ]
    seqs = [dict() for _ in range(nseq)]
    lag = 1
    for tick in range(len(stages) + lag * (nseq - 1)):
        for bi in range(nseq):
            if 0 <= tick - lag * bi < len(stages):
                stages[tick - lag * bi](bi, seqs[bi])
    for bi, s in enumerate(seqs):
        for p, (s_new, _) in enumerate(s["done"]):
            st_ref[bi * RWKV_PAIRS + p] = s_new
        o_ref[bi] = s["out"]

    @pl.when(t_idx == pl.num_programs(2) - 1)
    def _():
        for bi in range(nseq):
            for p in range(RWKV_PAIRS):
                s = st_ref[bi * RWKV_PAIRS + p]
                so_ref[bi, 2 * p] = s[:N, :N]
                so_ref[bi, 2 * p + 1] = s[N:, N:]


def _rwkv_scan(proj, lw, a, k_k, k_a, r_k, gn_g, gn_b, states, j, B, L):
    tc = min(L, RWKV_CHUNK)
    nt = L // tc
    W = RWKV_PAIRS * LANES
    ng = D_MODEL // W
    hpg = 2 * RWKV_PAIRS
    ns = RWKV_SEQS
    assert B % ns == 0
    hd = RWKV_HEAD_DIM
    tok = lambda m: pl.BlockSpec((1, ns, tc, W), lambda b, g, t: (m, b, t, g))
    vec = pl.BlockSpec((ns, tc, W), lambda b, g, t: (b, t, g))
    par = pl.BlockSpec((1, W), lambda b, g, t: (0, g))
    st = pl.BlockSpec((ns, hpg, hd, hd), lambda b, g, t: (b, g, 0, 0))
    st0 = pl.BlockSpec((None, ns, hpg, hd, hd), lambda b, g, t: (j, b, g, 0, 0))
    proj4 = proj.reshape(4, B, L, D_MODEL)
    o, st_out = pl.pallas_call(
        functools.partial(_rwkv_chunk_kernel, lreal=tc),
        grid=(B // ns, ng, nt),
        in_specs=[tok(0), tok(1), tok(2), tok(3), vec, vec, par, par, par, par, par, st0],
        out_specs=[vec, st],
        out_shape=[jax.ShapeDtypeStruct((B, L, D_MODEL), BF16),
                   jax.ShapeDtypeStruct((B, RWKV_HEADS, hd, hd), F32)],
        scratch_shapes=[pltpu.VMEM((ns * RWKV_PAIRS, LANES, LANES), F32)],
        compiler_params=_cparams(("parallel", "parallel", "arbitrary")),
        name="rwkv_scan",
    )(proj4, proj4, proj4, proj4, lw.reshape(B, L, D_MODEL), a.reshape(B, L, D_MODEL), k_k.reshape(1, D_MODEL),
      k_a.reshape(1, D_MODEL), r_k.reshape(1, D_MODEL), gn_g.reshape(1, D_MODEL), gn_b.reshape(1, D_MODEL), states)
    return o.reshape(B * L, D_MODEL), st_out


def _rwkv_layer(x, shift, states, j, p, ln_g, ln_b, B, L):
    x3 = x.reshape(B, L, D_MODEL)
    if L % PROJ_ROWS == 0 and L % LORA_ROWS == 0:
        xprev = None
    else:
        xprev = jnp.concatenate([shift[:, None, :], x3[:, :-1]], axis=1).reshape(B * L, D_MODEL)
    proj = _rwkv_proj(x, xprev, shift, L, p["mu4"], p["w4"])
    lw, a = _rwkv_lora(x, xprev, shift, L, p["mu2"], p["w1"], p["w2"], p["w0"], p["a1"], p["a2"], p["a0"])
    o, st = _rwkv_scan(proj, lw, a, p["k_k"], p["k_a"], p["r_k"], p["gn_g"], p["gn_b"], states, j, B, L)
    return _mm_ln(o, p["w_out"], x, ln_g, ln_b), st, x3[:, -1]


def _pad_rank(w, axis):
    pad = [(0, 0), (0, 0)]
    pad[axis] = (0, LANES - w.shape[axis])
    return jnp.pad(w, pad).astype(BF16)


def _trunk(x3, cache, w):
    B, L, _ = x3.shape
    prompt = cache is None
    x = x3.reshape(B * L, D_MODEL)
    new_k, new_v, new_gla, new_wkv, new_shift = [], [], [], [], []
    for layer in range(DEPTH):
        kind, j = layer % 3, layer // 3
        g, b = w["ln_g"][layer], w["ln_b"][layer]
        if kind == 0:
            c = None if prompt else (cache["k"], cache["v"])
            x, nk, nv = _swa_layer(x, c, j, (w["swa_w_in"], j), w["swa_sink"][j], (w["swa_w_out"], j), g, b, B, L)
            new_k.append(nk)
            new_v.append(nv)
        elif kind == 1:
            st = None if prompt else cache["gla"]
            x, st = _gla_layer(x, st, j, (w["gla_w_in"], j), w["gla_w_low"][j], w["gla_w_a2"][j], w["gla_b_a"][j],
                               w["gla_norm_g"][j], (w["gla_w_out"], j), g, b, B, L)
            new_gla.append(st)
        else:
            if prompt:
                shift0 = jnp.zeros((B, D_MODEL), F32)
                s0, js = jnp.zeros((1, B, RWKV_HEADS, RWKV_HEAD_DIM, RWKV_HEAD_DIM), F32), 0
            else:
                shift0, s0, js = cache["shift"][j], cache["wkv"], j
            x, st, sh = _rwkv_layer(x, shift0, s0, js, w["rwkv"][j], g, b, B, L)
            new_wkv.append(st)
            new_shift.append(sh)
    stack = lambda parts: parts[0][None] if len(parts) == 1 else jnp.stack(parts)
    return (x.reshape(B, L, D_MODEL), stack(new_k), stack(new_v), stack(new_gla), stack(new_wkv), stack(new_shift))


def kernel(x_prompt, x_sample, cache_swa_k, cache_swa_v, state_gla, state_rwkv, state_rwkv_shift, ln_g, ln_b, swa_w_in, swa_sink, swa_w_out, gla_w_in, gla_w_a2, gla_b_a, gla_norm_g, gla_w_out, rwkv_mu, rwkv_w_rkvg, rwkv_w0, rwkv_w1, rwkv_w2, rwkv_a0, rwkv_a1, rwkv_a2, rwkv_k_k, rwkv_k_a, rwkv_r_k, rwkv_gn_g, rwkv_gn_b, rwkv_w_out):
    n_rwkv = rwkv_mu.shape[0]
    gla_main = 2 * GLA_KEY_DIM + 2 * GLA_VAL_DIM
    rwkv = []
    for j in range(n_rwkv):
        rwkv.append(dict(
            mu4=rwkv_mu[j][jnp.array([0, 2, 3, 5])].reshape(4, 1, D_MODEL),
            mu2=rwkv_mu[j][jnp.array([1, 4])].reshape(2, 1, D_MODEL),
            w4=rwkv_w_rkvg[j].astype(BF16),
            w1=_pad_rank(rwkv_w1[j], 1), w2=_pad_rank(rwkv_w2[j], 0), w0=rwkv_w0[j],
            a1=_pad_rank(rwkv_a1[j], 1), a2=_pad_rank(rwkv_a2[j], 0), a0=rwkv_a0[j],
            k_k=rwkv_k_k[j], k_a=rwkv_k_a[j], r_k=rwkv_r_k[j], gn_g=rwkv_gn_g[j], gn_b=rwkv_gn_b[j],
            w_out=rwkv_w_out[j].astype(BF16)))
    w = dict(ln_g=ln_g, ln_b=ln_b,
             swa_w_in=swa_w_in.astype(BF16), swa_sink=swa_sink, swa_w_out=swa_w_out.astype(BF16),
             gla_w_in=gla_w_in.astype(BF16),
             gla_w_low=[_pad_rank(gla_w_in[j][:, gla_main:], 1) for j in range(gla_w_in.shape[0])],
             gla_w_a2=[_pad_rank(gla_w_a2[j], 0) for j in range(gla_w_a2.shape[0])],
             gla_b_a=gla_b_a, gla_norm_g=gla_norm_g, gla_w_out=gla_w_out.astype(BF16), rwkv=rwkv)
    y_p, p_k, p_v, p_gla, p_wkv, p_shift = _trunk(x_prompt, None, w)
    cache = dict(k=cache_swa_k, v=cache_swa_v, gla=state_gla, wkv=state_rwkv, shift=state_rwkv_shift)
    y_s, s_k, s_v, s_gla, s_wkv, s_shift = _trunk(x_sample, cache, w)
    return (y_p, y_s, p_k, p_v, p_gla, p_wkv, p_shift, s_k, s_v, s_gla, s_wkv, s_shift)
```
